```python
import jax, jax.numpy as jnp
from jax import lax
import numpy as np

D_MODEL = 1024
BATCH = 8
SEQ = 2048
DEPTH = 1

A_HEADS = 8
A_KV_HEADS = 2
A_HEAD_DIM = 64
IDX_HEADS = 8
IDX_DIM = 64
TOPK_MAX = 256
B_HEADS = 8
B_HEAD_DIM = 64
A_WIDTH = A_HEADS * A_HEAD_DIM
B_WIDTH = B_HEADS * B_HEAD_DIM
MIX_WIDTH = A_WIDTH + B_WIDTH
A_KV_WIDTH = A_KV_HEADS * A_HEAD_DIM
IDX_Q_WIDTH = IDX_HEADS * IDX_DIM
IN_SPLITS = (A_WIDTH, A_KV_WIDTH, A_KV_WIDTH, IDX_Q_WIDTH, IDX_DIM, IDX_HEADS,
             B_WIDTH, B_WIDTH, B_WIDTH, B_HEADS)
IN_COLS = sum(IN_SPLITS)
Q_BLOCK = 128
N_GROUPS = 4
EXPERTS_PER_GROUP = 8
N_EXPERTS = N_GROUPS * EXPERTS_PER_GROUP
TOPK_EXPERT = 2
D_FF_EXPERT = D_MODEL // 4
N_MOD = 6
EPS = 1e-6

kernel_name = 'hymba_dsa_fox_hmoe_adaln_block'


def rms_norm(x, g):
    xf = x.astype(jnp.float32)
    y = xf * lax.rsqrt(jnp.mean(xf * xf, axis=-1, keepdims=True) + EPS)
    return (y * g.astype(jnp.float32)).astype(x.dtype)


def alibi_slopes(n):
    return jnp.exp2(-8.0 * jnp.arange(1, n + 1, dtype=jnp.float32) / n)


def to_blocks(a, nb):
    return jnp.moveaxis(a.reshape(a.shape[0], nb, Q_BLOCK, *a.shape[2:]), 1, 0)


def dsa_attention(q, k, v, iq, ik, iw):
    bsz, L = q.shape[0], q.shape[1]
    n_top = min(TOPK_MAX, L // 4)
    nb = L // Q_BLOCK
    rep = A_HEADS // A_KV_HEADS
    slopes = alibi_slopes(A_HEADS).reshape(A_KV_HEADS, rep)
    scale = A_HEAD_DIM ** -0.5
    kpos = jnp.arange(L)
    ikf = ik.astype(jnp.float32)

    def block(args):
        qb, iqb, iwb, t0 = args
        tpos = t0 + jnp.arange(Q_BLOCK)
        dots = jnp.einsum('bqhd,bsd->bqhs', iqb.astype(jnp.float32), ikf)
        score = jnp.einsum('bqh,bqhs->bqs', iwb.astype(jnp.float32), jax.nn.relu(dots))
        causal = kpos[None, :] <= tpos[:, None]
        score = jnp.where(causal[None], score, -jnp.inf)
        _, idx = lax.top_k(score, n_top)
        valid = idx <= tpos[None, :, None]
        ksel = jax.vmap(lambda kb, ib: kb[ib])(k, idx)
        vsel = jax.vmap(lambda vb, ib: vb[ib])(v, idx)
        qg = qb.reshape(bsz, Q_BLOCK, A_KV_HEADS, rep, A_HEAD_DIM)
        logits = jnp.einsum('bqgrd,bqkgd->bqgrk', qg, ksel).astype(jnp.float32) * scale
        dist = (tpos[None, :, None] - idx).astype(jnp.float32)
        logits = logits - slopes[None, None, :, :, None] * dist[:, :, None, None, :]
        logits = jnp.where(valid[:, :, None, None, :], logits, -jnp.inf)
        p = jax.nn.softmax(logits, axis=-1).astype(v.dtype)
        o = jnp.einsum('bqgrk,bqkgd->bqgrd', p, vsel)
        return o.reshape(bsz, Q_BLOCK, A_WIDTH)

    out = lax.map(block, (to_blocks(q, nb), to_blocks(iq, nb), to_blocks(iw, nb),
                          jnp.arange(nb, dtype=jnp.int32) * Q_BLOCK))
    return jnp.moveaxis(out, 0, 1).reshape(bsz, L, A_WIDTH)


def forgetting_attention(q, k, v, log_f):
    bsz, L = q.shape[0], q.shape[1]
    nb = L // Q_BLOCK
    scale = B_HEAD_DIM ** -0.5
    cum = jnp.cumsum(log_f, axis=1)
    cum_k = jnp.transpose(cum, (0, 2, 1))
    kpos = jnp.arange(L)

    def block(args):
        qb, cb, t0 = args
        tpos = t0 + jnp.arange(Q_BLOCK)
        logits = jnp.einsum('bqhd,bshd->bhqs', qb, k).astype(jnp.float32) * scale
        logits = logits + jnp.transpose(cb, (0, 2, 1))[..., None] - cum_k[:, :, None, :]
        causal = kpos[None, :] <= tpos[:, None]
        logits = jnp.where(causal[None, None], logits, -jnp.inf)
        p = jax.nn.softmax(logits, axis=-1).astype(v.dtype)
        o = jnp.einsum('bhqs,bshd->bqhd', p, v)
        return o.reshape(bsz, Q_BLOCK, B_WIDTH)

    out = lax.map(block, (to_blocks(q, nb), to_blocks(cum, nb),
                          jnp.arange(nb, dtype=jnp.int32) * Q_BLOCK))
    return jnp.moveaxis(out, 0, 1).reshape(bsz, L, B_WIDTH)


def hybrid_mixer(h, w_in, b_forget, g_out_a, g_out_b, w_out):
    bsz, L, _ = h.shape
    proj = h @ w_in
    (aq, ak, av, iq, ik, iw, bq, bk, bv, bf) = jnp.split(
        proj, list(np.cumsum(IN_SPLITS)[:-1]), axis=-1)
    oa = dsa_attention(aq.reshape(bsz, L, A_HEADS, A_HEAD_DIM),
                       ak.reshape(bsz, L, A_KV_HEADS, A_HEAD_DIM),
                       av.reshape(bsz, L, A_KV_HEADS, A_HEAD_DIM),
                       iq.reshape(bsz, L, IDX_HEADS, IDX_DIM), ik, iw)
    log_f = jax.nn.log_sigmoid(bf.astype(jnp.float32) + b_forget.astype(jnp.float32))
    ob = forgetting_attention(bq.reshape(bsz, L, B_HEADS, B_HEAD_DIM),
                              bk.reshape(bsz, L, B_HEADS, B_HEAD_DIM),
                              bv.reshape(bsz, L, B_HEADS, B_HEAD_DIM), log_f)
    o = jnp.concatenate([rms_norm(oa, g_out_a), rms_norm(ob, g_out_b)], axis=-1)
    return o @ w_out


def hierarchical_moe(h, w_group, b_group, w_router, b_router, w_gate, w_up, w_down):
    bsz, L, d = h.shape
    t = h.reshape(-1, d)
    g_logits = (t @ w_group).astype(jnp.float32) + b_group.astype(jnp.float32)
    g_prob = jax.nn.softmax(g_logits, axis=-1)
    g_sel = jnp.argmax(g_logits, axis=-1)
    p_group = jnp.take_along_axis(g_prob, g_sel[:, None], axis=1)[:, 0]
    e_all = jnp.einsum('nd,gde->nge', t, w_router).astype(jnp.float32) + b_router.astype(jnp.float32)
    e_logits = jnp.take_along_axis(e_all, g_sel[:, None, None], axis=1)[:, 0]
    top_v, top_i = lax.top_k(e_logits, TOPK_EXPERT)
    w_top = jax.nn.softmax(top_v, axis=-1) * p_group[:, None]
    eid = g_sel[:, None] * EXPERTS_PER_GROUP + top_i
    comb = jnp.sum(jax.nn.one_hot(eid, N_EXPERTS, dtype=jnp.float32) * w_top[..., None], axis=1)
    hg = jnp.einsum('nd,edf->nef', t, w_gate)
    hu = jnp.einsum('nd,edf->nef', t, w_up)
    a = jax.nn.silu(hg) * hu * comb[:, :, None].astype(h.dtype)
    y = jnp.einsum('nef,efd->nd', a, w_down)
    return y.reshape(bsz, L, d)


def setup_inputs(seed: int = 0) -> dict:
    key = jax.random.key(seed)
    ks = jax.random.split(key, 20)
    f32 = jnp.float32
    nrm = lambda k, shape, s: jax.random.normal(k, shape, f32) * s
    return {
        'x': nrm(ks[0], (BATCH, SEQ, D_MODEL), 1.0),
        'c': nrm(ks[1], (BATCH, D_MODEL), 1.0),
        'w_ada': nrm(ks[2], (DEPTH, D_MODEL, N_MOD * D_MODEL), 0.5 * D_MODEL ** -0.5),
        'b_ada': nrm(ks[3], (DEPTH, N_MOD * D_MODEL), 0.01),
        'g_mix': 1.0 + nrm(ks[4], (DEPTH, D_MODEL), 0.01),
        'w_in': nrm(ks[5], (DEPTH, D_MODEL, IN_COLS), D_MODEL ** -0.5),
        'b_forget': jax.random.uniform(ks[6], (DEPTH, B_HEADS), f32, 1.0, 4.0),
        'g_out_a': 1.0 + nrm(ks[7], (DEPTH, A_WIDTH), 0.01),
        'g_out_b': 1.0 + nrm(ks[8], (DEPTH, B_WIDTH), 0.01),
        'w_out': nrm(ks[9], (DEPTH, MIX_WIDTH, D_MODEL), MIX_WIDTH ** -0.5),
        'g_ffn': 1.0 + nrm(ks[10], (DEPTH, D_MODEL), 0.01),
        'w_group': nrm(ks[11], (DEPTH, D_MODEL, N_GROUPS), D_MODEL ** -0.5),
        'b_group': nrm(ks[12], (DEPTH, N_GROUPS), 0.01),
        'w_router': nrm(ks[13], (DEPTH, N_GROUPS, D_MODEL, EXPERTS_PER_GROUP), D_MODEL ** -0.5),
        'b_router': nrm(ks[14], (DEPTH, N_GROUPS, EXPERTS_PER_GROUP), 0.01),
        'w_gate': nrm(ks[15], (DEPTH, N_EXPERTS, D_MODEL, D_FF_EXPERT), D_MODEL ** -0.5),
        'w_up': nrm(ks[16], (DEPTH, N_EXPERTS, D_MODEL, D_FF_EXPERT), D_MODEL ** -0.5),
        'w_down': nrm(ks[17], (DEPTH, N_EXPERTS, D_FF_EXPERT, D_MODEL), D_FF_EXPERT ** -0.5),
        'g_final': 1.0 + nrm(ks[18], (D_MODEL,), 0.01),
    }


def reference(x, c, w_ada, b_ada, g_mix, w_in, b_forget, g_out_a, g_out_b, w_out,
              g_ffn, w_group, b_group, w_router, b_router, w_gate, w_up, w_down, g_final):
    for l in range(DEPTH):
        mod = (jax.nn.silu(c) @ w_ada[l] + b_ada[l])[:, None, :]
        shift_m, scale_m, gate_m, shift_f, scale_f, gate_f = jnp.split(mod, N_MOD, axis=-1)
        h = rms_norm(x, g_mix[l]) * (1.0 + scale_m) + shift_m
        x = x + gate_m * hybrid_mixer(h, w_in[l], b_forget[l], g_out_a[l], g_out_b[l], w_out[l])
        h = rms_norm(x, g_ffn[l]) * (1.0 + scale_f) + shift_f
        x = x + gate_f * hierarchical_moe(h, w_group[l], b_group[l], w_router[l], b_router[l],
                                          w_gate[l], w_up[l], w_down[l])
    return rms_norm(x, g_final)
```

```python
import functools

import jax
import jax.numpy as jnp
from jax import lax
from jax.experimental import pallas as pl
from jax.experimental.pallas import tpu as pltpu

F32 = jnp.float32
BF16 = jnp.bfloat16

EPS = 1e-6
A_HEADS = 8
A_KV_HEADS = 2
HEAD_DIM = 64
IDX_HEADS = 8
IDX_DIM = 64
TOPK = 256
B_HEADS = 8
N_GROUPS = 4
EXPERTS_PER_GROUP = 8
N_EXPERTS = N_GROUPS * EXPERTS_PER_GROUP
N_MOD = 6

NEG_BIG = -1e30
VMEM_LIMIT = 48 * 1024 * 1024

TQ = 256
KC = 256
assert TQ == KC == TOPK


def _split_bf16(x):
    hi = x.astype(BF16)
    lo = (x - hi.astype(F32)).astype(BF16)
    return hi, lo


def _dot(a, b):
    return jnp.dot(a, b, preferred_element_type=F32)


def _dot_nt(a, b):
    return lax.dot_general(a, b, (((1,), (1,)), ((), ())), preferred_element_type=F32)


def _dot3(a_hi, a_lo, b_hi, b_lo):
    return _dot(a_hi, b_hi) + _dot(a_lo, b_hi) + _dot(a_hi, b_lo)


def _rms(x, g):
    return x * lax.rsqrt(jnp.mean(x * x, axis=-1, keepdims=True) + EPS) * g


def _ada_kernel(c_ref, w_ref, b_ref, o_ref):
    c = c_ref[...]
    s = c * jax.nn.sigmoid(c)
    s_hi, s_lo = _split_bf16(s)
    w_hi, w_lo = _split_bf16(w_ref[...])
    o_ref[...] = _dot3(s_hi, s_lo, w_hi, w_lo) + b_ref[...]


def _ada_mod(c, w_ada, b_ada):
    bsz, d = c.shape
    n = w_ada.shape[1]
    tn = 1024
    return pl.pallas_call(
        _ada_kernel,
        out_shape=jax.ShapeDtypeStruct((bsz, n), F32),
        grid=(n // tn,),
        in_specs=[pl.BlockSpec((bsz, d), lambda j: (0, 0)),
                  pl.BlockSpec((d, tn), lambda j: (0, j)),
                  pl.BlockSpec((1, tn), lambda j: (0, j))],
        out_specs=pl.BlockSpec((bsz, tn), lambda j: (0, j)),
        compiler_params=pltpu.CompilerParams(dimension_semantics=("arbitrary",),
                                             vmem_limit_bytes=VMEM_LIMIT),
        name="ada_mod",
    )(c, w_ada, b_ada.reshape(1, n))


IDX_COLS = 640


def _in_proj_kernel(x_ref, mod_ref, g_ref, wm_ref, wih_ref, wil_ref,
                    aq_ref, akv_ref, bq_ref, bk_ref, bv_ref, idx_ref):
    x = x_ref[...]
    shift = mod_ref[0, 0:1, :]
    scale = mod_ref[0, 1:2, :]
    h = _rms(x, g_ref[...]) * (1.0 + scale) + shift
    h_hi, h_lo = _split_bf16(h)
    aq_ref[...] = _dot(h_hi, wm_ref[:, 0:512]).astype(BF16)
    akv_ref[...] = _dot(h_hi, wm_ref[:, 512:768]).astype(BF16)
    bq_ref[...] = _dot(h_hi, wm_ref[:, 768:1280]).astype(BF16)
    bk_ref[...] = _dot(h_hi, wm_ref[:, 1280:1792]).astype(BF16)
    bv_ref[...] = _dot(h_hi, wm_ref[:, 1792:2304]).astype(BF16)
    idx_ref[...] = _dot3(h_hi, h_lo, wih_ref[...], wil_ref[...])


def _in_proj(x2, mod3, g_mix, w_main, w_idx_hi, w_idx_lo, seq):
    n, d = x2.shape
    tm = 512
    per_b = seq // tm
    row = lambda i: (i, 0)
    const = lambda i: (0, 0)
    outs = [jax.ShapeDtypeStruct((n, 512), BF16), jax.ShapeDtypeStruct((n, 256), BF16),
            jax.ShapeDtypeStruct((n, 512), BF16), jax.ShapeDtypeStruct((n, 512), BF16),
            jax.ShapeDtypeStruct((n, 512), BF16), jax.ShapeDtypeStruct((n, IDX_COLS), F32)]
    return pl.pallas_call(
        _in_proj_kernel,
        out_shape=outs,
        grid=(n // tm,),
        in_specs=[pl.BlockSpec((tm, d), row),
                  pl.BlockSpec((1, N_MOD, d), lambda i: (i // per_b, 0, 0)),
                  pl.BlockSpec((1, d), const),
                  pl.BlockSpec(w_main.shape, const),
                  pl.BlockSpec(w_idx_hi.shape, const),
                  pl.BlockSpec(w_idx_lo.shape, const)],
        out_specs=[pl.BlockSpec((tm, 512), row), pl.BlockSpec((tm, 256), row),
                   pl.BlockSpec((tm, 512), row), pl.BlockSpec((tm, 512), row),
                   pl.BlockSpec((tm, 512), row), pl.BlockSpec((tm, IDX_COLS), row)],
        compiler_params=pltpu.CompilerParams(dimension_semantics=("arbitrary",),
                                             vmem_limit_bytes=VMEM_LIMIT),
        name="in_proj",
    )(x2, mod3, g_mix.reshape(1, d), w_main, w_idx_hi, w_idx_lo)


CB = 256


def _cum_kernel(bft_ref, bfor_ref, cumt_ref):
    seq = bft_ref.shape[2]
    r = lax.broadcasted_iota(jnp.int32, (CB, CB), 0)
    cidx = lax.broadcasted_iota(jnp.int32, (CB, CB), 1)
    tri = jnp.where(r <= cidx, 1.0, 0.0).astype(BF16)
    carry = jnp.zeros((8, 1), F32)
    for blk in range(seq // CB):
        z = bft_ref[0, :, blk * CB:(blk + 1) * CB] + bfor_ref[...]
        logf = jnp.minimum(z, 0.0) - jnp.log(1.0 + jnp.exp(-jnp.abs(z)))
        p1 = logf.astype(BF16).astype(F32)
        r1 = logf - p1
        p2 = r1.astype(BF16).astype(F32)
        p3 = (r1 - p2).astype(BF16).astype(F32)
        pieces = jnp.concatenate([p1, p2, p3, jnp.zeros_like(p1)], axis=0).astype(BF16)
        parts = _dot(pieces, tri)
        cum = parts[0:8] + parts[8:16] + parts[16:24] + carry
        cumt_ref[0, :, blk * CB:(blk + 1) * CB] = cum
        carry = cum[:, CB - 1:CB]


def _fox_cum(bft, b_forget):
    bsz, nh, seq = bft.shape
    return pl.pallas_call(
        _cum_kernel,
        out_shape=jax.ShapeDtypeStruct((bsz, nh, seq), F32),
        grid=(bsz,),
        in_specs=[pl.BlockSpec((1, nh, seq), lambda b: (b, 0, 0)),
                  pl.BlockSpec((nh, 1), lambda b: (0, 0))],
        out_specs=pl.BlockSpec((1, nh, seq), lambda b: (b, 0, 0)),
        compiler_params=pltpu.CompilerParams(dimension_semantics=("arbitrary",),
                                             vmem_limit_bytes=VMEM_LIMIT),
        name="fox_cum",
    )(bft, b_forget.reshape(nh, 1))


def _attend(q, nch, kv_fn, bias_fn):
    def body(kc, carry):
        m, l, acc = carry
        k, v = kv_fn(kc)
        s = _dot_nt(q, k) + bias_fn(kc)
        m_new = jnp.maximum(m, jnp.max(s, axis=-1, keepdims=True))
        alpha = jnp.exp(m - m_new)
        p = jnp.exp(s - m_new)
        l = alpha * l + jnp.sum(p, axis=-1, keepdims=True)
        acc = alpha * acc + _dot(p.astype(BF16), v)
        return m_new, l, acc

    init = (jnp.full((TQ, 1), NEG_BIG, F32), jnp.zeros((TQ, 1), F32), jnp.zeros((TQ, HEAD_DIM), F32))
    _, l, acc = lax.fori_loop(0, nch, body, init)
    return acc / l


def _row_minus_col():
    return (lax.broadcasted_iota(jnp.int32, (TQ, KC), 0)
            - lax.broadcasted_iota(jnp.int32, (TQ, KC), 1))


def _fox_kernel(q_ref, k_ref, v_ref, cum_ref, cumt_ref, o_ref):
    qi = pl.program_id(1)
    nch = qi + 1
    d0 = _row_minus_col()
    for h in range(B_HEADS):
        lanes = slice(h * HEAD_DIM, (h + 1) * HEAD_DIM)
        q = q_ref[0, :, lanes] * 0.125
        cq = cum_ref[0, :, h:h + 1]

        def kv_fn(kc, lanes=lanes):
            rows = pl.ds(pl.multiple_of(kc * KC, KC), KC)
            return k_ref[0, rows, lanes], v_ref[0, rows, lanes]

        def bias_fn(kc, h=h, cq=cq):
            cols = pl.ds(pl.multiple_of(kc * KC, KC), KC)
            ck = cumt_ref[0, h:h + 1, cols]
            causal = d0 >= (kc - qi) * KC
            return jnp.where(causal, cq - ck, NEG_BIG)

        o_ref[0, :, lanes] = _attend(q, nch, kv_fn, bias_fn)


def _fox_attn(bq, bk, bv, cum, cumt):
    bsz, seq, w = bq.shape
    blk = lambda b, i: (b, i, 0)
    full = lambda b, i: (b, 0, 0)
    return pl.pallas_call(
        _fox_kernel,
        out_shape=jax.ShapeDtypeStruct((bsz, seq, w), F32),
        grid=(bsz, seq // TQ),
        in_specs=[pl.BlockSpec((1, TQ, w), blk),
                  pl.BlockSpec((1, seq, w), full),
                  pl.BlockSpec((1, seq, w), full),
                  pl.BlockSpec((1, TQ, B_HEADS), blk),
                  pl.BlockSpec((1, B_HEADS, seq), full)],
        out_specs=pl.BlockSpec((1, TQ, w), blk),
        compiler_params=pltpu.CompilerParams(dimension_semantics=("arbitrary", "arbitrary"),
                                             vmem_limit_bytes=VMEM_LIMIT),
        name="fox_attn",
    )(bq, bk, bv, cum, cumt)


def _chunk(kc):
    return pl.ds(pl.multiple_of(kc * KC, KC), KC)


def _fold_lanes(x, op):
    out = x[:, 0:128]
    for j in range(1, KC // 128):
        out = op(out, x[:, j * 128:(j + 1) * 128])
    return out


def _dsa_kernel(iq_ref, ik_ref, iw_ref, q_ref, kv_ref, o_ref, s_ref):
    qi = pl.program_id(1)
    nch = qi + 1
    d0 = _row_minus_col()

    iw = iw_ref[0]

    def score_body(kc, _):
        ik = ik_ref[0, _chunk(kc), :]
        acc = jnp.zeros((TQ, KC), F32)
        for h in range(IDX_HEADS):
            d = _dot_nt(iq_ref[0, :, h * 256:(h + 1) * 256], ik)
            acc = acc + iw[:, h:h + 1] * jnp.maximum(d, 0.0)
        acc = jnp.where(acc == 0.0, 0.0, acc)
        causal = d0 >= (kc - qi) * KC
        s_ref[:, _chunk(kc)] = jnp.where(causal, acc, -jnp.inf)
        return 0

    lax.fori_loop(0, nch, score_body, 0)

    @pl.when(qi == 0)
    def _():
        s_ref[:, 0:KC] = jnp.where(d0 >= 0, 0.0, NEG_BIG)

    @pl.when(qi > 0)
    def _():
        def scan(fn, init):
            return lax.fori_loop(0, nch, lambda kc, c: fn(s_ref[:, _chunk(kc)], c), init)

        zeros = jnp.zeros((TQ, 128), F32)
        pinf = jnp.full((TQ, 128), jnp.inf, F32)
        ninf = jnp.full((TQ, 128), -jnp.inf, F32)

        def init_fn(s, c):
            lo, hi = c
            lo = jnp.minimum(lo, _fold_lanes(jnp.where(s > -jnp.inf, s, jnp.inf), jnp.minimum))
            hi = jnp.maximum(hi, _fold_lanes(s, jnp.maximum))
            return lo, hi

        lo, hi = scan(init_fn, (pinf, ninf))
        lo = jnp.min(lo, axis=-1, keepdims=True)
        hi = jnp.max(hi, axis=-1, keepdims=True)

        def cond(carry):
            return carry[2] > 0

        def step(carry):
            lo, hi, _ = carry
            mid = lo + (hi - lo) * 0.5
            mid = jnp.where(mid <= lo, hi, mid)

            def fn(s, c):
                cnt, a, b = c
                ge = s >= mid
                cnt = cnt + _fold_lanes(jnp.where(ge, 1.0, 0.0), jnp.add)
                b = jnp.minimum(b, _fold_lanes(jnp.where(ge, s, jnp.inf), jnp.minimum))
                a = jnp.maximum(a, _fold_lanes(jnp.where(ge, -jnp.inf, s), jnp.maximum))
                return cnt, a, b

            cnt, a, b = scan(fn, (zeros, ninf, pinf))
            cnt = jnp.sum(cnt, axis=-1, keepdims=True)
            a = jnp.max(a, axis=-1, keepdims=True)
            b = jnp.min(b, axis=-1, keepdims=True)
            enough = cnt >= float(TOPK)
            new_lo = jnp.where(enough, b, lo)
            new_hi = jnp.where(enough, jnp.where(cnt == float(TOPK), b, hi), a)
            active = jnp.max(jnp.where(new_lo < new_hi, 1, 0))
            return new_lo, new_hi, active

        first_active = jnp.max(jnp.where(lo < hi, 1, 0))
        thr, _, _ = lax.while_loop(cond, step, (lo, hi, first_active))

        n_gt = scan(lambda s, c: c + _fold_lanes(jnp.where(s > thr, 1.0, 0.0), jnp.add), zeros)
        need = float(TOPK) - jnp.sum(n_gt, axis=-1, keepdims=True)
        upper = jnp.where(d0 < 0, 1.0, 0.0).astype(BF16)

        def sel_body(kc, run):
            s = s_ref[:, _chunk(kc)]
            eq = s == thr
            eqf = jnp.where(eq, 1.0, 0.0)
            before = _dot(eqf.astype(BF16), upper) + run
            sel = (s > thr) | (eq & (before < need))
            s_ref[:, _chunk(kc)] = jnp.where(sel, 0.0, NEG_BIG)
            return run + jnp.sum(eqf, axis=-1, keepdims=True)

        lax.fori_loop(0, nch, sel_body, jnp.zeros((TQ, 1), F32))

    rep = A_HEADS // A_KV_HEADS
    d0f = d0.astype(F32)
    for h in range(A_HEADS):
        g = h // rep
        slope = 2.0 ** (-8.0 * (h + 1) / A_HEADS)
        lanes = slice(h * HEAD_DIM, (h + 1) * HEAD_DIM)
        q = q_ref[0, :, lanes] * 0.125

        def kv_fn(kc, g=g):
            rows = _chunk(kc)
            return (kv_ref[0, rows, g * HEAD_DIM:(g + 1) * HEAD_DIM],
                    kv_ref[0, rows, 128 + g * HEAD_DIM:128 + (g + 1) * HEAD_DIM])

        def bias_fn(kc, slope=slope):
            dist = d0f + ((qi - kc) * KC).astype(F32)
            return s_ref[:, _chunk(kc)] - slope * dist

        o_ref[0, :, lanes] = _attend(q, nch, kv_fn, bias_fn)


def _dsa_attn(iq4, ik4, iw, aq, akv):
    bsz, seq, _ = aq.shape
    blk = lambda b, i: (b, i, 0)
    full = lambda b, i: (b, 0, 0)
    return pl.pallas_call(
        _dsa_kernel,
        out_shape=jax.ShapeDtypeStruct((bsz, seq, A_HEADS * HEAD_DIM), F32),
        grid=(bsz, seq // TQ),
        in_specs=[pl.BlockSpec((1, TQ, IDX_HEADS * 256), blk),
                  pl.BlockSpec((1, seq, 256), full),
                  pl.BlockSpec((1, TQ, IDX_HEADS), blk),
                  pl.BlockSpec((1, TQ, A_HEADS * HEAD_DIM), blk),
                  pl.BlockSpec((1, seq, 256), full)],
        out_specs=pl.BlockSpec((1, TQ, A_HEADS * HEAD_DIM), blk),
        scratch_shapes=[pltpu.VMEM((TQ, seq), F32)],
        compiler_params=pltpu.CompilerParams(dimension_semantics=("arbitrary", "arbitrary"),
                                             vmem_limit_bytes=VMEM_LIMIT),
        name="dsa_attn",
    )(iq4, ik4, iw, aq, akv)


def _out_proj_kernel(oa_ref, ob_ref, x_ref, mod_ref, ga_ref, gb_ref, w_ref, o_ref):
    oa = _rms(oa_ref[...], ga_ref[...]).astype(BF16)
    ob = _rms(ob_ref[...], gb_ref[...]).astype(BF16)
    y = _dot(oa, w_ref[0:512, :]) + _dot(ob, w_ref[512:1024, :])
    o_ref[...] = x_ref[...] + mod_ref[0, 2:3, :] * y


def _out_proj(oa2, ob2, x2, mod3, g_out_a, g_out_b, w_out_bf, seq):
    n, d = x2.shape
    tm = 512
    per_b = seq // tm
    row = lambda i: (i, 0)
    const = lambda i: (0, 0)
    return pl.pallas_call(
        _out_proj_kernel,
        out_shape=jax.ShapeDtypeStruct((n, d), F32),
        grid=(n // tm,),
        in_specs=[pl.BlockSpec((tm, 512), row), pl.BlockSpec((tm, 512), row),
                  pl.BlockSpec((tm, d), row),
                  pl.BlockSpec((1, N_MOD, d), lambda i: (i // per_b, 0, 0)),
                  pl.BlockSpec((1, 512), const), pl.BlockSpec((1, 512), const),
                  pl.BlockSpec((d, d), const)],
        out_specs=pl.BlockSpec((tm, d), row),
        compiler_params=pltpu.CompilerParams(dimension_semantics=("arbitrary",),
                                             vmem_limit_bytes=VMEM_LIMIT),
        name="out_proj",
    )(oa2, ob2, x2, mod3, g_out_a.reshape(1, -1), g_out_b.reshape(1, -1), w_out_bf)


MOE_TM = 1024
ROUTE_COLS = 128


def _first_index_of_max(vals, lane, big):
    m = jnp.max(vals, axis=-1, keepdims=True)
    idx = jnp.min(jnp.where(vals == m, lane, big), axis=-1, keepdims=True)
    return m, idx


def _route(h, wr_hi_ref, wr_lo_ref, br_ref):
    h_hi, h_lo = _split_bf16(h)
    logits = _dot3(h_hi, h_lo, wr_hi_ref[...], wr_lo_ref[...]) + br_ref[...]
    lane = lax.broadcasted_iota(jnp.int32, logits.shape, 1)
    ninf = -jnp.inf
    gl = jnp.where(lane < N_GROUPS, logits, ninf)
    gmax, gsel = _first_index_of_max(gl, lane, 1 << 20)
    p_group = 1.0 / jnp.sum(jnp.exp(gl - gmax), axis=-1, keepdims=True)
    base = 32 + gsel * EXPERTS_PER_GROUP
    el = jnp.where((lane >= base) & (lane < base + EXPERTS_PER_GROUP), logits, ninf)
    v1, i1 = _first_index_of_max(el, lane, 1 << 20)
    el2 = jnp.where(lane == i1, ninf, el)
    v2, i2 = _first_index_of_max(el2, lane, 1 << 20)
    e2 = jnp.exp(v2 - v1)
    w1 = 1.0 / (1.0 + e2)
    w2 = e2 / (1.0 + e2)
    return (jnp.where(lane == i1, w1 * p_group, 0.0)
            + jnp.where(lane == i2, w2 * p_group, 0.0))


def _moe_kernel(x_ref, mod_ref, gf_ref, wr_hi_ref, wr_lo_ref, br_ref, wg_ref, wu_ref, wd_ref,
                gfin_ref, o_ref, h_ref, comb_ref, y_ref):
    e = pl.program_id(1)

    @pl.when(e == 0)
    def _():
        h = _rms(x_ref[...], gf_ref[...]) * (1.0 + mod_ref[0, 4:5, :]) + mod_ref[0, 3:4, :]
        comb_ref[...] = _route(h, wr_hi_ref, wr_lo_ref, br_ref)
        h_ref[...] = h.astype(BF16)
        y_ref[...] = jnp.zeros_like(y_ref)

    h = h_ref[...]
    hg = _dot(h, wg_ref[0])
    hu = _dot(h, wu_ref[0])
    lane = lax.broadcasted_iota(jnp.int32, comb_ref.shape, 1)
    comb_e = jnp.sum(jnp.where(lane == 32 + e, comb_ref[...], 0.0), axis=-1, keepdims=True)
    a = hg * jax.nn.sigmoid(hg) * hu * comb_e
    y_ref[...] += _dot(a.astype(BF16), wd_ref[0])

    @pl.when(e == N_EXPERTS - 1)
    def _():
        x2 = x_ref[...] + mod_ref[0, 5:6, :] * y_ref[...]
        o_ref[...] = _rms(x2, gfin_ref[...])


def _moe(x1, mod3, g_ffn, wr_hi, wr_lo, b_route, wg, wu, wd, g_final, seq):
    n, d = x1.shape
    tm = MOE_TM
    per_b = seq // tm
    ff = wg.shape[-1]
    row = lambda i, e: (i, 0)
    const = lambda i, e: (0, 0)
    return pl.pallas_call(
        _moe_kernel,
        out_shape=jax.ShapeDtypeStruct((n, d), F32),
        grid=(n // tm, N_EXPERTS),
        in_specs=[pl.BlockSpec((tm, d), row),
                  pl.BlockSpec((1, N_MOD, d), lambda i, e: (i // per_b, 0, 0)),
                  pl.BlockSpec((1, d), const),
                  pl.BlockSpec((d, ROUTE_COLS), const), pl.BlockSpec((d, ROUTE_COLS), const),
                  pl.BlockSpec((1, ROUTE_COLS), const),
                  pl.BlockSpec((1, d, ff), lambda i, e: (e, 0, 0)),
                  pl.BlockSpec((1, d, ff), lambda i, e: (e, 0, 0)),
                  pl.BlockSpec((1, ff, d), lambda i, e: (e, 0, 0)),
                  pl.BlockSpec((1, d), const)],
        out_specs=pl.BlockSpec((tm, d), row),
        scratch_shapes=[pltpu.VMEM((tm, d), BF16), pltpu.VMEM((tm, ROUTE_COLS), F32),
                        pltpu.VMEM((tm, d), F32)],
        compiler_params=pltpu.CompilerParams(dimension_semantics=("arbitrary", "arbitrary"),
                                             vmem_limit_bytes=VMEM_LIMIT),
        name="moe",
    )(x1, mod3, g_ffn.reshape(1, d), wr_hi, wr_lo, b_route, wg, wu, wd, g_final.reshape(1, d))


def _layer(x2, c, seq, w_ada, b_ada, g_mix, w_in, b_forget, g_out_a, g_out_b, w_out,
           g_ffn, w_group, b_group, w_router, b_router, w_gate, w_up, w_down, g_final):
    n, d = x2.shape
    bsz = n // seq
    mod3 = _ada_mod(c, w_ada, b_ada).reshape(bsz, N_MOD, d)

    w_main = jnp.concatenate([w_in[:, 0:768], w_in[:, 1352:2888]], axis=1).astype(BF16)
    w_idx = jnp.concatenate([w_in[:, 768:1352], w_in[:, 2888:2896],
                             jnp.zeros((d, IDX_COLS - 592), F32)], axis=1)
    w_idx_hi, w_idx_lo = _split_bf16(w_idx)

    aq, akv, bq, bk, bv, idx = _in_proj(x2, mod3, g_mix, w_main, w_idx_hi, w_idx_lo, seq)

    bft = jnp.transpose(idx[:, 584:592].reshape(bsz, seq, B_HEADS), (0, 2, 1))
    cumt = _fox_cum(bft, b_forget)
    cum = jnp.transpose(cumt, (0, 2, 1))
    r3 = lambda a: a.reshape(bsz, seq, a.shape[-1])
    ob = _fox_attn(r3(bq), r3(bk), r3(bv), cum, cumt)

    iq = idx[:, 0:512].reshape(n, IDX_HEADS, IDX_DIM)
    iq_hi, iq_lo = _split_bf16(iq)
    iq4 = jnp.concatenate([iq_hi, iq_lo, iq_hi, iq_lo], axis=-1).reshape(bsz, seq, IDX_HEADS * 256)
    ik_hi, ik_lo = _split_bf16(idx[:, 512:576])
    ik4 = jnp.concatenate([ik_hi, ik_hi, ik_lo, ik_lo], axis=-1).reshape(bsz, seq, 256)
    iw = idx[:, 576:584].reshape(bsz, seq, IDX_HEADS)
    oa = _dsa_attn(iq4, ik4, iw, r3(aq), r3(akv))

    x1 = _out_proj(oa.reshape(n, -1), ob.reshape(n, -1), x2, mod3, g_out_a, g_out_b,
                   w_out.astype(BF16), seq)

    w_r = jnp.concatenate([w_group, jnp.zeros((d, 32 - N_GROUPS), F32),
                           jnp.transpose(w_router, (1, 0, 2)).reshape(d, N_EXPERTS),
                           jnp.zeros((d, ROUTE_COLS - 64), F32)], axis=1)
    b_r = jnp.concatenate([b_group, jnp.zeros((32 - N_GROUPS,), F32), b_router.reshape(-1),
                           jnp.zeros((ROUTE_COLS - 64,), F32)]).reshape(1, ROUTE_COLS)
    wr_hi, wr_lo = _split_bf16(w_r)
    return _moe(x1, mod3, g_ffn, wr_hi, wr_lo, b_r, w_gate.astype(BF16), w_up.astype(BF16),
                w_down.astype(BF16), g_final, seq)


def kernel(x, c, w_ada, b_ada, g_mix, w_in, b_forget, g_out_a, g_out_b, w_out, g_ffn, w_group,
           b_group, w_router, b_router, w_gate, w_up, w_down, g_final):
    bsz, seq, d = x.shape
    depth = w_ada.shape[0]
    assert depth == 1, "final norm is fused into the single layer's MoE kernel"
    out = _layer(x.reshape(bsz * seq, d), c, seq, w_ada[0], b_ada[0], g_mix[0], w_in[0], b_forget[0],
                 g_out_a[0], g_out_b[0], w_out[0], g_ffn[0], w_group[0], b_group[0], w_router[0],
                 b_router[0], w_gate[0], w_up[0], w_down[0], g_final)
    return out.reshape(bsz, seq, d)
```

```python
import jax
import jax.numpy as jnp
from jax import lax
from jax.experimental import pallas as pl
from jax.experimental.pallas import tpu as pltpu

F32 = jnp.float32
BF16 = jnp.bfloat16

EPS = 1e-6
A_HEADS = 8
A_KV_HEADS = 2
HEAD_DIM = 64
IDX_HEADS = 8
IDX_DIM = 64
TOPK = 256
B_HEADS = 8
N_GROUPS = 4
EXPERTS_PER_GROUP = 8
N_EXPERTS = N_GROUPS * EXPERTS_PER_GROUP
N_MOD = 6

NEG_BIG = -1e30
VMEM_LIMIT = 48 * 1024 * 1024

TQ = 256
KC = 256
AUG = 128
IDX_K = 256
assert TQ == KC == TOPK


def _split_bf16(x):
    hi = x.astype(BF16)
    lo = (x - hi.astype(F32)).astype(BF16)
    return hi, lo


def _split3_f32(x):
    p1 = x.astype(BF16).astype(F32)
    r1 = x - p1
    p2 = r1.astype(BF16).astype(F32)
    p3 = (r1 - p2).astype(BF16).astype(F32)
    return p1, p2, p3


def _dot(a, b):
    return jnp.dot(a, b, preferred_element_type=F32)


def _dot_nt(a, b):
    return lax.dot_general(a, b, (((1,), (1,)), ((), ())), preferred_element_type=F32)


def _dot_tn(a, b):
    return lax.dot_general(a, b, (((0,), (0,)), ((), ())), preferred_element_type=F32)


def _dot3(a_hi, a_lo, b_hi, b_lo):
    return _dot(a_hi, b_hi) + _dot(a_lo, b_hi) + _dot(a_hi, b_lo)


def _dot3_nt(a_hi, a_lo, b_hi, b_lo):
    return _dot_nt(a_hi, b_hi) + _dot_nt(a_lo, b_hi) + _dot_nt(a_hi, b_lo)


def _rms(x, g):
    return x * lax.rsqrt(jnp.mean(x * x, axis=-1, keepdims=True) + EPS) * g


def _chunk(kc):
    return pl.ds(pl.multiple_of(kc * KC, KC), KC)


def _key_minus_query():
    return (lax.broadcasted_iota(jnp.int32, (KC, TQ), 0)
            - lax.broadcasted_iota(jnp.int32, (KC, TQ), 1))


def _ada_kernel(c_ref, w_ref, b_ref, o_ref):
    c = c_ref[...]
    s = c * jax.nn.sigmoid(c)
    s_hi, s_lo = _split_bf16(s)
    w_hi, w_lo = _split_bf16(w_ref[...])
    o_ref[...] = _dot3(s_hi, s_lo, w_hi, w_lo) + b_ref[...]


def _ada_mod(c, w_ada, b_ada):
    bsz, d = c.shape
    n = w_ada.shape[1]
    tn = 1024
    return pl.pallas_call(
        _ada_kernel,
        out_shape=jax.ShapeDtypeStruct((bsz, n), F32),
        grid=(n // tn,),
        in_specs=[pl.BlockSpec((bsz, d), lambda j: (0, 0)),
                  pl.BlockSpec((d, tn), lambda j: (0, j)),
                  pl.BlockSpec((1, tn), lambda j: (0, j))],
        out_specs=pl.BlockSpec((bsz, tn), lambda j: (0, j)),
        compiler_params=pltpu.CompilerParams(dimension_semantics=("arbitrary",),
                                             vmem_limit_bytes=VMEM_LIMIT),
        name="ada_mod",
    )(c, w_ada, b_ada.reshape(1, n))


def _in_proj_kernel(x_ref, mod_ref, g_ref,
                    waq_ref, wak_ref, wav_ref, wbq_ref, wbk_ref, wbv_ref,
                    wiqh_ref, wiql_ref, wikh_ref, wikl_ref, wwfh_ref, wwfl_ref,
                    aqt_ref, ak_ref, avt_ref, bqt_ref, bk_ref, bvt_ref, iqt_ref, ik_ref, wft_ref):
    x = x_ref[0]
    h = _rms(x, g_ref[...]) * (1.0 + mod_ref[0, 1:2, :]) + mod_ref[0, 0:1, :]
    h_hi, h_lo = _split_bf16(h)
    aqt_ref[0] = (_dot_nt(waq_ref[...], h_hi) * 0.125).astype(BF16)
    ak_ref[0] = _dot(h_hi, wak_ref[...]).astype(BF16)
    avt_ref[0] = _dot_nt(wav_ref[...], h_hi).astype(BF16)
    bqt_ref[0] = (_dot_nt(wbq_ref[...], h_hi) * 0.125).astype(BF16)
    bk_ref[0] = _dot(h_hi, wbk_ref[...]).astype(BF16)
    bvt_ref[0] = _dot_nt(wbv_ref[...], h_hi).astype(BF16)
    iqt = _dot3_nt(wiqh_ref[...], wiql_ref[...], h_hi, h_lo)
    for hd in range(IDX_HEADS):
        q_hi, q_lo = _split_bf16(iqt[hd * IDX_DIM:(hd + 1) * IDX_DIM, :])
        iqt_ref[0, hd * IDX_K:(hd + 1) * IDX_K, :] = jnp.concatenate([q_hi, q_lo, q_hi, q_lo], axis=0)
    ik2 = _dot3(h_hi, h_lo, wikh_ref[...], wikl_ref[...])
    k_hi, k_lo = _split_bf16(ik2)
    ik_ref[0] = jnp.concatenate([k_hi, k_lo], axis=1)
    wft_ref[0] = _dot3_nt(wwfh_ref[...], wwfl_ref[...], h_hi, h_lo)


def _in_proj(x3, mod3, g_mix, weights):
    bsz, seq, d = x3.shape
    tm = 512
    blk_t = lambda b, i: (b, 0, i)
    blk_r = lambda b, i: (b, i, 0)
    const = lambda b, i: (0, 0)
    ak_w = A_KV_HEADS * AUG
    bk_w = B_HEADS * AUG
    outs = [jax.ShapeDtypeStruct((bsz, 512, seq), BF16), jax.ShapeDtypeStruct((bsz, seq, ak_w), BF16),
            jax.ShapeDtypeStruct((bsz, 128, seq), BF16), jax.ShapeDtypeStruct((bsz, 512, seq), BF16),
            jax.ShapeDtypeStruct((bsz, seq, bk_w), BF16), jax.ShapeDtypeStruct((bsz, 512, seq), BF16),
            jax.ShapeDtypeStruct((bsz, IDX_HEADS * IDX_K, seq), BF16),
            jax.ShapeDtypeStruct((bsz, seq, 256), BF16), jax.ShapeDtypeStruct((bsz, 16, seq), F32)]
    out_specs = [pl.BlockSpec((1, 512, tm), blk_t), pl.BlockSpec((1, tm, ak_w), blk_r),
                 pl.BlockSpec((1, 128, tm), blk_t), pl.BlockSpec((1, 512, tm), blk_t),
                 pl.BlockSpec((1, tm, bk_w), blk_r), pl.BlockSpec((1, 512, tm), blk_t),
                 pl.BlockSpec((1, IDX_HEADS * IDX_K, tm), blk_t),
                 pl.BlockSpec((1, tm, 256), blk_r), pl.BlockSpec((1, 16, tm), blk_t)]
    return pl.pallas_call(
        _in_proj_kernel,
        out_shape=outs,
        grid=(bsz, seq // tm),
        in_specs=[pl.BlockSpec((1, tm, d), blk_r),
                  pl.BlockSpec((1, N_MOD, d), lambda b, i: (b, 0, 0)),
                  pl.BlockSpec((1, d), const)] + [pl.BlockSpec(w.shape, const) for w in weights],
        out_specs=out_specs,
        compiler_params=pltpu.CompilerParams(dimension_semantics=("arbitrary", "arbitrary"),
                                             vmem_limit_bytes=VMEM_LIMIT),
        name="in_proj",
    )(x3, mod3, g_mix.reshape(1, d), *weights)


CB = 256


def _cum_kernel(wft_ref, bfor_ref, k_ref, cumt_ref, kaug_ref):
    seq = wft_ref.shape[2]
    r = lax.broadcasted_iota(jnp.int32, (CB, CB), 0)
    cidx = lax.broadcasted_iota(jnp.int32, (CB, CB), 1)
    tri = jnp.where(r <= cidx, 1.0, 0.0).astype(BF16)
    row128 = lax.broadcasted_iota(jnp.int32, (AUG, CB), 0)
    ones_rows = jnp.where((row128 >= HEAD_DIM + 3) & (row128 < HEAD_DIM + 6), 1.0, 0.0)
    carry = jnp.zeros((8, 1), F32)
    for blk in range(seq // CB):
        cols = slice(blk * CB, (blk + 1) * CB)
        z = wft_ref[0, 8:16, cols] + bfor_ref[...]
        logf = jnp.minimum(z, 0.0) - jnp.log(1.0 + jnp.exp(-jnp.abs(z)))
        p1, p2, p3 = _split3_f32(logf)
        pieces = jnp.concatenate([p1, p2, p3, jnp.zeros_like(p1)], axis=0).astype(BF16)
        parts = _dot(pieces, tri)
        cum = parts[0:8] + parts[8:16] + parts[16:24] + carry
        cumt_ref[0, :, cols] = cum
        carry = cum[:, CB - 1:CB]
        c1, c2, c3 = _split3_f32(cum)
        for h in range(B_HEADS):
            spare = jnp.where(row128 == HEAD_DIM, -c1[h:h + 1], ones_rows)
            spare = jnp.where(row128 == HEAD_DIM + 1, -c2[h:h + 1], spare)
            spare = jnp.where(row128 == HEAD_DIM + 2, -c3[h:h + 1], spare)
            lanes = slice(h * AUG, (h + 1) * AUG)
            kaug_ref[0, cols, lanes] = k_ref[0, cols, lanes] + spare.T.astype(BF16)


def _fox_cum(wft, b_forget, bk):
    bsz, _, seq = wft.shape
    nh = B_HEADS
    kw = bk.shape[-1]
    return pl.pallas_call(
        _cum_kernel,
        out_shape=[jax.ShapeDtypeStruct((bsz, nh, seq), F32), jax.ShapeDtypeStruct((bsz, seq, kw), BF16)],
        grid=(bsz,),
        in_specs=[pl.BlockSpec((1, 16, seq), lambda b: (b, 0, 0)),
                  pl.BlockSpec((nh, 1), lambda b: (0, 0)),
                  pl.BlockSpec((1, seq, kw), lambda b: (b, 0, 0))],
        out_specs=[pl.BlockSpec((1, nh, seq), lambda b: (b, 0, 0)),
                   pl.BlockSpec((1, seq, kw), lambda b: (b, 0, 0))],
        compiler_params=pltpu.CompilerParams(dimension_semantics=("arbitrary",),
                                             vmem_limit_bytes=VMEM_LIMIT),
        name="fox_cum",
    )(wft, b_forget.reshape(nh, 1), bk)


def _softmax_init(m_ref, l_ref, acc_ref):
    m_ref[...] = jnp.full(m_ref.shape, NEG_BIG, F32)
    l_ref[...] = jnp.zeros(l_ref.shape, F32)
    acc_ref[...] = jnp.zeros(acc_ref.shape, F32)


def _attend_chunk(n_heads, score_fn, vt_fn, m_ref, l_ref, acc_ref):
    scores = [score_fn(h) for h in range(n_heads)]
    probs, alphas = [], []
    for h in range(n_heads):
        s = scores[h]
        m_old = m_ref[h]
        m_new = jnp.maximum(m_old, jnp.max(s, axis=0, keepdims=True))
        alpha = jnp.exp(m_old - m_new)
        p = jnp.exp(s - m_new)
        l_ref[h] = alpha * l_ref[h] + jnp.sum(p, axis=0, keepdims=True)
        m_ref[h] = m_new
        probs.append(p.astype(BF16))
        alphas.append(alpha)
    for h in range(n_heads):
        rows = slice(h * HEAD_DIM, (h + 1) * HEAD_DIM)
        acc_ref[rows, :] = alphas[h] * acc_ref[rows, :] + _dot(vt_fn(h), probs[h])


def _softmax_finish(o_ref, n_heads, l_ref, acc_ref):
    for h in range(n_heads):
        rows = slice(h * HEAD_DIM, (h + 1) * HEAD_DIM)
        o_ref[0, rows, :] = acc_ref[rows, :] / l_ref[h]


def _fox_kernel(qt_ref, k_ref, vt_ref, cumt_ref, o_ref, w_ref, m_ref, l_ref, acc_ref):
    qi = pl.program_id(1)
    _softmax_init(m_ref, l_ref, acc_ref)

    row64 = lax.broadcasted_iota(jnp.int32, (AUG - HEAD_DIM, TQ), 0)
    for h in range(B_HEADS):
        c1, c2, c3 = _split3_f32(cumt_ref[0, h:h + 1, :])
        spare = jnp.where(row64 < 3, 1.0, 0.0)
        spare = jnp.where(row64 == 3, c1, spare)
        spare = jnp.where(row64 == 4, c2, spare)
        spare = jnp.where(row64 == 5, c3, spare)
        w_ref[h] = jnp.concatenate([qt_ref[0, h * HEAD_DIM:(h + 1) * HEAD_DIM, :],
                                    spare.astype(BF16)], axis=0)

    def tile(kc, diagonal):
        rows = _chunk(kc)

        def score_fn(h):
            s = _dot(k_ref[0, rows, h * AUG:(h + 1) * AUG], w_ref[h])
            if diagonal:
                s = jnp.where(_key_minus_query() <= 0, s, NEG_BIG)
            return s

        def vt_fn(h):
            return vt_ref[0, h * HEAD_DIM:(h + 1) * HEAD_DIM, rows]

        _attend_chunk(B_HEADS, score_fn, vt_fn, m_ref, l_ref, acc_ref)

    def body(kc, _):
        tile(kc, False)
        return 0

    lax.fori_loop(0, qi, body, 0)
    tile(qi, True)
    _softmax_finish(o_ref, B_HEADS, l_ref, acc_ref)


def _fox_attn(bqt, kaug, bvt, cumt):
    bsz, w, seq = bqt.shape
    blk_t = lambda b, i: (b, 0, i)
    full = lambda b, i: (b, 0, 0)
    return pl.pallas_call(
        _fox_kernel,
        out_shape=jax.ShapeDtypeStruct((bsz, w, seq), F32),
        grid=(bsz, seq // TQ),
        in_specs=[pl.BlockSpec((1, w, TQ), blk_t),
                  pl.BlockSpec((1, seq, B_HEADS * AUG), full),
                  pl.BlockSpec((1, w, seq), full),
                  pl.BlockSpec((1, B_HEADS, TQ), blk_t)],
        out_specs=pl.BlockSpec((1, w, TQ), blk_t),
        scratch_shapes=[pltpu.VMEM((B_HEADS, AUG, TQ), BF16), pltpu.VMEM((B_HEADS, 1, TQ), F32),
                        pltpu.VMEM((B_HEADS, 1, TQ), F32), pltpu.VMEM((B_HEADS * HEAD_DIM, TQ), F32)],
        compiler_params=pltpu.CompilerParams(dimension_semantics=("arbitrary", "arbitrary"),
                                             vmem_limit_bytes=VMEM_LIMIT),
        name="fox_attn",
    )(bqt, kaug, bvt, cumt)


def _dsa_kernel(iqt_ref, ik_ref, wft_ref, qt_ref, k_ref, posx_ref, vt_ref, o_ref,
                s_ref, w_ref, m_ref, l_ref, acc_ref):
    qi = pl.program_id(1)
    nch = qi + 1
    kmq = _key_minus_query()

    def score_body(kc, _):
        ik = ik_ref[0, _chunk(kc), :]
        acc = jnp.zeros((KC, TQ), F32)
        for h in range(IDX_HEADS):
            d = _dot(ik, iqt_ref[0, h * IDX_K:(h + 1) * IDX_K, :])
            acc = acc + wft_ref[0, h:h + 1, :] * jnp.maximum(d, 0.0)
        acc = jnp.where(acc == 0.0, 0.0, acc)
        causal = kmq <= (qi - kc) * KC
        s_ref[_chunk(kc), :] = jnp.where(causal, acc, -jnp.inf)
        return 0

    lax.fori_loop(0, nch, score_body, 0)

    @pl.when(qi == 0)
    def _():
        s_ref[0:KC, :] = jnp.where(kmq <= 0, 0.0, NEG_BIG)

    @pl.when(qi > 0)
    def _():
        def scan(fn, init):
            return lax.fori_loop(0, nch, lambda kc, c: fn(s_ref[_chunk(kc), :], c), init)

        def cmin(x):
            return jnp.min(x, axis=0, keepdims=True)

        def cmax(x):
            return jnp.max(x, axis=0, keepdims=True)

        def csum(x):
            return jnp.sum(x, axis=0, keepdims=True)

        zeros = jnp.zeros((1, TQ), F32)
        pinf = jnp.full((1, TQ), jnp.inf, F32)
        ninf = jnp.full((1, TQ), -jnp.inf, F32)

        def init_fn(s, c):
            lo, hi = c
            lo = jnp.minimum(lo, cmin(jnp.where(s > -jnp.inf, s, jnp.inf)))
            hi = jnp.maximum(hi, cmax(s))
            return lo, hi

        lo, hi = scan(init_fn, (pinf, ninf))

        def cond(carry):
            return carry[2] > 0

        def step(carry):
            lo, hi, _ = carry
            mid = lo + (hi - lo) * 0.5
            mid = jnp.where(mid <= lo, hi, mid)

            def fn(s, c):
                cnt, a, b = c
                ge = s >= mid
                cnt = cnt + csum(jnp.where(ge, 1.0, 0.0))
                b = jnp.minimum(b, cmin(jnp.where(ge, s, jnp.inf)))
                a = jnp.maximum(a, cmax(jnp.where(ge, -jnp.inf, s)))
                return cnt, a, b

            cnt, a, b = scan(fn, (zeros, ninf, pinf))
            enough = cnt >= float(TOPK)
            new_lo = jnp.where(enough, b, lo)
            new_hi = jnp.where(enough, jnp.where(cnt == float(TOPK), b, hi), a)
            active = jnp.max(jnp.where(new_lo < new_hi, 1, 0))
            return new_lo, new_hi, active

        first_active = jnp.max(jnp.where(lo < hi, 1, 0))
        thr, _, _ = lax.while_loop(cond, step, (lo, hi, first_active))

        n_gt = scan(lambda s, c: c + csum(jnp.where(s > thr, 1.0, 0.0)), zeros)
        need = float(TOPK) - n_gt
        lower = jnp.where(lax.broadcasted_iota(jnp.int32, (KC, KC), 1)
                          < lax.broadcasted_iota(jnp.int32, (KC, KC), 0), 1.0, 0.0).astype(BF16)

        def sel_body(kc, run):
            s = s_ref[_chunk(kc), :]
            eq = s == thr
            eqf = jnp.where(eq, 1.0, 0.0)
            before = _dot(lower, eqf.astype(BF16)) + run
            sel = (s > thr) | (eq & (before < need))
            s_ref[_chunk(kc), :] = jnp.where(sel, 0.0, NEG_BIG)
            return run + csum(eqf)

        lax.fori_loop(0, nch, sel_body, zeros)

    _softmax_init(m_ref, l_ref, acc_ref)
    rep = A_HEADS // A_KV_HEADS
    row64 = lax.broadcasted_iota(jnp.int32, (AUG - HEAD_DIM, TQ), 0)
    qpos = qi * TQ + lax.broadcasted_iota(jnp.int32, (AUG - HEAD_DIM, TQ), 1)
    q_hi = (qpos >> 7).astype(F32)
    q_lo = (qpos & 127).astype(F32)
    for h in range(A_HEADS):
        slope = 2.0 ** (-8.0 * (h + 1) / A_HEADS)
        spare = jnp.where(row64 == 0, 128.0 * slope, 0.0)
        spare = jnp.where(row64 == 1, slope, spare)
        spare = jnp.where(row64 == 2, (-128.0 * slope) * q_hi, spare)
        spare = jnp.where(row64 == 3, (-slope) * q_lo, spare)
        w_ref[h] = jnp.concatenate([qt_ref[0, h * HEAD_DIM:(h + 1) * HEAD_DIM, :],
                                    spare.astype(BF16)], axis=0)

    def attn_body(kc, _):
        rows = _chunk(kc)
        pos = posx_ref[rows, :]
        lhs = [k_ref[0, rows, g * AUG:(g + 1) * AUG] + pos for g in range(A_KV_HEADS)]
        bias = s_ref[rows, :]

        def score_fn(h):
            return _dot(lhs[h // rep], w_ref[h]) + bias

        def vt_fn(h):
            g = h // rep
            return vt_ref[0, g * HEAD_DIM:(g + 1) * HEAD_DIM, rows]

        _attend_chunk(A_HEADS, score_fn, vt_fn, m_ref, l_ref, acc_ref)
        return 0

    lax.fori_loop(0, nch, attn_body, 0)
    _softmax_finish(o_ref, A_HEADS, l_ref, acc_ref)


def _dsa_attn(iqt, ik4, wft, aqt, ak, posx, avt):
    bsz, w, seq = aqt.shape
    blk_t = lambda b, i: (b, 0, i)
    full = lambda b, i: (b, 0, 0)
    return pl.pallas_call(
        _dsa_kernel,
        out_shape=jax.ShapeDtypeStruct((bsz, w, seq), F32),
        grid=(bsz, seq // TQ),
        in_specs=[pl.BlockSpec((1, IDX_HEADS * IDX_K, TQ), blk_t),
                  pl.BlockSpec((1, seq, 256), full),
                  pl.BlockSpec((1, 16, TQ), blk_t),
                  pl.BlockSpec((1, w, TQ), blk_t),
                  pl.BlockSpec((1, seq, A_KV_HEADS * AUG), full),
                  pl.BlockSpec((seq, AUG), lambda b, i: (0, 0)),
                  pl.BlockSpec((1, 128, seq), full)],
        out_specs=pl.BlockSpec((1, w, TQ), blk_t),
        scratch_shapes=[pltpu.VMEM((seq, TQ), F32), pltpu.VMEM((A_HEADS, AUG, TQ), BF16),
                        pltpu.VMEM((A_HEADS, 1, TQ), F32), pltpu.VMEM((A_HEADS, 1, TQ), F32),
                        pltpu.VMEM((A_HEADS * HEAD_DIM, TQ), F32)],
        compiler_params=pltpu.CompilerParams(dimension_semantics=("arbitrary", "arbitrary"),
                                             vmem_limit_bytes=VMEM_LIMIT),
        name="dsa_attn",
    )(iqt, ik4, wft, aqt, ak, posx, avt)


def _rms_cols(xt, g_col):
    return xt * lax.rsqrt(jnp.mean(xt * xt, axis=0, keepdims=True) + EPS) * g_col


def _out_proj_kernel(oat_ref, obt_ref, x_ref, mod_ref, ga_ref, gb_ref, w_ref, o_ref):
    oa = _rms_cols(oat_ref[0], ga_ref[...]).astype(BF16)
    ob = _rms_cols(obt_ref[0], gb_ref[...]).astype(BF16)
    y = _dot_tn(oa, w_ref[0:512, :]) + _dot_tn(ob, w_ref[512:1024, :])
    o_ref[0] = x_ref[0] + mod_ref[0, 2:3, :] * y


def _out_proj(oat, obt, x3, mod3, g_out_a, g_out_b, w_out_bf):
    bsz, seq, d = x3.shape
    tm = 512
    blk_t = lambda b, i: (b, 0, i)
    blk_r = lambda b, i: (b, i, 0)
    const = lambda b, i: (0, 0)
    return pl.pallas_call(
        _out_proj_kernel,
        out_shape=jax.ShapeDtypeStruct((bsz, seq, d), F32),
        grid=(bsz, seq // tm),
        in_specs=[pl.BlockSpec((1, 512, tm), blk_t), pl.BlockSpec((1, 512, tm), blk_t),
                  pl.BlockSpec((1, tm, d), blk_r),
                  pl.BlockSpec((1, N_MOD, d), lambda b, i: (b, 0, 0)),
                  pl.BlockSpec((512, 1), const), pl.BlockSpec((512, 1), const),
                  pl.BlockSpec((d, d), const)],
        out_specs=pl.BlockSpec((1, tm, d), blk_r),
        compiler_params=pltpu.CompilerParams(dimension_semantics=("arbitrary", "arbitrary"),
                                             vmem_limit_bytes=VMEM_LIMIT),
        name="out_proj",
    )(oat, obt, x3, mod3, g_out_a.reshape(-1, 1), g_out_b.reshape(-1, 1), w_out_bf)


MOE_TM = 1024
ROUTE_COLS = 128


def _first_index_of_max(vals, lane, big):
    m = jnp.max(vals, axis=-1, keepdims=True)
    idx = jnp.min(jnp.where(vals == m, lane, big), axis=-1, keepdims=True)
    return m, idx


def _route(h, wr_hi_ref, wr_lo_ref, br_ref):
    h_hi, h_lo = _split_bf16(h)
    logits = _dot3(h_hi, h_lo, wr_hi_ref[...], wr_lo_ref[...]) + br_ref[...]
    lane = lax.broadcasted_iota(jnp.int32, logits.shape, 1)
    ninf = -jnp.inf
    gl = jnp.where(lane < N_GROUPS, logits, ninf)
    gmax, gsel = _first_index_of_max(gl, lane, 1 << 20)
    p_group = 1.0 / jnp.sum(jnp.exp(gl - gmax), axis=-1, keepdims=True)
    base = 32 + gsel * EXPERTS_PER_GROUP
    el = jnp.where((lane >= base) & (lane < base + EXPERTS_PER_GROUP), logits, ninf)
    v1, i1 = _first_index_of_max(el, lane, 1 << 20)
    el2 = jnp.where(lane == i1, ninf, el)
    v2, i2 = _first_index_of_max(el2, lane, 1 << 20)
    e2 = jnp.exp(v2 - v1)
    w1 = 1.0 / (1.0 + e2)
    w2 = e2 / (1.0 + e2)
    return (jnp.where(lane == i1, w1 * p_group, 0.0)
            + jnp.where(lane == i2, w2 * p_group, 0.0))


def _moe_kernel(x_ref, mod_ref, gf_ref, wr_hi_ref, wr_lo_ref, br_ref, wg_ref, wu_ref, wd_ref,
                gfin_ref, o_ref, h_ref, comb_ref, y_ref):
    e = pl.program_id(1)

    @pl.when(e == 0)
    def _():
        h = _rms(x_ref[...], gf_ref[...]) * (1.0 + mod_ref[0, 4:5, :]) + mod_ref[0, 3:4, :]
        comb_ref[...] = _route(h, wr_hi_ref, wr_lo_ref, br_ref)
        h_ref[...] = h.astype(BF16)
        y_ref[...] = jnp.zeros_like(y_ref)

    h = h_ref[...]
    hg = _dot(h, wg_ref[0])
    hu = _dot(h, wu_ref[0])
    lane = lax.broadcasted_iota(jnp.int32, comb_ref.shape, 1)
    comb_e = jnp.sum(jnp.where(lane == 32 + e, comb_ref[...], 0.0), axis=-1, keepdims=True)
    a = hg * jax.nn.sigmoid(hg) * hu * comb_e
    y_ref[...] += _dot(a.astype(BF16), wd_ref[0])

    @pl.when(e == N_EXPERTS - 1)
    def _():
        x2 = x_ref[...] + mod_ref[0, 5:6, :] * y_ref[...]
        o_ref[...] = _rms(x2, gfin_ref[...])


def _moe(x1, mod3, g_ffn, wr_hi, wr_lo, b_route, wg, wu, wd, g_final, seq):
    n, d = x1.shape
    tm = MOE_TM
    per_b = seq // tm
    ff = wg.shape[-1]
    row = lambda i, e: (i, 0)
    const = lambda i, e: (0, 0)
    return pl.pallas_call(
        _moe_kernel,
        out_shape=jax.ShapeDtypeStruct((n, d), F32),
        grid=(n // tm, N_EXPERTS),
        in_specs=[pl.BlockSpec((tm, d), row),
                  pl.BlockSpec((1, N_MOD, d), lambda i, e: (i // per_b, 0, 0)),
                  pl.BlockSpec((1, d), const),
                  pl.BlockSpec((d, ROUTE_COLS), const), pl.BlockSpec((d, ROUTE_COLS), const),
                  pl.BlockSpec((1, ROUTE_COLS), const),
                  pl.BlockSpec((1, d, ff), lambda i, e: (e, 0, 0)),
                  pl.BlockSpec((1, d, ff), lambda i, e: (e, 0, 0)),
                  pl.BlockSpec((1, ff, d), lambda i, e: (e, 0, 0)),
                  pl.BlockSpec((1, d), const)],
        out_specs=pl.BlockSpec((tm, d), row),
        scratch_shapes=[pltpu.VMEM((tm, d), BF16), pltpu.VMEM((tm, ROUTE_COLS), F32),
                        pltpu.VMEM((tm, d), F32)],
        compiler_params=pltpu.CompilerParams(dimension_semantics=("arbitrary", "arbitrary"),
                                             vmem_limit_bytes=VMEM_LIMIT),
        name="moe",
    )(x1, mod3, g_ffn.reshape(1, d), wr_hi, wr_lo, b_route, wg, wu, wd, g_final.reshape(1, d))


def _layer(x3, c, w_ada, b_ada, g_mix, w_in, b_forget, g_out_a, g_out_b, w_out,
           g_ffn, w_group, b_group, w_router, b_router, w_gate, w_up, w_down, g_final):
    bsz, seq, d = x3.shape
    mod3 = _ada_mod(c, w_ada, b_ada).reshape(bsz, N_MOD, d)

    w_t = w_in.T

    def pad_heads(w, n_heads):
        w = w.reshape(d, n_heads, HEAD_DIM)
        return jnp.concatenate([w, jnp.zeros_like(w)], axis=-1).reshape(d, n_heads * AUG)

    w_ik2 = jnp.concatenate([w_in[:, 1280:1344], w_in[:, 1280:1344]], axis=1)
    w_wf_t = jnp.concatenate([w_t[1344:1352], w_t[2888:2896]], axis=0)
    weights = [w_t[0:512].astype(BF16), pad_heads(w_in[:, 512:640], A_KV_HEADS).astype(BF16),
               w_t[640:768].astype(BF16), w_t[1352:1864].astype(BF16),
               pad_heads(w_in[:, 1864:2376], B_HEADS).astype(BF16), w_t[2376:2888].astype(BF16),
               *_split_bf16(w_t[768:1280]), *_split_bf16(w_ik2), *_split_bf16(w_wf_t)]
    aqt, ak, avt, bqt, bk, bvt, iqt, ik4, wft = _in_proj(x3, mod3, g_mix, weights)

    cumt, kaug = _fox_cum(wft, b_forget, bk)
    obt = _fox_attn(bqt, kaug, bvt, cumt)

    pos = jnp.arange(seq, dtype=jnp.int32)[:, None]
    lane = jnp.arange(AUG, dtype=jnp.int32)[None, :] - HEAD_DIM
    posx = jnp.where(lane == 0, pos >> 7, jnp.where(lane == 1, pos & 127,
                     jnp.where((lane == 2) | (lane == 3), 1, 0))).astype(BF16)
    oat = _dsa_attn(iqt, ik4, wft, aqt, ak, posx, avt)

    x1 = _out_proj(oat, obt, x3, mod3, g_out_a, g_out_b, w_out.astype(BF16))

    w_r = jnp.concatenate([w_group, jnp.zeros((d, 32 - N_GROUPS), F32),
                           jnp.transpose(w_router, (1, 0, 2)).reshape(d, N_EXPERTS),
                           jnp.zeros((d, ROUTE_COLS - 64), F32)], axis=1)
    b_r = jnp.concatenate([b_group, jnp.zeros((32 - N_GROUPS,), F32), b_router.reshape(-1),
                           jnp.zeros((ROUTE_COLS - 64,), F32)]).reshape(1, ROUTE_COLS)
    wr_hi, wr_lo = _split_bf16(w_r)
    out = _moe(x1.reshape(bsz * seq, d), mod3, g_ffn, wr_hi, wr_lo, b_r, w_gate.astype(BF16),
               w_up.astype(BF16), w_down.astype(BF16), g_final, seq)
    return out.reshape(bsz, seq, d)


def kernel(x, c, w_ada, b_ada, g_mix, w_in, b_forget, g_out_a, g_out_b, w_out, g_ffn, w_group,
           b_group, w_router, b_router, w_gate, w_up, w_down, g_final):
    depth = w_ada.shape[0]
    assert depth == 1, "final norm is fused into the single layer's MoE kernel"
    return _layer(x, c, w_ada[0], b_ada[0], g_mix[0], w_in[0], b_forget[0], g_out_a[0], g_out_b[0],
                  w_out[0], g_ffn[0], w_group[0], b_group[0], w_router[0], b_router[0], w_gate[0],
                  w_up[0], w_down[0], g_final)
```

```python
import jax
import jax.numpy as jnp
from jax import lax
from jax.experimental import pallas as pl
from jax.experimental.pallas import tpu as pltpu

F32 = jnp.float32
BF16 = jnp.bfloat16

EPS = 1e-6
A_HEADS = 8
A_KV_HEADS = 2
HEAD_DIM = 64
IDX_HEADS = 8
IDX_DIM = 64
TOPK = 256
B_HEADS = 8
N_GROUPS = 4
EXPERTS_PER_GROUP = 8
N_EXPERTS = N_GROUPS * EXPERTS_PER_GROUP
N_MOD = 6

NEG_BIG = -1e30
VMEM_LIMIT = 48 * 1024 * 1024

TQ = 256
KC = 256
AUG = 128
IDX_K = 256
COARSE_STEPS = 12
assert TQ == KC == TOPK


def _split_bf16(x):
    hi = x.astype(BF16)
    lo = (x - hi.astype(F32)).astype(BF16)
    return hi, lo


def _split3_f32(x):
    p1 = x.astype(BF16).astype(F32)
    r1 = x - p1
    p2 = r1.astype(BF16).astype(F32)
    p3 = (r1 - p2).astype(BF16).astype(F32)
    return p1, p2, p3


def _dot(a, b):
    return jnp.dot(a, b, preferred_element_type=F32)


def _dot_nt(a, b):
    return lax.dot_general(a, b, (((1,), (1,)), ((), ())), preferred_element_type=F32)


def _dot_tn(a, b):
    return lax.dot_general(a, b, (((0,), (0,)), ((), ())), preferred_element_type=F32)


def _dot3(a_hi, a_lo, b_hi, b_lo):
    return _dot(a_hi, b_hi) + _dot(a_lo, b_hi) + _dot(a_hi, b_lo)


def _dot3_nt(a_hi, a_lo, b_hi, b_lo):
    return _dot_nt(a_hi, b_hi) + _dot_nt(a_lo, b_hi) + _dot_nt(a_hi, b_lo)


def _rms(x, g):
    return x * lax.rsqrt(jnp.mean(x * x, axis=-1, keepdims=True) + EPS) * g


def _chunk(kc):
    return pl.ds(pl.multiple_of(kc * KC, KC), KC)


def _key_minus_query():
    return (lax.broadcasted_iota(jnp.int32, (KC, TQ), 0)
            - lax.broadcasted_iota(jnp.int32, (KC, TQ), 1))


def _ada_kernel(c_ref, w_ref, b_ref, o_ref):
    c = c_ref[...]
    s = c * jax.nn.sigmoid(c)
    s_hi, s_lo = _split_bf16(s)
    w_hi, w_lo = _split_bf16(w_ref[...])
    o_ref[...] = _dot3(s_hi, s_lo, w_hi, w_lo) + b_ref[...]


def _ada_mod(c, w_ada, b_ada):
    bsz, d = c.shape
    n = w_ada.shape[1]
    tn = 1024
    return pl.pallas_call(
        _ada_kernel,
        out_shape=jax.ShapeDtypeStruct((bsz, n), F32),
        grid=(n // tn,),
        in_specs=[pl.BlockSpec((bsz, d), lambda j: (0, 0)),
                  pl.BlockSpec((d, tn), lambda j: (0, j)),
                  pl.BlockSpec((1, tn), lambda j: (0, j))],
        out_specs=pl.BlockSpec((bsz, tn), lambda j: (0, j)),
        compiler_params=pltpu.CompilerParams(dimension_semantics=("arbitrary",),
                                             vmem_limit_bytes=VMEM_LIMIT),
        name="ada_mod",
    )(c, w_ada, b_ada.reshape(1, n))


def _in_proj_kernel(x_ref, mod_ref, g_ref,
                    waq_ref, wak_ref, wav_ref, wbq_ref, wbk_ref, wbv_ref,
                    wiqh_ref, wiql_ref, wikh_ref, wikl_ref, wwfh_ref, wwfl_ref,
                    aqt_ref, ak_ref, avt_ref, bqt_ref, bk_ref, bvt_ref, iqt_ref, ik_ref, wft_ref):
    x = x_ref[0]
    h = _rms(x, g_ref[...]) * (1.0 + mod_ref[0, 1:2, :]) + mod_ref[0, 0:1, :]
    h_hi, h_lo = _split_bf16(h)
    aqt_ref[0] = (_dot_nt(waq_ref[...], h_hi) * 0.125).astype(BF16)
    ak_ref[0] = _dot(h_hi, wak_ref[...]).astype(BF16)
    avt_ref[0] = _dot_nt(wav_ref[...], h_hi).astype(BF16)
    bqt_ref[0] = (_dot_nt(wbq_ref[...], h_hi) * 0.125).astype(BF16)
    bk_ref[0] = _dot(h_hi, wbk_ref[...]).astype(BF16)
    bvt_ref[0] = _dot_nt(wbv_ref[...], h_hi).astype(BF16)
    iqt = _dot3_nt(wiqh_ref[...], wiql_ref[...], h_hi, h_lo)
    for hd in range(IDX_HEADS):
        q_hi, q_lo = _split_bf16(iqt[hd * IDX_DIM:(hd + 1) * IDX_DIM, :])
        iqt_ref[0, hd * IDX_K:(hd + 1) * IDX_K, :] = jnp.concatenate([q_hi, q_lo, q_hi, q_lo], axis=0)
    ik2 = _dot3(h_hi, h_lo, wikh_ref[...], wikl_ref[...])
    k_hi, k_lo = _split_bf16(ik2)
    ik_ref[0] = jnp.concatenate([k_hi, k_lo], axis=1)
    wft_ref[0] = _dot3_nt(wwfh_ref[...], wwfl_ref[...], h_hi, h_lo)


def _in_proj(x3, mod3, g_mix, weights):
    bsz, seq, d = x3.shape
    tm = 512
    blk_t = lambda b, i: (b, 0, i)
    blk_r = lambda b, i: (b, i, 0)
    const = lambda b, i: (0, 0)
    ak_w = A_KV_HEADS * AUG
    bk_w = B_HEADS * AUG
    outs = [jax.ShapeDtypeStruct((bsz, 512, seq), BF16), jax.ShapeDtypeStruct((bsz, seq, ak_w), BF16),
            jax.ShapeDtypeStruct((bsz, 128, seq), BF16), jax.ShapeDtypeStruct((bsz, 512, seq), BF16),
            jax.ShapeDtypeStruct((bsz, seq, bk_w), BF16), jax.ShapeDtypeStruct((bsz, 512, seq), BF16),
            jax.ShapeDtypeStruct((bsz, IDX_HEADS * IDX_K, seq), BF16),
            jax.ShapeDtypeStruct((bsz, seq, 256), BF16), jax.ShapeDtypeStruct((bsz, 16, seq), F32)]
    out_specs = [pl.BlockSpec((1, 512, tm), blk_t), pl.BlockSpec((1, tm, ak_w), blk_r),
                 pl.BlockSpec((1, 128, tm), blk_t), pl.BlockSpec((1, 512, tm), blk_t),
                 pl.BlockSpec((1, tm, bk_w), blk_r), pl.BlockSpec((1, 512, tm), blk_t),
                 pl.BlockSpec((1, IDX_HEADS * IDX_K, tm), blk_t),
                 pl.BlockSpec((1, tm, 256), blk_r), pl.BlockSpec((1, 16, tm), blk_t)]
    return pl.pallas_call(
        _in_proj_kernel,
        out_shape=outs,
        grid=(bsz, seq // tm),
        in_specs=[pl.BlockSpec((1, tm, d), blk_r),
                  pl.BlockSpec((1, N_MOD, d), lambda b, i: (b, 0, 0)),
                  pl.BlockSpec((1, d), const)] + [pl.BlockSpec(w.shape, const) for w in weights],
        out_specs=out_specs,
        compiler_params=pltpu.CompilerParams(dimension_semantics=("arbitrary", "arbitrary"),
                                             vmem_limit_bytes=VMEM_LIMIT),
        name="in_proj",
    )(x3, mod3, g_mix.reshape(1, d), *weights)


CB = 256


def _cum_kernel(wft_ref, bfor_ref, k_ref, cumt_ref, kaug_ref):
    seq = wft_ref.shape[2]
    r = lax.broadcasted_iota(jnp.int32, (CB, CB), 0)
    cidx = lax.broadcasted_iota(jnp.int32, (CB, CB), 1)
    tri = jnp.where(r <= cidx, 1.0, 0.0).astype(BF16)
    row128 = lax.broadcasted_iota(jnp.int32, (AUG, CB), 0)
    ones_rows = jnp.where((row128 >= HEAD_DIM + 3) & (row128 < HEAD_DIM + 6), 1.0, 0.0)
    carry = jnp.zeros((8, 1), F32)
    for blk in range(seq // CB):
        cols = slice(blk * CB, (blk + 1) * CB)
        z = wft_ref[0, 8:16, cols] + bfor_ref[...]
        logf = jnp.minimum(z, 0.0) - jnp.log(1.0 + jnp.exp(-jnp.abs(z)))
        p1, p2, p3 = _split3_f32(logf)
        pieces = jnp.concatenate([p1, p2, p3, jnp.zeros_like(p1)], axis=0).astype(BF16)
        parts = _dot(pieces, tri)
        cum = parts[0:8] + parts[8:16] + parts[16:24] + carry
        cumt_ref[0, :, cols] = cum
        carry = cum[:, CB - 1:CB]
        c1, c2, c3 = _split3_f32(cum)
        for h in range(B_HEADS):
            spare = jnp.where(row128 == HEAD_DIM, -c1[h:h + 1], ones_rows)
            spare = jnp.where(row128 == HEAD_DIM + 1, -c2[h:h + 1], spare)
            spare = jnp.where(row128 == HEAD_DIM + 2, -c3[h:h + 1], spare)
            lanes = slice(h * AUG, (h + 1) * AUG)
            kaug_ref[0, cols, lanes] = k_ref[0, cols, lanes] + spare.T.astype(BF16)


def _fox_cum(wft, b_forget, bk):
    bsz, _, seq = wft.shape
    nh = B_HEADS
    kw = bk.shape[-1]
    return pl.pallas_call(
        _cum_kernel,
        out_shape=[jax.ShapeDtypeStruct((bsz, nh, seq), F32), jax.ShapeDtypeStruct((bsz, seq, kw), BF16)],
        grid=(bsz,),
        in_specs=[pl.BlockSpec((1, 16, seq), lambda b: (b, 0, 0)),
                  pl.BlockSpec((nh, 1), lambda b: (0, 0)),
                  pl.BlockSpec((1, seq, kw), lambda b: (b, 0, 0))],
        out_specs=[pl.BlockSpec((1, nh, seq), lambda b: (b, 0, 0)),
                   pl.BlockSpec((1, seq, kw), lambda b: (b, 0, 0))],
        compiler_params=pltpu.CompilerParams(dimension_semantics=("arbitrary",),
                                             vmem_limit_bytes=VMEM_LIMIT),
        name="fox_cum",
    )(wft, b_forget.reshape(nh, 1), bk)


def _softmax_init(m_ref, l_ref, acc_ref):
    m_ref[...] = jnp.full(m_ref.shape, NEG_BIG, F32)
    l_ref[...] = jnp.zeros(l_ref.shape, F32)
    acc_ref[...] = jnp.zeros(acc_ref.shape, F32)


def _attend_chunk(n_heads, score_fn, vt_fn, m_ref, l_ref, acc_ref):
    scores = [score_fn(h) for h in range(n_heads)]
    probs, alphas = [], []
    for h in range(n_heads):
        s = scores[h]
        m_old = m_ref[h]
        m_new = jnp.maximum(m_old, jnp.max(s, axis=0, keepdims=True))
        alpha = jnp.exp(m_old - m_new)
        p = jnp.exp(s - m_new)
        l_ref[h] = alpha * l_ref[h] + jnp.sum(p, axis=0, keepdims=True)
        m_ref[h] = m_new
        probs.append(p.astype(BF16))
        alphas.append(alpha)
    for h in range(n_heads):
        rows = slice(h * HEAD_DIM, (h + 1) * HEAD_DIM)
        acc_ref[rows, :] = alphas[h] * acc_ref[rows, :] + _dot(vt_fn(h), probs[h])


def _softmax_finish(o_ref, n_heads, l_ref, acc_ref):
    for h in range(n_heads):
        rows = slice(h * HEAD_DIM, (h + 1) * HEAD_DIM)
        o_ref[0, rows, :] = acc_ref[rows, :] / l_ref[h]


def _fox_kernel(qt_ref, k_ref, vt_ref, cumt_ref, o_ref, w_ref, m_ref, l_ref, acc_ref):
    qi = pl.program_id(1)
    _softmax_init(m_ref, l_ref, acc_ref)

    row64 = lax.broadcasted_iota(jnp.int32, (AUG - HEAD_DIM, TQ), 0)
    for h in range(B_HEADS):
        c1, c2, c3 = _split3_f32(cumt_ref[0, h:h + 1, :])
        spare = jnp.where(row64 < 3, 1.0, 0.0)
        spare = jnp.where(row64 == 3, c1, spare)
        spare = jnp.where(row64 == 4, c2, spare)
        spare = jnp.where(row64 == 5, c3, spare)
        w_ref[h] = jnp.concatenate([qt_ref[0, h * HEAD_DIM:(h + 1) * HEAD_DIM, :],
                                    spare.astype(BF16)], axis=0)

    def tile(kc, diagonal):
        rows = _chunk(kc)

        def score_fn(h):
            s = _dot(k_ref[0, rows, h * AUG:(h + 1) * AUG], w_ref[h])
            if diagonal:
                s = jnp.where(_key_minus_query() <= 0, s, NEG_BIG)
            return s

        def vt_fn(h):
            return vt_ref[0, h * HEAD_DIM:(h + 1) * HEAD_DIM, rows]

        _attend_chunk(B_HEADS, score_fn, vt_fn, m_ref, l_ref, acc_ref)

    def body(kc, _):
        tile(kc, False)
        return 0

    lax.fori_loop(0, qi, body, 0)
    tile(qi, True)
    _softmax_finish(o_ref, B_HEADS, l_ref, acc_ref)


def _fox_attn(bqt, kaug, bvt, cumt):
    bsz, w, seq = bqt.shape
    blk_t = lambda b, i: (b, 0, i)
    full = lambda b, i: (b, 0, 0)
    return pl.pallas_call(
        _fox_kernel,
        out_shape=jax.ShapeDtypeStruct((bsz, w, seq), F32),
        grid=(bsz, seq // TQ),
        in_specs=[pl.BlockSpec((1, w, TQ), blk_t),
                  pl.BlockSpec((1, seq, B_HEADS * AUG), full),
                  pl.BlockSpec((1, w, seq), full),
                  pl.BlockSpec((1, B_HEADS, TQ), blk_t)],
        out_specs=pl.BlockSpec((1, w, TQ), blk_t),
        scratch_shapes=[pltpu.VMEM((B_HEADS, AUG, TQ), BF16), pltpu.VMEM((B_HEADS, 1, TQ), F32),
                        pltpu.VMEM((B_HEADS, 1, TQ), F32), pltpu.VMEM((B_HEADS * HEAD_DIM, TQ), F32)],
        compiler_params=pltpu.CompilerParams(dimension_semantics=("arbitrary", "arbitrary"),
                                             vmem_limit_bytes=VMEM_LIMIT),
        name="fox_attn",
    )(bqt, kaug, bvt, cumt)


def _dsa_kernel(iqt_ref, ik_ref, wft_ref, qt_ref, k_ref, posx_ref, vt_ref, o_ref,
                s_ref, w_ref, m_ref, l_ref, acc_ref):
    qi = pl.program_id(1)
    nch = qi + 1
    kmq = _key_minus_query()

    def score_body(kc, _):
        ik = ik_ref[0, _chunk(kc), :]
        acc = jnp.zeros((KC, TQ), F32)
        for h in range(IDX_HEADS):
            d = _dot(ik, iqt_ref[0, h * IDX_K:(h + 1) * IDX_K, :])
            acc = acc + wft_ref[0, h:h + 1, :] * jnp.maximum(d, 0.0)
        acc = jnp.where(acc == 0.0, 0.0, acc)
        causal = kmq <= (qi - kc) * KC
        s_ref[_chunk(kc), :] = jnp.where(causal, acc, -jnp.inf)
        return 0

    lax.fori_loop(0, nch, score_body, 0)

    @pl.when(qi == 0)
    def _():
        s_ref[0:KC, :] = jnp.where(kmq <= 0, 0.0, NEG_BIG)

    @pl.when(qi > 0)
    def _():
        def scan(fn, init):
            return lax.fori_loop(0, nch, lambda kc, c: fn(s_ref[_chunk(kc), :], c), init)

        def cmin(x):
            return jnp.min(x, axis=0, keepdims=True)

        def cmax(x):
            return jnp.max(x, axis=0, keepdims=True)

        def csum(x):
            return jnp.sum(x, axis=0, keepdims=True)

        zeros = jnp.zeros((1, TQ), F32)
        pinf = jnp.full((1, TQ), jnp.inf, F32)
        ninf = jnp.full((1, TQ), -jnp.inf, F32)

        def init_fn(s, c):
            lo, hi = c
            lo = jnp.minimum(lo, cmin(jnp.where(s > -jnp.inf, s, jnp.inf)))
            hi = jnp.maximum(hi, cmax(s))
            return lo, hi

        lo, hi = scan(init_fn, (pinf, ninf))

        def coarse_step(_, carry):
            lo, hi = carry
            mid = lo + (hi - lo) * 0.5
            cnt = scan(lambda s, c: c + csum(jnp.where(s >= mid, 1.0, 0.0)), zeros)
            enough = cnt >= float(TOPK)
            return jnp.where(enough, mid, lo), jnp.where(enough, hi, mid)

        lo, hi = lax.fori_loop(0, COARSE_STEPS, coarse_step, (lo, hi))

        def cond(carry):
            return carry[2] > 0

        def step(carry):
            lo, hi, _ = carry
            mid = lo + (hi - lo) * 0.5
            mid = jnp.where(mid <= lo, hi, mid)

            def fn(s, c):
                cnt, a, b = c
                ge = s >= mid
                cnt = cnt + csum(jnp.where(ge, 1.0, 0.0))
                b = jnp.minimum(b, cmin(jnp.where(ge, s, jnp.inf)))
                a = jnp.maximum(a, cmax(jnp.where(ge, -jnp.inf, s)))
                return cnt, a, b

            cnt, a, b = scan(fn, (zeros, ninf, pinf))
            enough = cnt >= float(TOPK)
            new_lo = jnp.where(enough, b, lo)
            new_hi = jnp.where(enough, jnp.where(cnt == float(TOPK), b, hi), a)
            active = jnp.max(jnp.where(new_lo < new_hi, 1, 0))
            return new_lo, new_hi, active

        first_active = jnp.max(jnp.where(lo < hi, 1, 0))
        thr, _, _ = lax.while_loop(cond, step, (lo, hi, first_active))

        n_gt = scan(lambda s, c: c + csum(jnp.where(s > thr, 1.0, 0.0)), zeros)
        need = float(TOPK) - n_gt
        lower = jnp.where(lax.broadcasted_iota(jnp.int32, (KC, KC), 1)
                          < lax.broadcasted_iota(jnp.int32, (KC, KC), 0), 1.0, 0.0).astype(BF16)

        def sel_body(kc, run):
            s = s_ref[_chunk(kc), :]
            eq = s == thr
            eqf = jnp.where(eq, 1.0, 0.0)
            before = _dot(lower, eqf.astype(BF16)) + run
            sel = (s > thr) | (eq & (before < need))
            s_ref[_chunk(kc), :] = jnp.where(sel, 0.0, NEG_BIG)
            return run + csum(eqf)

        lax.fori_loop(0, nch, sel_body, zeros)

    _softmax_init(m_ref, l_ref, acc_ref)
    rep = A_HEADS // A_KV_HEADS
    row64 = lax.broadcasted_iota(jnp.int32, (AUG - HEAD_DIM, TQ), 0)
    qpos = qi * TQ + lax.broadcasted_iota(jnp.int32, (AUG - HEAD_DIM, TQ), 1)
    q_hi = (qpos >> 7).astype(F32)
    q_lo = (qpos & 127).astype(F32)
    for h in range(A_HEADS):
        slope = 2.0 ** (-8.0 * (h + 1) / A_HEADS)
        spare = jnp.where(row64 == 0, 128.0 * slope, 0.0)
        spare = jnp.where(row64 == 1, slope, spare)
        spare = jnp.where(row64 == 2, (-128.0 * slope) * q_hi, spare)
        spare = jnp.where(row64 == 3, (-slope) * q_lo, spare)
        w_ref[h] = jnp.concatenate([qt_ref[0, h * HEAD_DIM:(h + 1) * HEAD_DIM, :],
                                    spare.astype(BF16)], axis=0)

    def attn_body(kc, _):
        rows = _chunk(kc)
        pos = posx_ref[rows, :]
        lhs = [k_ref[0, rows, g * AUG:(g + 1) * AUG] + pos for g in range(A_KV_HEADS)]
        bias = s_ref[rows, :]

        def score_fn(h):
            return _dot(lhs[h // rep], w_ref[h]) + bias

        def vt_fn(h):
            g = h // rep
            return vt_ref[0, g * HEAD_DIM:(g + 1) * HEAD_DIM, rows]

        _attend_chunk(A_HEADS, score_fn, vt_fn, m_ref, l_ref, acc_ref)
        return 0

    lax.fori_loop(0, nch, attn_body, 0)
    _softmax_finish(o_ref, A_HEADS, l_ref, acc_ref)


def _dsa_attn(iqt, ik4, wft, aqt, ak, posx, avt):
    bsz, w, seq = aqt.shape
    blk_t = lambda b, i: (b, 0, i)
    full = lambda b, i: (b, 0, 0)
    return pl.pallas_call(
        _dsa_kernel,
        out_shape=jax.ShapeDtypeStruct((bsz, w, seq), F32),
        grid=(bsz, seq // TQ),
        in_specs=[pl.BlockSpec((1, IDX_HEADS * IDX_K, TQ), blk_t),
                  pl.BlockSpec((1, seq, 256), full),
                  pl.BlockSpec((1, 16, TQ), blk_t),
                  pl.BlockSpec((1, w, TQ), blk_t),
                  pl.BlockSpec((1, seq, A_KV_HEADS * AUG), full),
                  pl.BlockSpec((seq, AUG), lambda b, i: (0, 0)),
                  pl.BlockSpec((1, 128, seq), full)],
        out_specs=pl.BlockSpec((1, w, TQ), blk_t),
        scratch_shapes=[pltpu.VMEM((seq, TQ), F32), pltpu.VMEM((A_HEADS, AUG, TQ), BF16),
                        pltpu.VMEM((A_HEADS, 1, TQ), F32), pltpu.VMEM((A_HEADS, 1, TQ), F32),
                        pltpu.VMEM((A_HEADS * HEAD_DIM, TQ), F32)],
        compiler_params=pltpu.CompilerParams(dimension_semantics=("arbitrary", "arbitrary"),
                                             vmem_limit_bytes=VMEM_LIMIT),
        name="dsa_attn",
    )(iqt, ik4, wft, aqt, ak, posx, avt)


def _rms_cols(xt, g_col):
    return xt * lax.rsqrt(jnp.mean(xt * xt, axis=0, keepdims=True) + EPS) * g_col


def _out_proj_kernel(oat_ref, obt_ref, x_ref, mod_ref, ga_ref, gb_ref, w_ref, o_ref):
    oa = _rms_cols(oat_ref[0], ga_ref[...]).astype(BF16)
    ob = _rms_cols(obt_ref[0], gb_ref[...]).astype(BF16)
    y = _dot_tn(oa, w_ref[0:512, :]) + _dot_tn(ob, w_ref[512:1024, :])
    o_ref[0] = x_ref[0] + mod_ref[0, 2:3, :] * y


def _out_proj(oat, obt, x3, mod3, g_out_a, g_out_b, w_out_bf):
    bsz, seq, d = x3.shape
    tm = 512
    blk_t = lambda b, i: (b, 0, i)
    blk_r = lambda b, i: (b, i, 0)
    const = lambda b, i: (0, 0)
    return pl.pallas_call(
        _out_proj_kernel,
        out_shape=jax.ShapeDtypeStruct((bsz, seq, d), F32),
        grid=(bsz, seq // tm),
        in_specs=[pl.BlockSpec((1, 512, tm), blk_t), pl.BlockSpec((1, 512, tm), blk_t),
                  pl.BlockSpec((1, tm, d), blk_r),
                  pl.BlockSpec((1, N_MOD, d), lambda b, i: (b, 0, 0)),
                  pl.BlockSpec((512, 1), const), pl.BlockSpec((512, 1), const),
                  pl.BlockSpec((d, d), const)],
        out_specs=pl.BlockSpec((1, tm, d), blk_r),
        compiler_params=pltpu.CompilerParams(dimension_semantics=("arbitrary", "arbitrary"),
                                             vmem_limit_bytes=VMEM_LIMIT),
        name="out_proj",
    )(oat, obt, x3, mod3, g_out_a.reshape(-1, 1), g_out_b.reshape(-1, 1), w_out_bf)


MOE_TMG = 256
MOE_TMC = 256
ROUTE_COLS = 128


def _first_index_of_max(vals, lane, big):
    m = jnp.max(vals, axis=-1, keepdims=True)
    idx = jnp.min(jnp.where(vals == m, lane, big), axis=-1, keepdims=True)
    return m, idx


def _route(h, wr_hi_ref, wr_lo_ref, br_ref):
    h_hi, h_lo = _split_bf16(h)
    logits = _dot3(h_hi, h_lo, wr_hi_ref[...], wr_lo_ref[...]) + br_ref[...]
    lane = lax.broadcasted_iota(jnp.int32, logits.shape, 1)
    ninf = -jnp.inf
    gl = jnp.where(lane < N_GROUPS, logits, ninf)
    gmax, gsel = _first_index_of_max(gl, lane, 1 << 20)
    p_group = 1.0 / jnp.sum(jnp.exp(gl - gmax), axis=-1, keepdims=True)
    base = 32 + gsel * EXPERTS_PER_GROUP
    el = jnp.where((lane >= base) & (lane < base + EXPERTS_PER_GROUP), logits, ninf)
    v1, i1 = _first_index_of_max(el, lane, 1 << 20)
    el2 = jnp.where(lane == i1, ninf, el)
    v2, i2 = _first_index_of_max(el2, lane, 1 << 20)
    e2 = jnp.exp(v2 - v1)
    w1 = 1.0 / (1.0 + e2)
    w2 = e2 / (1.0 + e2)
    return i1, i2, w1 * p_group, w2 * p_group


def _moe_route_kernel(x_ref, mod_ref, gf_ref, wr_hi_ref, wr_lo_ref, br_ref,
                      h_ref, info_ref, cnt_ref, run_ref):
    @pl.when(pl.program_id(0) == 0)
    def _():
        run_ref[...] = jnp.zeros_like(run_ref)

    h = _rms(x_ref[...], gf_ref[...]) * (1.0 + mod_ref[0, 4:5, :]) + mod_ref[0, 3:4, :]
    h_ref[...] = h
    i1, i2, w1, w2 = _route(h, wr_hi_ref, wr_lo_ref, br_ref)
    tm = h.shape[0]
    lane = lax.broadcasted_iota(jnp.int32, (tm, ROUTE_COLS), 1)
    picked = jnp.where((lane == i1) | (lane == i2), 1.0, 0.0)
    earlier = jnp.where(lax.broadcasted_iota(jnp.int32, (tm, tm), 1)
                        < lax.broadcasted_iota(jnp.int32, (tm, tm), 0), 1.0, 0.0).astype(BF16)
    before = _dot(earlier, picked.astype(BF16)) + run_ref[...]
    rank1 = jnp.sum(jnp.where(lane == i1, before, 0.0), axis=-1, keepdims=True)
    rank2 = jnp.sum(jnp.where(lane == i2, before, 0.0), axis=-1, keepdims=True)
    run_ref[...] += jnp.sum(picked, axis=0, keepdims=True)
    cnt_ref[...] = run_ref[...]
    info = jnp.where(lane == 0, (i1 - 32).astype(F32), 0.0)
    info = jnp.where(lane == 1, (i2 - 32).astype(F32), info)
    info = jnp.where(lane == 2, rank1, info)
    info = jnp.where(lane == 3, rank2, info)
    info = jnp.where(lane == 4, w1, info)
    info_ref[...] = jnp.where(lane == 5, w2, info)


def _moe_route(x1, mod3, g_ffn, wr_hi, wr_lo, b_route, seq):
    n, d = x1.shape
    tm = 512
    per_b = seq // tm
    row = lambda i: (i, 0)
    const = lambda i: (0, 0)
    return pl.pallas_call(
        _moe_route_kernel,
        out_shape=[jax.ShapeDtypeStruct((n, d), F32), jax.ShapeDtypeStruct((n, ROUTE_COLS), F32),
                   jax.ShapeDtypeStruct((1, ROUTE_COLS), F32)],
        grid=(n // tm,),
        in_specs=[pl.BlockSpec((tm, d), row),
                  pl.BlockSpec((1, N_MOD, d), lambda i: (i // per_b, 0, 0)),
                  pl.BlockSpec((1, d), const),
                  pl.BlockSpec((d, ROUTE_COLS), const), pl.BlockSpec((d, ROUTE_COLS), const),
                  pl.BlockSpec((1, ROUTE_COLS), const)],
        out_specs=[pl.BlockSpec((tm, d), row), pl.BlockSpec((tm, ROUTE_COLS), row),
                   pl.BlockSpec((1, ROUTE_COLS), const)],
        scratch_shapes=[pltpu.VMEM((1, ROUTE_COLS), F32)],
        compiler_params=pltpu.CompilerParams(dimension_semantics=("arbitrary",),
                                             vmem_limit_bytes=VMEM_LIMIT),
        name="moe_route",
    )(x1, mod3, g_ffn.reshape(1, d), wr_hi, wr_lo, b_route)


def _start_row_gather(src_hbm, dst, sem, idx_ref, base, n_rows):
    for r in range(n_rows):
        pltpu.make_async_copy(src_hbm.at[pl.ds(idx_ref[base + r], 1), :],
                              dst.at[pl.ds(r, 1), :], sem).start()


def _wait_row_gather(src_hbm, dst, sem, n_rows):
    pltpu.make_async_copy(src_hbm.at[pl.ds(0, n_rows), :], dst, sem).wait()


def _moe_expert_kernel(te_ref, rt_ref, h_hbm, rw_ref, wg_ref, wu_ref, wd_ref, y_ref, xbuf, sem):
    i = pl.program_id(0)
    nt = pl.num_programs(0)
    slot = lax.rem(i, 2)

    @pl.when(i == 0)
    def _():
        _start_row_gather(h_hbm, xbuf.at[0], sem.at[0], rt_ref, 0, MOE_TMG)

    _wait_row_gather(h_hbm, xbuf.at[slot], sem.at[slot], MOE_TMG)

    @pl.when(i + 1 < nt)
    def _():
        _start_row_gather(h_hbm, xbuf.at[1 - slot], sem.at[1 - slot], rt_ref, (i + 1) * MOE_TMG, MOE_TMG)

    x = xbuf[slot].astype(BF16)
    hg = _dot(x, wg_ref[0])
    hu = _dot(x, wu_ref[0])
    a = hg * jax.nn.sigmoid(hg) * hu * rw_ref[...]
    y_ref[...] = _dot(a.astype(BF16), wd_ref[0])


def _moe_experts(tile_expert, row_token, h2, row_w, wg, wu, wd):
    n_rows = row_token.shape[0]
    d = h2.shape[1]
    ff = wg.shape[-1]
    nt = n_rows // MOE_TMG
    grid_spec = pltpu.PrefetchScalarGridSpec(
        num_scalar_prefetch=2,
        grid=(nt,),
        in_specs=[pl.BlockSpec(memory_space=pl.ANY),
                  pl.BlockSpec((MOE_TMG, 1), lambda i, te, rt: (i, 0)),
                  pl.BlockSpec((1, d, ff), lambda i, te, rt: (te[i], 0, 0)),
                  pl.BlockSpec((1, d, ff), lambda i, te, rt: (te[i], 0, 0)),
                  pl.BlockSpec((1, ff, d), lambda i, te, rt: (te[i], 0, 0))],
        out_specs=pl.BlockSpec((MOE_TMG, d), lambda i, te, rt: (i, 0)),
        scratch_shapes=[pltpu.VMEM((2, MOE_TMG, d), F32), pltpu.SemaphoreType.DMA((2,))])
    return pl.pallas_call(
        _moe_expert_kernel,
        out_shape=jax.ShapeDtypeStruct((n_rows, d), F32),
        grid_spec=grid_spec,
        compiler_params=pltpu.CompilerParams(dimension_semantics=("arbitrary",),
                                             vmem_limit_bytes=VMEM_LIMIT),
        name="moe_experts",
    )(tile_expert, row_token, h2, row_w, wg, wu, wd)


def _moe_combine_kernel(dest_ref, y_hbm, x_ref, mod_ref, gfin_ref, o_ref, ybuf, sem):
    i = pl.program_id(0)
    nt = pl.num_programs(0)
    slot = lax.rem(i, 2)
    n_pairs = 2 * MOE_TMC

    @pl.when(i == 0)
    def _():
        _start_row_gather(y_hbm, ybuf.at[0], sem.at[0], dest_ref, 0, n_pairs)

    _wait_row_gather(y_hbm, ybuf.at[slot], sem.at[slot], n_pairs)

    @pl.when(i + 1 < nt)
    def _():
        _start_row_gather(y_hbm, ybuf.at[1 - slot], sem.at[1 - slot], dest_ref, (i + 1) * n_pairs, n_pairs)

    y = ybuf[slot, 0:MOE_TMC, :] + ybuf[slot, MOE_TMC:n_pairs, :]
    x2 = x_ref[...] + mod_ref[0, 5:6, :] * y
    o_ref[...] = _rms(x2, gfin_ref[...])


def _moe_combine(dest, ysorted, x1, mod3, g_final, seq):
    n, d = x1.shape
    per_b = seq // MOE_TMC
    grid_spec = pltpu.PrefetchScalarGridSpec(
        num_scalar_prefetch=1,
        grid=(n // MOE_TMC,),
        in_specs=[pl.BlockSpec(memory_space=pl.ANY),
                  pl.BlockSpec((MOE_TMC, d), lambda i, ds: (i, 0)),
                  pl.BlockSpec((1, N_MOD, d), lambda i, ds: (i // per_b, 0, 0)),
                  pl.BlockSpec((1, d), lambda i, ds: (0, 0))],
        out_specs=pl.BlockSpec((MOE_TMC, d), lambda i, ds: (i, 0)),
        scratch_shapes=[pltpu.VMEM((2, 2 * MOE_TMC, d), F32), pltpu.SemaphoreType.DMA((2,))])
    return pl.pallas_call(
        _moe_combine_kernel,
        out_shape=jax.ShapeDtypeStruct((n, d), F32),
        grid_spec=grid_spec,
        compiler_params=pltpu.CompilerParams(dimension_semantics=("arbitrary",),
                                             vmem_limit_bytes=VMEM_LIMIT),
        name="moe_combine",
    )(dest, ysorted, x1, mod3, g_final.reshape(1, d))


def _moe(x1, mod3, g_ffn, wr_hi, wr_lo, b_route, wg, wu, wd, g_final, seq):
    n, d = x1.shape
    h2, info, counts = _moe_route(x1, mod3, g_ffn, wr_hi, wr_lo, b_route, seq)

    e1, e2, rank1, rank2 = [info[:, k].astype(jnp.int32) for k in range(4)]
    cnt = counts[0, 32:32 + N_EXPERTS].astype(jnp.int32)
    padded = ((cnt + MOE_TMG - 1) // MOE_TMG) * MOE_TMG
    seg_end = jnp.cumsum(padded)
    seg_start = seg_end - padded
    dest1 = seg_start[e1] + rank1
    dest2 = seg_start[e2] + rank2
    n_rows = 2 * n + N_EXPERTS * MOE_TMG
    tok = jnp.arange(n, dtype=jnp.int32)
    dest_pairs = jnp.concatenate([dest1, dest2])
    row_token = jnp.zeros((n_rows,), jnp.int32).at[dest_pairs].set(jnp.concatenate([tok, tok]))
    row_w = jnp.zeros((n_rows,), F32).at[dest_pairs].set(jnp.concatenate([info[:, 4], info[:, 5]]))
    tile_start = jnp.arange(n_rows // MOE_TMG, dtype=jnp.int32) * MOE_TMG
    tile_expert = jnp.minimum(jnp.searchsorted(seg_end, tile_start, side="right"),
                              N_EXPERTS - 1).astype(jnp.int32)

    ysorted = _moe_experts(tile_expert, row_token, h2, row_w.reshape(n_rows, 1), wg, wu, wd)

    dest = jnp.concatenate([dest1.reshape(-1, MOE_TMC), dest2.reshape(-1, MOE_TMC)], axis=1).reshape(-1)
    return _moe_combine(dest, ysorted, x1, mod3, g_final, seq)


def _layer(x3, c, w_ada, b_ada, g_mix, w_in, b_forget, g_out_a, g_out_b, w_out,
           g_ffn, w_group, b_group, w_router, b_router, w_gate, w_up, w_down, g_final):
    bsz, seq, d = x3.shape
    mod3 = _ada_mod(c, w_ada, b_ada).reshape(bsz, N_MOD, d)

    w_t = w_in.T

    def pad_heads(w, n_heads):
        w = w.reshape(d, n_heads, HEAD_DIM)
        return jnp.concatenate([w, jnp.zeros_like(w)], axis=-1).reshape(d, n_heads * AUG)

    w_ik2 = jnp.concatenate([w_in[:, 1280:1344], w_in[:, 1280:1344]], axis=1)
    w_wf_t = jnp.concatenate([w_t[1344:1352], w_t[2888:2896]], axis=0)
    weights = [w_t[0:512].astype(BF16), pad_heads(w_in[:, 512:640], A_KV_HEADS).astype(BF16),
               w_t[640:768].astype(BF16), w_t[1352:1864].astype(BF16),
               pad_heads(w_in[:, 1864:2376], B_HEADS).astype(BF16), w_t[2376:2888].astype(BF16),
               *_split_bf16(w_t[768:1280]), *_split_bf16(w_ik2), *_split_bf16(w_wf_t)]
    aqt, ak, avt, bqt, bk, bvt, iqt, ik4, wft = _in_proj(x3, mod3, g_mix, weights)

    cumt, kaug = _fox_cum(wft, b_forget, bk)
    obt = _fox_attn(bqt, kaug, bvt, cumt)

    pos = jnp.arange(seq, dtype=jnp.int32)[:, None]
    lane = jnp.arange(AUG, dtype=jnp.int32)[None, :] - HEAD_DIM
    posx = jnp.where(lane == 0, pos >> 7, jnp.where(lane == 1, pos & 127,
                     jnp.where((lane == 2) | (lane == 3), 1, 0))).astype(BF16)
    oat = _dsa_attn(iqt, ik4, wft, aqt, ak, posx, avt)

    x1 = _out_proj(oat, obt, x3, mod3, g_out_a, g_out_b, w_out.astype(BF16))

    w_r = jnp.concatenate([w_group, jnp.zeros((d, 32 - N_GROUPS), F32),
                           jnp.transpose(w_router, (1, 0, 2)).reshape(d, N_EXPERTS),
                           jnp.zeros((d, ROUTE_COLS - 64), F32)], axis=1)
    b_r = jnp.concatenate([b_group, jnp.zeros((32 - N_GROUPS,), F32), b_router.reshape(-1),
                           jnp.zeros((ROUTE_COLS - 64,), F32)]).reshape(1, ROUTE_COLS)
    wr_hi, wr_lo = _split_bf16(w_r)
    out = _moe(x1.reshape(bsz * seq, d), mod3, g_ffn, wr_hi, wr_lo, b_r, w_gate.astype(BF16),
               w_up.astype(BF16), w_down.astype(BF16), g_final, seq)
    return out.reshape(bsz, seq, d)


def kernel(x, c, w_ada, b_ada, g_mix, w_in, b_forget, g_out_a, g_out_b, w_out, g_ffn, w_group,
           b_group, w_router, b_router, w_gate, w_up, w_down, g_final):
    depth = w_ada.shape[0]
    assert depth == 1, "final norm is fused into the single layer's MoE kernel"
    return _layer(x, c, w_ada[0], b_ada[0], g_mix[0], w_in[0], b_forget[0], g_out_a[0], g_out_b[0],
                  w_out[0], g_ffn[0], w_group[0], b_group[0], w_router[0], b_router[0], w_gate[0],
                  w_up[0], w_down[0], g_final)
```

```python
import jax
import jax.numpy as jnp
from jax import lax
from jax.experimental import pallas as pl
from jax.experimental.pallas import tpu as pltpu

F32 = jnp.float32
BF16 = jnp.bfloat16

EPS = 1e-6
A_HEADS = 8
A_KV_HEADS = 2
HEAD_DIM = 64
IDX_HEADS = 8
IDX_DIM = 64
TOPK = 256
B_HEADS = 8
N_GROUPS = 4
EXPERTS_PER_GROUP = 8
N_EXPERTS = N_GROUPS * EXPERTS_PER_GROUP
N_MOD = 6

NEG_BIG = -1e30
VMEM_LIMIT = 48 * 1024 * 1024

TQ = 256
KC = 256
AUG = 128
IDX_K = 256
COARSE_STEPS = 12
assert TQ == KC == TOPK


def _split_bf16(x):
    hi = x.astype(BF16)
    lo = (x - hi.astype(F32)).astype(BF16)
    return hi, lo


def _split3_f32(x):
    p1 = x.astype(BF16).astype(F32)
    r1 = x - p1
    p2 = r1.astype(BF16).astype(F32)
    p3 = (r1 - p2).astype(BF16).astype(F32)
    return p1, p2, p3


def _dot(a, b):
    return jnp.dot(a, b, preferred_element_type=F32)


def _dot_nt(a, b):
    return lax.dot_general(a, b, (((1,), (1,)), ((), ())), preferred_element_type=F32)


def _dot_tn(a, b):
    return lax.dot_general(a, b, (((0,), (0,)), ((), ())), preferred_element_type=F32)


def _dot3(a_hi, a_lo, b_hi, b_lo):
    return _dot(a_hi, b_hi) + _dot(a_lo, b_hi) + _dot(a_hi, b_lo)


def _dot3_nt(a_hi, a_lo, b_hi, b_lo):
    return _dot_nt(a_hi, b_hi) + _dot_nt(a_lo, b_hi) + _dot_nt(a_hi, b_lo)


def _rms(x, g):
    return x * lax.rsqrt(jnp.mean(x * x, axis=-1, keepdims=True) + EPS) * g


def _chunk(kc):
    return pl.ds(pl.multiple_of(kc * KC, KC), KC)


def _key_minus_query():
    return (lax.broadcasted_iota(jnp.int32, (KC, TQ), 0)
            - lax.broadcasted_iota(jnp.int32, (KC, TQ), 1))


def _ada_kernel(c_ref, w_ref, b_ref, o_ref):
    c = c_ref[...]
    s = c * jax.nn.sigmoid(c)
    s_hi, s_lo = _split_bf16(s)
    w_hi, w_lo = _split_bf16(w_ref[...])
    o_ref[...] = _dot3(s_hi, s_lo, w_hi, w_lo) + b_ref[...]


def _ada_mod(c, w_ada, b_ada):
    bsz, d = c.shape
    n = w_ada.shape[1]
    tn = 1024
    return pl.pallas_call(
        _ada_kernel,
        out_shape=jax.ShapeDtypeStruct((bsz, n), F32),
        grid=(n // tn,),
        in_specs=[pl.BlockSpec((bsz, d), lambda j: (0, 0)),
                  pl.BlockSpec((d, tn), lambda j: (0, j)),
                  pl.BlockSpec((1, tn), lambda j: (0, j))],
        out_specs=pl.BlockSpec((bsz, tn), lambda j: (0, j)),
        compiler_params=pltpu.CompilerParams(dimension_semantics=("arbitrary",),
                                             vmem_limit_bytes=VMEM_LIMIT),
        name="ada_mod",
    )(c, w_ada, b_ada.reshape(1, n))


def _in_proj_kernel(x_ref, mod_ref, g_ref,
                    waq_ref, wak_ref, wav_ref, wbq_ref, wbk_ref, wbv_ref,
                    wiqh_ref, wiql_ref, wikh_ref, wikl_ref, wwfh_ref, wwfl_ref,
                    aqt_ref, ak_ref, avt_ref, bqt_ref, bk_ref, bvt_ref, iqt_ref, ik_ref, wft_ref):
    x = x_ref[0]
    h = _rms(x, g_ref[...]) * (1.0 + mod_ref[0, 1:2, :]) + mod_ref[0, 0:1, :]
    h_hi, h_lo = _split_bf16(h)
    aqt_ref[0] = (_dot_nt(waq_ref[...], h_hi) * 0.125).astype(BF16)
    ak_ref[0] = _dot(h_hi, wak_ref[...]).astype(BF16)
    avt_ref[0] = _dot_nt(wav_ref[...], h_hi).astype(BF16)
    bqt_ref[0] = (_dot_nt(wbq_ref[...], h_hi) * 0.125).astype(BF16)
    bk_ref[0] = _dot(h_hi, wbk_ref[...]).astype(BF16)
    bvt_ref[0] = _dot_nt(wbv_ref[...], h_hi).astype(BF16)
    iqt = _dot3_nt(wiqh_ref[...], wiql_ref[...], h_hi, h_lo)
    for hd in range(IDX_HEADS):
        q_hi, q_lo = _split_bf16(iqt[hd * IDX_DIM:(hd + 1) * IDX_DIM, :])
        iqt_ref[0, hd * IDX_K:(hd + 1) * IDX_K, :] = jnp.concatenate([q_hi, q_lo, q_hi, q_lo], axis=0)
    ik2 = _dot3(h_hi, h_lo, wikh_ref[...], wikl_ref[...])
    k_hi, k_lo = _split_bf16(ik2)
    ik_ref[0] = jnp.concatenate([k_hi, k_lo], axis=1)
    wft_ref[0] = _dot3_nt(wwfh_ref[...], wwfl_ref[...], h_hi, h_lo)


def _in_proj(x3, mod3, g_mix, weights):
    bsz, seq, d = x3.shape
    tm = 512
    blk_t = lambda b, i: (b, 0, i)
    blk_r = lambda b, i: (b, i, 0)
    const = lambda b, i: (0, 0)
    ak_w = A_KV_HEADS * AUG
    bk_w = B_HEADS * AUG
    outs = [jax.ShapeDtypeStruct((bsz, 512, seq), BF16), jax.ShapeDtypeStruct((bsz, seq, ak_w), BF16),
            jax.ShapeDtypeStruct((bsz, 128, seq), BF16), jax.ShapeDtypeStruct((bsz, 512, seq), BF16),
            jax.ShapeDtypeStruct((bsz, seq, bk_w), BF16), jax.ShapeDtypeStruct((bsz, 512, seq), BF16),
            jax.ShapeDtypeStruct((bsz, IDX_HEADS * IDX_K, seq), BF16),
            jax.ShapeDtypeStruct((bsz, seq, 256), BF16), jax.ShapeDtypeStruct((bsz, 16, seq), F32)]
    out_specs = [pl.BlockSpec((1, 512, tm), blk_t), pl.BlockSpec((1, tm, ak_w), blk_r),
                 pl.BlockSpec((1, 128, tm), blk_t), pl.BlockSpec((1, 512, tm), blk_t),
                 pl.BlockSpec((1, tm, bk_w), blk_r), pl.BlockSpec((1, 512, tm), blk_t),
                 pl.BlockSpec((1, IDX_HEADS * IDX_K, tm), blk_t),
                 pl.BlockSpec((1, tm, 256), blk_r), pl.BlockSpec((1, 16, tm), blk_t)]
    return pl.pallas_call(
        _in_proj_kernel,
        out_shape=outs,
        grid=(bsz, seq // tm),
        in_specs=[pl.BlockSpec((1, tm, d), blk_r),
                  pl.BlockSpec((1, N_MOD, d), lambda b, i: (b, 0, 0)),
                  pl.BlockSpec((1, d), const)] + [pl.BlockSpec(w.shape, const) for w in weights],
        out_specs=out_specs,
        compiler_params=pltpu.CompilerParams(dimension_semantics=("arbitrary", "arbitrary"),
                                             vmem_limit_bytes=VMEM_LIMIT),
        name="in_proj",
    )(x3, mod3, g_mix.reshape(1, d), *weights)


CB = 256


def _cum_kernel(wft_ref, bfor_ref, k_ref, cumt_ref, kaug_ref):
    seq = wft_ref.shape[2]
    r = lax.broadcasted_iota(jnp.int32, (CB, CB), 0)
    cidx = lax.broadcasted_iota(jnp.int32, (CB, CB), 1)
    tri = jnp.where(r <= cidx, 1.0, 0.0).astype(BF16)
    row128 = lax.broadcasted_iota(jnp.int32, (AUG, CB), 0)
    ones_rows = jnp.where((row128 >= HEAD_DIM + 3) & (row128 < HEAD_DIM + 6), 1.0, 0.0)
    carry = jnp.zeros((8, 1), F32)
    for blk in range(seq // CB):
        cols = slice(blk * CB, (blk + 1) * CB)
        z = wft_ref[0, 8:16, cols] + bfor_ref[...]
        logf = jnp.minimum(z, 0.0) - jnp.log(1.0 + jnp.exp(-jnp.abs(z)))
        p1, p2, p3 = _split3_f32(logf)
        pieces = jnp.concatenate([p1, p2, p3, jnp.zeros_like(p1)], axis=0).astype(BF16)
        parts = _dot(pieces, tri)
        cum = parts[0:8] + parts[8:16] + parts[16:24] + carry
        cumt_ref[0, :, cols] = cum
        carry = cum[:, CB - 1:CB]
        c1, c2, c3 = _split3_f32(cum)
        for h in range(B_HEADS):
            spare = jnp.where(row128 == HEAD_DIM, -c1[h:h + 1], ones_rows)
            spare = jnp.where(row128 == HEAD_DIM + 1, -c2[h:h + 1], spare)
            spare = jnp.where(row128 == HEAD_DIM + 2, -c3[h:h + 1], spare)
            lanes = slice(h * AUG, (h + 1) * AUG)
            kaug_ref[0, cols, lanes] = k_ref[0, cols, lanes] + spare.T.astype(BF16)


def _fox_cum(wft, b_forget, bk):
    bsz, _, seq = wft.shape
    nh = B_HEADS
    kw = bk.shape[-1]
    return pl.pallas_call(
        _cum_kernel,
        out_shape=[jax.ShapeDtypeStruct((bsz, nh, seq), F32), jax.ShapeDtypeStruct((bsz, seq, kw), BF16)],
        grid=(bsz,),
        in_specs=[pl.BlockSpec((1, 16, seq), lambda b: (b, 0, 0)),
                  pl.BlockSpec((nh, 1), lambda b: (0, 0)),
                  pl.BlockSpec((1, seq, kw), lambda b: (b, 0, 0))],
        out_specs=[pl.BlockSpec((1, nh, seq), lambda b: (b, 0, 0)),
                   pl.BlockSpec((1, seq, kw), lambda b: (b, 0, 0))],
        compiler_params=pltpu.CompilerParams(dimension_semantics=("arbitrary",),
                                             vmem_limit_bytes=VMEM_LIMIT),
        name="fox_cum",
    )(wft, b_forget.reshape(nh, 1), bk)


def _softmax_init(m_ref, l_ref, acc_ref):
    m_ref[...] = jnp.full(m_ref.shape, NEG_BIG, F32)
    l_ref[...] = jnp.zeros(l_ref.shape, F32)
    acc_ref[...] = jnp.zeros(acc_ref.shape, F32)


def _attend_chunk(n_heads, score_fn, vt_fn, m_ref, l_ref, acc_ref):
    scores = [score_fn(h) for h in range(n_heads)]
    probs, alphas = [], []
    for h in range(n_heads):
        s = scores[h]
        m_old = m_ref[h]
        m_new = jnp.maximum(m_old, jnp.max(s, axis=0, keepdims=True))
        alpha = jnp.exp(m_old - m_new)
        p = jnp.exp(s - m_new)
        l_ref[h] = alpha * l_ref[h] + jnp.sum(p, axis=0, keepdims=True)
        m_ref[h] = m_new
        probs.append(p.astype(BF16))
        alphas.append(alpha)
    for h in range(n_heads):
        rows = slice(h * HEAD_DIM, (h + 1) * HEAD_DIM)
        acc_ref[rows, :] = alphas[h] * acc_ref[rows, :] + _dot(vt_fn(h), probs[h])


def _softmax_finish(o_ref, n_heads, l_ref, acc_ref):
    for h in range(n_heads):
        rows = slice(h * HEAD_DIM, (h + 1) * HEAD_DIM)
        o_ref[0, rows, :] = acc_ref[rows, :] / l_ref[h]


def _fox_kernel(qt_ref, k_ref, vt_ref, cumt_ref, o_ref, w_ref, m_ref, l_ref, acc_ref):
    qi = pl.program_id(1)
    _softmax_init(m_ref, l_ref, acc_ref)

    row64 = lax.broadcasted_iota(jnp.int32, (AUG - HEAD_DIM, TQ), 0)
    for h in range(B_HEADS):
        c1, c2, c3 = _split3_f32(cumt_ref[0, h:h + 1, :])
        spare = jnp.where(row64 < 3, 1.0, 0.0)
        spare = jnp.where(row64 == 3, c1, spare)
        spare = jnp.where(row64 == 4, c2, spare)
        spare = jnp.where(row64 == 5, c3, spare)
        w_ref[h] = jnp.concatenate([qt_ref[0, h * HEAD_DIM:(h + 1) * HEAD_DIM, :],
                                    spare.astype(BF16)], axis=0)

    def tile(kc, diagonal):
        rows = _chunk(kc)

        def score_fn(h):
            s = _dot(k_ref[0, rows, h * AUG:(h + 1) * AUG], w_ref[h])
            if diagonal:
                s = jnp.where(_key_minus_query() <= 0, s, NEG_BIG)
            return s

        def vt_fn(h):
            return vt_ref[0, h * HEAD_DIM:(h + 1) * HEAD_DIM, rows]

        _attend_chunk(B_HEADS, score_fn, vt_fn, m_ref, l_ref, acc_ref)

    def body(kc, _):
        tile(kc, False)
        return 0

    lax.fori_loop(0, qi, body, 0)
    tile(qi, True)
    _softmax_finish(o_ref, B_HEADS, l_ref, acc_ref)


def _fox_attn(bqt, kaug, bvt, cumt):
    bsz, w, seq = bqt.shape
    blk_t = lambda b, i: (b, 0, i)
    full = lambda b, i: (b, 0, 0)
    return pl.pallas_call(
        _fox_kernel,
        out_shape=jax.ShapeDtypeStruct((bsz, w, seq), F32),
        grid=(bsz, seq // TQ),
        in_specs=[pl.BlockSpec((1, w, TQ), blk_t),
                  pl.BlockSpec((1, seq, B_HEADS * AUG), full),
                  pl.BlockSpec((1, w, seq), full),
                  pl.BlockSpec((1, B_HEADS, TQ), blk_t)],
        out_specs=pl.BlockSpec((1, w, TQ), blk_t),
        scratch_shapes=[pltpu.VMEM((B_HEADS, AUG, TQ), BF16), pltpu.VMEM((B_HEADS, 1, TQ), F32),
                        pltpu.VMEM((B_HEADS, 1, TQ), F32), pltpu.VMEM((B_HEADS * HEAD_DIM, TQ), F32)],
        compiler_params=pltpu.CompilerParams(dimension_semantics=("arbitrary", "arbitrary"),
                                             vmem_limit_bytes=VMEM_LIMIT),
        name="fox_attn",
    )(bqt, kaug, bvt, cumt)


def _dsa_kernel(iqt_ref, ik_ref, wft_ref, qt_ref, k_ref, posx_ref, vt_ref, o_ref,
                s_ref, w_ref, m_ref, l_ref, acc_ref):
    qi = pl.program_id(1)
    nch = qi + 1
    kmq = _key_minus_query()

    def score_body(kc, _):
        ik = ik_ref[0, _chunk(kc), :]
        acc = jnp.zeros((KC, TQ), F32)
        for h in range(IDX_HEADS):
            d = _dot(ik, iqt_ref[0, h * IDX_K:(h + 1) * IDX_K, :])
            acc = acc + wft_ref[0, h:h + 1, :] * jnp.maximum(d, 0.0)
        acc = jnp.where(acc == 0.0, 0.0, acc)
        causal = kmq <= (qi - kc) * KC
        s_ref[_chunk(kc), :] = jnp.where(causal, acc, -jnp.inf)
        return 0

    lax.fori_loop(0, nch, score_body, 0)

    @pl.when(qi == 0)
    def _():
        s_ref[0:KC, :] = jnp.where(kmq <= 0, 0.0, NEG_BIG)

    @pl.when(qi > 0)
    def _():
        def scan(fn, init):
            return lax.fori_loop(0, nch, lambda kc, c: fn(s_ref[_chunk(kc), :], c), init)

        def cmin(x):
            return jnp.min(x, axis=0, keepdims=True)

        def cmax(x):
            return jnp.max(x, axis=0, keepdims=True)

        def csum(x):
            return jnp.sum(x, axis=0, keepdims=True)

        zeros = jnp.zeros((1, TQ), F32)
        pinf = jnp.full((1, TQ), jnp.inf, F32)
        ninf = jnp.full((1, TQ), -jnp.inf, F32)

        def init_fn(s, c):
            lo, hi = c
            lo = jnp.minimum(lo, cmin(jnp.where(s > -jnp.inf, s, jnp.inf)))
            hi = jnp.maximum(hi, cmax(s))
            return lo, hi

        lo, hi = scan(init_fn, (pinf, ninf))

        def coarse_step(_, carry):
            lo, hi = carry
            mid = lo + (hi - lo) * 0.5
            cnt = scan(lambda s, c: c + csum(jnp.where(s >= mid, 1.0, 0.0)), zeros)
            enough = cnt >= float(TOPK)
            return jnp.where(enough, mid, lo), jnp.where(enough, hi, mid)

        lo, hi = lax.fori_loop(0, COARSE_STEPS, coarse_step, (lo, hi))

        def cond(carry):
            return carry[2] > 0

        def step(carry):
            lo, hi, _ = carry
            mid = lo + (hi - lo) * 0.5
            mid = jnp.where(mid <= lo, hi, mid)

            def fn(s, c):
                cnt, a, b = c
                ge = s >= mid
                cnt = cnt + csum(jnp.where(ge, 1.0, 0.0))
                b = jnp.minimum(b, cmin(jnp.where(ge, s, jnp.inf)))
                a = jnp.maximum(a, cmax(jnp.where(ge, -jnp.inf, s)))
                return cnt, a, b

            cnt, a, b = scan(fn, (zeros, ninf, pinf))
            enough = cnt >= float(TOPK)
            new_lo = jnp.where(enough, b, lo)
            new_hi = jnp.where(enough, jnp.where(cnt == float(TOPK), b, hi), a)
            active = jnp.max(jnp.where(new_lo < new_hi, 1, 0))
            return new_lo, new_hi, active

        first_active = jnp.max(jnp.where(lo < hi, 1, 0))
        thr, _, _ = lax.while_loop(cond, step, (lo, hi, first_active))

        n_gt = scan(lambda s, c: c + csum(jnp.where(s > thr, 1.0, 0.0)), zeros)
        need = float(TOPK) - n_gt
        lower = jnp.where(lax.broadcasted_iota(jnp.int32, (KC, KC), 1)
                          < lax.broadcasted_iota(jnp.int32, (KC, KC), 0), 1.0, 0.0).astype(BF16)

        def sel_body(kc, run):
            s = s_ref[_chunk(kc), :]
            eq = s == thr
            eqf = jnp.where(eq, 1.0, 0.0)
            before = _dot(lower, eqf.astype(BF16)) + run
            sel = (s > thr) | (eq & (before < need))
            s_ref[_chunk(kc), :] = jnp.where(sel, 0.0, NEG_BIG)
            return run + csum(eqf)

        lax.fori_loop(0, nch, sel_body, zeros)

    _softmax_init(m_ref, l_ref, acc_ref)
    rep = A_HEADS // A_KV_HEADS
    row64 = lax.broadcasted_iota(jnp.int32, (AUG - HEAD_DIM, TQ), 0)
    qpos = qi * TQ + lax.broadcasted_iota(jnp.int32, (AUG - HEAD_DIM, TQ), 1)
    q_hi = (qpos >> 7).astype(F32)
    q_lo = (qpos & 127).astype(F32)
    for h in range(A_HEADS):
        slope = 2.0 ** (-8.0 * (h + 1) / A_HEADS)
        spare = jnp.where(row64 == 0, 128.0 * slope, 0.0)
        spare = jnp.where(row64 == 1, slope, spare)
        spare = jnp.where(row64 == 2, (-128.0 * slope) * q_hi, spare)
        spare = jnp.where(row64 == 3, (-slope) * q_lo, spare)
        w_ref[h] = jnp.concatenate([qt_ref[0, h * HEAD_DIM:(h + 1) * HEAD_DIM, :],
                                    spare.astype(BF16)], axis=0)

    def attn_body(kc, _):
        rows = _chunk(kc)
        pos = posx_ref[rows, :]
        lhs = [k_ref[0, rows, g * AUG:(g + 1) * AUG] + pos for g in range(A_KV_HEADS)]
        bias = s_ref[rows, :]

        def score_fn(h):
            return _dot(lhs[h // rep], w_ref[h]) + bias

        def vt_fn(h):
            g = h // rep
            return vt_ref[0, g * HEAD_DIM:(g + 1) * HEAD_DIM, rows]

        _attend_chunk(A_HEADS, score_fn, vt_fn, m_ref, l_ref, acc_ref)
        return 0

    lax.fori_loop(0, nch, attn_body, 0)
    _softmax_finish(o_ref, A_HEADS, l_ref, acc_ref)


def _dsa_attn(iqt, ik4, wft, aqt, ak, posx, avt):
    bsz, w, seq = aqt.shape
    blk_t = lambda b, i: (b, 0, i)
    full = lambda b, i: (b, 0, 0)
    return pl.pallas_call(
        _dsa_kernel,
        out_shape=jax.ShapeDtypeStruct((bsz, w, seq), F32),
        grid=(bsz, seq // TQ),
        in_specs=[pl.BlockSpec((1, IDX_HEADS * IDX_K, TQ), blk_t),
                  pl.BlockSpec((1, seq, 256), full),
                  pl.BlockSpec((1, 16, TQ), blk_t),
                  pl.BlockSpec((1, w, TQ), blk_t),
                  pl.BlockSpec((1, seq, A_KV_HEADS * AUG), full),
                  pl.BlockSpec((seq, AUG), lambda b, i: (0, 0)),
                  pl.BlockSpec((1, 128, seq), full)],
        out_specs=pl.BlockSpec((1, w, TQ), blk_t),
        scratch_shapes=[pltpu.VMEM((seq, TQ), F32), pltpu.VMEM((A_HEADS, AUG, TQ), BF16),
                        pltpu.VMEM((A_HEADS, 1, TQ), F32), pltpu.VMEM((A_HEADS, 1, TQ), F32),
                        pltpu.VMEM((A_HEADS * HEAD_DIM, TQ), F32)],
        compiler_params=pltpu.CompilerParams(dimension_semantics=("arbitrary", "arbitrary"),
                                             vmem_limit_bytes=VMEM_LIMIT),
        name="dsa_attn",
    )(iqt, ik4, wft, aqt, ak, posx, avt)


def _rms_cols(xt, g_col):
    return xt * lax.rsqrt(jnp.mean(xt * xt, axis=0, keepdims=True) + EPS) * g_col


def _out_proj_kernel(oat_ref, obt_ref, x_ref, mod_ref, ga_ref, gb_ref, w_ref, o_ref):
    oa = _rms_cols(oat_ref[0], ga_ref[...]).astype(BF16)
    ob = _rms_cols(obt_ref[0], gb_ref[...]).astype(BF16)
    y = _dot_tn(oa, w_ref[0:512, :]) + _dot_tn(ob, w_ref[512:1024, :])
    o_ref[0] = x_ref[0] + mod_ref[0, 2:3, :] * y


def _out_proj(oat, obt, x3, mod3, g_out_a, g_out_b, w_out_bf):
    bsz, seq, d = x3.shape
    tm = 512
    blk_t = lambda b, i: (b, 0, i)
    blk_r = lambda b, i: (b, i, 0)
    const = lambda b, i: (0, 0)
    return pl.pallas_call(
        _out_proj_kernel,
        out_shape=jax.ShapeDtypeStruct((bsz, seq, d), F32),
        grid=(bsz, seq // tm),
        in_specs=[pl.BlockSpec((1, 512, tm), blk_t), pl.BlockSpec((1, 512, tm), blk_t),
                  pl.BlockSpec((1, tm, d), blk_r),
                  pl.BlockSpec((1, N_MOD, d), lambda b, i: (b, 0, 0)),
                  pl.BlockSpec((512, 1), const), pl.BlockSpec((512, 1), const),
                  pl.BlockSpec((d, d), const)],
        out_specs=pl.BlockSpec((1, tm, d), blk_r),
        compiler_params=pltpu.CompilerParams(dimension_semantics=("arbitrary", "arbitrary"),
                                             vmem_limit_bytes=VMEM_LIMIT),
        name="out_proj",
    )(oat, obt, x3, mod3, g_out_a.reshape(-1, 1), g_out_b.reshape(-1, 1), w_out_bf)


MOE_TMG = 256
MOE_TMC = 256
ROW_TILE = 8
ROUTE_COLS = 128


def _first_index_of_max(vals, lane, big):
    m = jnp.max(vals, axis=-1, keepdims=True)
    idx = jnp.min(jnp.where(vals == m, lane, big), axis=-1, keepdims=True)
    return m, idx


def _route(h, wr_hi_ref, wr_lo_ref, br_ref):
    h_hi, h_lo = _split_bf16(h)
    logits = _dot3(h_hi, h_lo, wr_hi_ref[...], wr_lo_ref[...]) + br_ref[...]
    lane = lax.broadcasted_iota(jnp.int32, logits.shape, 1)
    ninf = -jnp.inf
    gl = jnp.where(lane < N_GROUPS, logits, ninf)
    gmax, gsel = _first_index_of_max(gl, lane, 1 << 20)
    p_group = 1.0 / jnp.sum(jnp.exp(gl - gmax), axis=-1, keepdims=True)
    base = 32 + gsel * EXPERTS_PER_GROUP
    el = jnp.where((lane >= base) & (lane < base + EXPERTS_PER_GROUP), logits, ninf)
    v1, i1 = _first_index_of_max(el, lane, 1 << 20)
    el2 = jnp.where(lane == i1, ninf, el)
    v2, i2 = _first_index_of_max(el2, lane, 1 << 20)
    e2 = jnp.exp(v2 - v1)
    w1 = 1.0 / (1.0 + e2)
    w2 = e2 / (1.0 + e2)
    return i1, i2, w1 * p_group, w2 * p_group


def _token_rows(s, n_tokens, first_token=0):
    return pl.ds(first_token * ROW_TILE + s, n_tokens, stride=ROW_TILE)


def _to_token_tiles(ref, x):
    for s in range(ROW_TILE):
        ref[_token_rows(s, x.shape[0]), :] = x[:, s * 128:(s + 1) * 128]


def _token_tile(ref, t):
    return ref.at[pl.ds(pl.multiple_of(t * ROW_TILE, ROW_TILE), ROW_TILE), :]


def _moe_route_kernel(x_ref, mod_ref, gf_ref, wr_hi_ref, wr_lo_ref, br_ref,
                      h_ref, info_ref, infot_ref, cnt_ref, run_ref):
    @pl.when(pl.program_id(0) == 0)
    def _():
        run_ref[...] = jnp.zeros_like(run_ref)

    h = _rms(x_ref[...], gf_ref[...]) * (1.0 + mod_ref[0, 4:5, :]) + mod_ref[0, 3:4, :]
    _to_token_tiles(h_ref, h)
    i1, i2, w1, w2 = _route(h, wr_hi_ref, wr_lo_ref, br_ref)
    tm = h.shape[0]
    lane = lax.broadcasted_iota(jnp.int32, (tm, ROUTE_COLS), 1)
    picked = jnp.where((lane == i1) | (lane == i2), 1.0, 0.0)
    earlier = jnp.where(lax.broadcasted_iota(jnp.int32, (tm, tm), 1)
                        < lax.broadcasted_iota(jnp.int32, (tm, tm), 0), 1.0, 0.0).astype(BF16)
    before = _dot(earlier, picked.astype(BF16)) + run_ref[...]
    rank1 = jnp.sum(jnp.where(lane == i1, before, 0.0), axis=-1, keepdims=True)
    rank2 = jnp.sum(jnp.where(lane == i2, before, 0.0), axis=-1, keepdims=True)
    run_ref[...] += jnp.sum(picked, axis=0, keepdims=True)
    cnt_ref[...] = run_ref[...]
    info = jnp.where(lane == 0, (i1 - 32).astype(F32), 0.0)
    info = jnp.where(lane == 1, (i2 - 32).astype(F32), info)
    info = jnp.where(lane == 2, rank1, info)
    info = jnp.where(lane == 3, rank2, info)
    info = jnp.where(lane == 4, w1, info)
    info = jnp.where(lane == 5, w2, info)
    info_ref[...] = info
    infot_ref[...] = info.T[0:8, :]


def _moe_route(x1, mod3, g_ffn, wr_hi, wr_lo, b_route, seq):
    n, d = x1.shape
    tm = 512
    per_b = seq // tm
    row = lambda i: (i, 0)
    const = lambda i: (0, 0)
    return pl.pallas_call(
        _moe_route_kernel,
        out_shape=[jax.ShapeDtypeStruct((n * ROW_TILE, d // ROW_TILE), F32),
                   jax.ShapeDtypeStruct((n, ROUTE_COLS), F32), jax.ShapeDtypeStruct((8, n), F32),
                   jax.ShapeDtypeStruct((1, ROUTE_COLS), F32)],
        grid=(n // tm,),
        in_specs=[pl.BlockSpec((tm, d), row),
                  pl.BlockSpec((1, N_MOD, d), lambda i: (i // per_b, 0, 0)),
                  pl.BlockSpec((1, d), const),
                  pl.BlockSpec((d, ROUTE_COLS), const), pl.BlockSpec((d, ROUTE_COLS), const),
                  pl.BlockSpec((1, ROUTE_COLS), const)],
        out_specs=[pl.BlockSpec((tm * ROW_TILE, d // ROW_TILE), row),
                   pl.BlockSpec((tm, ROUTE_COLS), row), pl.BlockSpec((8, tm), lambda i: (0, i)),
                   pl.BlockSpec((1, ROUTE_COLS), const)],
        scratch_shapes=[pltpu.VMEM((1, ROUTE_COLS), F32)],
        compiler_params=pltpu.CompilerParams(dimension_semantics=("arbitrary",),
                                             vmem_limit_bytes=VMEM_LIMIT),
        name="moe_route",
    )(x1, mod3, g_ffn.reshape(1, d), wr_hi, wr_lo, b_route)


def _wait_token_copies(src_hbm, dst, sem, n_tokens):
    pltpu.make_async_copy(src_hbm.at[pl.ds(0, n_tokens * ROW_TILE), :], dst, sem).wait()


def _moe_scatter_kernel(dest_ref, h_hbm, zeros_hbm, xs_hbm, sem):
    del zeros_hbm
    i = pl.program_id(0)
    n_pairs = 2 * MOE_TMC

    def wait_batch():
        _wait_token_copies(h_hbm, xs_hbm.at[pl.ds(0, n_pairs * ROW_TILE), :], sem, n_pairs)

    @pl.when(i > 0)
    def _():
        wait_batch()

    for r in range(MOE_TMC):
        for j in range(2):
            pltpu.make_async_copy(_token_tile(h_hbm, i * MOE_TMC + r),
                                  _token_tile(xs_hbm, dest_ref[i * n_pairs + j * MOE_TMC + r]),
                                  sem).start()

    @pl.when(i == pl.num_programs(0) - 1)
    def _():
        wait_batch()


def _moe_scatter(dest, h2, n_rows):
    n = h2.shape[0] // ROW_TILE
    grid_spec = pltpu.PrefetchScalarGridSpec(
        num_scalar_prefetch=1,
        grid=(n // MOE_TMC,),
        in_specs=[pl.BlockSpec(memory_space=pl.ANY), pl.BlockSpec(memory_space=pl.ANY)],
        out_specs=pl.BlockSpec(memory_space=pl.ANY),
        scratch_shapes=[pltpu.SemaphoreType.DMA(())])
    return pl.pallas_call(
        _moe_scatter_kernel,
        out_shape=jax.ShapeDtypeStruct((n_rows * ROW_TILE, h2.shape[1]), F32),
        grid_spec=grid_spec,
        input_output_aliases={2: 0},
        compiler_params=pltpu.CompilerParams(dimension_semantics=("arbitrary",),
                                             vmem_limit_bytes=VMEM_LIMIT),
        name="moe_scatter",
    )(dest, h2, jnp.zeros((n_rows * ROW_TILE, h2.shape[1]), F32))


def _moe_expert_kernel(te_ref, nu_ref, x_ref, wg_ref, wu_ref, wd_ref, y_ref, xs_ref):
    i = pl.program_id(0)

    @pl.when(i < nu_ref[0])
    def _():
        for s in range(ROW_TILE):
            xs_ref[:, s * 128:(s + 1) * 128] = x_ref[_token_rows(s, MOE_TMG), :].astype(BF16)
        x = xs_ref[...]
        hg = _dot(x, wg_ref[0])
        hu = _dot(x, wu_ref[0])
        a = hg * jax.nn.sigmoid(hg) * hu
        _to_token_tiles(y_ref, _dot(a.astype(BF16), wd_ref[0]))

    @pl.when(i >= nu_ref[0])
    def _():
        y_ref[...] = jnp.zeros_like(y_ref)


def _moe_experts(tile_expert, n_used, xsorted, wg, wu, wd):
    n_rows = xsorted.shape[0] // ROW_TILE
    d, ff = wg.shape[1], wg.shape[2]
    tile = (MOE_TMG * ROW_TILE, xsorted.shape[1])
    grid_spec = pltpu.PrefetchScalarGridSpec(
        num_scalar_prefetch=2,
        grid=(n_rows // MOE_TMG,),
        in_specs=[pl.BlockSpec(tile, lambda i, te, nu: (i, 0)),
                  pl.BlockSpec((1, d, ff), lambda i, te, nu: (te[i], 0, 0)),
                  pl.BlockSpec((1, d, ff), lambda i, te, nu: (te[i], 0, 0)),
                  pl.BlockSpec((1, ff, d), lambda i, te, nu: (te[i], 0, 0))],
        out_specs=pl.BlockSpec(tile, lambda i, te, nu: (i, 0)),
        scratch_shapes=[pltpu.VMEM((MOE_TMG, d), BF16)])
    return pl.pallas_call(
        _moe_expert_kernel,
        out_shape=jax.ShapeDtypeStruct(xsorted.shape, F32),
        grid_spec=grid_spec,
        compiler_params=pltpu.CompilerParams(dimension_semantics=("arbitrary",),
                                             vmem_limit_bytes=VMEM_LIMIT),
        name="moe_experts",
    )(tile_expert, n_used, xsorted, wg, wu, wd)


def _moe_combine_kernel(dest_ref, y_hbm, x_ref, info_ref, mod_ref, gfin_ref, o_ref, ybuf, x2_ref, sem):
    i = pl.program_id(0)
    nt = pl.num_programs(0)
    slot = lax.rem(i, 2)
    n_pairs = 2 * MOE_TMC

    def start_gather(tile, to_slot):
        for r in range(n_pairs):
            pltpu.make_async_copy(_token_tile(y_hbm, dest_ref[tile * n_pairs + r]),
                                  _token_tile(ybuf.at[to_slot], r), sem.at[to_slot]).start()

    @pl.when(i == 0)
    def _():
        start_gather(0, 0)

    _wait_token_copies(y_hbm, ybuf.at[slot], sem.at[slot], n_pairs)

    @pl.when(i + 1 < nt)
    def _():
        start_gather(i + 1, 1 - slot)

    w1 = info_ref[:, 4:5]
    w2 = info_ref[:, 5:6]
    sumsq = jnp.zeros((MOE_TMC, 1), F32)
    for s in range(ROW_TILE):
        cols = slice(s * 128, (s + 1) * 128)
        y = (w1 * ybuf[slot, _token_rows(s, MOE_TMC), :]
             + w2 * ybuf[slot, _token_rows(s, MOE_TMC, first_token=MOE_TMC), :])
        x2 = x_ref[:, cols] + mod_ref[0, 5:6, cols] * y
        x2_ref[:, cols] = x2
        sumsq = sumsq + jnp.sum(x2 * x2, axis=-1, keepdims=True)
    d = x2_ref.shape[1]
    o_ref[...] = x2_ref[...] * lax.rsqrt(sumsq / d + EPS) * gfin_ref[...]


def _moe_combine(dest, ysorted, x1, info, mod3, g_final, seq):
    n, d = x1.shape
    per_b = seq // MOE_TMC
    grid_spec = pltpu.PrefetchScalarGridSpec(
        num_scalar_prefetch=1,
        grid=(n // MOE_TMC,),
        in_specs=[pl.BlockSpec(memory_space=pl.ANY),
                  pl.BlockSpec((MOE_TMC, d), lambda i, ds: (i, 0)),
                  pl.BlockSpec((MOE_TMC, ROUTE_COLS), lambda i, ds: (i, 0)),
                  pl.BlockSpec((1, N_MOD, d), lambda i, ds: (i // per_b, 0, 0)),
                  pl.BlockSpec((1, d), lambda i, ds: (0, 0))],
        out_specs=pl.BlockSpec((MOE_TMC, d), lambda i, ds: (i, 0)),
        scratch_shapes=[pltpu.VMEM((2, 2 * MOE_TMC * ROW_TILE, ysorted.shape[1]), F32),
                        pltpu.VMEM((MOE_TMC, d), F32), pltpu.SemaphoreType.DMA((2,))])
    return pl.pallas_call(
        _moe_combine_kernel,
        out_shape=jax.ShapeDtypeStruct((n, d), F32),
        grid_spec=grid_spec,
        compiler_params=pltpu.CompilerParams(dimension_semantics=("arbitrary",),
                                             vmem_limit_bytes=VMEM_LIMIT),
        name="moe_combine",
    )(dest, ysorted, x1, info, mod3, g_final.reshape(1, d))


def _moe(x1, mod3, g_ffn, wr_hi, wr_lo, b_route, wg, wu, wd, g_final, seq):
    n, d = x1.shape
    h2, info, infot, counts = _moe_route(x1, mod3, g_ffn, wr_hi, wr_lo, b_route, seq)

    e1, e2, rank1, rank2 = [infot[k].astype(jnp.int32) for k in range(4)]
    cnt = counts[0, 32:32 + N_EXPERTS].astype(jnp.int32)
    padded = ((cnt + MOE_TMG - 1) // MOE_TMG) * MOE_TMG
    seg_end = jnp.cumsum(padded)
    expert_ids = jnp.arange(N_EXPERTS, dtype=jnp.int32)

    def seg_start_of(e):
        return jnp.sum(jnp.where(expert_ids[None, :] < e[:, None], padded[None, :], 0), axis=1)

    dest1 = seg_start_of(e1) + rank1
    dest2 = seg_start_of(e2) + rank2
    n_rows = 2 * n + N_EXPERTS * MOE_TMG
    tile_start = jnp.arange(n_rows // MOE_TMG, dtype=jnp.int32) * MOE_TMG
    tile_expert = jnp.minimum(jnp.sum((tile_start[:, None] >= seg_end[None, :]).astype(jnp.int32), axis=1),
                              N_EXPERTS - 1)
    n_used = (seg_end[N_EXPERTS - 1:] // MOE_TMG).astype(jnp.int32)
    dest = jnp.concatenate([dest1.reshape(-1, MOE_TMC), dest2.reshape(-1, MOE_TMC)], axis=1).reshape(-1)

    xsorted = _moe_scatter(dest, h2, n_rows)
    ysorted = _moe_experts(tile_expert, n_used, xsorted, wg, wu, wd)
    return _moe_combine(dest, ysorted, x1, info, mod3, g_final, seq)


def _layer(x3, c, w_ada, b_ada, g_mix, w_in, b_forget, g_out_a, g_out_b, w_out,
           g_ffn, w_group, b_group, w_router, b_router, w_gate, w_up, w_down, g_final):
    bsz, seq, d = x3.shape
    mod3 = _ada_mod(c, w_ada, b_ada).reshape(bsz, N_MOD, d)

    w_t = w_in.T

    def pad_heads(w, n_heads):
        w = w.reshape(d, n_heads, HEAD_DIM)
        return jnp.concatenate([w, jnp.zeros_like(w)], axis=-1).reshape(d, n_heads * AUG)

    w_ik2 = jnp.concatenate([w_in[:, 1280:1344], w_in[:, 1280:1344]], axis=1)
    w_wf_t = jnp.concatenate([w_t[1344:1352], w_t[2888:2896]], axis=0)
    weights = [w_t[0:512].astype(BF16), pad_heads(w_in[:, 512:640], A_KV_HEADS).astype(BF16),
               w_t[640:768].astype(BF16), w_t[1352:1864].astype(BF16),
               pad_heads(w_in[:, 1864:2376], B_HEADS).astype(BF16), w_t[2376:2888].astype(BF16),
               *_split_bf16(w_t[768:1280]), *_split_bf16(w_ik2), *_split_bf16(w_wf_t)]
    aqt, ak, avt, bqt, bk, bvt, iqt, ik4, wft = _in_proj(x3, mod3, g_mix, weights)

    cumt, kaug = _fox_cum(wft, b_forget, bk)
    obt = _fox_attn(bqt, kaug, bvt, cumt)

    pos = jnp.arange(seq, dtype=jnp.int32)[:, None]
    lane = jnp.arange(AUG, dtype=jnp.int32)[None, :] - HEAD_DIM
    posx = jnp.where(lane == 0, pos >> 7, jnp.where(lane == 1, pos & 127,
                     jnp.where((lane == 2) | (lane == 3), 1, 0))).astype(BF16)
    oat = _dsa_attn(iqt, ik4, wft, aqt, ak, posx, avt)

    x1 = _out_proj(oat, obt, x3, mod3, g_out_a, g_out_b, w_out.astype(BF16))

    w_r = jnp.concatenate([w_group, jnp.zeros((d, 32 - N_GROUPS), F32),
                           jnp.transpose(w_router, (1, 0, 2)).reshape(d, N_EXPERTS),
                           jnp.zeros((d, ROUTE_COLS - 64), F32)], axis=1)
    b_r = jnp.concatenate([b_group, jnp.zeros((32 - N_GROUPS,), F32), b_router.reshape(-1),
                           jnp.zeros((ROUTE_COLS - 64,), F32)]).reshape(1, ROUTE_COLS)
    wr_hi, wr_lo = _split_bf16(w_r)
    out = _moe(x1.reshape(bsz * seq, d), mod3, g_ffn, wr_hi, wr_lo, b_r, w_gate.astype(BF16),
               w_up.astype(BF16), w_down.astype(BF16), g_final, seq)
    return out.reshape(bsz, seq, d)


def kernel(x, c, w_ada, b_ada, g_mix, w_in, b_forget, g_out_a, g_out_b, w_out, g_ffn, w_group,
           b_group, w_router, b_router, w_gate, w_up, w_down, g_final):
    depth = w_ada.shape[0]
    assert depth == 1, "final norm is fused into the single layer's MoE kernel"
    return _layer(x, c, w_ada[0], b_ada[0], g_mix[0], w_in[0], b_forget[0], g_out_a[0], g_out_b[0],
                  w_out[0], g_ffn[0], w_group[0], b_group[0], w_router[0], b_router[0], w_gate[0],
                  w_up[0], w_down[0], g_final)
```

```python
import jax
import jax.numpy as jnp
from jax import lax
from jax.experimental import pallas as pl
from jax.experimental.pallas import tpu as pltpu

F32 = jnp.float32
BF16 = jnp.bfloat16

EPS = 1e-6
A_HEADS = 8
A_KV_HEADS = 2
HEAD_DIM = 64
IDX_HEADS = 8
IDX_DIM = 64
TOPK = 256
B_HEADS = 8
N_GROUPS = 4
EXPERTS_PER_GROUP = 8
N_EXPERTS = N_GROUPS * EXPERTS_PER_GROUP
N_MOD = 6

NEG_BIG = -1e30
VMEM_LIMIT = 48 * 1024 * 1024

TQ = 256
KC = 256
AUG = 128
IDX_K = 256
COARSE_STEPS = 12
assert TQ == KC == TOPK


def _split_bf16(x):
    hi = x.astype(BF16)
    lo = (x - hi.astype(F32)).astype(BF16)
    return hi, lo


def _split3_f32(x):
    p1 = x.astype(BF16).astype(F32)
    r1 = x - p1
    p2 = r1.astype(BF16).astype(F32)
    p3 = (r1 - p2).astype(BF16).astype(F32)
    return p1, p2, p3


def _dot(a, b):
    return jnp.dot(a, b, preferred_element_type=F32)


def _dot_nt(a, b):
    return lax.dot_general(a, b, (((1,), (1,)), ((), ())), preferred_element_type=F32)


def _dot_tn(a, b):
    return lax.dot_general(a, b, (((0,), (0,)), ((), ())), preferred_element_type=F32)


def _dot3(a_hi, a_lo, b_hi, b_lo):
    return _dot(a_hi, b_hi) + _dot(a_lo, b_hi) + _dot(a_hi, b_lo)


def _dot3_nt(a_hi, a_lo, b_hi, b_lo):
    return _dot_nt(a_hi, b_hi) + _dot_nt(a_lo, b_hi) + _dot_nt(a_hi, b_lo)


def _rms(x, g):
    return x * lax.rsqrt(jnp.mean(x * x, axis=-1, keepdims=True) + EPS) * g


def _chunk(kc):
    return pl.ds(pl.multiple_of(kc * KC, KC), KC)


def _key_minus_query():
    return (lax.broadcasted_iota(jnp.int32, (KC, TQ), 0)
            - lax.broadcasted_iota(jnp.int32, (KC, TQ), 1))


def _ada_kernel(c_ref, w_ref, b_ref, o_ref):
    c = c_ref[...]
    s = c * jax.nn.sigmoid(c)
    s_hi, s_lo = _split_bf16(s)
    w_hi, w_lo = _split_bf16(w_ref[...])
    o_ref[...] = _dot3(s_hi, s_lo, w_hi, w_lo) + b_ref[...]


def _ada_mod(c, w_ada, b_ada):
    bsz, d = c.shape
    n = w_ada.shape[1]
    tn = 1024
    return pl.pallas_call(
        _ada_kernel,
        out_shape=jax.ShapeDtypeStruct((bsz, n), F32),
        grid=(n // tn,),
        in_specs=[pl.BlockSpec((bsz, d), lambda j: (0, 0)),
                  pl.BlockSpec((d, tn), lambda j: (0, j)),
                  pl.BlockSpec((1, tn), lambda j: (0, j))],
        out_specs=pl.BlockSpec((bsz, tn), lambda j: (0, j)),
        compiler_params=pltpu.CompilerParams(dimension_semantics=("arbitrary",),
                                             vmem_limit_bytes=VMEM_LIMIT),
        name="ada_mod",
    )(c, w_ada, b_ada.reshape(1, n))


def _in_proj_kernel(x_ref, mod_ref, g_ref,
                    waq_ref, wak_ref, wav_ref, wbq_ref, wbk_ref, wbv_ref,
                    wiqh_ref, wiql_ref, wikh_ref, wikl_ref, wwfh_ref, wwfl_ref,
                    aqt_ref, ak_ref, avt_ref, bqt_ref, bk_ref, bvt_ref, iqt_ref, ik_ref, wft_ref):
    x = x_ref[0]
    h = _rms(x, g_ref[...]) * (1.0 + mod_ref[0, 1:2, :]) + mod_ref[0, 0:1, :]
    h_hi, h_lo = _split_bf16(h)
    aqt_ref[0] = (_dot_nt(waq_ref[...], h_hi) * 0.125).astype(BF16)
    ak_ref[0] = _dot(h_hi, wak_ref[...]).astype(BF16)
    avt_ref[0] = _dot_nt(wav_ref[...], h_hi).astype(BF16)
    bqt_ref[0] = (_dot_nt(wbq_ref[...], h_hi) * 0.125).astype(BF16)
    bk_ref[0] = _dot(h_hi, wbk_ref[...]).astype(BF16)
    bvt_ref[0] = _dot_nt(wbv_ref[...], h_hi).astype(BF16)
    iqt = _dot3_nt(wiqh_ref[...], wiql_ref[...], h_hi, h_lo)
    for hd in range(IDX_HEADS):
        q_hi, q_lo = _split_bf16(iqt[hd * IDX_DIM:(hd + 1) * IDX_DIM, :])
        iqt_ref[0, hd * IDX_K:(hd + 1) * IDX_K, :] = jnp.concatenate([q_hi, q_lo, q_hi, q_lo], axis=0)
    ik2 = _dot3(h_hi, h_lo, wikh_ref[...], wikl_ref[...])
    k_hi, k_lo = _split_bf16(ik2)
    ik_ref[0] = jnp.concatenate([k_hi, k_lo], axis=1)
    wft_ref[0] = _dot3_nt(wwfh_ref[...], wwfl_ref[...], h_hi, h_lo)


def _in_proj(x3, mod3, g_mix, weights):
    bsz, seq, d = x3.shape
    tm = 512
    blk_t = lambda b, i: (b, 0, i)
    blk_r = lambda b, i: (b, i, 0)
    const = lambda b, i: (0, 0)
    ak_w = A_KV_HEADS * AUG
    bk_w = B_HEADS * AUG
    outs = [jax.ShapeDtypeStruct((bsz, 512, seq), BF16), jax.ShapeDtypeStruct((bsz, seq, ak_w), BF16),
            jax.ShapeDtypeStruct((bsz, 128, seq), BF16), jax.ShapeDtypeStruct((bsz, 512, seq), BF16),
            jax.ShapeDtypeStruct((bsz, seq, bk_w), BF16), jax.ShapeDtypeStruct((bsz, 512, seq), BF16),
            jax.ShapeDtypeStruct((bsz, IDX_HEADS * IDX_K, seq), BF16),
            jax.ShapeDtypeStruct((bsz, seq, 256), BF16), jax.ShapeDtypeStruct((bsz, 16, seq), F32)]
    out_specs = [pl.BlockSpec((1, 512, tm), blk_t), pl.BlockSpec((1, tm, ak_w), blk_r),
                 pl.BlockSpec((1, 128, tm), blk_t), pl.BlockSpec((1, 512, tm), blk_t),
                 pl.BlockSpec((1, tm, bk_w), blk_r), pl.BlockSpec((1, 512, tm), blk_t),
                 pl.BlockSpec((1, IDX_HEADS * IDX_K, tm), blk_t),
                 pl.BlockSpec((1, tm, 256), blk_r), pl.BlockSpec((1, 16, tm), blk_t)]
    return pl.pallas_call(
        _in_proj_kernel,
        out_shape=outs,
        grid=(bsz, seq // tm),
        in_specs=[pl.BlockSpec((1, tm, d), blk_r),
                  pl.BlockSpec((1, N_MOD, d), lambda b, i: (b, 0, 0)),
                  pl.BlockSpec((1, d), const)] + [pl.BlockSpec(w.shape, const) for w in weights],
        out_specs=out_specs,
        compiler_params=pltpu.CompilerParams(dimension_semantics=("arbitrary", "arbitrary"),
                                             vmem_limit_bytes=VMEM_LIMIT),
        name="in_proj",
    )(x3, mod3, g_mix.reshape(1, d), *weights)


CB = 256


def _cum_kernel(wft_ref, bfor_ref, k_ref, cumt_ref, kaug_ref):
    seq = wft_ref.shape[2]
    r = lax.broadcasted_iota(jnp.int32, (CB, CB), 0)
    cidx = lax.broadcasted_iota(jnp.int32, (CB, CB), 1)
    tri = jnp.where(r <= cidx, 1.0, 0.0).astype(BF16)
    row128 = lax.broadcasted_iota(jnp.int32, (AUG, CB), 0)
    ones_rows = jnp.where((row128 >= HEAD_DIM + 3) & (row128 < HEAD_DIM + 6), 1.0, 0.0)
    carry = jnp.zeros((8, 1), F32)
    for blk in range(seq // CB):
        cols = slice(blk * CB, (blk + 1) * CB)
        z = wft_ref[0, 8:16, cols] + bfor_ref[...]
        logf = jnp.minimum(z, 0.0) - jnp.log(1.0 + jnp.exp(-jnp.abs(z)))
        p1, p2, p3 = _split3_f32(logf)
        pieces = jnp.concatenate([p1, p2, p3, jnp.zeros_like(p1)], axis=0).astype(BF16)
        parts = _dot(pieces, tri)
        cum = parts[0:8] + parts[8:16] + parts[16:24] + carry
        cumt_ref[0, :, cols] = cum
        carry = cum[:, CB - 1:CB]
        c1, c2, c3 = _split3_f32(cum)
        for h in range(B_HEADS):
            spare = jnp.where(row128 == HEAD_DIM, -c1[h:h + 1], ones_rows)
            spare = jnp.where(row128 == HEAD_DIM + 1, -c2[h:h + 1], spare)
            spare = jnp.where(row128 == HEAD_DIM + 2, -c3[h:h + 1], spare)
            lanes = slice(h * AUG, (h + 1) * AUG)
            kaug_ref[0, cols, lanes] = k_ref[0, cols, lanes] + spare.T.astype(BF16)


def _fox_cum(wft, b_forget, bk):
    bsz, _, seq = wft.shape
    nh = B_HEADS
    kw = bk.shape[-1]
    return pl.pallas_call(
        _cum_kernel,
        out_shape=[jax.ShapeDtypeStruct((bsz, nh, seq), F32), jax.ShapeDtypeStruct((bsz, seq, kw), BF16)],
        grid=(bsz,),
        in_specs=[pl.BlockSpec((1, 16, seq), lambda b: (b, 0, 0)),
                  pl.BlockSpec((nh, 1), lambda b: (0, 0)),
                  pl.BlockSpec((1, seq, kw), lambda b: (b, 0, 0))],
        out_specs=[pl.BlockSpec((1, nh, seq), lambda b: (b, 0, 0)),
                   pl.BlockSpec((1, seq, kw), lambda b: (b, 0, 0))],
        compiler_params=pltpu.CompilerParams(dimension_semantics=("arbitrary",),
                                             vmem_limit_bytes=VMEM_LIMIT),
        name="fox_cum",
    )(wft, b_forget.reshape(nh, 1), bk)


def _softmax_init(m_ref, l_ref, acc_ref):
    m_ref[...] = jnp.full(m_ref.shape, NEG_BIG, F32)
    l_ref[...] = jnp.zeros(l_ref.shape, F32)
    acc_ref[...] = jnp.zeros(acc_ref.shape, F32)


def _attend_chunk(n_heads, score_fn, vt_fn, m_ref, l_ref, acc_ref):
    scores = [score_fn(h) for h in range(n_heads)]
    probs, alphas = [], []
    for h in range(n_heads):
        s = scores[h]
        m_old = m_ref[h]
        m_new = jnp.maximum(m_old, jnp.max(s, axis=0, keepdims=True))
        alpha = jnp.exp(m_old - m_new)
        p = jnp.exp(s - m_new)
        l_ref[h] = alpha * l_ref[h] + jnp.sum(p, axis=0, keepdims=True)
        m_ref[h] = m_new
        probs.append(p.astype(BF16))
        alphas.append(alpha)
    for h in range(n_heads):
        rows = slice(h * HEAD_DIM, (h + 1) * HEAD_DIM)
        acc_ref[rows, :] = alphas[h] * acc_ref[rows, :] + _dot(vt_fn(h), probs[h])


def _softmax_finish(o_ref, n_heads, l_ref, acc_ref):
    for h in range(n_heads):
        rows = slice(h * HEAD_DIM, (h + 1) * HEAD_DIM)
        o_ref[0, rows, :] = acc_ref[rows, :] / l_ref[h]


def _fox_kernel(qt_ref, k_ref, vt_ref, cumt_ref, o_ref, w_ref, m_ref, l_ref, acc_ref):
    qi = pl.program_id(1)
    _softmax_init(m_ref, l_ref, acc_ref)

    row64 = lax.broadcasted_iota(jnp.int32, (AUG - HEAD_DIM, TQ), 0)
    for h in range(B_HEADS):
        c1, c2, c3 = _split3_f32(cumt_ref[0, h:h + 1, :])
        spare = jnp.where(row64 < 3, 1.0, 0.0)
        spare = jnp.where(row64 == 3, c1, spare)
        spare = jnp.where(row64 == 4, c2, spare)
        spare = jnp.where(row64 == 5, c3, spare)
        w_ref[h] = jnp.concatenate([qt_ref[0, h * HEAD_DIM:(h + 1) * HEAD_DIM, :],
                                    spare.astype(BF16)], axis=0)

    def tile(kc, diagonal):
        rows = _chunk(kc)

        def score_fn(h):
            s = _dot(k_ref[0, rows, h * AUG:(h + 1) * AUG], w_ref[h])
            if diagonal:
                s = jnp.where(_key_minus_query() <= 0, s, NEG_BIG)
            return s

        def vt_fn(h):
            return vt_ref[0, h * HEAD_DIM:(h + 1) * HEAD_DIM, rows]

        _attend_chunk(B_HEADS, score_fn, vt_fn, m_ref, l_ref, acc_ref)

    def body(kc, _):
        tile(kc, False)
        return 0

    lax.fori_loop(0, qi, body, 0)
    tile(qi, True)
    _softmax_finish(o_ref, B_HEADS, l_ref, acc_ref)


def _fox_attn(bqt, kaug, bvt, cumt):
    bsz, w, seq = bqt.shape
    blk_t = lambda b, i: (b, 0, i)
    full = lambda b, i: (b, 0, 0)
    return pl.pallas_call(
        _fox_kernel,
        out_shape=jax.ShapeDtypeStruct((bsz, w, seq), F32),
        grid=(bsz, seq // TQ),
        in_specs=[pl.BlockSpec((1, w, TQ), blk_t),
                  pl.BlockSpec((1, seq, B_HEADS * AUG), full),
                  pl.BlockSpec((1, w, seq), full),
                  pl.BlockSpec((1, B_HEADS, TQ), blk_t)],
        out_specs=pl.BlockSpec((1, w, TQ), blk_t),
        scratch_shapes=[pltpu.VMEM((B_HEADS, AUG, TQ), BF16), pltpu.VMEM((B_HEADS, 1, TQ), F32),
                        pltpu.VMEM((B_HEADS, 1, TQ), F32), pltpu.VMEM((B_HEADS * HEAD_DIM, TQ), F32)],
        compiler_params=pltpu.CompilerParams(dimension_semantics=("arbitrary", "arbitrary"),
                                             vmem_limit_bytes=VMEM_LIMIT),
        name="fox_attn",
    )(bqt, kaug, bvt, cumt)


def _dsa_kernel(iqt_ref, ik_ref, wft_ref, qt_ref, k_ref, posx_ref, vt_ref, o_ref,
                s_ref, w_ref, m_ref, l_ref, acc_ref):
    qi = pl.program_id(1)
    nch = qi + 1
    kmq = _key_minus_query()

    def score_body(kc, _):
        ik = ik_ref[0, _chunk(kc), :]
        acc = jnp.zeros((KC, TQ), F32)
        for h in range(IDX_HEADS):
            d = _dot(ik, iqt_ref[0, h * IDX_K:(h + 1) * IDX_K, :])
            acc = acc + wft_ref[0, h:h + 1, :] * jnp.maximum(d, 0.0)
        acc = jnp.where(acc == 0.0, 0.0, acc)
        causal = kmq <= (qi - kc) * KC
        s_ref[_chunk(kc), :] = jnp.where(causal, acc, -jnp.inf)
        return 0

    lax.fori_loop(0, nch, score_body, 0)

    @pl.when(qi == 0)
    def _():
        s_ref[0:KC, :] = jnp.where(kmq <= 0, 0.0, NEG_BIG)

    @pl.when(qi > 0)
    def _():
        def scan(fn, init):
            return lax.fori_loop(0, nch, lambda kc, c: fn(s_ref[_chunk(kc), :], c), init)

        def cmin(x):
            return jnp.min(x, axis=0, keepdims=True)

        def cmax(x):
            return jnp.max(x, axis=0, keepdims=True)

        def csum(x):
            return jnp.sum(x, axis=0, keepdims=True)

        zeros = jnp.zeros((1, TQ), F32)
        pinf = jnp.full((1, TQ), jnp.inf, F32)
        ninf = jnp.full((1, TQ), -jnp.inf, F32)

        def init_fn(s, c):
            lo, hi = c
            lo = jnp.minimum(lo, cmin(jnp.where(s > -jnp.inf, s, jnp.inf)))
            hi = jnp.maximum(hi, cmax(s))
            return lo, hi

        lo, hi = scan(init_fn, (pinf, ninf))

        def coarse_step(_, carry):
            lo, hi = carry
            mid = lo + (hi - lo) * 0.5
            cnt = scan(lambda s, c: c + csum(jnp.where(s >= mid, 1.0, 0.0)), zeros)
            enough = cnt >= float(TOPK)
            return jnp.where(enough, mid, lo), jnp.where(enough, hi, mid)

        lo, hi = lax.fori_loop(0, COARSE_STEPS, coarse_step, (lo, hi))

        def cond(carry):
            return carry[2] > 0

        def step(carry):
            lo, hi, _ = carry
            mid = lo + (hi - lo) * 0.5
            mid = jnp.where(mid <= lo, hi, mid)

            def fn(s, c):
                cnt, a, b = c
                ge = s >= mid
                cnt = cnt + csum(jnp.where(ge, 1.0, 0.0))
                b = jnp.minimum(b, cmin(jnp.where(ge, s, jnp.inf)))
                a = jnp.maximum(a, cmax(jnp.where(ge, -jnp.inf, s)))
                return cnt, a, b

            cnt, a, b = scan(fn, (zeros, ninf, pinf))
            enough = cnt >= float(TOPK)
            new_lo = jnp.where(enough, b, lo)
            new_hi = jnp.where(enough, jnp.where(cnt == float(TOPK), b, hi), a)
            active = jnp.max(jnp.where(new_lo < new_hi, 1, 0))
            return new_lo, new_hi, active

        first_active = jnp.max(jnp.where(lo < hi, 1, 0))
        thr, _, _ = lax.while_loop(cond, step, (lo, hi, first_active))

        n_gt = scan(lambda s, c: c + csum(jnp.where(s > thr, 1.0, 0.0)), zeros)
        need = float(TOPK) - n_gt
        lower = jnp.where(lax.broadcasted_iota(jnp.int32, (KC, KC), 1)
                          < lax.broadcasted_iota(jnp.int32, (KC, KC), 0), 1.0, 0.0).astype(BF16)

        def sel_body(kc, run):
            s = s_ref[_chunk(kc), :]
            eq = s == thr
            eqf = jnp.where(eq, 1.0, 0.0)
            before = _dot(lower, eqf.astype(BF16)) + run
            sel = (s > thr) | (eq & (before < need))
            s_ref[_chunk(kc), :] = jnp.where(sel, 0.0, NEG_BIG)
            return run + csum(eqf)

        lax.fori_loop(0, nch, sel_body, zeros)

    _softmax_init(m_ref, l_ref, acc_ref)
    rep = A_HEADS // A_KV_HEADS
    row64 = lax.broadcasted_iota(jnp.int32, (AUG - HEAD_DIM, TQ), 0)
    qpos = qi * TQ + lax.broadcasted_iota(jnp.int32, (AUG - HEAD_DIM, TQ), 1)
    q_hi = (qpos >> 7).astype(F32)
    q_lo = (qpos & 127).astype(F32)
    for h in range(A_HEADS):
        slope = 2.0 ** (-8.0 * (h + 1) / A_HEADS)
        spare = jnp.where(row64 == 0, 128.0 * slope, 0.0)
        spare = jnp.where(row64 == 1, slope, spare)
        spare = jnp.where(row64 == 2, (-128.0 * slope) * q_hi, spare)
        spare = jnp.where(row64 == 3, (-slope) * q_lo, spare)
        w_ref[h] = jnp.concatenate([qt_ref[0, h * HEAD_DIM:(h + 1) * HEAD_DIM, :],
                                    spare.astype(BF16)], axis=0)

    def attn_body(kc, _):
        rows = _chunk(kc)
        pos = posx_ref[rows, :]
        lhs = [k_ref[0, rows, g * AUG:(g + 1) * AUG] + pos for g in range(A_KV_HEADS)]
        bias = s_ref[rows, :]

        def score_fn(h):
            return _dot(lhs[h // rep], w_ref[h]) + bias

        def vt_fn(h):
            g = h // rep
            return vt_ref[0, g * HEAD_DIM:(g + 1) * HEAD_DIM, rows]

        _attend_chunk(A_HEADS, score_fn, vt_fn, m_ref, l_ref, acc_ref)
        return 0

    lax.fori_loop(0, nch, attn_body, 0)
    _softmax_finish(o_ref, A_HEADS, l_ref, acc_ref)


def _dsa_attn(iqt, ik4, wft, aqt, ak, posx, avt):
    bsz, w, seq = aqt.shape
    blk_t = lambda b, i: (b, 0, i)
    full = lambda b, i: (b, 0, 0)
    return pl.pallas_call(
        _dsa_kernel,
        out_shape=jax.ShapeDtypeStruct((bsz, w, seq), F32),
        grid=(bsz, seq // TQ),
        in_specs=[pl.BlockSpec((1, IDX_HEADS * IDX_K, TQ), blk_t),
                  pl.BlockSpec((1, seq, 256), full),
                  pl.BlockSpec((1, 16, TQ), blk_t),
                  pl.BlockSpec((1, w, TQ), blk_t),
                  pl.BlockSpec((1, seq, A_KV_HEADS * AUG), full),
                  pl.BlockSpec((seq, AUG), lambda b, i: (0, 0)),
                  pl.BlockSpec((1, 128, seq), full)],
        out_specs=pl.BlockSpec((1, w, TQ), blk_t),
        scratch_shapes=[pltpu.VMEM((seq, TQ), F32), pltpu.VMEM((A_HEADS, AUG, TQ), BF16),
                        pltpu.VMEM((A_HEADS, 1, TQ), F32), pltpu.VMEM((A_HEADS, 1, TQ), F32),
                        pltpu.VMEM((A_HEADS * HEAD_DIM, TQ), F32)],
        compiler_params=pltpu.CompilerParams(dimension_semantics=("arbitrary", "arbitrary"),
                                             vmem_limit_bytes=VMEM_LIMIT),
        name="dsa_attn",
    )(iqt, ik4, wft, aqt, ak, posx, avt)


def _rms_cols(xt, g_col):
    return xt * lax.rsqrt(jnp.mean(xt * xt, axis=0, keepdims=True) + EPS) * g_col


def _out_proj_kernel(oat_ref, obt_ref, x_ref, mod_ref, ga_ref, gb_ref, w_ref, o_ref):
    oa = _rms_cols(oat_ref[0], ga_ref[...]).astype(BF16)
    ob = _rms_cols(obt_ref[0], gb_ref[...]).astype(BF16)
    y = _dot_tn(oa, w_ref[0:512, :]) + _dot_tn(ob, w_ref[512:1024, :])
    o_ref[0] = x_ref[0] + mod_ref[0, 2:3, :] * y


def _out_proj(oat, obt, x3, mod3, g_out_a, g_out_b, w_out_bf):
    bsz, seq, d = x3.shape
    tm = 512
    blk_t = lambda b, i: (b, 0, i)
    blk_r = lambda b, i: (b, i, 0)
    const = lambda b, i: (0, 0)
    return pl.pallas_call(
        _out_proj_kernel,
        out_shape=jax.ShapeDtypeStruct((bsz, seq, d), F32),
        grid=(bsz, seq // tm),
        in_specs=[pl.BlockSpec((1, 512, tm), blk_t), pl.BlockSpec((1, 512, tm), blk_t),
                  pl.BlockSpec((1, tm, d), blk_r),
                  pl.BlockSpec((1, N_MOD, d), lambda b, i: (b, 0, 0)),
                  pl.BlockSpec((512, 1), const), pl.BlockSpec((512, 1), const),
                  pl.BlockSpec((d, d), const)],
        out_specs=pl.BlockSpec((1, tm, d), blk_r),
        compiler_params=pltpu.CompilerParams(dimension_semantics=("arbitrary", "arbitrary"),
                                             vmem_limit_bytes=VMEM_LIMIT),
        name="out_proj",
    )(oat, obt, x3, mod3, g_out_a.reshape(-1, 1), g_out_b.reshape(-1, 1), w_out_bf)


MOE_TMG = 256
MOE_TMC = 256
ROW_TILE = 8
ROUTE_COLS = 128


def _first_index_of_max(vals, lane, big):
    m = jnp.max(vals, axis=-1, keepdims=True)
    idx = jnp.min(jnp.where(vals == m, lane, big), axis=-1, keepdims=True)
    return m, idx


def _route(h, wr_hi_ref, wr_lo_ref, br_ref):
    h_hi, h_lo = _split_bf16(h)
    logits = _dot3(h_hi, h_lo, wr_hi_ref[...], wr_lo_ref[...]) + br_ref[...]
    lane = lax.broadcasted_iota(jnp.int32, logits.shape, 1)
    ninf = -jnp.inf
    gl = jnp.where(lane < N_GROUPS, logits, ninf)
    gmax, gsel = _first_index_of_max(gl, lane, 1 << 20)
    p_group = 1.0 / jnp.sum(jnp.exp(gl - gmax), axis=-1, keepdims=True)
    base = 32 + gsel * EXPERTS_PER_GROUP
    el = jnp.where((lane >= base) & (lane < base + EXPERTS_PER_GROUP), logits, ninf)
    v1, i1 = _first_index_of_max(el, lane, 1 << 20)
    el2 = jnp.where(lane == i1, ninf, el)
    v2, i2 = _first_index_of_max(el2, lane, 1 << 20)
    e2 = jnp.exp(v2 - v1)
    w1 = 1.0 / (1.0 + e2)
    w2 = e2 / (1.0 + e2)
    return i1, i2, w1 * p_group, w2 * p_group


def _token_rows(s, n_tokens, first_token=0):
    return pl.ds(first_token * ROW_TILE + s, n_tokens, stride=ROW_TILE)


def _to_token_tiles(ref, x):
    for s in range(ROW_TILE):
        ref[_token_rows(s, x.shape[0]), :] = x[:, s * 128:(s + 1) * 128]


def _token_tile(ref, t):
    start = t * ROW_TILE if isinstance(t, int) else pl.multiple_of(t * ROW_TILE, ROW_TILE)
    return ref.at[pl.ds(start, ROW_TILE), :]


def _moe_route_kernel(x_ref, mod_ref, gf_ref, wr_hi_ref, wr_lo_ref, br_ref,
                      h_ref, info_ref, infot_ref, cnt_ref, run_ref):
    @pl.when(pl.program_id(0) == 0)
    def _():
        run_ref[...] = jnp.zeros_like(run_ref)

    h = _rms(x_ref[...], gf_ref[...]) * (1.0 + mod_ref[0, 4:5, :]) + mod_ref[0, 3:4, :]
    _to_token_tiles(h_ref, h)
    i1, i2, w1, w2 = _route(h, wr_hi_ref, wr_lo_ref, br_ref)
    tm = h.shape[0]
    lane = lax.broadcasted_iota(jnp.int32, (tm, ROUTE_COLS), 1)
    picked = jnp.where((lane == i1) | (lane == i2), 1.0, 0.0)
    earlier = jnp.where(lax.broadcasted_iota(jnp.int32, (tm, tm), 1)
                        < lax.broadcasted_iota(jnp.int32, (tm, tm), 0), 1.0, 0.0).astype(BF16)
    before = _dot(earlier, picked.astype(BF16)) + run_ref[...]
    rank1 = jnp.sum(jnp.where(lane == i1, before, 0.0), axis=-1, keepdims=True)
    rank2 = jnp.sum(jnp.where(lane == i2, before, 0.0), axis=-1, keepdims=True)
    run_ref[...] += jnp.sum(picked, axis=0, keepdims=True)
    cnt_ref[...] = run_ref[...]
    info = jnp.where(lane == 0, (i1 - 32).astype(F32), 0.0)
    info = jnp.where(lane == 1, (i2 - 32).astype(F32), info)
    info = jnp.where(lane == 2, rank1, info)
    info = jnp.where(lane == 3, rank2, info)
    info = jnp.where(lane == 4, w1, info)
    info = jnp.where(lane == 5, w2, info)
    info_ref[...] = info
    infot_ref[...] = info.T[0:8, :]


def _moe_route(x1, mod3, g_ffn, wr_hi, wr_lo, b_route, seq):
    n, d = x1.shape
    tm = 512
    per_b = seq // tm
    row = lambda i: (i, 0)
    const = lambda i: (0, 0)
    return pl.pallas_call(
        _moe_route_kernel,
        out_shape=[jax.ShapeDtypeStruct((n * ROW_TILE, d // ROW_TILE), F32),
                   jax.ShapeDtypeStruct((n, ROUTE_COLS), F32), jax.ShapeDtypeStruct((8, n), F32),
                   jax.ShapeDtypeStruct((1, ROUTE_COLS), F32)],
        grid=(n // tm,),
        in_specs=[pl.BlockSpec((tm, d), row),
                  pl.BlockSpec((1, N_MOD, d), lambda i: (i // per_b, 0, 0)),
                  pl.BlockSpec((1, d), const),
                  pl.BlockSpec((d, ROUTE_COLS), const), pl.BlockSpec((d, ROUTE_COLS), const),
                  pl.BlockSpec((1, ROUTE_COLS), const)],
        out_specs=[pl.BlockSpec((tm * ROW_TILE, d // ROW_TILE), row),
                   pl.BlockSpec((tm, ROUTE_COLS), row), pl.BlockSpec((8, tm), lambda i: (0, i)),
                   pl.BlockSpec((1, ROUTE_COLS), const)],
        scratch_shapes=[pltpu.VMEM((1, ROUTE_COLS), F32)],
        compiler_params=pltpu.CompilerParams(dimension_semantics=("arbitrary",),
                                             vmem_limit_bytes=VMEM_LIMIT),
        name="moe_route",
    )(x1, mod3, g_ffn.reshape(1, d), wr_hi, wr_lo, b_route)


def _wait_token_copies(src_hbm, dst, sem, n_tokens):
    pltpu.make_async_copy(src_hbm.at[pl.ds(0, n_tokens * ROW_TILE), :], dst, sem).wait()


def _moe_scatter_kernel(dest_ref, h_ref, zeros_hbm, xs_hbm, sem):
    del zeros_hbm
    i = pl.program_id(0)
    n_pairs = 2 * MOE_TMC
    for r in range(MOE_TMC):
        for j in range(2):
            pltpu.make_async_copy(_token_tile(h_ref, r),
                                  _token_tile(xs_hbm, dest_ref[i * n_pairs + j * MOE_TMC + r]),
                                  sem).start(priority=j)
    _wait_token_copies(xs_hbm, xs_hbm.at[pl.ds(0, n_pairs * ROW_TILE), :], sem, n_pairs)


def _moe_scatter(dest, h2, n_rows):
    n = h2.shape[0] // ROW_TILE
    grid_spec = pltpu.PrefetchScalarGridSpec(
        num_scalar_prefetch=1,
        grid=(n // MOE_TMC,),
        in_specs=[pl.BlockSpec((MOE_TMC * ROW_TILE, h2.shape[1]), lambda i, ds: (i, 0)),
                  pl.BlockSpec(memory_space=pl.ANY)],
        out_specs=pl.BlockSpec(memory_space=pl.ANY),
        scratch_shapes=[pltpu.SemaphoreType.DMA(())])
    return pl.pallas_call(
        _moe_scatter_kernel,
        out_shape=jax.ShapeDtypeStruct((n_rows * ROW_TILE, h2.shape[1]), F32),
        grid_spec=grid_spec,
        input_output_aliases={2: 0},
        compiler_params=pltpu.CompilerParams(dimension_semantics=("arbitrary",),
                                             vmem_limit_bytes=VMEM_LIMIT),
        name="moe_scatter",
    )(dest, h2, jnp.zeros((n_rows * ROW_TILE, h2.shape[1]), F32))


def _moe_expert_kernel(te_ref, nu_ref, x_ref, wg_ref, wu_ref, wd_ref, y_ref, xs_ref):
    i = pl.program_id(0)

    @pl.when(i < nu_ref[0])
    def _():
        for s in range(ROW_TILE):
            xs_ref[:, s * 128:(s + 1) * 128] = x_ref[_token_rows(s, MOE_TMG), :].astype(BF16)
        x = xs_ref[...]
        hg = _dot(x, wg_ref[0])
        hu = _dot(x, wu_ref[0])
        a = hg * jax.nn.sigmoid(hg) * hu
        _to_token_tiles(y_ref, _dot(a.astype(BF16), wd_ref[0]))

    @pl.when(i >= nu_ref[0])
    def _():
        y_ref[...] = jnp.zeros_like(y_ref)


def _moe_experts(tile_expert, n_used, xsorted, wg, wu, wd):
    n_rows = xsorted.shape[0] // ROW_TILE
    d, ff = wg.shape[1], wg.shape[2]
    tile = (MOE_TMG * ROW_TILE, xsorted.shape[1])
    grid_spec = pltpu.PrefetchScalarGridSpec(
        num_scalar_prefetch=2,
        grid=(n_rows // MOE_TMG,),
        in_specs=[pl.BlockSpec(tile, lambda i, te, nu: (i, 0)),
                  pl.BlockSpec((1, d, ff), lambda i, te, nu: (te[i], 0, 0)),
                  pl.BlockSpec((1, d, ff), lambda i, te, nu: (te[i], 0, 0)),
                  pl.BlockSpec((1, ff, d), lambda i, te, nu: (te[i], 0, 0))],
        out_specs=pl.BlockSpec(tile, lambda i, te, nu: (i, 0)),
        scratch_shapes=[pltpu.VMEM((MOE_TMG, d), BF16)])
    return pl.pallas_call(
        _moe_expert_kernel,
        out_shape=jax.ShapeDtypeStruct(xsorted.shape, F32),
        grid_spec=grid_spec,
        compiler_params=pltpu.CompilerParams(dimension_semantics=("arbitrary",),
                                             vmem_limit_bytes=VMEM_LIMIT),
        name="moe_experts",
    )(tile_expert, n_used, xsorted, wg, wu, wd)


def _moe_combine_kernel(dest_ref, y_hbm, x_ref, info_ref, mod_ref, gfin_ref, o_ref, ybuf, x2_ref, sem):
    i = pl.program_id(0)
    nt = pl.num_programs(0)
    slot = lax.rem(i, 2)
    n_pairs = 2 * MOE_TMC

    def start_gather(tile, to_slot):
        for r in range(n_pairs):
            pltpu.make_async_copy(_token_tile(y_hbm, dest_ref[tile * n_pairs + r]),
                                  _token_tile(ybuf.at[to_slot], r), sem.at[to_slot]).start(priority=r % 2)

    @pl.when(i == 0)
    def _():
        start_gather(0, 0)

    _wait_token_copies(y_hbm, ybuf.at[slot], sem.at[slot], n_pairs)

    @pl.when(i + 1 < nt)
    def _():
        start_gather(i + 1, 1 - slot)

    w1 = info_ref[:, 4:5]
    w2 = info_ref[:, 5:6]
    sumsq = jnp.zeros((MOE_TMC, 1), F32)
    for s in range(ROW_TILE):
        cols = slice(s * 128, (s + 1) * 128)
        y = (w1 * ybuf[slot, _token_rows(s, MOE_TMC), :]
             + w2 * ybuf[slot, _token_rows(s, MOE_TMC, first_token=MOE_TMC), :])
        x2 = x_ref[:, cols] + mod_ref[0, 5:6, cols] * y
        x2_ref[:, cols] = x2
        sumsq = sumsq + jnp.sum(x2 * x2, axis=-1, keepdims=True)
    d = x2_ref.shape[1]
    o_ref[...] = x2_ref[...] * lax.rsqrt(sumsq / d + EPS) * gfin_ref[...]


def _moe_combine(dest, ysorted, x1, info, mod3, g_final, seq):
    n, d = x1.shape
    per_b = seq // MOE_TMC
    grid_spec = pltpu.PrefetchScalarGridSpec(
        num_scalar_prefetch=1,
        grid=(n // MOE_TMC,),
        in_specs=[pl.BlockSpec(memory_space=pl.ANY),
                  pl.BlockSpec((MOE_TMC, d), lambda i, ds: (i, 0)),
                  pl.BlockSpec((MOE_TMC, ROUTE_COLS), lambda i, ds: (i, 0)),
                  pl.BlockSpec((1, N_MOD, d), lambda i, ds: (i // per_b, 0, 0)),
                  pl.BlockSpec((1, d), lambda i, ds: (0, 0))],
        out_specs=pl.BlockSpec((MOE_TMC, d), lambda i, ds: (i, 0)),
        scratch_shapes=[pltpu.VMEM((2, 2 * MOE_TMC * ROW_TILE, ysorted.shape[1]), F32),
                        pltpu.VMEM((MOE_TMC, d), F32), pltpu.SemaphoreType.DMA((2,))])
    return pl.pallas_call(
        _moe_combine_kernel,
        out_shape=jax.ShapeDtypeStruct((n, d), F32),
        grid_spec=grid_spec,
        compiler_params=pltpu.CompilerParams(dimension_semantics=("arbitrary",),
                                             vmem_limit_bytes=VMEM_LIMIT),
        name="moe_combine",
    )(dest, ysorted, x1, info, mod3, g_final.reshape(1, d))


def _moe(x1, mod3, g_ffn, wr_hi, wr_lo, b_route, wg, wu, wd, g_final, seq):
    n, d = x1.shape
    h2, info, infot, counts = _moe_route(x1, mod3, g_ffn, wr_hi, wr_lo, b_route, seq)

    e1, e2, rank1, rank2 = [infot[k].astype(jnp.int32) for k in range(4)]
    cnt = counts[0, 32:32 + N_EXPERTS].astype(jnp.int32)
    padded = ((cnt + MOE_TMG - 1) // MOE_TMG) * MOE_TMG
    seg_end = jnp.cumsum(padded)
    expert_ids = jnp.arange(N_EXPERTS, dtype=jnp.int32)

    def seg_start_of(e):
        return jnp.sum(jnp.where(expert_ids[None, :] < e[:, None], padded[None, :], 0), axis=1)

    dest1 = seg_start_of(e1) + rank1
    dest2 = seg_start_of(e2) + rank2
    n_rows = 2 * n + N_EXPERTS * MOE_TMG
    tile_start = jnp.arange(n_rows // MOE_TMG, dtype=jnp.int32) * MOE_TMG
    tile_expert = jnp.minimum(jnp.sum((tile_start[:, None] >= seg_end[None, :]).astype(jnp.int32), axis=1),
                              N_EXPERTS - 1)
    n_used = (seg_end[N_EXPERTS - 1:] // MOE_TMG).astype(jnp.int32)
    dest = jnp.concatenate([dest1.reshape(-1, MOE_TMC), dest2.reshape(-1, MOE_TMC)], axis=1).reshape(-1)

    xsorted = _moe_scatter(dest, h2, n_rows)
    ysorted = _moe_experts(tile_expert, n_used, xsorted, wg, wu, wd)
    return _moe_combine(dest, ysorted, x1, info, mod3, g_final, seq)


def _layer(x3, c, w_ada, b_ada, g_mix, w_in, b_forget, g_out_a, g_out_b, w_out,
           g_ffn, w_group, b_group, w_router, b_router, w_gate, w_up, w_down, g_final):
    bsz, seq, d = x3.shape
    mod3 = _ada_mod(c, w_ada, b_ada).reshape(bsz, N_MOD, d)

    w_t = w_in.T

    def pad_heads(w, n_heads):
        w = w.reshape(d, n_heads, HEAD_DIM)
        return jnp.concatenate([w, jnp.zeros_like(w)], axis=-1).reshape(d, n_heads * AUG)

    w_ik2 = jnp.concatenate([w_in[:, 1280:1344], w_in[:, 1280:1344]], axis=1)
    w_wf_t = jnp.concatenate([w_t[1344:1352], w_t[2888:2896]], axis=0)
    weights = [w_t[0:512].astype(BF16), pad_heads(w_in[:, 512:640], A_KV_HEADS).astype(BF16),
               w_t[640:768].astype(BF16), w_t[1352:1864].astype(BF16),
               pad_heads(w_in[:, 1864:2376], B_HEADS).astype(BF16), w_t[2376:2888].astype(BF16),
               *_split_bf16(w_t[768:1280]), *_split_bf16(w_ik2), *_split_bf16(w_wf_t)]
    aqt, ak, avt, bqt, bk, bvt, iqt, ik4, wft = _in_proj(x3, mod3, g_mix, weights)

    cumt, kaug = _fox_cum(wft, b_forget, bk)
    obt = _fox_attn(bqt, kaug, bvt, cumt)

    pos = jnp.arange(seq, dtype=jnp.int32)[:, None]
    lane = jnp.arange(AUG, dtype=jnp.int32)[None, :] - HEAD_DIM
    posx = jnp.where(lane == 0, pos >> 7, jnp.where(lane == 1, pos & 127,
                     jnp.where((lane == 2) | (lane == 3), 1, 0))).astype(BF16)
    oat = _dsa_attn(iqt, ik4, wft, aqt, ak, posx, avt)

    x1 = _out_proj(oat, obt, x3, mod3, g_out_a, g_out_b, w_out.astype(BF16))

    w_r = jnp.concatenate([w_group, jnp.zeros((d, 32 - N_GROUPS), F32),
                           jnp.transpose(w_router, (1, 0, 2)).reshape(d, N_EXPERTS),
                           jnp.zeros((d, ROUTE_COLS - 64), F32)], axis=1)
    b_r = jnp.concatenate([b_group, jnp.zeros((32 - N_GROUPS,), F32), b_router.reshape(-1),
                           jnp.zeros((ROUTE_COLS - 64,), F32)]).reshape(1, ROUTE_COLS)
    wr_hi, wr_lo = _split_bf16(w_r)
    out = _moe(x1.reshape(bsz * seq, d), mod3, g_ffn, wr_hi, wr_lo, b_r, w_gate.astype(BF16),
               w_up.astype(BF16), w_down.astype(BF16), g_final, seq)
    return out.reshape(bsz, seq, d)


def kernel(x, c, w_ada, b_ada, g_mix, w_in, b_forget, g_out_a, g_out_b, w_out, g_ffn, w_group,
           b_group, w_router, b_router, w_gate, w_up, w_down, g_final):
    depth = w_ada.shape[0]
    assert depth == 1, "final norm is fused into the single layer's MoE kernel"
    return _layer(x, c, w_ada[0], b_ada[0], g_mix[0], w_in[0], b_forget[0], g_out_a[0], g_out_b[0],
                  w_out[0], g_ffn[0], w_group[0], b_group[0], w_router[0], b_router[0], w_gate[0],
                  w_up[0], w_down[0], g_final)
```

```python
import math

import jax
import jax.numpy as jnp
import numpy as np
from jax import lax
from jax.experimental import pallas as pl
from jax.experimental.pallas import tpu as pltpu

F32 = jnp.float32
BF16 = jnp.bfloat16

EPS = 1e-6
A_HEADS = 8
A_KV_HEADS = 2
HEAD_DIM = 64
IDX_HEADS = 8
IDX_DIM = 64
TOPK = 256
B_HEADS = 8
N_GROUPS = 4
EXPERTS_PER_GROUP = 8
N_EXPERTS = N_GROUPS * EXPERTS_PER_GROUP
N_MOD = 6

NEG_BIG = -1e30
LOG2E = math.log2(math.e)
Q_SCALE = HEAD_DIM ** -0.5 * LOG2E
VMEM_LIMIT = 48 * 1024 * 1024

TQ = 256
KC = 256
AUG = 128
IDX_K = 256
COARSE_STEPS = 12
FOX_LOOKAHEAD = 6
DSA_LOOKAHEAD = 4
assert TQ == KC == TOPK


def _split_bf16(x):
    hi = x.astype(BF16)
    lo = (x - hi.astype(F32)).astype(BF16)
    return hi, lo


def _split3_f32(x):
    p1 = x.astype(BF16).astype(F32)
    r1 = x - p1
    p2 = r1.astype(BF16).astype(F32)
    p3 = (r1 - p2).astype(BF16).astype(F32)
    return p1, p2, p3


def _dot(a, b):
    return jnp.dot(a, b, preferred_element_type=F32)


def _dot_nt(a, b):
    return lax.dot_general(a, b, (((1,), (1,)), ((), ())), preferred_element_type=F32)


def _dot_tn(a, b):
    return lax.dot_general(a, b, (((0,), (0,)), ((), ())), preferred_element_type=F32)


def _dot3(a_hi, a_lo, b_hi, b_lo):
    return _dot(a_hi, b_hi) + _dot(a_lo, b_hi) + _dot(a_hi, b_lo)


def _dot3_nt(a_hi, a_lo, b_hi, b_lo):
    return _dot_nt(a_hi, b_hi) + _dot_nt(a_lo, b_hi) + _dot_nt(a_hi, b_lo)


def _rms(x, g):
    return x * lax.rsqrt(jnp.mean(x * x, axis=-1, keepdims=True) + EPS) * g


def _chunk(kc):
    return pl.ds(pl.multiple_of(kc * KC, KC), KC)


def _key_minus_query():
    return (lax.broadcasted_iota(jnp.int32, (KC, TQ), 0)
            - lax.broadcasted_iota(jnp.int32, (KC, TQ), 1))


def _ada_kernel(c_ref, w_ref, b_ref, o_ref):
    c = c_ref[...]
    s = c * jax.nn.sigmoid(c)
    s_hi, s_lo = _split_bf16(s)
    w_hi, w_lo = _split_bf16(w_ref[...])
    o_ref[...] = _dot3(s_hi, s_lo, w_hi, w_lo) + b_ref[...]


def _ada_mod(c, w_ada, b_ada):
    bsz, d = c.shape
    n = w_ada.shape[1]
    tn = 1024
    return pl.pallas_call(
        _ada_kernel,
        out_shape=jax.ShapeDtypeStruct((bsz, n), F32),
        grid=(n // tn,),
        in_specs=[pl.BlockSpec((bsz, d), lambda j: (0, 0)),
                  pl.BlockSpec((d, tn), lambda j: (0, j)),
                  pl.BlockSpec((1, tn), lambda j: (0, j))],
        out_specs=pl.BlockSpec((bsz, tn), lambda j: (0, j)),
        compiler_params=pltpu.CompilerParams(dimension_semantics=("arbitrary",),
                                             vmem_limit_bytes=VMEM_LIMIT),
        name="ada_mod",
    )(c, w_ada, b_ada.reshape(1, n))


def _in_proj_kernel(x_ref, mod_ref, g_ref,
                    waq_ref, wak_ref, wav_ref, wbq_ref, wbk_ref, wbv_ref,
                    wiqh_ref, wiql_ref, wikh_ref, wikl_ref, wwfh_ref, wwfl_ref,
                    aqt_ref, ak_ref, avt_ref, bqt_ref, bk_ref, bvt_ref, iqt_ref, ik_ref, wft_ref):
    x = x_ref[0]
    h = _rms(x, g_ref[...]) * (1.0 + mod_ref[0, 1:2, :]) + mod_ref[0, 0:1, :]
    h_hi, h_lo = _split_bf16(h)
    aqt_ref[0] = (_dot_nt(waq_ref[...], h_hi) * Q_SCALE).astype(BF16)
    ak_ref[0] = _dot(h_hi, wak_ref[...]).astype(BF16)
    avt_ref[0] = _dot_nt(wav_ref[...], h_hi).astype(BF16)
    bqt_ref[0] = (_dot_nt(wbq_ref[...], h_hi) * Q_SCALE).astype(BF16)
    bk_ref[0] = _dot(h_hi, wbk_ref[...]).astype(BF16)
    bvt_ref[0] = _dot_nt(wbv_ref[...], h_hi).astype(BF16)
    iqt = _dot3_nt(wiqh_ref[...], wiql_ref[...], h_hi, h_lo)
    for hd in range(IDX_HEADS):
        q_hi, q_lo = _split_bf16(iqt[hd * IDX_DIM:(hd + 1) * IDX_DIM, :])
        iqt_ref[0, hd * IDX_K:(hd + 1) * IDX_K, :] = jnp.concatenate([q_hi, q_lo, q_hi, q_lo], axis=0)
    ik2 = _dot3(h_hi, h_lo, wikh_ref[...], wikl_ref[...])
    k_hi, k_lo = _split_bf16(ik2)
    ik_ref[0] = jnp.concatenate([k_hi, k_lo], axis=1)
    wft_ref[0] = _dot3_nt(wwfh_ref[...], wwfl_ref[...], h_hi, h_lo)


def _in_proj(x3, mod3, g_mix, weights):
    bsz, seq, d = x3.shape
    tm = 512
    blk_t = lambda b, i: (b, 0, i)
    blk_r = lambda b, i: (b, i, 0)
    const = lambda b, i: (0, 0)
    ak_w = A_KV_HEADS * AUG
    bk_w = B_HEADS * AUG
    outs = [jax.ShapeDtypeStruct((bsz, 512, seq), BF16), jax.ShapeDtypeStruct((bsz, seq, ak_w), BF16),
            jax.ShapeDtypeStruct((bsz, 128, seq), BF16), jax.ShapeDtypeStruct((bsz, 512, seq), BF16),
            jax.ShapeDtypeStruct((bsz, seq, bk_w), BF16), jax.ShapeDtypeStruct((bsz, 512, seq), BF16),
            jax.ShapeDtypeStruct((bsz, IDX_HEADS * IDX_K, seq), BF16),
            jax.ShapeDtypeStruct((bsz, seq, 256), BF16), jax.ShapeDtypeStruct((bsz, 16, seq), F32)]
    out_specs = [pl.BlockSpec((1, 512, tm), blk_t), pl.BlockSpec((1, tm, ak_w), blk_r),
                 pl.BlockSpec((1, 128, tm), blk_t), pl.BlockSpec((1, 512, tm), blk_t),
                 pl.BlockSpec((1, tm, bk_w), blk_r), pl.BlockSpec((1, 512, tm), blk_t),
                 pl.BlockSpec((1, IDX_HEADS * IDX_K, tm), blk_t),
                 pl.BlockSpec((1, tm, 256), blk_r), pl.BlockSpec((1, 16, tm), blk_t)]
    return pl.pallas_call(
        _in_proj_kernel,
        out_shape=outs,
        grid=(bsz, seq // tm),
        in_specs=[pl.BlockSpec((1, tm, d), blk_r),
                  pl.BlockSpec((1, N_MOD, d), lambda b, i: (b, 0, 0)),
                  pl.BlockSpec((1, d), const)] + [pl.BlockSpec(w.shape, const) for w in weights],
        out_specs=out_specs,
        compiler_params=pltpu.CompilerParams(dimension_semantics=("arbitrary", "arbitrary"),
                                             vmem_limit_bytes=VMEM_LIMIT),
        name="in_proj",
    )(x3, mod3, g_mix.reshape(1, d), *weights)


CB = 256


def _cum_kernel(wft_ref, bfor_ref, k_ref, cumt_ref, kaug_ref):
    seq = wft_ref.shape[2]
    r = lax.broadcasted_iota(jnp.int32, (CB, CB), 0)
    cidx = lax.broadcasted_iota(jnp.int32, (CB, CB), 1)
    tri = jnp.where(r <= cidx, 1.0, 0.0).astype(BF16)
    row128 = lax.broadcasted_iota(jnp.int32, (AUG, CB), 0)
    ones_rows = jnp.where((row128 >= HEAD_DIM + 3) & (row128 < HEAD_DIM + 6), 1.0, 0.0)
    carry = jnp.zeros((8, 1), F32)
    for blk in range(seq // CB):
        cols = slice(blk * CB, (blk + 1) * CB)
        z = wft_ref[0, 8:16, cols] + bfor_ref[...]
        logf = jnp.minimum(z, 0.0) - jnp.log(1.0 + jnp.exp(-jnp.abs(z)))
        p1, p2, p3 = _split3_f32(logf)
        pieces = jnp.concatenate([p1, p2, p3, jnp.zeros_like(p1)], axis=0).astype(BF16)
        parts = _dot(pieces, tri)
        cum = parts[0:8] + parts[8:16] + parts[16:24] + carry
        carry = cum[:, CB - 1:CB]
        cum2 = cum * LOG2E
        cumt_ref[0, :, cols] = cum2
        c1, c2, c3 = _split3_f32(cum2)
        for h in range(B_HEADS):
            spare = jnp.where(row128 == HEAD_DIM, -c1[h:h + 1], ones_rows)
            spare = jnp.where(row128 == HEAD_DIM + 1, -c2[h:h + 1], spare)
            spare = jnp.where(row128 == HEAD_DIM + 2, -c3[h:h + 1], spare)
            lanes = slice(h * AUG, (h + 1) * AUG)
            kaug_ref[0, cols, lanes] = k_ref[0, cols, lanes] + spare.T.astype(BF16)


def _fox_cum(wft, b_forget, bk):
    bsz, _, seq = wft.shape
    nh = B_HEADS
    kw = bk.shape[-1]
    return pl.pallas_call(
        _cum_kernel,
        out_shape=[jax.ShapeDtypeStruct((bsz, nh, seq), F32), jax.ShapeDtypeStruct((bsz, seq, kw), BF16)],
        grid=(bsz,),
        in_specs=[pl.BlockSpec((1, 16, seq), lambda b: (b, 0, 0)),
                  pl.BlockSpec((nh, 1), lambda b: (0, 0)),
                  pl.BlockSpec((1, seq, kw), lambda b: (b, 0, 0))],
        out_specs=[pl.BlockSpec((1, nh, seq), lambda b: (b, 0, 0)),
                   pl.BlockSpec((1, seq, kw), lambda b: (b, 0, 0))],
        compiler_params=pltpu.CompilerParams(dimension_semantics=("arbitrary",),
                                             vmem_limit_bytes=VMEM_LIMIT),
        name="fox_cum",
    )(wft, b_forget.reshape(nh, 1), bk)


def _softmax_init(m_ref, l_ref, acc_ref):
    m_ref[...] = jnp.full(m_ref.shape, NEG_BIG, F32)
    l_ref[...] = jnp.zeros(l_ref.shape, F32)
    acc_ref[...] = jnp.zeros(acc_ref.shape, F32)


def _attend_chunks(items, score_fn, vt_fn, m_ref, l_ref, acc_ref, lookahead):
    ahead = min(lookahead, len(items))
    scores = {i: score_fn(items[i]) for i in range(ahead)}
    for i, item in enumerate(items):
        if i + ahead < len(items):
            scores[i + ahead] = score_fn(items[i + ahead])
        s = scores.pop(i)
        h = item[1]
        m_old = m_ref[h]
        m_new = jnp.maximum(m_old, jnp.max(s, axis=0, keepdims=True))
        alpha = jnp.exp2(m_old - m_new)
        p = jnp.exp2(s - m_new)
        l_ref[h] = alpha * l_ref[h] + jnp.sum(p, axis=0, keepdims=True)
        m_ref[h] = m_new
        rows = slice(h * HEAD_DIM, (h + 1) * HEAD_DIM)
        acc_ref[rows, :] = alpha * acc_ref[rows, :] + _dot(vt_fn(item), p.astype(BF16))


def _softmax_finish(o_ref, n_heads, l_ref, acc_ref):
    for h in range(n_heads):
        rows = slice(h * HEAD_DIM, (h + 1) * HEAD_DIM)
        o_ref[0, rows, :] = acc_ref[rows, :] / l_ref[h]


def _fox_kernel(qt_ref, k_ref, vt_ref, cumt_ref, o_ref, w_ref, m_ref, l_ref, acc_ref):
    qi = pl.program_id(1)
    _softmax_init(m_ref, l_ref, acc_ref)

    row64 = lax.broadcasted_iota(jnp.int32, (AUG - HEAD_DIM, TQ), 0)
    for h in range(B_HEADS):
        c1, c2, c3 = _split3_f32(cumt_ref[0, h:h + 1, :])
        spare = jnp.where(row64 < 3, 1.0, 0.0)
        spare = jnp.where(row64 == 3, c1, spare)
        spare = jnp.where(row64 == 4, c2, spare)
        spare = jnp.where(row64 == 5, c3, spare)
        w_ref[h] = jnp.concatenate([qt_ref[0, h * HEAD_DIM:(h + 1) * HEAD_DIM, :],
                                    spare.astype(BF16)], axis=0)

    def tiles(chunks):
        rows = [_chunk(kc) for kc, _ in chunks]

        def score_fn(item):
            j, h = item
            s = _dot(k_ref[0, rows[j], h * AUG:(h + 1) * AUG], w_ref[h])
            if chunks[j][1]:
                s = jnp.where(_key_minus_query() <= 0, s, NEG_BIG)
            return s

        def vt_fn(item):
            j, h = item
            return vt_ref[0, h * HEAD_DIM:(h + 1) * HEAD_DIM, rows[j]]

        items = [(j, h) for j in range(len(chunks)) for h in range(B_HEADS)]
        _attend_chunks(items, score_fn, vt_fn, m_ref, l_ref, acc_ref, FOX_LOOKAHEAD)

    def body(pair, _):
        tiles([(2 * pair, False), (2 * pair + 1, False)])
        return 0

    lax.fori_loop(0, qi // 2, body, 0)

    @pl.when(qi % 2 == 1)
    def _():
        tiles([(qi - 1, False), (qi, True)])

    @pl.when(qi % 2 == 0)
    def _():
        tiles([(qi, True)])

    _softmax_finish(o_ref, B_HEADS, l_ref, acc_ref)


def _fox_attn(bqt, kaug, bvt, cumt):
    bsz, w, seq = bqt.shape
    blk_t = lambda b, i: (b, 0, i)
    full = lambda b, i: (b, 0, 0)
    return pl.pallas_call(
        _fox_kernel,
        out_shape=jax.ShapeDtypeStruct((bsz, w, seq), F32),
        grid=(bsz, seq // TQ),
        in_specs=[pl.BlockSpec((1, w, TQ), blk_t),
                  pl.BlockSpec((1, seq, B_HEADS * AUG), full),
                  pl.BlockSpec((1, w, seq), full),
                  pl.BlockSpec((1, B_HEADS, TQ), blk_t)],
        out_specs=pl.BlockSpec((1, w, TQ), blk_t),
        scratch_shapes=[pltpu.VMEM((B_HEADS, AUG, TQ), BF16), pltpu.VMEM((B_HEADS, 1, TQ), F32),
                        pltpu.VMEM((B_HEADS, 1, TQ), F32), pltpu.VMEM((B_HEADS * HEAD_DIM, TQ), F32)],
        compiler_params=pltpu.CompilerParams(dimension_semantics=("arbitrary", "arbitrary"),
                                             vmem_limit_bytes=VMEM_LIMIT),
        name="fox_attn",
    )(bqt, kaug, bvt, cumt)


def _bf16_pieces(value):
    pieces = []
    rest = np.float32(value)
    for _ in range(3):
        piece = np.asarray(rest).astype(BF16).astype(np.float32)
        pieces.append(float(piece))
        rest = np.float32(rest - piece)
    return pieces


def _dsa_kernel(iqt_ref, ik_ref, wft_ref, qt_ref, k_ref, posx_ref, vt_ref, o_ref,
                s_ref, w_ref, m_ref, l_ref, acc_ref):
    qi = pl.program_id(1)
    nch = qi + 1
    kmq = _key_minus_query()

    def score_body(kc, _):
        ik = ik_ref[0, _chunk(kc), :]
        acc = jnp.zeros((KC, TQ), F32)
        for h in range(IDX_HEADS):
            d = _dot(ik, iqt_ref[0, h * IDX_K:(h + 1) * IDX_K, :])
            acc = acc + wft_ref[0, h:h + 1, :] * jnp.maximum(d, 0.0)
        acc = jnp.where(acc == 0.0, 0.0, acc)
        causal = kmq <= (qi - kc) * KC
        s_ref[_chunk(kc), :] = jnp.where(causal, acc, -jnp.inf)
        return 0

    lax.fori_loop(0, nch, score_body, 0)

    @pl.when(qi == 0)
    def _():
        s_ref[0:KC, :] = jnp.where(kmq <= 0, 0.0, NEG_BIG)

    @pl.when(qi > 0)
    def _():
        def scan(fn, init):
            return lax.fori_loop(0, nch, lambda kc, c: fn(s_ref[_chunk(kc), :], c), init)

        def cmin(x):
            return jnp.min(x, axis=0, keepdims=True)

        def cmax(x):
            return jnp.max(x, axis=0, keepdims=True)

        def csum(x):
            return jnp.sum(x, axis=0, keepdims=True)

        zeros = jnp.zeros((1, TQ), F32)
        pinf = jnp.full((1, TQ), jnp.inf, F32)
        ninf = jnp.full((1, TQ), -jnp.inf, F32)

        def init_fn(s, c):
            lo, hi = c
            lo = jnp.minimum(lo, cmin(jnp.where(s > -jnp.inf, s, jnp.inf)))
            hi = jnp.maximum(hi, cmax(s))
            return lo, hi

        lo, hi = scan(init_fn, (pinf, ninf))

        def coarse_step(_, carry):
            lo, hi = carry
            mid = lo + (hi - lo) * 0.5
            cnt = scan(lambda s, c: c + csum(jnp.where(s >= mid, 1.0, 0.0)), zeros)
            enough = cnt >= float(TOPK)
            return jnp.where(enough, mid, lo), jnp.where(enough, hi, mid)

        lo, hi = lax.fori_loop(0, COARSE_STEPS, coarse_step, (lo, hi))

        def cond(carry):
            return carry[2] > 0

        def step(carry):
            lo, hi, _ = carry
            mid = lo + (hi - lo) * 0.5
            mid = jnp.where(mid <= lo, hi, mid)

            def fn(s, c):
                cnt, a, b = c
                ge = s >= mid
                cnt = cnt + csum(jnp.where(ge, 1.0, 0.0))
                b = jnp.minimum(b, cmin(jnp.where(ge, s, jnp.inf)))
                a = jnp.maximum(a, cmax(jnp.where(ge, -jnp.inf, s)))
                return cnt, a, b

            cnt, a, b = scan(fn, (zeros, ninf, pinf))
            enough = cnt >= float(TOPK)
            new_lo = jnp.where(enough, b, lo)
            new_hi = jnp.where(enough, jnp.where(cnt == float(TOPK), b, hi), a)
            active = jnp.max(jnp.where(new_lo < new_hi, 1, 0))
            return new_lo, new_hi, active

        first_active = jnp.max(jnp.where(lo < hi, 1, 0))
        thr, _, _ = lax.while_loop(cond, step, (lo, hi, first_active))

        n_gt = scan(lambda s, c: c + csum(jnp.where(s > thr, 1.0, 0.0)), zeros)
        need = float(TOPK) - n_gt
        lower = jnp.where(lax.broadcasted_iota(jnp.int32, (KC, KC), 1)
                          < lax.broadcasted_iota(jnp.int32, (KC, KC), 0), 1.0, 0.0).astype(BF16)

        def sel_body(kc, run):
            s = s_ref[_chunk(kc), :]
            eq = s == thr
            eqf = jnp.where(eq, 1.0, 0.0)
            before = _dot(lower, eqf.astype(BF16)) + run
            sel = (s > thr) | (eq & (before < need))
            s_ref[_chunk(kc), :] = jnp.where(sel, 0.0, NEG_BIG)
            return run + csum(eqf)

        lax.fori_loop(0, nch, sel_body, zeros)

    _softmax_init(m_ref, l_ref, acc_ref)
    rep = A_HEADS // A_KV_HEADS
    row64 = lax.broadcasted_iota(jnp.int32, (AUG - HEAD_DIM, TQ), 0)
    qpos = (qi * TQ + lax.broadcasted_iota(jnp.int32, (AUG - HEAD_DIM, TQ), 1)).astype(F32)
    for h in range(A_HEADS):
        slope = np.float32(2.0 ** (-8.0 * (h + 1) / A_HEADS) * LOG2E)
        u1, u2, u3 = _split3_f32(-slope * qpos)
        spare = jnp.zeros((AUG - HEAD_DIM, TQ), F32)
        for p, s_p in enumerate(_bf16_pieces(slope)):
            spare = jnp.where(row64 == p, 128.0 * s_p, spare)
            spare = jnp.where(row64 == 3 + p, s_p, spare)
        spare = jnp.where(row64 == 6, u1, spare)
        spare = jnp.where(row64 == 7, u2, spare)
        spare = jnp.where(row64 == 8, u3, spare)
        w_ref[h] = jnp.concatenate([qt_ref[0, h * HEAD_DIM:(h + 1) * HEAD_DIM, :],
                                    spare.astype(BF16)], axis=0)

    def attn_tiles(chunks):
        rows = [_chunk(kc) for kc in chunks]
        lhs = [[k_ref[0, r, g * AUG:(g + 1) * AUG] + posx_ref[r, :] for g in range(A_KV_HEADS)]
               for r in rows]

        def score_fn(item):
            j, h = item
            return _dot(lhs[j][h // rep], w_ref[h]) + s_ref[rows[j], :]

        def vt_fn(item):
            j, h = item
            g = h // rep
            return vt_ref[0, g * HEAD_DIM:(g + 1) * HEAD_DIM, rows[j]]

        items = [(j, h) for j in range(len(chunks)) for h in range(A_HEADS)]
        _attend_chunks(items, score_fn, vt_fn, m_ref, l_ref, acc_ref, DSA_LOOKAHEAD)

    def attn_body(pair, _):
        attn_tiles([2 * pair, 2 * pair + 1])
        return 0

    lax.fori_loop(0, nch // 2, attn_body, 0)

    @pl.when(nch % 2 == 1)
    def _():
        attn_tiles([nch - 1])

    _softmax_finish(o_ref, A_HEADS, l_ref, acc_ref)


def _dsa_attn(iqt, ik4, wft, aqt, ak, posx, avt):
    bsz, w, seq = aqt.shape
    blk_t = lambda b, i: (b, 0, i)
    full = lambda b, i: (b, 0, 0)
    return pl.pallas_call(
        _dsa_kernel,
        out_shape=jax.ShapeDtypeStruct((bsz, w, seq), F32),
        grid=(bsz, seq // TQ),
        in_specs=[pl.BlockSpec((1, IDX_HEADS * IDX_K, TQ), blk_t),
                  pl.BlockSpec((1, seq, 256), full),
                  pl.BlockSpec((1, 16, TQ), blk_t),
                  pl.BlockSpec((1, w, TQ), blk_t),
                  pl.BlockSpec((1, seq, A_KV_HEADS * AUG), full),
                  pl.BlockSpec((seq, AUG), lambda b, i: (0, 0)),
                  pl.BlockSpec((1, 128, seq), full)],
        out_specs=pl.BlockSpec((1, w, TQ), blk_t),
        scratch_shapes=[pltpu.VMEM((seq, TQ), F32), pltpu.VMEM((A_HEADS, AUG, TQ), BF16),
                        pltpu.VMEM((A_HEADS, 1, TQ), F32), pltpu.VMEM((A_HEADS, 1, TQ), F32),
                        pltpu.VMEM((A_HEADS * HEAD_DIM, TQ), F32)],
        compiler_params=pltpu.CompilerParams(dimension_semantics=("arbitrary", "arbitrary"),
                                             vmem_limit_bytes=VMEM_LIMIT),
        name="dsa_attn",
    )(iqt, ik4, wft, aqt, ak, posx, avt)


def _rms_cols(xt, g_col):
    return xt * lax.rsqrt(jnp.mean(xt * xt, axis=0, keepdims=True) + EPS) * g_col


def _out_proj_kernel(oat_ref, obt_ref, x_ref, mod_ref, ga_ref, gb_ref, w_ref, o_ref):
    oa = _rms_cols(oat_ref[0], ga_ref[...]).astype(BF16)
    ob = _rms_cols(obt_ref[0], gb_ref[...]).astype(BF16)
    y = _dot_tn(oa, w_ref[0:512, :]) + _dot_tn(ob, w_ref[512:1024, :])
    o_ref[0] = x_ref[0] + mod_ref[0, 2:3, :] * y


def _out_proj(oat, obt, x3, mod3, g_out_a, g_out_b, w_out_bf):
    bsz, seq, d = x3.shape
    tm = 512
    blk_t = lambda b, i: (b, 0, i)
    blk_r = lambda b, i: (b, i, 0)
    const = lambda b, i: (0, 0)
    return pl.pallas_call(
        _out_proj_kernel,
        out_shape=jax.ShapeDtypeStruct((bsz, seq, d), F32),
        grid=(bsz, seq // tm),
        in_specs=[pl.BlockSpec((1, 512, tm), blk_t), pl.BlockSpec((1, 512, tm), blk_t),
                  pl.BlockSpec((1, tm, d), blk_r),
                  pl.BlockSpec((1, N_MOD, d), lambda b, i: (b, 0, 0)),
                  pl.BlockSpec((512, 1), const), pl.BlockSpec((512, 1), const),
                  pl.BlockSpec((d, d), const)],
        out_specs=pl.BlockSpec((1, tm, d), blk_r),
        compiler_params=pltpu.CompilerParams(dimension_semantics=("arbitrary", "arbitrary"),
                                             vmem_limit_bytes=VMEM_LIMIT),
        name="out_proj",
    )(oat, obt, x3, mod3, g_out_a.reshape(-1, 1), g_out_b.reshape(-1, 1), w_out_bf)


MOE_TMG = 256
MOE_TMC = 256
ROW_TILE = 8
ROUTE_COLS = 128


def _first_index_of_max(vals, lane, big):
    m = jnp.max(vals, axis=-1, keepdims=True)
    idx = jnp.min(jnp.where(vals == m, lane, big), axis=-1, keepdims=True)
    return m, idx


def _route(h, wr_hi_ref, wr_lo_ref, br_ref):
    h_hi, h_lo = _split_bf16(h)
    logits = _dot3(h_hi, h_lo, wr_hi_ref[...], wr_lo_ref[...]) + br_ref[...]
    lane = lax.broadcasted_iota(jnp.int32, logits.shape, 1)
    ninf = -jnp.inf
    gl = jnp.where(lane < N_GROUPS, logits, ninf)
    gmax, gsel = _first_index_of_max(gl, lane, 1 << 20)
    p_group = 1.0 / jnp.sum(jnp.exp(gl - gmax), axis=-1, keepdims=True)
    base = 32 + gsel * EXPERTS_PER_GROUP
    el = jnp.where((lane >= base) & (lane < base + EXPERTS_PER_GROUP), logits, ninf)
    v1, i1 = _first_index_of_max(el, lane, 1 << 20)
    el2 = jnp.where(lane == i1, ninf, el)
    v2, i2 = _first_index_of_max(el2, lane, 1 << 20)
    e2 = jnp.exp(v2 - v1)
    w1 = 1.0 / (1.0 + e2)
    w2 = e2 / (1.0 + e2)
    return i1, i2, w1 * p_group, w2 * p_group


def _token_rows(s, n_tokens, first_token=0):
    return pl.ds(first_token * ROW_TILE + s, n_tokens, stride=ROW_TILE)


def _to_token_tiles(ref, x):
    for s in range(ROW_TILE):
        ref[_token_rows(s, x.shape[0]), :] = x[:, s * 128:(s + 1) * 128]


def _token_tile(ref, t):
    start = t * ROW_TILE if isinstance(t, int) else pl.multiple_of(t * ROW_TILE, ROW_TILE)
    return ref.at[pl.ds(start, ROW_TILE), :]


def _moe_route_kernel(x_ref, mod_ref, gf_ref, wr_hi_ref, wr_lo_ref, br_ref,
                      h_ref, info_ref, infot_ref, cnt_ref, run_ref):
    @pl.when(pl.program_id(0) == 0)
    def _():
        run_ref[...] = jnp.zeros_like(run_ref)

    h = _rms(x_ref[...], gf_ref[...]) * (1.0 + mod_ref[0, 4:5, :]) + mod_ref[0, 3:4, :]
    _to_token_tiles(h_ref, h)
    i1, i2, w1, w2 = _route(h, wr_hi_ref, wr_lo_ref, br_ref)
    tm = h.shape[0]
    lane = lax.broadcasted_iota(jnp.int32, (tm, ROUTE_COLS), 1)
    picked = jnp.where((lane == i1) | (lane == i2), 1.0, 0.0)
    earlier = jnp.where(lax.broadcasted_iota(jnp.int32, (tm, tm), 1)
                        < lax.broadcasted_iota(jnp.int32, (tm, tm), 0), 1.0, 0.0).astype(BF16)
    before = _dot(earlier, picked.astype(BF16)) + run_ref[...]
    rank1 = jnp.sum(jnp.where(lane == i1, before, 0.0), axis=-1, keepdims=True)
    rank2 = jnp.sum(jnp.where(lane == i2, before, 0.0), axis=-1, keepdims=True)
    run_ref[...] += jnp.sum(picked, axis=0, keepdims=True)
    cnt_ref[...] = run_ref[...]
    info = jnp.where(lane == 0, (i1 - 32).astype(F32), 0.0)
    info = jnp.where(lane == 1, (i2 - 32).astype(F32), info)
    info = jnp.where(lane == 2, rank1, info)
    info = jnp.where(lane == 3, rank2, info)
    info = jnp.where(lane == 4, w1, info)
    info = jnp.where(lane == 5, w2, info)
    info_ref[...] = info
    infot_ref[...] = info.T[0:8, :]


def _moe_route(x1, mod3, g_ffn, wr_hi, wr_lo, b_route, seq):
    n, d = x1.shape
    tm = 512
    per_b = seq // tm
    row = lambda i: (i, 0)
    const = lambda i: (0, 0)
    return pl.pallas_call(
        _moe_route_kernel,
        out_shape=[jax.ShapeDtypeStruct((n * ROW_TILE, d // ROW_TILE), F32),
                   jax.ShapeDtypeStruct((n, ROUTE_COLS), F32), jax.ShapeDtypeStruct((8, n), F32),
                   jax.ShapeDtypeStruct((1, ROUTE_COLS), F32)],
        grid=(n // tm,),
        in_specs=[pl.BlockSpec((tm, d), row),
                  pl.BlockSpec((1, N_MOD, d), lambda i: (i // per_b, 0, 0)),
                  pl.BlockSpec((1, d), const),
                  pl.BlockSpec((d, ROUTE_COLS), const), pl.BlockSpec((d, ROUTE_COLS), const),
                  pl.BlockSpec((1, ROUTE_COLS), const)],
        out_specs=[pl.BlockSpec((tm * ROW_TILE, d // ROW_TILE), row),
                   pl.BlockSpec((tm, ROUTE_COLS), row), pl.BlockSpec((8, tm), lambda i: (0, i)),
                   pl.BlockSpec((1, ROUTE_COLS), const)],
        scratch_shapes=[pltpu.VMEM((1, ROUTE_COLS), F32)],
        compiler_params=pltpu.CompilerParams(dimension_semantics=("arbitrary",),
                                             vmem_limit_bytes=VMEM_LIMIT),
        name="moe_route",
    )(x1, mod3, g_ffn.reshape(1, d), wr_hi, wr_lo, b_route)


def _wait_token_copies(src_hbm, dst, sem, n_tokens):
    pltpu.make_async_copy(src_hbm.at[pl.ds(0, n_tokens * ROW_TILE), :], dst, sem).wait()


def _moe_scatter_kernel(dest_ref, h_ref, zeros_hbm, xs_hbm, sem):
    del zeros_hbm
    i = pl.program_id(0)
    n_pairs = 2 * MOE_TMC
    for r in range(MOE_TMC):
        for j in range(2):
            pltpu.make_async_copy(_token_tile(h_ref, r),
                                  _token_tile(xs_hbm, dest_ref[i * n_pairs + j * MOE_TMC + r]),
                                  sem).start(priority=j)
    _wait_token_copies(xs_hbm, xs_hbm.at[pl.ds(0, n_pairs * ROW_TILE), :], sem, n_pairs)


def _moe_scatter(dest, h2, n_rows):
    n = h2.shape[0] // ROW_TILE
    grid_spec = pltpu.PrefetchScalarGridSpec(
        num_scalar_prefetch=1,
        grid=(n // MOE_TMC,),
        in_specs=[pl.BlockSpec((MOE_TMC * ROW_TILE, h2.shape[1]), lambda i, ds: (i, 0)),
                  pl.BlockSpec(memory_space=pl.ANY)],
        out_specs=pl.BlockSpec(memory_space=pl.ANY),
        scratch_shapes=[pltpu.SemaphoreType.DMA(())])
    return pl.pallas_call(
        _moe_scatter_kernel,
        out_shape=jax.ShapeDtypeStruct((n_rows * ROW_TILE, h2.shape[1]), F32),
        grid_spec=grid_spec,
        input_output_aliases={2: 0},
        compiler_params=pltpu.CompilerParams(dimension_semantics=("arbitrary",),
                                             vmem_limit_bytes=VMEM_LIMIT),
        name="moe_scatter",
    )(dest, h2, jnp.zeros((n_rows * ROW_TILE, h2.shape[1]), F32))


def _moe_expert_kernel(te_ref, nu_ref, x_ref, wg_ref, wu_ref, wd_ref, y_ref,
                       xs_ref, wgb_ref, wub_ref, wdb_ref):
    i = pl.program_id(0)
    used = i < nu_ref[0]
    new_expert = jnp.logical_or(i == 0, te_ref[i] != te_ref[jnp.maximum(i - 1, 0)])

    @pl.when(jnp.logical_and(used, new_expert))
    def _():
        wgb_ref[...] = wg_ref[0].astype(BF16)
        wub_ref[...] = wu_ref[0].astype(BF16)
        wdb_ref[...] = wd_ref[0].astype(BF16)

    @pl.when(used)
    def _():
        for s in range(ROW_TILE):
            xs_ref[:, s * 128:(s + 1) * 128] = x_ref[_token_rows(s, MOE_TMG), :].astype(BF16)
        x = xs_ref[...]
        hg = _dot(x, wgb_ref[...])
        hu = _dot(x, wub_ref[...])
        a = hg * jax.nn.sigmoid(hg) * hu
        _to_token_tiles(y_ref, _dot(a.astype(BF16), wdb_ref[...]))

    @pl.when(i >= nu_ref[0])
    def _():
        y_ref[...] = jnp.zeros_like(y_ref)


def _moe_experts(tile_expert, n_used, xsorted, wg, wu, wd):
    n_rows = xsorted.shape[0] // ROW_TILE
    d, ff = wg.shape[1], wg.shape[2]
    tile = (MOE_TMG * ROW_TILE, xsorted.shape[1])
    grid_spec = pltpu.PrefetchScalarGridSpec(
        num_scalar_prefetch=2,
        grid=(n_rows // MOE_TMG,),
        in_specs=[pl.BlockSpec(tile, lambda i, te, nu: (i, 0)),
                  pl.BlockSpec((1, d, ff), lambda i, te, nu: (te[i], 0, 0)),
                  pl.BlockSpec((1, d, ff), lambda i, te, nu: (te[i], 0, 0)),
                  pl.BlockSpec((1, ff, d), lambda i, te, nu: (te[i], 0, 0))],
        out_specs=pl.BlockSpec(tile, lambda i, te, nu: (i, 0)),
        scratch_shapes=[pltpu.VMEM((MOE_TMG, d), BF16), pltpu.VMEM((d, ff), BF16),
                        pltpu.VMEM((d, ff), BF16), pltpu.VMEM((ff, d), BF16)])
    return pl.pallas_call(
        _moe_expert_kernel,
        out_shape=jax.ShapeDtypeStruct(xsorted.shape, F32),
        grid_spec=grid_spec,
        compiler_params=pltpu.CompilerParams(dimension_semantics=("arbitrary",),
                                             vmem_limit_bytes=VMEM_LIMIT),
        name="moe_experts",
    )(tile_expert, n_used, xsorted, wg, wu, wd)


def _moe_combine_kernel(dest_ref, y_hbm, x_ref, info_ref, mod_ref, gfin_ref, o_ref, ybuf, x2_ref, sem):
    i = pl.program_id(0)
    nt = pl.num_programs(0)
    slot = lax.rem(i, 2)
    n_pairs = 2 * MOE_TMC

    def start_gather(tile, to_slot):
        for r in range(n_pairs):
            pltpu.make_async_copy(_token_tile(y_hbm, dest_ref[tile * n_pairs + r]),
                                  _token_tile(ybuf.at[to_slot], r), sem.at[to_slot]).start(priority=r % 2)

    @pl.when(i == 0)
    def _():
        start_gather(0, 0)

    _wait_token_copies(y_hbm, ybuf.at[slot], sem.at[slot], n_pairs)

    @pl.when(i + 1 < nt)
    def _():
        start_gather(i + 1, 1 - slot)

    w1 = info_ref[:, 4:5]
    w2 = info_ref[:, 5:6]
    sumsq = jnp.zeros((MOE_TMC, 1), F32)
    for s in range(ROW_TILE):
        cols = slice(s * 128, (s + 1) * 128)
        y = (w1 * ybuf[slot, _token_rows(s, MOE_TMC), :]
             + w2 * ybuf[slot, _token_rows(s, MOE_TMC, first_token=MOE_TMC), :])
        x2 = x_ref[:, cols] + mod_ref[0, 5:6, cols] * y
        x2_ref[:, cols] = x2
        sumsq = sumsq + jnp.sum(x2 * x2, axis=-1, keepdims=True)
    d = x2_ref.shape[1]
    o_ref[...] = x2_ref[...] * lax.rsqrt(sumsq / d + EPS) * gfin_ref[...]


def _moe_combine(dest, ysorted, x1, info, mod3, g_final, seq):
    n, d = x1.shape
    per_b = seq // MOE_TMC
    grid_spec = pltpu.PrefetchScalarGridSpec(
        num_scalar_prefetch=1,
        grid=(n // MOE_TMC,),
        in_specs=[pl.BlockSpec(memory_space=pl.ANY),
                  pl.BlockSpec((MOE_TMC, d), lambda i, ds: (i, 0)),
                  pl.BlockSpec((MOE_TMC, ROUTE_COLS), lambda i, ds: (i, 0)),
                  pl.BlockSpec((1, N_MOD, d), lambda i, ds: (i // per_b, 0, 0)),
                  pl.BlockSpec((1, d), lambda i, ds: (0, 0))],
        out_specs=pl.BlockSpec((MOE_TMC, d), lambda i, ds: (i, 0)),
        scratch_shapes=[pltpu.VMEM((2, 2 * MOE_TMC * ROW_TILE, ysorted.shape[1]), F32),
                        pltpu.VMEM((MOE_TMC, d), F32), pltpu.SemaphoreType.DMA((2,))])
    return pl.pallas_call(
        _moe_combine_kernel,
        out_shape=jax.ShapeDtypeStruct((n, d), F32),
        grid_spec=grid_spec,
        compiler_params=pltpu.CompilerParams(dimension_semantics=("arbitrary",),
                                             vmem_limit_bytes=VMEM_LIMIT),
        name="moe_combine",
    )(dest, ysorted, x1, info, mod3, g_final.reshape(1, d))


def _moe(x1, mod3, g_ffn, wr_hi, wr_lo, b_route, wg, wu, wd, g_final, seq):
    n, d = x1.shape
    h2, info, infot, counts = _moe_route(x1, mod3, g_ffn, wr_hi, wr_lo, b_route, seq)

    e1, e2, rank1, rank2 = [infot[k].astype(jnp.int32) for k in range(4)]
    cnt = counts[0, 32:32 + N_EXPERTS].astype(jnp.int32)
    padded = ((cnt + MOE_TMG - 1) // MOE_TMG) * MOE_TMG
    seg_end = jnp.cumsum(padded)
    expert_ids = jnp.arange(N_EXPERTS, dtype=jnp.int32)

    def seg_start_of(e):
        return jnp.sum(jnp.where(expert_ids[None, :] < e[:, None], padded[None, :], 0), axis=1)

    dest1 = seg_start_of(e1) + rank1
    dest2 = seg_start_of(e2) + rank2
    n_rows = 2 * n + N_EXPERTS * MOE_TMG
    tile_start = jnp.arange(n_rows // MOE_TMG, dtype=jnp.int32) * MOE_TMG
    tile_expert = jnp.minimum(jnp.sum((tile_start[:, None] >= seg_end[None, :]).astype(jnp.int32), axis=1),
                              N_EXPERTS - 1)
    n_used = (seg_end[N_EXPERTS - 1:] // MOE_TMG).astype(jnp.int32)
    dest = jnp.concatenate([dest1.reshape(-1, MOE_TMC), dest2.reshape(-1, MOE_TMC)], axis=1).reshape(-1)

    xsorted = _moe_scatter(dest, h2, n_rows)
    ysorted = _moe_experts(tile_expert, n_used, xsorted, wg, wu, wd)
    return _moe_combine(dest, ysorted, x1, info, mod3, g_final, seq)


def _layer(x3, c, w_ada, b_ada, g_mix, w_in, b_forget, g_out_a, g_out_b, w_out,
           g_ffn, w_group, b_group, w_router, b_router, w_gate, w_up, w_down, g_final):
    bsz, seq, d = x3.shape
    mod3 = _ada_mod(c, w_ada, b_ada).reshape(bsz, N_MOD, d)

    w_t = w_in.T

    def pad_heads(w, n_heads):
        w = w.reshape(d, n_heads, HEAD_DIM)
        return jnp.concatenate([w, jnp.zeros_like(w)], axis=-1).reshape(d, n_heads * AUG)

    w_ik2 = jnp.concatenate([w_in[:, 1280:1344], w_in[:, 1280:1344]], axis=1)
    w_wf_t = jnp.concatenate([w_t[1344:1352], w_t[2888:2896]], axis=0)
    weights = [w_t[0:512].astype(BF16), pad_heads(w_in[:, 512:640], A_KV_HEADS).astype(BF16),
               w_t[640:768].astype(BF16), w_t[1352:1864].astype(BF16),
               pad_heads(w_in[:, 1864:2376], B_HEADS).astype(BF16), w_t[2376:2888].astype(BF16),
               *_split_bf16(w_t[768:1280]), *_split_bf16(w_ik2), *_split_bf16(w_wf_t)]
    aqt, ak, avt, bqt, bk, bvt, iqt, ik4, wft = _in_proj(x3, mod3, g_mix, weights)

    cumt, kaug = _fox_cum(wft, b_forget, bk)
    obt = _fox_attn(bqt, kaug, bvt, cumt)

    pos = jnp.arange(seq, dtype=jnp.int32)[:, None]
    lane = jnp.arange(AUG, dtype=jnp.int32)[None, :] - HEAD_DIM
    posx = jnp.where((lane >= 0) & (lane < 3), pos >> 7,
                     jnp.where((lane >= 3) & (lane < 6), pos & 127,
                               jnp.where((lane >= 6) & (lane < 9), 1, 0))).astype(BF16)
    oat = _dsa_attn(iqt, ik4, wft, aqt, ak, posx, avt)

    x1 = _out_proj(oat, obt, x3, mod3, g_out_a, g_out_b, w_out.astype(BF16))

    w_r = jnp.concatenate([w_group, jnp.zeros((d, 32 - N_GROUPS), F32),
                           jnp.transpose(w_router, (1, 0, 2)).reshape(d, N_EXPERTS),
                           jnp.zeros((d, ROUTE_COLS - 64), F32)], axis=1)
    b_r = jnp.concatenate([b_group, jnp.zeros((32 - N_GROUPS,), F32), b_router.reshape(-1),
                           jnp.zeros((ROUTE_COLS - 64,), F32)]).reshape(1, ROUTE_COLS)
    wr_hi, wr_lo = _split_bf16(w_r)
    out = _moe(x1.reshape(bsz * seq, d), mod3, g_ffn, wr_hi, wr_lo, b_r, w_gate, w_up, w_down,
               g_final, seq)
    return out.reshape(bsz, seq, d)


def kernel(x, c, w_ada, b_ada, g_mix, w_in, b_forget, g_out_a, g_out_b, w_out, g_ffn, w_group,
           b_group, w_router, b_router, w_gate, w_up, w_down, g_final):
    depth = w_ada.shape[0]
    assert depth == 1, "final norm is fused into the single layer's MoE kernel"
    return _layer(x, c, w_ada[0], b_ada[0], g_mix[0], w_in[0], b_forget[0], g_out_a[0], g_out_b[0],
                  w_out[0], g_ffn[0], w_group[0], b_group[0], w_router[0], b_router[0], w_gate[0],
                  w_up[0], w_down[0], g_final)
```

```python
import math

import jax
import jax.numpy as jnp
import numpy as np
from jax import lax
from jax.experimental import pallas as pl
from jax.experimental.pallas import tpu as pltpu

F32 = jnp.float32
BF16 = jnp.bfloat16

EPS = 1e-6
A_HEADS = 8
A_KV_HEADS = 2
HEAD_DIM = 64
IDX_HEADS = 8
IDX_DIM = 64
TOPK = 256
B_HEADS = 8
N_GROUPS = 4
EXPERTS_PER_GROUP = 8
N_EXPERTS = N_GROUPS * EXPERTS_PER_GROUP
N_MOD = 6

NEG_BIG = -1e30
LOWEST = float(np.finfo(np.float32).min)
LOG2E = math.log2(math.e)
Q_SCALE = HEAD_DIM ** -0.5 * LOG2E
VMEM_LIMIT = 48 * 1024 * 1024

TQ = 256
KC = 256
AUG = 128
IDX_K = 256
COARSE_STEPS = 12
FOX_LOOKAHEAD = 6
DSA_LOOKAHEAD = 4
assert TQ == KC == TOPK


def _split_bf16(x):
    hi = x.astype(BF16)
    lo = (x - hi.astype(F32)).astype(BF16)
    return hi, lo


def _split3_f32(x):
    p1 = x.astype(BF16).astype(F32)
    r1 = x - p1
    p2 = r1.astype(BF16).astype(F32)
    p3 = (r1 - p2).astype(BF16).astype(F32)
    return p1, p2, p3


def _dot(a, b):
    return jnp.dot(a, b, preferred_element_type=F32)


def _dot_nt(a, b):
    return lax.dot_general(a, b, (((1,), (1,)), ((), ())), preferred_element_type=F32)


def _dot_tn(a, b):
    return lax.dot_general(a, b, (((0,), (0,)), ((), ())), preferred_element_type=F32)


def _dot3(a_hi, a_lo, b_hi, b_lo):
    return _dot(a_hi, b_hi) + _dot(a_lo, b_hi) + _dot(a_hi, b_lo)


def _dot3_nt(a_hi, a_lo, b_hi, b_lo):
    return _dot_nt(a_hi, b_hi) + _dot_nt(a_lo, b_hi) + _dot_nt(a_hi, b_lo)


def _rms(x, g):
    return x * lax.rsqrt(jnp.mean(x * x, axis=-1, keepdims=True) + EPS) * g


def _chunk(kc):
    return pl.ds(pl.multiple_of(kc * KC, KC), KC)


def _key_minus_query():
    return (lax.broadcasted_iota(jnp.int32, (KC, TQ), 0)
            - lax.broadcasted_iota(jnp.int32, (KC, TQ), 1))


def _ada_kernel(c_ref, w_ref, b_ref, o_ref):
    c = c_ref[...]
    s = c * jax.nn.sigmoid(c)
    s_hi, s_lo = _split_bf16(s)
    w_hi, w_lo = _split_bf16(w_ref[...])
    o_ref[...] = _dot3(s_hi, s_lo, w_hi, w_lo) + b_ref[...]


def _ada_mod(c, w_ada, b_ada):
    bsz, d = c.shape
    n = w_ada.shape[1]
    tn = 1024
    return pl.pallas_call(
        _ada_kernel,
        out_shape=jax.ShapeDtypeStruct((bsz, n), F32),
        grid=(n // tn,),
        in_specs=[pl.BlockSpec((bsz, d), lambda j: (0, 0)),
                  pl.BlockSpec((d, tn), lambda j: (0, j)),
                  pl.BlockSpec((1, tn), lambda j: (0, j))],
        out_specs=pl.BlockSpec((bsz, tn), lambda j: (0, j)),
        compiler_params=pltpu.CompilerParams(dimension_semantics=("arbitrary",),
                                             vmem_limit_bytes=VMEM_LIMIT),
        name="ada_mod",
    )(c, w_ada, b_ada.reshape(1, n))


def _in_proj_kernel(x_ref, mod_ref, g_ref,
                    waq_ref, wak_ref, wav_ref, wbq_ref, wbk_ref, wbv_ref,
                    wiqh_ref, wiql_ref, wikh_ref, wikl_ref, wwfh_ref, wwfl_ref,
                    aqt_ref, ak_ref, avt_ref, bqt_ref, bk_ref, bvt_ref, iqt_ref, ik_ref, wft_ref):
    x = x_ref[0]
    h = _rms(x, g_ref[...]) * (1.0 + mod_ref[0, 1:2, :]) + mod_ref[0, 0:1, :]
    h_hi, h_lo = _split_bf16(h)
    aqt_ref[0] = (_dot_nt(waq_ref[...], h_hi) * Q_SCALE).astype(BF16)
    ak_ref[0] = _dot(h_hi, wak_ref[...]).astype(BF16)
    avt_ref[0] = _dot_nt(wav_ref[...], h_hi).astype(BF16)
    bqt_ref[0] = (_dot_nt(wbq_ref[...], h_hi) * Q_SCALE).astype(BF16)
    bk_ref[0] = _dot(h_hi, wbk_ref[...]).astype(BF16)
    bvt_ref[0] = _dot_nt(wbv_ref[...], h_hi).astype(BF16)
    iqt = _dot3_nt(wiqh_ref[...], wiql_ref[...], h_hi, h_lo)
    for hd in range(IDX_HEADS):
        q_hi, q_lo = _split_bf16(iqt[hd * IDX_DIM:(hd + 1) * IDX_DIM, :])
        iqt_ref[0, hd * IDX_K:(hd + 1) * IDX_K, :] = jnp.concatenate([q_hi, q_lo, q_hi, q_lo], axis=0)
    ik2 = _dot3(h_hi, h_lo, wikh_ref[...], wikl_ref[...])
    k_hi, k_lo = _split_bf16(ik2)
    ik_ref[0] = jnp.concatenate([k_hi, k_lo], axis=1)
    wft_ref[0] = _dot3_nt(wwfh_ref[...], wwfl_ref[...], h_hi, h_lo)


def _in_proj(x3, mod3, g_mix, weights):
    bsz, seq, d = x3.shape
    tm = 512
    blk_t = lambda b, i: (b, 0, i)
    blk_r = lambda b, i: (b, i, 0)
    const = lambda b, i: (0, 0)
    ak_w = A_KV_HEADS * AUG
    bk_w = B_HEADS * AUG
    outs = [jax.ShapeDtypeStruct((bsz, 512, seq), BF16), jax.ShapeDtypeStruct((bsz, seq, ak_w), BF16),
            jax.ShapeDtypeStruct((bsz, 128, seq), BF16), jax.ShapeDtypeStruct((bsz, 512, seq), BF16),
            jax.ShapeDtypeStruct((bsz, seq, bk_w), BF16), jax.ShapeDtypeStruct((bsz, 512, seq), BF16),
            jax.ShapeDtypeStruct((bsz, IDX_HEADS * IDX_K, seq), BF16),
            jax.ShapeDtypeStruct((bsz, seq, 256), BF16), jax.ShapeDtypeStruct((bsz, 16, seq), F32)]
    out_specs = [pl.BlockSpec((1, 512, tm), blk_t), pl.BlockSpec((1, tm, ak_w), blk_r),
                 pl.BlockSpec((1, 128, tm), blk_t), pl.BlockSpec((1, 512, tm), blk_t),
                 pl.BlockSpec((1, tm, bk_w), blk_r), pl.BlockSpec((1, 512, tm), blk_t),
                 pl.BlockSpec((1, IDX_HEADS * IDX_K, tm), blk_t),
                 pl.BlockSpec((1, tm, 256), blk_r), pl.BlockSpec((1, 16, tm), blk_t)]
    return pl.pallas_call(
        _in_proj_kernel,
        out_shape=outs,
        grid=(bsz, seq // tm),
        in_specs=[pl.BlockSpec((1, tm, d), blk_r),
                  pl.BlockSpec((1, N_MOD, d), lambda b, i: (b, 0, 0)),
                  pl.BlockSpec((1, d), const)] + [pl.BlockSpec(w.shape, const) for w in weights],
        out_specs=out_specs,
        compiler_params=pltpu.CompilerParams(dimension_semantics=("arbitrary", "arbitrary"),
                                             vmem_limit_bytes=VMEM_LIMIT),
        name="in_proj",
    )(x3, mod3, g_mix.reshape(1, d), *weights)


CB = 256


def _cum_kernel(wft_ref, bfor_ref, k_ref, cumt_ref, kaug_ref):
    seq = wft_ref.shape[2]
    r = lax.broadcasted_iota(jnp.int32, (CB, CB), 0)
    cidx = lax.broadcasted_iota(jnp.int32, (CB, CB), 1)
    tri = jnp.where(r <= cidx, 1.0, 0.0).astype(BF16)
    row128 = lax.broadcasted_iota(jnp.int32, (AUG, CB), 0)
    ones_rows = jnp.where((row128 >= HEAD_DIM + 3) & (row128 < HEAD_DIM + 6), 1.0, 0.0)
    carry = jnp.zeros((8, 1), F32)
    for blk in range(seq // CB):
        cols = slice(blk * CB, (blk + 1) * CB)
        z = wft_ref[0, 8:16, cols] + bfor_ref[...]
        logf = jnp.minimum(z, 0.0) - jnp.log(1.0 + jnp.exp(-jnp.abs(z)))
        p1, p2, p3 = _split3_f32(logf)
        pieces = jnp.concatenate([p1, p2, p3, jnp.zeros_like(p1)], axis=0).astype(BF16)
        parts = _dot(pieces, tri)
        cum = parts[0:8] + parts[8:16] + parts[16:24] + carry
        carry = cum[:, CB - 1:CB]
        cum2 = cum * LOG2E
        cumt_ref[0, :, cols] = cum2
        c1, c2, c3 = _split3_f32(cum2)
        for h in range(B_HEADS):
            spare = jnp.where(row128 == HEAD_DIM, -c1[h:h + 1], ones_rows)
            spare = jnp.where(row128 == HEAD_DIM + 1, -c2[h:h + 1], spare)
            spare = jnp.where(row128 == HEAD_DIM + 2, -c3[h:h + 1], spare)
            lanes = slice(h * AUG, (h + 1) * AUG)
            kaug_ref[0, cols, lanes] = k_ref[0, cols, lanes] + spare.T.astype(BF16)


def _fox_cum(wft, b_forget, bk):
    bsz, _, seq = wft.shape
    nh = B_HEADS
    kw = bk.shape[-1]
    return pl.pallas_call(
        _cum_kernel,
        out_shape=[jax.ShapeDtypeStruct((bsz, nh, seq), F32), jax.ShapeDtypeStruct((bsz, seq, kw), BF16)],
        grid=(bsz,),
        in_specs=[pl.BlockSpec((1, 16, seq), lambda b: (b, 0, 0)),
                  pl.BlockSpec((nh, 1), lambda b: (0, 0)),
                  pl.BlockSpec((1, seq, kw), lambda b: (b, 0, 0))],
        out_specs=[pl.BlockSpec((1, nh, seq), lambda b: (b, 0, 0)),
                   pl.BlockSpec((1, seq, kw), lambda b: (b, 0, 0))],
        compiler_params=pltpu.CompilerParams(dimension_semantics=("arbitrary",),
                                             vmem_limit_bytes=VMEM_LIMIT),
        name="fox_cum",
    )(wft, b_forget.reshape(nh, 1), bk)


def _softmax_init(m_ref, l_ref, acc_ref):
    m_ref[...] = jnp.full(m_ref.shape, NEG_BIG, F32)
    l_ref[...] = jnp.zeros(l_ref.shape, F32)
    acc_ref[...] = jnp.zeros(acc_ref.shape, F32)


def _attend_chunks(items, score_fn, vt_fn, m_ref, l_ref, acc_ref, lookahead):
    ahead = min(lookahead, len(items))
    scores = {i: score_fn(items[i]) for i in range(ahead)}
    for i, item in enumerate(items):
        if i + ahead < len(items):
            scores[i + ahead] = score_fn(items[i + ahead])
        s = scores.pop(i)
        h = item[1]
        m_old = m_ref[h]
        m_new = jnp.maximum(m_old, jnp.max(s, axis=0, keepdims=True))
        alpha = jnp.exp2(m_old - m_new)
        p = jnp.exp2(s - m_new)
        l_ref[h] = alpha * l_ref[h] + jnp.sum(p, axis=0, keepdims=True)
        m_ref[h] = m_new
        rows = slice(h * HEAD_DIM, (h + 1) * HEAD_DIM)
        acc_ref[rows, :] = alpha * acc_ref[rows, :] + _dot(vt_fn(item), p.astype(BF16))


def _softmax_finish(o_ref, n_heads, l_ref, acc_ref):
    for h in range(n_heads):
        rows = slice(h * HEAD_DIM, (h + 1) * HEAD_DIM)
        o_ref[0, rows, :] = acc_ref[rows, :] / l_ref[h]


def _fox_kernel(qt_ref, k_ref, vt_ref, cumt_ref, o_ref, w_ref, m_ref, l_ref, acc_ref):
    qi = pl.program_id(1)
    _softmax_init(m_ref, l_ref, acc_ref)

    row64 = lax.broadcasted_iota(jnp.int32, (AUG - HEAD_DIM, TQ), 0)
    for h in range(B_HEADS):
        c1, c2, c3 = _split3_f32(cumt_ref[0, h:h + 1, :])
        spare = jnp.where(row64 < 3, 1.0, 0.0)
        spare = jnp.where(row64 == 3, c1, spare)
        spare = jnp.where(row64 == 4, c2, spare)
        spare = jnp.where(row64 == 5, c3, spare)
        w_ref[h] = jnp.concatenate([qt_ref[0, h * HEAD_DIM:(h + 1) * HEAD_DIM, :],
                                    spare.astype(BF16)], axis=0)

    def tiles(chunks):
        rows = [_chunk(kc) for kc, _ in chunks]

        def score_fn(item):
            j, h = item
            s = _dot(k_ref[0, rows[j], h * AUG:(h + 1) * AUG], w_ref[h])
            if chunks[j][1]:
                s = jnp.where(_key_minus_query() <= 0, s, NEG_BIG)
            return s

        def vt_fn(item):
            j, h = item
            return vt_ref[0, h * HEAD_DIM:(h + 1) * HEAD_DIM, rows[j]]

        items = [(j, h) for j in range(len(chunks)) for h in range(B_HEADS)]
        _attend_chunks(items, score_fn, vt_fn, m_ref, l_ref, acc_ref, FOX_LOOKAHEAD)

    def body(pair, _):
        tiles([(2 * pair, False), (2 * pair + 1, False)])
        return 0

    lax.fori_loop(0, qi // 2, body, 0)

    @pl.when(qi % 2 == 1)
    def _():
        tiles([(qi - 1, False), (qi, True)])

    @pl.when(qi % 2 == 0)
    def _():
        tiles([(qi, True)])

    _softmax_finish(o_ref, B_HEADS, l_ref, acc_ref)


def _fox_attn(bqt, kaug, bvt, cumt):
    bsz, w, seq = bqt.shape
    blk_t = lambda b, i: (b, 0, i)
    full = lambda b, i: (b, 0, 0)
    return pl.pallas_call(
        _fox_kernel,
        out_shape=jax.ShapeDtypeStruct((bsz, w, seq), F32),
        grid=(bsz, seq // TQ),
        in_specs=[pl.BlockSpec((1, w, TQ), blk_t),
                  pl.BlockSpec((1, seq, B_HEADS * AUG), full),
                  pl.BlockSpec((1, w, seq), full),
                  pl.BlockSpec((1, B_HEADS, TQ), blk_t)],
        out_specs=pl.BlockSpec((1, w, TQ), blk_t),
        scratch_shapes=[pltpu.VMEM((B_HEADS, AUG, TQ), BF16), pltpu.VMEM((B_HEADS, 1, TQ), F32),
                        pltpu.VMEM((B_HEADS, 1, TQ), F32), pltpu.VMEM((B_HEADS * HEAD_DIM, TQ), F32)],
        compiler_params=pltpu.CompilerParams(dimension_semantics=("arbitrary", "arbitrary"),
                                             vmem_limit_bytes=VMEM_LIMIT),
        name="fox_attn",
    )(bqt, kaug, bvt, cumt)


def _bf16_pieces(value):
    pieces = []
    rest = np.float32(value)
    for _ in range(3):
        piece = np.asarray(rest).astype(BF16).astype(np.float32)
        pieces.append(float(piece))
        rest = np.float32(rest - piece)
    return pieces


def _dsa_kernel(iqt_ref, ik_ref, wft_ref, qt_ref, k_ref, posx_ref, vt_ref, o_ref,
                s_ref, thr_ref, w_ref, m_ref, l_ref, acc_ref):
    qi = pl.program_id(1)
    nch = qi + 1
    kmq = _key_minus_query()

    def score_chunk(kc):
        ik = ik_ref[0, _chunk(kc), :]
        acc = jnp.zeros((KC, TQ), F32)
        for h in range(IDX_HEADS):
            d = _dot(ik, iqt_ref[0, h * IDX_K:(h + 1) * IDX_K, :])
            acc = acc + wft_ref[0, h:h + 1, :] * jnp.maximum(d, 0.0)
        acc = jnp.where(acc == 0.0, 0.0, acc)
        causal = kmq <= (qi - kc) * KC
        s_ref[_chunk(kc), :] = jnp.where(causal, acc, -jnp.inf)

    def score_body(pair, _):
        score_chunk(2 * pair)
        score_chunk(2 * pair + 1)
        return 0

    lax.fori_loop(0, nch // 2, score_body, 0)

    @pl.when(nch % 2 == 1)
    def _():
        score_chunk(nch - 1)

    def scan(fn, init):
        def pair_body(pair, c):
            c = fn(2 * pair, s_ref[_chunk(2 * pair), :], c)
            return fn(2 * pair + 1, s_ref[_chunk(2 * pair + 1), :], c)

        c = lax.fori_loop(0, nch // 2, pair_body, init)
        return lax.cond(nch % 2 == 1, lambda c: fn(nch - 1, s_ref[_chunk(nch - 1), :], c),
                        lambda c: c, c)

    @pl.when(qi == 0)
    def _():
        thr_ref[...] = jnp.full(thr_ref.shape, LOWEST, F32)

    @pl.when(qi > 0)
    def _():
        def cmin(x):
            return jnp.min(x, axis=0, keepdims=True)

        def cmax(x):
            return jnp.max(x, axis=0, keepdims=True)

        def csum(x):
            return jnp.sum(x, axis=0, keepdims=True)

        zeros = jnp.zeros((1, TQ), F32)
        pinf = jnp.full((1, TQ), jnp.inf, F32)
        ninf = jnp.full((1, TQ), -jnp.inf, F32)

        def init_fn(_, s, c):
            lo, hi = c
            lo = jnp.minimum(lo, cmin(jnp.where(s > -jnp.inf, s, jnp.inf)))
            hi = jnp.maximum(hi, cmax(s))
            return lo, hi

        lo, hi = scan(init_fn, (pinf, ninf))

        def coarse_step(_, carry):
            lo, hi = carry
            mid = lo + (hi - lo) * 0.5
            cnt = scan(lambda _, s, c: c + csum(jnp.where(s >= mid, 1.0, 0.0)), zeros)
            enough = cnt >= float(TOPK)
            return jnp.where(enough, mid, lo), jnp.where(enough, hi, mid)

        lo, hi = lax.fori_loop(0, COARSE_STEPS, coarse_step, (lo, hi))

        def cond(carry):
            return carry[2] > 0

        def step(carry):
            lo, hi, _ = carry
            mid = lo + (hi - lo) * 0.5
            mid = jnp.where(mid <= lo, hi, mid)

            def fn(_, s, c):
                cnt, a, b = c
                ge = s >= mid
                cnt = cnt + csum(jnp.where(ge, 1.0, 0.0))
                b = jnp.minimum(b, cmin(jnp.where(ge, s, jnp.inf)))
                a = jnp.maximum(a, cmax(jnp.where(ge, -jnp.inf, s)))
                return cnt, a, b

            cnt, a, b = scan(fn, (zeros, ninf, pinf))
            enough = cnt >= float(TOPK)
            new_lo = jnp.where(enough, b, lo)
            new_hi = jnp.where(enough, jnp.where(cnt == float(TOPK), b, hi), a)
            active = jnp.max(jnp.where(new_lo < new_hi, 1, 0))
            return new_lo, new_hi, active

        first_active = jnp.max(jnp.where(lo < hi, 1, 0))
        thr, _, _ = lax.while_loop(cond, step, (lo, hi, first_active))

        def count_fn(_, s, c):
            n_gt, n_ge = c
            return (n_gt + csum(jnp.where(s > thr, 1.0, 0.0)),
                    n_ge + csum(jnp.where(s >= thr, 1.0, 0.0)))

        n_gt, n_ge = scan(count_fn, (zeros, zeros))
        thr_ref[...] = thr

        @pl.when(jnp.max(jnp.where(n_ge > float(TOPK), 1, 0)) > 0)
        def _():
            need = float(TOPK) - n_gt
            lower = jnp.where(lax.broadcasted_iota(jnp.int32, (KC, KC), 1)
                              < lax.broadcasted_iota(jnp.int32, (KC, KC), 0), 1.0, 0.0).astype(BF16)

            def sel_fn(kc, s, run):
                eq = s == thr
                eqf = jnp.where(eq, 1.0, 0.0)
                before = _dot(lower, eqf.astype(BF16)) + run
                sel = (s > thr) | (eq & (before < need))
                s_ref[_chunk(kc), :] = jnp.where(sel, 0.0, NEG_BIG)
                return run + csum(eqf)

            scan(sel_fn, zeros)
            thr_ref[...] = jnp.full(thr_ref.shape, 0.5 * NEG_BIG, F32)

    _softmax_init(m_ref, l_ref, acc_ref)
    rep = A_HEADS // A_KV_HEADS
    row64 = lax.broadcasted_iota(jnp.int32, (AUG - HEAD_DIM, TQ), 0)
    qpos = (qi * TQ + lax.broadcasted_iota(jnp.int32, (AUG - HEAD_DIM, TQ), 1)).astype(F32)
    for h in range(A_HEADS):
        slope = np.float32(2.0 ** (-8.0 * (h + 1) / A_HEADS) * LOG2E)
        u1, u2, u3 = _split3_f32(-slope * qpos)
        spare = jnp.zeros((AUG - HEAD_DIM, TQ), F32)
        for p, s_p in enumerate(_bf16_pieces(slope)):
            spare = jnp.where(row64 == p, 128.0 * s_p, spare)
            spare = jnp.where(row64 == 3 + p, s_p, spare)
        spare = jnp.where(row64 == 6, u1, spare)
        spare = jnp.where(row64 == 7, u2, spare)
        spare = jnp.where(row64 == 8, u3, spare)
        w_ref[h] = jnp.concatenate([qt_ref[0, h * HEAD_DIM:(h + 1) * HEAD_DIM, :],
                                    spare.astype(BF16)], axis=0)

    def attn_tiles(chunks):
        rows = [_chunk(kc) for kc in chunks]
        lhs = [[k_ref[0, r, g * AUG:(g + 1) * AUG] + posx_ref[r, :] for g in range(A_KV_HEADS)]
               for r in rows]
        bias = [jnp.where(s_ref[r, :] >= thr_ref[...], 0.0, NEG_BIG) for r in rows]

        def score_fn(item):
            j, h = item
            return _dot(lhs[j][h // rep], w_ref[h]) + bias[j]

        def vt_fn(item):
            j, h = item
            g = h // rep
            return vt_ref[0, g * HEAD_DIM:(g + 1) * HEAD_DIM, rows[j]]

        items = [(j, h) for j in range(len(chunks)) for h in range(A_HEADS)]
        _attend_chunks(items, score_fn, vt_fn, m_ref, l_ref, acc_ref, DSA_LOOKAHEAD)

    def attn_body(pair, _):
        attn_tiles([2 * pair, 2 * pair + 1])
        return 0

    lax.fori_loop(0, nch // 2, attn_body, 0)

    @pl.when(nch % 2 == 1)
    def _():
        attn_tiles([nch - 1])

    _softmax_finish(o_ref, A_HEADS, l_ref, acc_ref)


def _dsa_attn(iqt, ik4, wft, aqt, ak, posx, avt):
    bsz, w, seq = aqt.shape
    blk_t = lambda b, i: (b, 0, i)
    full = lambda b, i: (b, 0, 0)
    return pl.pallas_call(
        _dsa_kernel,
        out_shape=jax.ShapeDtypeStruct((bsz, w, seq), F32),
        grid=(bsz, seq // TQ),
        in_specs=[pl.BlockSpec((1, IDX_HEADS * IDX_K, TQ), blk_t),
                  pl.BlockSpec((1, seq, 256), full),
                  pl.BlockSpec((1, 16, TQ), blk_t),
                  pl.BlockSpec((1, w, TQ), blk_t),
                  pl.BlockSpec((1, seq, A_KV_HEADS * AUG), full),
                  pl.BlockSpec((seq, AUG), lambda b, i: (0, 0)),
                  pl.BlockSpec((1, 128, seq), full)],
        out_specs=pl.BlockSpec((1, w, TQ), blk_t),
        scratch_shapes=[pltpu.VMEM((seq, TQ), F32), pltpu.VMEM((1, TQ), F32),
                        pltpu.VMEM((A_HEADS, AUG, TQ), BF16),
                        pltpu.VMEM((A_HEADS, 1, TQ), F32), pltpu.VMEM((A_HEADS, 1, TQ), F32),
                        pltpu.VMEM((A_HEADS * HEAD_DIM, TQ), F32)],
        compiler_params=pltpu.CompilerParams(dimension_semantics=("arbitrary", "arbitrary"),
                                             vmem_limit_bytes=VMEM_LIMIT),
        name="dsa_attn",
    )(iqt, ik4, wft, aqt, ak, posx, avt)


def _rms_cols(xt, g_col):
    return xt * lax.rsqrt(jnp.mean(xt * xt, axis=0, keepdims=True) + EPS) * g_col


def _out_proj_kernel(oat_ref, obt_ref, x_ref, mod_ref, ga_ref, gb_ref, w_ref, o_ref):
    oa = _rms_cols(oat_ref[0], ga_ref[...]).astype(BF16)
    ob = _rms_cols(obt_ref[0], gb_ref[...]).astype(BF16)
    y = _dot_tn(oa, w_ref[0:512, :]) + _dot_tn(ob, w_ref[512:1024, :])
    o_ref[0] = x_ref[0] + mod_ref[0, 2:3, :] * y


def _out_proj(oat, obt, x3, mod3, g_out_a, g_out_b, w_out_bf):
    bsz, seq, d = x3.shape
    tm = 512
    blk_t = lambda b, i: (b, 0, i)
    blk_r = lambda b, i: (b, i, 0)
    const = lambda b, i: (0, 0)
    return pl.pallas_call(
        _out_proj_kernel,
        out_shape=jax.ShapeDtypeStruct((bsz, seq, d), F32),
        grid=(bsz, seq // tm),
        in_specs=[pl.BlockSpec((1, 512, tm), blk_t), pl.BlockSpec((1, 512, tm), blk_t),
                  pl.BlockSpec((1, tm, d), blk_r),
                  pl.BlockSpec((1, N_MOD, d), lambda b, i: (b, 0, 0)),
                  pl.BlockSpec((512, 1), const), pl.BlockSpec((512, 1), const),
                  pl.BlockSpec((d, d), const)],
        out_specs=pl.BlockSpec((1, tm, d), blk_r),
        compiler_params=pltpu.CompilerParams(dimension_semantics=("arbitrary", "arbitrary"),
                                             vmem_limit_bytes=VMEM_LIMIT),
        name="out_proj",
    )(oat, obt, x3, mod3, g_out_a.reshape(-1, 1), g_out_b.reshape(-1, 1), w_out_bf)


MOE_TMG = 256
MOE_TMC = 256
ROW_TILE = 8
ROUTE_COLS = 128


def _first_index_of_max(vals, lane, big):
    m = jnp.max(vals, axis=-1, keepdims=True)
    idx = jnp.min(jnp.where(vals == m, lane, big), axis=-1, keepdims=True)
    return m, idx


def _route(h, wr_hi_ref, wr_lo_ref, br_ref):
    h_hi, h_lo = _split_bf16(h)
    logits = _dot3(h_hi, h_lo, wr_hi_ref[...], wr_lo_ref[...]) + br_ref[...]
    lane = lax.broadcasted_iota(jnp.int32, logits.shape, 1)
    ninf = -jnp.inf
    gl = jnp.where(lane < N_GROUPS, logits, ninf)
    gmax, gsel = _first_index_of_max(gl, lane, 1 << 20)
    p_group = 1.0 / jnp.sum(jnp.exp(gl - gmax), axis=-1, keepdims=True)
    base = 32 + gsel * EXPERTS_PER_GROUP
    el = jnp.where((lane >= base) & (lane < base + EXPERTS_PER_GROUP), logits, ninf)
    v1, i1 = _first_index_of_max(el, lane, 1 << 20)
    el2 = jnp.where(lane == i1, ninf, el)
    v2, i2 = _first_index_of_max(el2, lane, 1 << 20)
    e2 = jnp.exp(v2 - v1)
    w1 = 1.0 / (1.0 + e2)
    w2 = e2 / (1.0 + e2)
    return i1, i2, w1 * p_group, w2 * p_group


def _token_rows(s, n_tokens, first_token=0):
    return pl.ds(first_token * ROW_TILE + s, n_tokens, stride=ROW_TILE)


def _to_token_tiles(ref, x):
    for s in range(ROW_TILE):
        ref[_token_rows(s, x.shape[0]), :] = x[:, s * 128:(s + 1) * 128]


def _token_tile(ref, t):
    start = t * ROW_TILE if isinstance(t, int) else pl.multiple_of(t * ROW_TILE, ROW_TILE)
    return ref.at[pl.ds(start, ROW_TILE), :]


def _moe_route_kernel(x_ref, mod_ref, gf_ref, wr_hi_ref, wr_lo_ref, br_ref,
                      h_ref, info_ref, infot_ref, cnt_ref, run_ref):
    @pl.when(pl.program_id(0) == 0)
    def _():
        run_ref[...] = jnp.zeros_like(run_ref)

    h = _rms(x_ref[...], gf_ref[...]) * (1.0 + mod_ref[0, 4:5, :]) + mod_ref[0, 3:4, :]
    _to_token_tiles(h_ref, h)
    i1, i2, w1, w2 = _route(h, wr_hi_ref, wr_lo_ref, br_ref)
    tm = h.shape[0]
    lane = lax.broadcasted_iota(jnp.int32, (tm, ROUTE_COLS), 1)
    picked = jnp.where((lane == i1) | (lane == i2), 1.0, 0.0)
    earlier = jnp.where(lax.broadcasted_iota(jnp.int32, (tm, tm), 1)
                        < lax.broadcasted_iota(jnp.int32, (tm, tm), 0), 1.0, 0.0).astype(BF16)
    before = _dot(earlier, picked.astype(BF16)) + run_ref[...]
    rank1 = jnp.sum(jnp.where(lane == i1, before, 0.0), axis=-1, keepdims=True)
    rank2 = jnp.sum(jnp.where(lane == i2, before, 0.0), axis=-1, keepdims=True)
    run_ref[...] += jnp.sum(picked, axis=0, keepdims=True)
    cnt_ref[...] = run_ref[...]
    info = jnp.where(lane == 0, (i1 - 32).astype(F32), 0.0)
    info = jnp.where(lane == 1, (i2 - 32).astype(F32), info)
    info = jnp.where(lane == 2, rank1, info)
    info = jnp.where(lane == 3, rank2, info)
    info = jnp.where(lane == 4, w1, info)
    info = jnp.where(lane == 5, w2, info)
    info_ref[...] = info
    infot_ref[...] = info.T[0:8, :]


def _moe_route(x1, mod3, g_ffn, wr_hi, wr_lo, b_route, seq):
    n, d = x1.shape
    tm = 512
    per_b = seq // tm
    row = lambda i: (i, 0)
    const = lambda i: (0, 0)
    return pl.pallas_call(
        _moe_route_kernel,
        out_shape=[jax.ShapeDtypeStruct((n * ROW_TILE, d // ROW_TILE), F32),
                   jax.ShapeDtypeStruct((n, ROUTE_COLS), F32), jax.ShapeDtypeStruct((8, n), F32),
                   jax.ShapeDtypeStruct((1, ROUTE_COLS), F32)],
        grid=(n // tm,),
        in_specs=[pl.BlockSpec((tm, d), row),
                  pl.BlockSpec((1, N_MOD, d), lambda i: (i // per_b, 0, 0)),
                  pl.BlockSpec((1, d), const),
                  pl.BlockSpec((d, ROUTE_COLS), const), pl.BlockSpec((d, ROUTE_COLS), const),
                  pl.BlockSpec((1, ROUTE_COLS), const)],
        out_specs=[pl.BlockSpec((tm * ROW_TILE, d // ROW_TILE), row),
                   pl.BlockSpec((tm, ROUTE_COLS), row), pl.BlockSpec((8, tm), lambda i: (0, i)),
                   pl.BlockSpec((1, ROUTE_COLS), const)],
        scratch_shapes=[pltpu.VMEM((1, ROUTE_COLS), F32)],
        compiler_params=pltpu.CompilerParams(dimension_semantics=("arbitrary",),
                                             vmem_limit_bytes=VMEM_LIMIT),
        name="moe_route",
    )(x1, mod3, g_ffn.reshape(1, d), wr_hi, wr_lo, b_route)


def _wait_token_copies(src_hbm, dst, sem, n_tokens):
    pltpu.make_async_copy(src_hbm.at[pl.ds(0, n_tokens * ROW_TILE), :], dst, sem).wait()


def _moe_scatter_kernel(dest_ref, h_ref, zeros_hbm, xs_hbm, sem):
    del zeros_hbm
    i = pl.program_id(0)
    n_pairs = 2 * MOE_TMC
    for r in range(MOE_TMC):
        for j in range(2):
            pltpu.make_async_copy(_token_tile(h_ref, r),
                                  _token_tile(xs_hbm, dest_ref[i * n_pairs + j * MOE_TMC + r]),
                                  sem).start(priority=j)
    _wait_token_copies(xs_hbm, xs_hbm.at[pl.ds(0, n_pairs * ROW_TILE), :], sem, n_pairs)


def _moe_scatter(dest, h2, n_rows):
    n = h2.shape[0] // ROW_TILE
    grid_spec = pltpu.PrefetchScalarGridSpec(
        num_scalar_prefetch=1,
        grid=(n // MOE_TMC,),
        in_specs=[pl.BlockSpec((MOE_TMC * ROW_TILE, h2.shape[1]), lambda i, ds: (i, 0)),
                  pl.BlockSpec(memory_space=pl.ANY)],
        out_specs=pl.BlockSpec(memory_space=pl.ANY),
        scratch_shapes=[pltpu.SemaphoreType.DMA(())])
    return pl.pallas_call(
        _moe_scatter_kernel,
        out_shape=jax.ShapeDtypeStruct((n_rows * ROW_TILE, h2.shape[1]), F32),
        grid_spec=grid_spec,
        input_output_aliases={2: 0},
        compiler_params=pltpu.CompilerParams(dimension_semantics=("arbitrary",),
                                             vmem_limit_bytes=VMEM_LIMIT),
        name="moe_scatter",
    )(dest, h2, jnp.zeros((n_rows * ROW_TILE, h2.shape[1]), F32))


def _moe_expert_kernel(te_ref, nu_ref, x_ref, wg_ref, wu_ref, wd_ref, y_ref,
                       xs_ref, wgb_ref, wub_ref, wdb_ref):
    i = pl.program_id(0)
    used = i < nu_ref[0]
    new_expert = jnp.logical_or(i == 0, te_ref[i] != te_ref[jnp.maximum(i - 1, 0)])

    @pl.when(jnp.logical_and(used, new_expert))
    def _():
        wgb_ref[...] = wg_ref[0].astype(BF16)
        wub_ref[...] = wu_ref[0].astype(BF16)
        wdb_ref[...] = wd_ref[0].astype(BF16)

    @pl.when(used)
    def _():
        for s in range(ROW_TILE):
            xs_ref[:, s * 128:(s + 1) * 128] = x_ref[_token_rows(s, MOE_TMG), :].astype(BF16)
        x = xs_ref[...]
        hg = _dot(x, wgb_ref[...])
        hu = _dot(x, wub_ref[...])
        a = hg * jax.nn.sigmoid(hg) * hu
        _to_token_tiles(y_ref, _dot(a.astype(BF16), wdb_ref[...]))

    @pl.when(i >= nu_ref[0])
    def _():
        y_ref[...] = jnp.zeros_like(y_ref)


def _moe_experts(tile_expert, n_used, xsorted, wg, wu, wd):
    n_rows = xsorted.shape[0] // ROW_TILE
    d, ff = wg.shape[1], wg.shape[2]
    tile = (MOE_TMG * ROW_TILE, xsorted.shape[1])
    grid_spec = pltpu.PrefetchScalarGridSpec(
        num_scalar_prefetch=2,
        grid=(n_rows // MOE_TMG,),
        in_specs=[pl.BlockSpec(tile, lambda i, te, nu: (i, 0)),
                  pl.BlockSpec((1, d, ff), lambda i, te, nu: (te[i], 0, 0)),
                  pl.BlockSpec((1, d, ff), lambda i, te, nu: (te[i], 0, 0)),
                  pl.BlockSpec((1, ff, d), lambda i, te, nu: (te[i], 0, 0))],
        out_specs=pl.BlockSpec(tile, lambda i, te, nu: (i, 0)),
        scratch_shapes=[pltpu.VMEM((MOE_TMG, d), BF16), pltpu.VMEM((d, ff), BF16),
                        pltpu.VMEM((d, ff), BF16), pltpu.VMEM((ff, d), BF16)])
    return pl.pallas_call(
        _moe_expert_kernel,
        out_shape=jax.ShapeDtypeStruct(xsorted.shape, F32),
        grid_spec=grid_spec,
        compiler_params=pltpu.CompilerParams(dimension_semantics=("arbitrary",),
                                             vmem_limit_bytes=VMEM_LIMIT),
        name="moe_experts",
    )(tile_expert, n_used, xsorted, wg, wu, wd)


def _moe_combine_kernel(dest_ref, y_hbm, x_ref, info_ref, mod_ref, gfin_ref, o_ref, ybuf, x2_ref, sem):
    i = pl.program_id(0)
    nt = pl.num_programs(0)
    slot = lax.rem(i, 2)
    n_pairs = 2 * MOE_TMC

    def start_gather(tile, to_slot):
        for r in range(n_pairs):
            pltpu.make_async_copy(_token_tile(y_hbm, dest_ref[tile * n_pairs + r]),
                                  _token_tile(ybuf.at[to_slot], r), sem.at[to_slot]).start(priority=r % 2)

    @pl.when(i == 0)
    def _():
        start_gather(0, 0)

    _wait_token_copies(y_hbm, ybuf.at[slot], sem.at[slot], n_pairs)

    @pl.when(i + 1 < nt)
    def _():
        start_gather(i + 1, 1 - slot)

    w1 = info_ref[:, 4:5]
    w2 = info_ref[:, 5:6]
    sumsq = jnp.zeros((MOE_TMC, 1), F32)
    for s in range(ROW_TILE):
        cols = slice(s * 128, (s + 1) * 128)
        y = (w1 * ybuf[slot, _token_rows(s, MOE_TMC), :]
             + w2 * ybuf[slot, _token_rows(s, MOE_TMC, first_token=MOE_TMC), :])
        x2 = x_ref[:, cols] + mod_ref[0, 5:6, cols] * y
        x2_ref[:, cols] = x2
        sumsq = sumsq + jnp.sum(x2 * x2, axis=-1, keepdims=True)
    d = x2_ref.shape[1]
    o_ref[...] = x2_ref[...] * lax.rsqrt(sumsq / d + EPS) * gfin_ref[...]


def _moe_combine(dest, ysorted, x1, info, mod3, g_final, seq):
    n, d = x1.shape
    per_b = seq // MOE_TMC
    grid_spec = pltpu.PrefetchScalarGridSpec(
        num_scalar_prefetch=1,
        grid=(n // MOE_TMC,),
        in_specs=[pl.BlockSpec(memory_space=pl.ANY),
                  pl.BlockSpec((MOE_TMC, d), lambda i, ds: (i, 0)),
                  pl.BlockSpec((MOE_TMC, ROUTE_COLS), lambda i, ds: (i, 0)),
                  pl.BlockSpec((1, N_MOD, d), lambda i, ds: (i // per_b, 0, 0)),
                  pl.BlockSpec((1, d), lambda i, ds: (0, 0))],
        out_specs=pl.BlockSpec((MOE_TMC, d), lambda i, ds: (i, 0)),
        scratch_shapes=[pltpu.VMEM((2, 2 * MOE_TMC * ROW_TILE, ysorted.shape[1]), F32),
                        pltpu.VMEM((MOE_TMC, d), F32), pltpu.SemaphoreType.DMA((2,))])
    return pl.pallas_call(
        _moe_combine_kernel,
        out_shape=jax.ShapeDtypeStruct((n, d), F32),
        grid_spec=grid_spec,
        compiler_params=pltpu.CompilerParams(dimension_semantics=("arbitrary",),
                                             vmem_limit_bytes=VMEM_LIMIT),
        name="moe_combine",
    )(dest, ysorted, x1, info, mod3, g_final.reshape(1, d))


def _moe(x1, mod3, g_ffn, wr_hi, wr_lo, b_route, wg, wu, wd, g_final, seq):
    n, d = x1.shape
    h2, info, infot, counts = _moe_route(x1, mod3, g_ffn, wr_hi, wr_lo, b_route, seq)

    e1, e2, rank1, rank2 = [infot[k].astype(jnp.int32) for k in range(4)]
    cnt = counts[0, 32:32 + N_EXPERTS].astype(jnp.int32)
    padded = ((cnt + MOE_TMG - 1) // MOE_TMG) * MOE_TMG
    seg_end = jnp.cumsum(padded)
    expert_ids = jnp.arange(N_EXPERTS, dtype=jnp.int32)

    def seg_start_of(e):
        return jnp.sum(jnp.where(expert_ids[None, :] < e[:, None], padded[None, :], 0), axis=1)

    dest1 = seg_start_of(e1) + rank1
    dest2 = seg_start_of(e2) + rank2
    n_rows = 2 * n + N_EXPERTS * MOE_TMG
    tile_start = jnp.arange(n_rows // MOE_TMG, dtype=jnp.int32) * MOE_TMG
    tile_expert = jnp.minimum(jnp.sum((tile_start[:, None] >= seg_end[None, :]).astype(jnp.int32), axis=1),
                              N_EXPERTS - 1)
    n_used = (seg_end[N_EXPERTS - 1:] // MOE_TMG).astype(jnp.int32)
    dest = jnp.concatenate([dest1.reshape(-1, MOE_TMC), dest2.reshape(-1, MOE_TMC)], axis=1).reshape(-1)

    xsorted = _moe_scatter(dest, h2, n_rows)
    ysorted = _moe_experts(tile_expert, n_used, xsorted, wg, wu, wd)
    return _moe_combine(dest, ysorted, x1, info, mod3, g_final, seq)


def _layer(x3, c, w_ada, b_ada, g_mix, w_in, b_forget, g_out_a, g_out_b, w_out,
           g_ffn, w_group, b_group, w_router, b_router, w_gate, w_up, w_down, g_final):
    bsz, seq, d = x3.shape
    mod3 = _ada_mod(c, w_ada, b_ada).reshape(bsz, N_MOD, d)

    w_t = w_in.T

    def pad_heads(w, n_heads):
        w = w.reshape(d, n_heads, HEAD_DIM)
        return jnp.concatenate([w, jnp.zeros_like(w)], axis=-1).reshape(d, n_heads * AUG)

    w_ik2 = jnp.concatenate([w_in[:, 1280:1344], w_in[:, 1280:1344]], axis=1)
    w_wf_t = jnp.concatenate([w_t[1344:1352], w_t[2888:2896]], axis=0)
    weights = [w_t[0:512].astype(BF16), pad_heads(w_in[:, 512:640], A_KV_HEADS).astype(BF16),
               w_t[640:768].astype(BF16), w_t[1352:1864].astype(BF16),
               pad_heads(w_in[:, 1864:2376], B_HEADS).astype(BF16), w_t[2376:2888].astype(BF16),
               *_split_bf16(w_t[768:1280]), *_split_bf16(w_ik2), *_split_bf16(w_wf_t)]
    aqt, ak, avt, bqt, bk, bvt, iqt, ik4, wft = _in_proj(x3, mod3, g_mix, weights)

    cumt, kaug = _fox_cum(wft, b_forget, bk)
    obt = _fox_attn(bqt, kaug, bvt, cumt)

    pos = jnp.arange(seq, dtype=jnp.int32)[:, None]
    lane = jnp.arange(AUG, dtype=jnp.int32)[None, :] - HEAD_DIM
    posx = jnp.where((lane >= 0) & (lane < 3), pos >> 7,
                     jnp.where((lane >= 3) & (lane < 6), pos & 127,
                               jnp.where((lane >= 6) & (lane < 9), 1, 0))).astype(BF16)
    oat = _dsa_attn(iqt, ik4, wft, aqt, ak, posx, avt)

    x1 = _out_proj(oat, obt, x3, mod3, g_out_a, g_out_b, w_out.astype(BF16))

    w_r = jnp.concatenate([w_group, jnp.zeros((d, 32 - N_GROUPS), F32),
                           jnp.transpose(w_router, (1, 0, 2)).reshape(d, N_EXPERTS),
                           jnp.zeros((d, ROUTE_COLS - 64), F32)], axis=1)
    b_r = jnp.concatenate([b_group, jnp.zeros((32 - N_GROUPS,), F32), b_router.reshape(-1),
                           jnp.zeros((ROUTE_COLS - 64,), F32)]).reshape(1, ROUTE_COLS)
    wr_hi, wr_lo = _split_bf16(w_r)
    out = _moe(x1.reshape(bsz * seq, d), mod3, g_ffn, wr_hi, wr_lo, b_r, w_gate, w_up, w_down,
               g_final, seq)
    return out.reshape(bsz, seq, d)


def kernel(x, c, w_ada, b_ada, g_mix, w_in, b_forget, g_out_a, g_out_b, w_out, g_ffn, w_group,
           b_group, w_router, b_router, w_gate, w_up, w_down, g_final):
    depth = w_ada.shape[0]
    assert depth == 1, "final norm is fused into the single layer's MoE kernel"
    return _layer(x, c, w_ada[0], b_ada[0], g_mix[0], w_in[0], b_forget[0], g_out_a[0], g_out_b[0],
                  w_out[0], g_ffn[0], w_group[0], b_group[0], w_router[0], b_router[0], w_gate[0],
                  w_up[0], w_down[0], g_final)
```

```python
import math

import jax
import jax.numpy as jnp
import numpy as np
from jax import lax
from jax.experimental import pallas as pl
from jax.experimental.pallas import tpu as pltpu

F32 = jnp.float32
BF16 = jnp.bfloat16

EPS = 1e-6
A_HEADS = 8
A_KV_HEADS = 2
HEAD_DIM = 64
IDX_HEADS = 8
IDX_DIM = 64
TOPK = 256
B_HEADS = 8
N_GROUPS = 4
EXPERTS_PER_GROUP = 8
N_EXPERTS = N_GROUPS * EXPERTS_PER_GROUP
N_MOD = 6

NEG_BIG = -1e30
LOWEST = float(np.finfo(np.float32).min)
LOG2E = math.log2(math.e)
Q_SCALE = HEAD_DIM ** -0.5 * LOG2E
VMEM_LIMIT = 48 * 1024 * 1024

TQ = 256
KC = 256
AUG = 128
IDX_K = 256
COARSE_STEPS = 12
FOX_LOOKAHEAD = 6
DSA_LOOKAHEAD = 4
assert TQ == KC == TOPK


def _split_bf16(x):
    hi = x.astype(BF16)
    lo = (x - hi.astype(F32)).astype(BF16)
    return hi, lo


def _split3_f32(x):
    p1 = x.astype(BF16).astype(F32)
    r1 = x - p1
    p2 = r1.astype(BF16).astype(F32)
    p3 = (r1 - p2).astype(BF16).astype(F32)
    return p1, p2, p3


def _dot(a, b):
    return jnp.dot(a, b, preferred_element_type=F32)


def _dot_nt(a, b):
    return lax.dot_general(a, b, (((1,), (1,)), ((), ())), preferred_element_type=F32)


def _dot_tn(a, b):
    return lax.dot_general(a, b, (((0,), (0,)), ((), ())), preferred_element_type=F32)


def _dot3(a_hi, a_lo, b_hi, b_lo):
    return _dot(a_hi, b_hi) + _dot(a_lo, b_hi) + _dot(a_hi, b_lo)


def _dot3_nt(a_hi, a_lo, b_hi, b_lo):
    return _dot_nt(a_hi, b_hi) + _dot_nt(a_lo, b_hi) + _dot_nt(a_hi, b_lo)


def _rms(x, g):
    return x * lax.rsqrt(jnp.mean(x * x, axis=-1, keepdims=True) + EPS) * g


def _chunk(kc):
    return pl.ds(pl.multiple_of(kc * KC, KC), KC)


def _key_minus_query():
    return (lax.broadcasted_iota(jnp.int32, (KC, TQ), 0)
            - lax.broadcasted_iota(jnp.int32, (KC, TQ), 1))


def _ada_kernel(c_ref, w_ref, b_ref, o_ref):
    c = c_ref[...]
    s = c * jax.nn.sigmoid(c)
    s_hi, s_lo = _split_bf16(s)
    w_hi, w_lo = _split_bf16(w_ref[...])
    o_ref[...] = _dot3(s_hi, s_lo, w_hi, w_lo) + b_ref[...]


def _ada_mod(c, w_ada, b_ada):
    bsz, d = c.shape
    n = w_ada.shape[1]
    tn = 1024
    return pl.pallas_call(
        _ada_kernel,
        out_shape=jax.ShapeDtypeStruct((bsz, n), F32),
        grid=(n // tn,),
        in_specs=[pl.BlockSpec((bsz, d), lambda j: (0, 0)),
                  pl.BlockSpec((d, tn), lambda j: (0, j)),
                  pl.BlockSpec((1, tn), lambda j: (0, j))],
        out_specs=pl.BlockSpec((bsz, tn), lambda j: (0, j)),
        compiler_params=pltpu.CompilerParams(dimension_semantics=("arbitrary",),
                                             vmem_limit_bytes=VMEM_LIMIT),
        name="ada_mod",
    )(c, w_ada, b_ada.reshape(1, n))


def _in_proj_kernel(x_ref, mod_ref, g_ref,
                    waq_ref, wak_ref, wav_ref, wbq_ref, wbk_ref, wbv_ref,
                    wiqh_ref, wiql_ref, wikh_ref, wikl_ref, wwfh_ref, wwfl_ref,
                    aqt_ref, ak_ref, avt_ref, bqt_ref, bk_ref, bvt_ref, iqt_ref, ik_ref, wft_ref):
    x = x_ref[0]
    h = _rms(x, g_ref[...]) * (1.0 + mod_ref[0, 1:2, :]) + mod_ref[0, 0:1, :]
    h_hi, h_lo = _split_bf16(h)
    aqt_ref[0] = (_dot_nt(waq_ref[...], h_hi) * Q_SCALE).astype(BF16)
    ak_ref[0] = _dot(h_hi, wak_ref[...]).astype(BF16)
    avt_ref[0] = _dot_nt(wav_ref[...], h_hi).astype(BF16)
    bqt_ref[0] = (_dot_nt(wbq_ref[...], h_hi) * Q_SCALE).astype(BF16)
    bk_ref[0] = _dot(h_hi, wbk_ref[...]).astype(BF16)
    bvt_ref[0] = _dot_nt(wbv_ref[...], h_hi).astype(BF16)
    iqt = _dot3_nt(wiqh_ref[...], wiql_ref[...], h_hi, h_lo)
    for hd in range(IDX_HEADS):
        q_hi, q_lo = _split_bf16(iqt[hd * IDX_DIM:(hd + 1) * IDX_DIM, :])
        iqt_ref[0, hd * IDX_K:(hd + 1) * IDX_K, :] = jnp.concatenate([q_hi, q_lo, q_hi, q_lo], axis=0)
    ik2 = _dot3(h_hi, h_lo, wikh_ref[...], wikl_ref[...])
    k_hi, k_lo = _split_bf16(ik2)
    ik_ref[0] = jnp.concatenate([k_hi, k_lo], axis=1)
    wft_ref[0] = _dot3_nt(wwfh_ref[...], wwfl_ref[...], h_hi, h_lo)


def _in_proj(x3, mod3, g_mix, weights):
    bsz, seq, d = x3.shape
    tm = 512
    blk_t = lambda b, i: (b, 0, i)
    blk_r = lambda b, i: (b, i, 0)
    const = lambda b, i: (0, 0)
    ak_w = A_KV_HEADS * AUG
    bk_w = B_HEADS * AUG
    outs = [jax.ShapeDtypeStruct((bsz, 512, seq), BF16), jax.ShapeDtypeStruct((bsz, seq, ak_w), BF16),
            jax.ShapeDtypeStruct((bsz, 128, seq), BF16), jax.ShapeDtypeStruct((bsz, 512, seq), BF16),
            jax.ShapeDtypeStruct((bsz, seq, bk_w), BF16), jax.ShapeDtypeStruct((bsz, 512, seq), BF16),
            jax.ShapeDtypeStruct((bsz, IDX_HEADS * IDX_K, seq), BF16),
            jax.ShapeDtypeStruct((bsz, seq, 256), BF16), jax.ShapeDtypeStruct((bsz, 16, seq), F32)]
    out_specs = [pl.BlockSpec((1, 512, tm), blk_t), pl.BlockSpec((1, tm, ak_w), blk_r),
                 pl.BlockSpec((1, 128, tm), blk_t), pl.BlockSpec((1, 512, tm), blk_t),
                 pl.BlockSpec((1, tm, bk_w), blk_r), pl.BlockSpec((1, 512, tm), blk_t),
                 pl.BlockSpec((1, IDX_HEADS * IDX_K, tm), blk_t),
                 pl.BlockSpec((1, tm, 256), blk_r), pl.BlockSpec((1, 16, tm), blk_t)]
    return pl.pallas_call(
        _in_proj_kernel,
        out_shape=outs,
        grid=(bsz, seq // tm),
        in_specs=[pl.BlockSpec((1, tm, d), blk_r),
                  pl.BlockSpec((1, N_MOD, d), lambda b, i: (b, 0, 0)),
                  pl.BlockSpec((1, d), const)] + [pl.BlockSpec(w.shape, const) for w in weights],
        out_specs=out_specs,
        compiler_params=pltpu.CompilerParams(dimension_semantics=("arbitrary", "arbitrary"),
                                             vmem_limit_bytes=VMEM_LIMIT),
        name="in_proj",
    )(x3, mod3, g_mix.reshape(1, d), *weights)


CB = 256


def _cum_kernel(wft_ref, bfor_ref, k_ref, cumt_ref, kaug_ref):
    seq = wft_ref.shape[2]
    r = lax.broadcasted_iota(jnp.int32, (CB, CB), 0)
    cidx = lax.broadcasted_iota(jnp.int32, (CB, CB), 1)
    tri = jnp.where(r <= cidx, 1.0, 0.0).astype(BF16)
    row128 = lax.broadcasted_iota(jnp.int32, (AUG, CB), 0)
    ones_rows = jnp.where((row128 >= HEAD_DIM + 3) & (row128 < HEAD_DIM + 6), 1.0, 0.0)
    carry = jnp.zeros((8, 1), F32)
    for blk in range(seq // CB):
        cols = slice(blk * CB, (blk + 1) * CB)
        z = wft_ref[0, 8:16, cols] + bfor_ref[...]
        logf = jnp.minimum(z, 0.0) - jnp.log(1.0 + jnp.exp(-jnp.abs(z)))
        p1, p2, p3 = _split3_f32(logf)
        pieces = jnp.concatenate([p1, p2, p3, jnp.zeros_like(p1)], axis=0).astype(BF16)
        parts = _dot(pieces, tri)
        cum = parts[0:8] + parts[8:16] + parts[16:24] + carry
        carry = cum[:, CB - 1:CB]
        cum2 = cum * LOG2E
        cumt_ref[0, :, cols] = cum2
        c1, c2, c3 = _split3_f32(cum2)
        for h in range(B_HEADS):
            spare = jnp.where(row128 == HEAD_DIM, -c1[h:h + 1], ones_rows)
            spare = jnp.where(row128 == HEAD_DIM + 1, -c2[h:h + 1], spare)
            spare = jnp.where(row128 == HEAD_DIM + 2, -c3[h:h + 1], spare)
            lanes = slice(h * AUG, (h + 1) * AUG)
            kaug_ref[0, cols, lanes] = k_ref[0, cols, lanes] + spare.T.astype(BF16)


def _fox_cum(wft, b_forget, bk):
    bsz, _, seq = wft.shape
    nh = B_HEADS
    kw = bk.shape[-1]
    return pl.pallas_call(
        _cum_kernel,
        out_shape=[jax.ShapeDtypeStruct((bsz, nh, seq), F32), jax.ShapeDtypeStruct((bsz, seq, kw), BF16)],
        grid=(bsz,),
        in_specs=[pl.BlockSpec((1, 16, seq), lambda b: (b, 0, 0)),
                  pl.BlockSpec((nh, 1), lambda b: (0, 0)),
                  pl.BlockSpec((1, seq, kw), lambda b: (b, 0, 0))],
        out_specs=[pl.BlockSpec((1, nh, seq), lambda b: (b, 0, 0)),
                   pl.BlockSpec((1, seq, kw), lambda b: (b, 0, 0))],
        compiler_params=pltpu.CompilerParams(dimension_semantics=("arbitrary",),
                                             vmem_limit_bytes=VMEM_LIMIT),
        name="fox_cum",
    )(wft, b_forget.reshape(nh, 1), bk)


def _softmax_init(m_ref, l_ref, acc_ref):
    m_ref[...] = jnp.full(m_ref.shape, NEG_BIG, F32)
    l_ref[...] = jnp.zeros(l_ref.shape, F32)
    acc_ref[...] = jnp.zeros(acc_ref.shape, F32)


def _attend_chunks(items, score_fn, vt_fn, m_ref, l_ref, acc_ref, lookahead):
    ahead = min(lookahead, len(items))
    scores = {i: score_fn(items[i]) for i in range(ahead)}
    for i, item in enumerate(items):
        if i + ahead < len(items):
            scores[i + ahead] = score_fn(items[i + ahead])
        s = scores.pop(i)
        h = item[1]
        m_old = m_ref[h]
        m_new = jnp.maximum(m_old, jnp.max(s, axis=0, keepdims=True))
        alpha = jnp.exp2(m_old - m_new)
        p = jnp.exp2(s - m_new)
        l_ref[h] = alpha * l_ref[h] + jnp.sum(p, axis=0, keepdims=True)
        m_ref[h] = m_new
        rows = slice(h * HEAD_DIM, (h + 1) * HEAD_DIM)
        acc_ref[rows, :] = alpha * acc_ref[rows, :] + _dot(vt_fn(item), p.astype(BF16))


def _softmax_finish(o_ref, n_heads, l_ref, acc_ref):
    for h in range(n_heads):
        rows = slice(h * HEAD_DIM, (h + 1) * HEAD_DIM)
        o_ref[0, rows, :] = acc_ref[rows, :] / l_ref[h]


def _fox_kernel(qt_ref, k_ref, vt_ref, cumt_ref, o_ref, w_ref, m_ref, l_ref, acc_ref):
    qi = pl.program_id(1)
    _softmax_init(m_ref, l_ref, acc_ref)

    row64 = lax.broadcasted_iota(jnp.int32, (AUG - HEAD_DIM, TQ), 0)
    for h in range(B_HEADS):
        c1, c2, c3 = _split3_f32(cumt_ref[0, h:h + 1, :])
        spare = jnp.where(row64 < 3, 1.0, 0.0)
        spare = jnp.where(row64 == 3, c1, spare)
        spare = jnp.where(row64 == 4, c2, spare)
        spare = jnp.where(row64 == 5, c3, spare)
        w_ref[h] = jnp.concatenate([qt_ref[0, h * HEAD_DIM:(h + 1) * HEAD_DIM, :],
                                    spare.astype(BF16)], axis=0)

    def tiles(chunks):
        rows = [_chunk(kc) for kc, _ in chunks]

        def score_fn(item):
            j, h = item
            s = _dot(k_ref[0, rows[j], h * AUG:(h + 1) * AUG], w_ref[h])
            if chunks[j][1]:
                s = jnp.where(_key_minus_query() <= 0, s, NEG_BIG)
            return s

        def vt_fn(item):
            j, h = item
            return vt_ref[0, h * HEAD_DIM:(h + 1) * HEAD_DIM, rows[j]]

        items = [(j, h) for j in range(len(chunks)) for h in range(B_HEADS)]
        _attend_chunks(items, score_fn, vt_fn, m_ref, l_ref, acc_ref, FOX_LOOKAHEAD)

    def body(pair, _):
        tiles([(2 * pair, False), (2 * pair + 1, False)])
        return 0

    lax.fori_loop(0, qi // 2, body, 0)

    @pl.when(qi % 2 == 1)
    def _():
        tiles([(qi - 1, False), (qi, True)])

    @pl.when(qi % 2 == 0)
    def _():
        tiles([(qi, True)])

    _softmax_finish(o_ref, B_HEADS, l_ref, acc_ref)


def _fox_attn(bqt, kaug, bvt, cumt):
    bsz, w, seq = bqt.shape
    blk_t = lambda b, i: (b, 0, i)
    full = lambda b, i: (b, 0, 0)
    return pl.pallas_call(
        _fox_kernel,
        out_shape=jax.ShapeDtypeStruct((bsz, w, seq), F32),
        grid=(bsz, seq // TQ),
        in_specs=[pl.BlockSpec((1, w, TQ), blk_t),
                  pl.BlockSpec((1, seq, B_HEADS * AUG), full),
                  pl.BlockSpec((1, w, seq), full),
                  pl.BlockSpec((1, B_HEADS, TQ), blk_t)],
        out_specs=pl.BlockSpec((1, w, TQ), blk_t),
        scratch_shapes=[pltpu.VMEM((B_HEADS, AUG, TQ), BF16), pltpu.VMEM((B_HEADS, 1, TQ), F32),
                        pltpu.VMEM((B_HEADS, 1, TQ), F32), pltpu.VMEM((B_HEADS * HEAD_DIM, TQ), F32)],
        compiler_params=pltpu.CompilerParams(dimension_semantics=("arbitrary", "arbitrary"),
                                             vmem_limit_bytes=VMEM_LIMIT),
        name="fox_attn",
    )(bqt, kaug, bvt, cumt)


def _bf16_pieces(value):
    pieces = []
    rest = np.float32(value)
    for _ in range(3):
        piece = np.asarray(rest).astype(BF16).astype(np.float32)
        pieces.append(float(piece))
        rest = np.float32(rest - piece)
    return pieces


def _dsa_kernel(iqt_ref, ik_ref, wft_ref, qt_ref, k_ref, posx_ref, vt_ref, o_ref,
                s_ref, thr_ref, w_ref, m_ref, l_ref, acc_ref):
    qi = pl.program_id(1)
    nch = qi + 1
    kmq = _key_minus_query()

    def score_chunk(kc):
        ik = ik_ref[0, _chunk(kc), :]
        acc = jnp.zeros((KC, TQ), F32)
        for h in range(IDX_HEADS):
            d = _dot(ik, iqt_ref[0, h * IDX_K:(h + 1) * IDX_K, :])
            acc = acc + wft_ref[0, h:h + 1, :] * jnp.maximum(d, 0.0)
        acc = jnp.where(acc == 0.0, 0.0, acc)
        causal = kmq <= (qi - kc) * KC
        s_ref[_chunk(kc), :] = jnp.where(causal, acc, -jnp.inf)

    def score_body(pair, _):
        score_chunk(2 * pair)
        score_chunk(2 * pair + 1)
        return 0

    lax.fori_loop(0, nch // 2, score_body, 0)

    @pl.when(nch % 2 == 1)
    def _():
        score_chunk(nch - 1)

    def scan(fn, init):
        def pair_body(pair, c):
            c = fn(2 * pair, s_ref[_chunk(2 * pair), :], c)
            return fn(2 * pair + 1, s_ref[_chunk(2 * pair + 1), :], c)

        c = lax.fori_loop(0, nch // 2, pair_body, init)
        return lax.cond(nch % 2 == 1, lambda c: fn(nch - 1, s_ref[_chunk(nch - 1), :], c),
                        lambda c: c, c)

    @pl.when(qi == 0)
    def _():
        thr_ref[...] = jnp.full(thr_ref.shape, LOWEST, F32)

    @pl.when(qi > 0)
    def _():
        def cmin(x):
            return jnp.min(x, axis=0, keepdims=True)

        def cmax(x):
            return jnp.max(x, axis=0, keepdims=True)

        def csum(x):
            return jnp.sum(x, axis=0, keepdims=True)

        zeros = jnp.zeros((1, TQ), F32)
        pinf = jnp.full((1, TQ), jnp.inf, F32)
        ninf = jnp.full((1, TQ), -jnp.inf, F32)

        def init_fn(_, s, c):
            lo, hi = c
            lo = jnp.minimum(lo, cmin(jnp.where(s > -jnp.inf, s, jnp.inf)))
            hi = jnp.maximum(hi, cmax(s))
            return lo, hi

        lo, hi = scan(init_fn, (pinf, ninf))

        def coarse_step(_, carry):
            lo, hi = carry
            mid = lo + (hi - lo) * 0.5
            cnt = scan(lambda _, s, c: c + csum(jnp.where(s >= mid, 1.0, 0.0)), zeros)
            enough = cnt >= float(TOPK)
            return jnp.where(enough, mid, lo), jnp.where(enough, hi, mid)

        lo, hi = lax.fori_loop(0, COARSE_STEPS, coarse_step, (lo, hi))

        def cond(carry):
            return carry[2] > 0

        def step(carry):
            lo, hi, _ = carry
            mid = lo + (hi - lo) * 0.5
            mid = jnp.where(mid <= lo, hi, mid)

            def fn(_, s, c):
                cnt, a, b = c
                ge = s >= mid
                cnt = cnt + csum(jnp.where(ge, 1.0, 0.0))
                b = jnp.minimum(b, cmin(jnp.where(ge, s, jnp.inf)))
                a = jnp.maximum(a, cmax(jnp.where(ge, -jnp.inf, s)))
                return cnt, a, b

            cnt, a, b = scan(fn, (zeros, ninf, pinf))
            enough = cnt >= float(TOPK)
            new_lo = jnp.where(enough, b, lo)
            new_hi = jnp.where(enough, jnp.where(cnt == float(TOPK), b, hi), a)
            active = jnp.max(jnp.where(new_lo < new_hi, 1, 0))
            return new_lo, new_hi, active

        first_active = jnp.max(jnp.where(lo < hi, 1, 0))
        thr, _, _ = lax.while_loop(cond, step, (lo, hi, first_active))

        def count_fn(_, s, c):
            n_gt, n_ge = c
            return (n_gt + csum(jnp.where(s > thr, 1.0, 0.0)),
                    n_ge + csum(jnp.where(s >= thr, 1.0, 0.0)))

        n_gt, n_ge = scan(count_fn, (zeros, zeros))
        thr_ref[...] = thr

        @pl.when(jnp.max(jnp.where(n_ge > float(TOPK), 1, 0)) > 0)
        def _():
            need = float(TOPK) - n_gt
            lower = jnp.where(lax.broadcasted_iota(jnp.int32, (KC, KC), 1)
                              < lax.broadcasted_iota(jnp.int32, (KC, KC), 0), 1.0, 0.0).astype(BF16)

            def sel_fn(kc, s, run):
                eq = s == thr
                eqf = jnp.where(eq, 1.0, 0.0)
                before = _dot(lower, eqf.astype(BF16)) + run
                sel = (s > thr) | (eq & (before < need))
                s_ref[_chunk(kc), :] = jnp.where(sel, 0.0, NEG_BIG)
                return run + csum(eqf)

            scan(sel_fn, zeros)
            thr_ref[...] = jnp.full(thr_ref.shape, 0.5 * NEG_BIG, F32)

    _softmax_init(m_ref, l_ref, acc_ref)
    rep = A_HEADS // A_KV_HEADS
    row64 = lax.broadcasted_iota(jnp.int32, (AUG - HEAD_DIM, TQ), 0)
    qpos = (qi * TQ + lax.broadcasted_iota(jnp.int32, (AUG - HEAD_DIM, TQ), 1)).astype(F32)
    for h in range(A_HEADS):
        slope = np.float32(2.0 ** (-8.0 * (h + 1) / A_HEADS) * LOG2E)
        u1, u2, u3 = _split3_f32(-slope * qpos)
        spare = jnp.zeros((AUG - HEAD_DIM, TQ), F32)
        for p, s_p in enumerate(_bf16_pieces(slope)):
            spare = jnp.where(row64 == p, 128.0 * s_p, spare)
            spare = jnp.where(row64 == 3 + p, s_p, spare)
        spare = jnp.where(row64 == 6, u1, spare)
        spare = jnp.where(row64 == 7, u2, spare)
        spare = jnp.where(row64 == 8, u3, spare)
        w_ref[h] = jnp.concatenate([qt_ref[0, h * HEAD_DIM:(h + 1) * HEAD_DIM, :],
                                    spare.astype(BF16)], axis=0)

    def attn_tiles(chunks):
        rows = [_chunk(kc) for kc in chunks]
        lhs = [[k_ref[0, r, g * AUG:(g + 1) * AUG] + posx_ref[r, :] for g in range(A_KV_HEADS)]
               for r in rows]
        bias = [jnp.where(s_ref[r, :] >= thr_ref[...], 0.0, NEG_BIG) for r in rows]

        def score_fn(item):
            j, h = item
            return _dot(lhs[j][h // rep], w_ref[h]) + bias[j]

        def vt_fn(item):
            j, h = item
            g = h // rep
            return vt_ref[0, g * HEAD_DIM:(g + 1) * HEAD_DIM, rows[j]]

        items = [(j, h) for j in range(len(chunks)) for h in range(A_HEADS)]
        _attend_chunks(items, score_fn, vt_fn, m_ref, l_ref, acc_ref, DSA_LOOKAHEAD)

    def attn_body(pair, _):
        attn_tiles([2 * pair, 2 * pair + 1])
        return 0

    lax.fori_loop(0, nch // 2, attn_body, 0)

    @pl.when(nch % 2 == 1)
    def _():
        attn_tiles([nch - 1])

    _softmax_finish(o_ref, A_HEADS, l_ref, acc_ref)


def _dsa_attn(iqt, ik4, wft, aqt, ak, posx, avt):
    bsz, w, seq = aqt.shape
    blk_t = lambda b, i: (b, 0, i)
    full = lambda b, i: (b, 0, 0)
    return pl.pallas_call(
        _dsa_kernel,
        out_shape=jax.ShapeDtypeStruct((bsz, w, seq), F32),
        grid=(bsz, seq // TQ),
        in_specs=[pl.BlockSpec((1, IDX_HEADS * IDX_K, TQ), blk_t),
                  pl.BlockSpec((1, seq, 256), full),
                  pl.BlockSpec((1, 16, TQ), blk_t),
                  pl.BlockSpec((1, w, TQ), blk_t),
                  pl.BlockSpec((1, seq, A_KV_HEADS * AUG), full),
                  pl.BlockSpec((seq, AUG), lambda b, i: (0, 0)),
                  pl.BlockSpec((1, 128, seq), full)],
        out_specs=pl.BlockSpec((1, w, TQ), blk_t),
        scratch_shapes=[pltpu.VMEM((seq, TQ), F32), pltpu.VMEM((1, TQ), F32),
                        pltpu.VMEM((A_HEADS, AUG, TQ), BF16),
                        pltpu.VMEM((A_HEADS, 1, TQ), F32), pltpu.VMEM((A_HEADS, 1, TQ), F32),
                        pltpu.VMEM((A_HEADS * HEAD_DIM, TQ), F32)],
        compiler_params=pltpu.CompilerParams(dimension_semantics=("arbitrary", "arbitrary"),
                                             vmem_limit_bytes=VMEM_LIMIT),
        name="dsa_attn",
    )(iqt, ik4, wft, aqt, ak, posx, avt)


def _rms_cols(xt, g_col):
    return xt * lax.rsqrt(jnp.mean(xt * xt, axis=0, keepdims=True) + EPS) * g_col


MOE_TMG = 256
MOE_TMC = 256
ROW_TILE = 8
ROUTE_COLS = 128


def _first_index_of_max(vals, lane, big):
    m = jnp.max(vals, axis=-1, keepdims=True)
    idx = jnp.min(jnp.where(vals == m, lane, big), axis=-1, keepdims=True)
    return m, idx


def _route(h, wr_hi_ref, wr_lo_ref, br_ref):
    h_hi, h_lo = _split_bf16(h)
    logits = _dot3(h_hi, h_lo, wr_hi_ref[...], wr_lo_ref[...]) + br_ref[...]
    lane = lax.broadcasted_iota(jnp.int32, logits.shape, 1)
    ninf = -jnp.inf
    gl = jnp.where(lane < N_GROUPS, logits, ninf)
    gmax, gsel = _first_index_of_max(gl, lane, 1 << 20)
    p_group = 1.0 / jnp.sum(jnp.exp(gl - gmax), axis=-1, keepdims=True)
    base = 32 + gsel * EXPERTS_PER_GROUP
    el = jnp.where((lane >= base) & (lane < base + EXPERTS_PER_GROUP), logits, ninf)
    v1, i1 = _first_index_of_max(el, lane, 1 << 20)
    el2 = jnp.where(lane == i1, ninf, el)
    v2, i2 = _first_index_of_max(el2, lane, 1 << 20)
    e2 = jnp.exp(v2 - v1)
    w1 = 1.0 / (1.0 + e2)
    w2 = e2 / (1.0 + e2)
    return i1, i2, w1 * p_group, w2 * p_group


def _token_rows(s, n_tokens, first_token=0):
    return pl.ds(first_token * ROW_TILE + s, n_tokens, stride=ROW_TILE)


def _to_token_tiles(ref, x):
    for s in range(ROW_TILE):
        ref[_token_rows(s, x.shape[0]), :] = x[:, s * 128:(s + 1) * 128]


def _token_tile(ref, t):
    start = t * ROW_TILE if isinstance(t, int) else pl.multiple_of(t * ROW_TILE, ROW_TILE)
    return ref.at[pl.ds(start, ROW_TILE), :]


def _moe_route_kernel(oat_ref, obt_ref, x_ref, mod_ref, ga_ref, gb_ref, wo_ref,
                      gf_ref, wr_hi_ref, wr_lo_ref, br_ref,
                      x1_ref, h_ref, info_ref, infot_ref, cnt_ref, run_ref):
    @pl.when(pl.program_id(0) == 0)
    def _():
        run_ref[...] = jnp.zeros_like(run_ref)

    oa = _rms_cols(oat_ref[0], ga_ref[...]).astype(BF16)
    ob = _rms_cols(obt_ref[0], gb_ref[...]).astype(BF16)
    y = _dot_tn(oa, wo_ref[0:512, :]) + _dot_tn(ob, wo_ref[512:1024, :])
    x1 = x_ref[...] + mod_ref[0, 2:3, :] * y
    x1_ref[...] = x1

    h = _rms(x1, gf_ref[...]) * (1.0 + mod_ref[0, 4:5, :]) + mod_ref[0, 3:4, :]
    _to_token_tiles(h_ref, h)
    i1, i2, w1, w2 = _route(h, wr_hi_ref, wr_lo_ref, br_ref)
    tm = h.shape[0]
    lane = lax.broadcasted_iota(jnp.int32, (tm, ROUTE_COLS), 1)
    picked = jnp.where((lane == i1) | (lane == i2), 1.0, 0.0)
    earlier = jnp.where(lax.broadcasted_iota(jnp.int32, (tm, tm), 1)
                        < lax.broadcasted_iota(jnp.int32, (tm, tm), 0), 1.0, 0.0).astype(BF16)
    before = _dot(earlier, picked.astype(BF16)) + run_ref[...]
    rank1 = jnp.sum(jnp.where(lane == i1, before, 0.0), axis=-1, keepdims=True)
    rank2 = jnp.sum(jnp.where(lane == i2, before, 0.0), axis=-1, keepdims=True)
    run_ref[...] += jnp.sum(picked, axis=0, keepdims=True)
    cnt_ref[...] = run_ref[...]
    info = jnp.where(lane == 0, (i1 - 32).astype(F32), 0.0)
    info = jnp.where(lane == 1, (i2 - 32).astype(F32), info)
    info = jnp.where(lane == 2, rank1, info)
    info = jnp.where(lane == 3, rank2, info)
    info = jnp.where(lane == 4, w1, info)
    info = jnp.where(lane == 5, w2, info)
    info_ref[...] = info
    infot_ref[...] = info.T[0:8, :]


def _moe_route(oat, obt, x2, mod3, g_out_a, g_out_b, w_out_bf, g_ffn, wr_hi, wr_lo, b_route, seq):
    n, d = x2.shape
    tm = 512
    per_b = seq // tm
    row = lambda i: (i, 0)
    const = lambda i: (0, 0)
    blk_t = lambda i: (i // per_b, 0, i % per_b)
    return pl.pallas_call(
        _moe_route_kernel,
        out_shape=[jax.ShapeDtypeStruct((n, d), F32),
                   jax.ShapeDtypeStruct((n * ROW_TILE, d // ROW_TILE), F32),
                   jax.ShapeDtypeStruct((n, ROUTE_COLS), F32), jax.ShapeDtypeStruct((8, n), F32),
                   jax.ShapeDtypeStruct((1, ROUTE_COLS), F32)],
        grid=(n // tm,),
        in_specs=[pl.BlockSpec((1, 512, tm), blk_t), pl.BlockSpec((1, 512, tm), blk_t),
                  pl.BlockSpec((tm, d), row),
                  pl.BlockSpec((1, N_MOD, d), lambda i: (i // per_b, 0, 0)),
                  pl.BlockSpec((512, 1), const), pl.BlockSpec((512, 1), const),
                  pl.BlockSpec((d, d), const),
                  pl.BlockSpec((1, d), const),
                  pl.BlockSpec((d, ROUTE_COLS), const), pl.BlockSpec((d, ROUTE_COLS), const),
                  pl.BlockSpec((1, ROUTE_COLS), const)],
        out_specs=[pl.BlockSpec((tm, d), row),
                   pl.BlockSpec((tm * ROW_TILE, d // ROW_TILE), row),
                   pl.BlockSpec((tm, ROUTE_COLS), row), pl.BlockSpec((8, tm), lambda i: (0, i)),
                   pl.BlockSpec((1, ROUTE_COLS), const)],
        scratch_shapes=[pltpu.VMEM((1, ROUTE_COLS), F32)],
        compiler_params=pltpu.CompilerParams(dimension_semantics=("arbitrary",),
                                             vmem_limit_bytes=VMEM_LIMIT),
        name="mix_out_moe_route",
    )(oat, obt, x2, mod3, g_out_a.reshape(-1, 1), g_out_b.reshape(-1, 1), w_out_bf,
      g_ffn.reshape(1, d), wr_hi, wr_lo, b_route)


def _wait_token_copies(src_hbm, dst, sem, n_tokens):
    pltpu.make_async_copy(src_hbm.at[pl.ds(0, n_tokens * ROW_TILE), :], dst, sem).wait()


def _moe_scatter_kernel(dest_ref, pad_start_ref, pad_len_ref, nu_ref, h_ref, xs_hbm, zero_ref, sem):
    i = pl.program_id(0)
    n_pairs = 2 * MOE_TMC

    @pl.when(i == 0)
    def _():
        zero_ref[...] = jnp.zeros_like(zero_ref)
        zero_tile = _token_tile(zero_ref, 0)
        for e in range(N_EXPERTS):
            def fill(r, _, e=e):
                pltpu.make_async_copy(zero_tile, _token_tile(xs_hbm, pad_start_ref[e] + r),
                                      sem.at[1]).start()
                return 0

            def drain(r, _):
                pltpu.make_async_copy(zero_tile, _token_tile(xs_hbm, 0), sem.at[1]).wait()
                return 0

            lax.fori_loop(0, pad_len_ref[e], fill, 0)
            lax.fori_loop(0, pad_len_ref[e], drain, 0)

        def row_tile(t):
            return xs_hbm.at[pl.ds(pl.multiple_of(t * (MOE_TMG * ROW_TILE), MOE_TMG * ROW_TILE),
                                   MOE_TMG * ROW_TILE), :]

        n_tiles = xs_hbm.shape[0] // (MOE_TMG * ROW_TILE)

        def fill_tile(t, _):
            pltpu.make_async_copy(zero_ref, row_tile(t), sem.at[1]).start()
            return 0

        def drain_tile(t, _):
            pltpu.make_async_copy(zero_ref, row_tile(0), sem.at[1]).wait()
            return 0

        lax.fori_loop(nu_ref[0], n_tiles, fill_tile, 0)
        lax.fori_loop(nu_ref[0], n_tiles, drain_tile, 0)

    for r in range(MOE_TMC):
        for j in range(2):
            pltpu.make_async_copy(_token_tile(h_ref, r),
                                  _token_tile(xs_hbm, dest_ref[i * n_pairs + j * MOE_TMC + r]),
                                  sem.at[0]).start(priority=j)
    _wait_token_copies(xs_hbm, xs_hbm.at[pl.ds(0, n_pairs * ROW_TILE), :], sem.at[0], n_pairs)


def _moe_scatter(dest, pad_start, pad_len, n_used, h2, n_rows):
    n = h2.shape[0] // ROW_TILE
    grid_spec = pltpu.PrefetchScalarGridSpec(
        num_scalar_prefetch=4,
        grid=(n // MOE_TMC,),
        in_specs=[pl.BlockSpec((MOE_TMC * ROW_TILE, h2.shape[1]), lambda i, *_: (i, 0))],
        out_specs=pl.BlockSpec(memory_space=pl.ANY),
        scratch_shapes=[pltpu.VMEM((MOE_TMG * ROW_TILE, h2.shape[1]), F32),
                        pltpu.SemaphoreType.DMA((2,))])
    return pl.pallas_call(
        _moe_scatter_kernel,
        out_shape=jax.ShapeDtypeStruct((n_rows * ROW_TILE, h2.shape[1]), F32),
        grid_spec=grid_spec,
        compiler_params=pltpu.CompilerParams(dimension_semantics=("arbitrary",),
                                             vmem_limit_bytes=VMEM_LIMIT),
        name="moe_scatter",
    )(dest, pad_start, pad_len, n_used, h2)


def _moe_expert_kernel(te_ref, nu_ref, x_ref, wg_ref, wu_ref, wd_ref, y_ref,
                       xs_ref, wgb_ref, wub_ref, wdb_ref):
    i = pl.program_id(0)
    used = i < nu_ref[0]
    new_expert = jnp.logical_or(i == 0, te_ref[i] != te_ref[jnp.maximum(i - 1, 0)])

    @pl.when(jnp.logical_and(used, new_expert))
    def _():
        wgb_ref[...] = wg_ref[0].astype(BF16)
        wub_ref[...] = wu_ref[0].astype(BF16)
        wdb_ref[...] = wd_ref[0].astype(BF16)

    @pl.when(used)
    def _():
        for s in range(ROW_TILE):
            xs_ref[:, s * 128:(s + 1) * 128] = x_ref[_token_rows(s, MOE_TMG), :].astype(BF16)
        x = xs_ref[...]
        hg = _dot(x, wgb_ref[...])
        hu = _dot(x, wub_ref[...])
        a = hg * jax.nn.sigmoid(hg) * hu
        _to_token_tiles(y_ref, _dot(a.astype(BF16), wdb_ref[...]))

    @pl.when(i >= nu_ref[0])
    def _():
        y_ref[...] = jnp.zeros_like(y_ref)


def _moe_experts(tile_expert, n_used, xsorted, wg, wu, wd):
    n_rows = xsorted.shape[0] // ROW_TILE
    d, ff = wg.shape[1], wg.shape[2]
    tile = (MOE_TMG * ROW_TILE, xsorted.shape[1])
    grid_spec = pltpu.PrefetchScalarGridSpec(
        num_scalar_prefetch=2,
        grid=(n_rows // MOE_TMG,),
        in_specs=[pl.BlockSpec(tile, lambda i, te, nu: (jnp.minimum(i, nu[0] - 1), 0)),
                  pl.BlockSpec((1, d, ff), lambda i, te, nu: (te[i], 0, 0)),
                  pl.BlockSpec((1, d, ff), lambda i, te, nu: (te[i], 0, 0)),
                  pl.BlockSpec((1, ff, d), lambda i, te, nu: (te[i], 0, 0))],
        out_specs=pl.BlockSpec(tile, lambda i, te, nu: (i, 0)),
        scratch_shapes=[pltpu.VMEM((MOE_TMG, d), BF16), pltpu.VMEM((d, ff), BF16),
                        pltpu.VMEM((d, ff), BF16), pltpu.VMEM((ff, d), BF16)])
    return pl.pallas_call(
        _moe_expert_kernel,
        out_shape=jax.ShapeDtypeStruct(xsorted.shape, F32),
        grid_spec=grid_spec,
        compiler_params=pltpu.CompilerParams(dimension_semantics=("arbitrary",),
                                             vmem_limit_bytes=VMEM_LIMIT),
        name="moe_experts",
    )(tile_expert, n_used, xsorted, wg, wu, wd)


def _moe_combine_kernel(dest_ref, y_hbm, x_ref, info_ref, mod_ref, gfin_ref, o_ref, ybuf, x2_ref, sem):
    i = pl.program_id(0)
    nt = pl.num_programs(0)
    slot = lax.rem(i, 2)
    n_pairs = 2 * MOE_TMC

    def start_gather(tile, to_slot):
        for r in range(n_pairs):
            pltpu.make_async_copy(_token_tile(y_hbm, dest_ref[tile * n_pairs + r]),
                                  _token_tile(ybuf.at[to_slot], r), sem.at[to_slot]).start(priority=r % 2)

    @pl.when(i == 0)
    def _():
        start_gather(0, 0)

    _wait_token_copies(y_hbm, ybuf.at[slot], sem.at[slot], n_pairs)

    @pl.when(i + 1 < nt)
    def _():
        start_gather(i + 1, 1 - slot)

    w1 = info_ref[:, 4:5]
    w2 = info_ref[:, 5:6]
    sumsq = jnp.zeros((MOE_TMC, 1), F32)
    for s in range(ROW_TILE):
        cols = slice(s * 128, (s + 1) * 128)
        y = (w1 * ybuf[slot, _token_rows(s, MOE_TMC), :]
             + w2 * ybuf[slot, _token_rows(s, MOE_TMC, first_token=MOE_TMC), :])
        x2 = x_ref[:, cols] + mod_ref[0, 5:6, cols] * y
        x2_ref[:, cols] = x2
        sumsq = sumsq + jnp.sum(x2 * x2, axis=-1, keepdims=True)
    d = x2_ref.shape[1]
    o_ref[...] = x2_ref[...] * lax.rsqrt(sumsq / d + EPS) * gfin_ref[...]


def _moe_combine(dest, ysorted, x1, info, mod3, g_final, seq):
    n, d = x1.shape
    per_b = seq // MOE_TMC
    grid_spec = pltpu.PrefetchScalarGridSpec(
        num_scalar_prefetch=1,
        grid=(n // MOE_TMC,),
        in_specs=[pl.BlockSpec(memory_space=pl.ANY),
                  pl.BlockSpec((MOE_TMC, d), lambda i, ds: (i, 0)),
                  pl.BlockSpec((MOE_TMC, ROUTE_COLS), lambda i, ds: (i, 0)),
                  pl.BlockSpec((1, N_MOD, d), lambda i, ds: (i // per_b, 0, 0)),
                  pl.BlockSpec((1, d), lambda i, ds: (0, 0))],
        out_specs=pl.BlockSpec((MOE_TMC, d), lambda i, ds: (i, 0)),
        scratch_shapes=[pltpu.VMEM((2, 2 * MOE_TMC * ROW_TILE, ysorted.shape[1]), F32),
                        pltpu.VMEM((MOE_TMC, d), F32), pltpu.SemaphoreType.DMA((2,))])
    return pl.pallas_call(
        _moe_combine_kernel,
        out_shape=jax.ShapeDtypeStruct((n, d), F32),
        grid_spec=grid_spec,
        compiler_params=pltpu.CompilerParams(dimension_semantics=("arbitrary",),
                                             vmem_limit_bytes=VMEM_LIMIT),
        name="moe_combine",
    )(dest, ysorted, x1, info, mod3, g_final.reshape(1, d))


def _mix_out_and_moe(oat, obt, x2, mod3, g_out_a, g_out_b, w_out_bf, g_ffn, wr_hi, wr_lo, b_route,
                     wg, wu, wd, g_final, seq):
    n, d = x2.shape
    x1, h2, info, infot, counts = _moe_route(oat, obt, x2, mod3, g_out_a, g_out_b, w_out_bf,
                                             g_ffn, wr_hi, wr_lo, b_route, seq)

    e1, e2, rank1, rank2 = [infot[k].astype(jnp.int32) for k in range(4)]
    cnt = counts[0, 32:32 + N_EXPERTS].astype(jnp.int32)
    padded = ((cnt + MOE_TMG - 1) // MOE_TMG) * MOE_TMG
    seg_end = jnp.cumsum(padded)
    expert_ids = jnp.arange(N_EXPERTS, dtype=jnp.int32)

    def seg_start_of(e):
        return jnp.sum(jnp.where(expert_ids[None, :] < e[:, None], padded[None, :], 0), axis=1)

    dest1 = seg_start_of(e1) + rank1
    dest2 = seg_start_of(e2) + rank2
    n_rows = 2 * n + N_EXPERTS * MOE_TMG
    tile_start = jnp.arange(n_rows // MOE_TMG, dtype=jnp.int32) * MOE_TMG
    tile_expert = jnp.minimum(jnp.sum((tile_start[:, None] >= seg_end[None, :]).astype(jnp.int32), axis=1),
                              N_EXPERTS - 1)
    n_used = (seg_end[N_EXPERTS - 1:] // MOE_TMG).astype(jnp.int32)
    dest = jnp.concatenate([dest1.reshape(-1, MOE_TMC), dest2.reshape(-1, MOE_TMC)], axis=1).reshape(-1)

    seg_start = seg_end - padded
    xsorted = _moe_scatter(dest, seg_start + cnt, padded - cnt, n_used, h2, n_rows)
    ysorted = _moe_experts(tile_expert, n_used, xsorted, wg, wu, wd)
    return _moe_combine(dest, ysorted, x1, info, mod3, g_final, seq)


def _layer(x3, c, w_ada, b_ada, g_mix, w_in, b_forget, g_out_a, g_out_b, w_out,
           g_ffn, w_group, b_group, w_router, b_router, w_gate, w_up, w_down, g_final):
    bsz, seq, d = x3.shape
    mod3 = _ada_mod(c, w_ada, b_ada).reshape(bsz, N_MOD, d)

    w_t = w_in.T

    def pad_heads(w, n_heads):
        w = w.reshape(d, n_heads, HEAD_DIM)
        return jnp.concatenate([w, jnp.zeros_like(w)], axis=-1).reshape(d, n_heads * AUG)

    w_ik2 = jnp.concatenate([w_in[:, 1280:1344], w_in[:, 1280:1344]], axis=1)
    w_wf_t = jnp.concatenate([w_t[1344:1352], w_t[2888:2896]], axis=0)
    weights = [w_t[0:512].astype(BF16), pad_heads(w_in[:, 512:640], A_KV_HEADS).astype(BF16),
               w_t[640:768].astype(BF16), w_t[1352:1864].astype(BF16),
               pad_heads(w_in[:, 1864:2376], B_HEADS).astype(BF16), w_t[2376:2888].astype(BF16),
               *_split_bf16(w_t[768:1280]), *_split_bf16(w_ik2), *_split_bf16(w_wf_t)]
    aqt, ak, avt, bqt, bk, bvt, iqt, ik4, wft = _in_proj(x3, mod3, g_mix, weights)

    cumt, kaug = _fox_cum(wft, b_forget, bk)
    obt = _fox_attn(bqt, kaug, bvt, cumt)

    pos = jnp.arange(seq, dtype=jnp.int32)[:, None]
    lane = jnp.arange(AUG, dtype=jnp.int32)[None, :] - HEAD_DIM
    posx = jnp.where((lane >= 0) & (lane < 3), pos >> 7,
                     jnp.where((lane >= 3) & (lane < 6), pos & 127,
                               jnp.where((lane >= 6) & (lane < 9), 1, 0))).astype(BF16)
    oat = _dsa_attn(iqt, ik4, wft, aqt, ak, posx, avt)

    w_r = jnp.concatenate([w_group, jnp.zeros((d, 32 - N_GROUPS), F32),
                           jnp.transpose(w_router, (1, 0, 2)).reshape(d, N_EXPERTS),
                           jnp.zeros((d, ROUTE_COLS - 64), F32)], axis=1)
    b_r = jnp.concatenate([b_group, jnp.zeros((32 - N_GROUPS,), F32), b_router.reshape(-1),
                           jnp.zeros((ROUTE_COLS - 64,), F32)]).reshape(1, ROUTE_COLS)
    wr_hi, wr_lo = _split_bf16(w_r)
    out = _mix_out_and_moe(oat, obt, x3.reshape(bsz * seq, d), mod3, g_out_a, g_out_b,
                           w_out.astype(BF16), g_ffn, wr_hi, wr_lo, b_r, w_gate, w_up, w_down,
                           g_final, seq)
    return out.reshape(bsz, seq, d)


def kernel(x, c, w_ada, b_ada, g_mix, w_in, b_forget, g_out_a, g_out_b, w_out, g_ffn, w_group,
           b_group, w_router, b_router, w_gate, w_up, w_down, g_final):
    depth = w_ada.shape[0]
    assert depth == 1, "final norm is fused into the single layer's MoE kernel"
    return _layer(x, c, w_ada[0], b_ada[0], g_mix[0], w_in[0], b_forget[0], g_out_a[0], g_out_b[0],
                  w_out[0], g_ffn[0], w_group[0], b_group[0], w_router[0], b_router[0], w_gate[0],
                  w_up[0], w_down[0], g_final)
```

```python
import math

import jax
import jax.numpy as jnp
import numpy as np
from jax import lax
from jax.experimental import pallas as pl
from jax.experimental.pallas import tpu as pltpu

F32 = jnp.float32
BF16 = jnp.bfloat16

EPS = 1e-6
A_HEADS = 8
A_KV_HEADS = 2
HEAD_DIM = 64
IDX_HEADS = 8
IDX_DIM = 64
TOPK = 256
B_HEADS = 8
N_GROUPS = 4
EXPERTS_PER_GROUP = 8
N_EXPERTS = N_GROUPS * EXPERTS_PER_GROUP
N_MOD = 6

NEG_BIG = -1e30
LOWEST = float(np.finfo(np.float32).min)
LOG2E = math.log2(math.e)
Q_SCALE = HEAD_DIM ** -0.5 * LOG2E
VMEM_LIMIT = 48 * 1024 * 1024

TQ = 256
KC = 256
AUG = 128
IDX_K = 256
COARSE_STEPS = 12
FOX_LOOKAHEAD = 6
DSA_LOOKAHEAD = 4
assert TQ == KC == TOPK


def _split_bf16(x):
    hi = x.astype(BF16)
    lo = (x - hi.astype(F32)).astype(BF16)
    return hi, lo


def _split3_f32(x):
    p1 = x.astype(BF16).astype(F32)
    r1 = x - p1
    p2 = r1.astype(BF16).astype(F32)
    p3 = (r1 - p2).astype(BF16).astype(F32)
    return p1, p2, p3


def _dot(a, b):
    return jnp.dot(a, b, preferred_element_type=F32)


def _dot_nt(a, b):
    return lax.dot_general(a, b, (((1,), (1,)), ((), ())), preferred_element_type=F32)


def _dot_tn(a, b):
    return lax.dot_general(a, b, (((0,), (0,)), ((), ())), preferred_element_type=F32)


def _dot3(a_hi, a_lo, b_hi, b_lo):
    return _dot(a_hi, b_hi) + _dot(a_lo, b_hi) + _dot(a_hi, b_lo)


def _dot3_nt(a_hi, a_lo, b_hi, b_lo):
    return _dot_nt(a_hi, b_hi) + _dot_nt(a_lo, b_hi) + _dot_nt(a_hi, b_lo)


def _rms(x, g):
    return x * lax.rsqrt(jnp.mean(x * x, axis=-1, keepdims=True) + EPS) * g


def _chunk(kc):
    return pl.ds(pl.multiple_of(kc * KC, KC), KC)


def _key_minus_query():
    return (lax.broadcasted_iota(jnp.int32, (KC, TQ), 0)
            - lax.broadcasted_iota(jnp.int32, (KC, TQ), 1))


def _ada_kernel(c_ref, w_ref, b_ref, o_ref):
    c = c_ref[...]
    s = c * jax.nn.sigmoid(c)
    s_hi, s_lo = _split_bf16(s)
    w_hi, w_lo = _split_bf16(w_ref[...])
    o_ref[...] = _dot3(s_hi, s_lo, w_hi, w_lo) + b_ref[...]


def _ada_mod(c, w_ada, b_ada):
    bsz, d = c.shape
    n = w_ada.shape[1]
    tn = 1024
    return pl.pallas_call(
        _ada_kernel,
        out_shape=jax.ShapeDtypeStruct((bsz, n), F32),
        grid=(n // tn,),
        in_specs=[pl.BlockSpec((bsz, d), lambda j: (0, 0)),
                  pl.BlockSpec((d, tn), lambda j: (0, j)),
                  pl.BlockSpec((1, tn), lambda j: (0, j))],
        out_specs=pl.BlockSpec((bsz, tn), lambda j: (0, j)),
        compiler_params=pltpu.CompilerParams(dimension_semantics=("arbitrary",),
                                             vmem_limit_bytes=VMEM_LIMIT),
        name="ada_mod",
    )(c, w_ada, b_ada.reshape(1, n))


def _in_proj_kernel(x_ref, mod_ref, g_ref,
                    waq_ref, wak_ref, wav_ref, wbq_ref, wbk_ref, wbv_ref,
                    wiqh_ref, wiql_ref, wikh_ref, wikl_ref, wwfh_ref, wwfl_ref,
                    aqt_ref, ak_ref, avt_ref, bqt_ref, bk_ref, bvt_ref, iqt_ref, ik_ref, wft_ref):
    x = x_ref[0]
    h = _rms(x, g_ref[...]) * (1.0 + mod_ref[0, 1:2, :]) + mod_ref[0, 0:1, :]
    h_hi, h_lo = _split_bf16(h)
    aqt_ref[0] = (_dot_nt(waq_ref[...], h_hi) * Q_SCALE).astype(BF16)
    ak_ref[0] = _dot(h_hi, wak_ref[...]).astype(BF16)
    avt_ref[0] = _dot_nt(wav_ref[...], h_hi).astype(BF16)
    bqt_ref[0] = (_dot_nt(wbq_ref[...], h_hi) * Q_SCALE).astype(BF16)
    bk_ref[0] = _dot(h_hi, wbk_ref[...]).astype(BF16)
    bvt_ref[0] = _dot_nt(wbv_ref[...], h_hi).astype(BF16)
    iqt = _dot3_nt(wiqh_ref[...], wiql_ref[...], h_hi, h_lo)
    for hd in range(IDX_HEADS):
        q_hi, q_lo = _split_bf16(iqt[hd * IDX_DIM:(hd + 1) * IDX_DIM, :])
        iqt_ref[0, hd * IDX_K:(hd + 1) * IDX_K, :] = jnp.concatenate([q_hi, q_lo, q_hi, q_lo], axis=0)
    ik2 = _dot3(h_hi, h_lo, wikh_ref[...], wikl_ref[...])
    k_hi, k_lo = _split_bf16(ik2)
    ik_ref[0] = jnp.concatenate([k_hi, k_lo], axis=1)
    wft_ref[0] = _dot3_nt(wwfh_ref[...], wwfl_ref[...], h_hi, h_lo)


def _in_proj(x3, mod3, g_mix, weights):
    bsz, seq, d = x3.shape
    tm = 512
    blk_t = lambda b, i: (b, 0, i)
    blk_r = lambda b, i: (b, i, 0)
    const = lambda b, i: (0, 0)
    ak_w = A_KV_HEADS * AUG
    bk_w = B_HEADS * AUG
    outs = [jax.ShapeDtypeStruct((bsz, 512, seq), BF16), jax.ShapeDtypeStruct((bsz, seq, ak_w), BF16),
            jax.ShapeDtypeStruct((bsz, 128, seq), BF16), jax.ShapeDtypeStruct((bsz, 512, seq), BF16),
            jax.ShapeDtypeStruct((bsz, seq, bk_w), BF16), jax.ShapeDtypeStruct((bsz, 512, seq), BF16),
            jax.ShapeDtypeStruct((bsz, IDX_HEADS * IDX_K, seq), BF16),
            jax.ShapeDtypeStruct((bsz, seq, 256), BF16), jax.ShapeDtypeStruct((bsz, 16, seq), F32)]
    out_specs = [pl.BlockSpec((1, 512, tm), blk_t), pl.BlockSpec((1, tm, ak_w), blk_r),
                 pl.BlockSpec((1, 128, tm), blk_t), pl.BlockSpec((1, 512, tm), blk_t),
                 pl.BlockSpec((1, tm, bk_w), blk_r), pl.BlockSpec((1, 512, tm), blk_t),
                 pl.BlockSpec((1, IDX_HEADS * IDX_K, tm), blk_t),
                 pl.BlockSpec((1, tm, 256), blk_r), pl.BlockSpec((1, 16, tm), blk_t)]
    return pl.pallas_call(
        _in_proj_kernel,
        out_shape=outs,
        grid=(bsz, seq // tm),
        in_specs=[pl.BlockSpec((1, tm, d), blk_r),
                  pl.BlockSpec((1, N_MOD, d), lambda b, i: (b, 0, 0)),
                  pl.BlockSpec((1, d), const)] + [pl.BlockSpec(w.shape, const) for w in weights],
        out_specs=out_specs,
        compiler_params=pltpu.CompilerParams(dimension_semantics=("arbitrary", "arbitrary"),
                                             vmem_limit_bytes=VMEM_LIMIT),
        name="in_proj",
    )(x3, mod3, g_mix.reshape(1, d), *weights)


CB = 256


def _cum_kernel(wft_ref, bfor_ref, k_ref, cumt_ref, kaug_ref):
    seq = wft_ref.shape[2]
    r = lax.broadcasted_iota(jnp.int32, (CB, CB), 0)
    cidx = lax.broadcasted_iota(jnp.int32, (CB, CB), 1)
    tri = jnp.where(r <= cidx, 1.0, 0.0).astype(BF16)
    row128 = lax.broadcasted_iota(jnp.int32, (AUG, CB), 0)
    ones_rows = jnp.where((row128 >= HEAD_DIM + 3) & (row128 < HEAD_DIM + 6), 1.0, 0.0)
    carry = jnp.zeros((8, 1), F32)
    for blk in range(seq // CB):
        cols = slice(blk * CB, (blk + 1) * CB)
        z = wft_ref[0, 8:16, cols] + bfor_ref[...]
        logf = jnp.minimum(z, 0.0) - jnp.log(1.0 + jnp.exp(-jnp.abs(z)))
        p1, p2, p3 = _split3_f32(logf)
        pieces = jnp.concatenate([p1, p2, p3, jnp.zeros_like(p1)], axis=0).astype(BF16)
        parts = _dot(pieces, tri)
        cum = parts[0:8] + parts[8:16] + parts[16:24] + carry
        carry = cum[:, CB - 1:CB]
        cum2 = cum * LOG2E
        cumt_ref[0, :, cols] = cum2
        c1, c2, c3 = _split3_f32(cum2)
        for h in range(B_HEADS):
            spare = jnp.where(row128 == HEAD_DIM, -c1[h:h + 1], ones_rows)
            spare = jnp.where(row128 == HEAD_DIM + 1, -c2[h:h + 1], spare)
            spare = jnp.where(row128 == HEAD_DIM + 2, -c3[h:h + 1], spare)
            lanes = slice(h * AUG, (h + 1) * AUG)
            kaug_ref[0, cols, lanes] = k_ref[0, cols, lanes] + spare.T.astype(BF16)


def _fox_cum(wft, b_forget, bk):
    bsz, _, seq = wft.shape
    nh = B_HEADS
    kw = bk.shape[-1]
    return pl.pallas_call(
        _cum_kernel,
        out_shape=[jax.ShapeDtypeStruct((bsz, nh, seq), F32), jax.ShapeDtypeStruct((bsz, seq, kw), BF16)],
        grid=(bsz,),
        in_specs=[pl.BlockSpec((1, 16, seq), lambda b: (b, 0, 0)),
                  pl.BlockSpec((nh, 1), lambda b: (0, 0)),
                  pl.BlockSpec((1, seq, kw), lambda b: (b, 0, 0))],
        out_specs=[pl.BlockSpec((1, nh, seq), lambda b: (b, 0, 0)),
                   pl.BlockSpec((1, seq, kw), lambda b: (b, 0, 0))],
        compiler_params=pltpu.CompilerParams(dimension_semantics=("arbitrary",),
                                             vmem_limit_bytes=VMEM_LIMIT),
        name="fox_cum",
    )(wft, b_forget.reshape(nh, 1), bk)


def _softmax_init(m_ref, l_ref, acc_ref):
    m_ref[...] = jnp.full(m_ref.shape, NEG_BIG, F32)
    l_ref[...] = jnp.zeros(l_ref.shape, F32)
    acc_ref[...] = jnp.zeros(acc_ref.shape, F32)


def _attend_chunks(items, score_fn, vt_fn, m_ref, l_ref, acc_ref, lookahead):
    ahead = min(lookahead, len(items))
    scores = {i: score_fn(items[i]) for i in range(ahead)}
    for i, item in enumerate(items):
        if i + ahead < len(items):
            scores[i + ahead] = score_fn(items[i + ahead])
        s = scores.pop(i)
        h = item[1]
        m_old = m_ref[h]
        m_new = jnp.maximum(m_old, jnp.max(s, axis=0, keepdims=True))
        alpha = jnp.exp2(m_old - m_new)
        p = jnp.exp2(s - m_new)
        l_ref[h] = alpha * l_ref[h] + jnp.sum(p, axis=0, keepdims=True)
        m_ref[h] = m_new
        rows = slice(h * HEAD_DIM, (h + 1) * HEAD_DIM)
        acc_ref[rows, :] = alpha * acc_ref[rows, :] + _dot(vt_fn(item), p.astype(BF16))


def _softmax_finish(o_ref, n_heads, l_ref, acc_ref):
    for h in range(n_heads):
        rows = slice(h * HEAD_DIM, (h + 1) * HEAD_DIM)
        o_ref[0, rows, :] = acc_ref[rows, :] / l_ref[h]


def _fox_kernel(qt_ref, k_ref, vt_ref, cumt_ref, o_ref, w_ref, m_ref, l_ref, acc_ref):
    qi = pl.program_id(1)
    _softmax_init(m_ref, l_ref, acc_ref)

    row64 = lax.broadcasted_iota(jnp.int32, (AUG - HEAD_DIM, TQ), 0)
    for h in range(B_HEADS):
        c1, c2, c3 = _split3_f32(cumt_ref[0, h:h + 1, :])
        spare = jnp.where(row64 < 3, 1.0, 0.0)
        spare = jnp.where(row64 == 3, c1, spare)
        spare = jnp.where(row64 == 4, c2, spare)
        spare = jnp.where(row64 == 5, c3, spare)
        w_ref[h] = jnp.concatenate([qt_ref[0, h * HEAD_DIM:(h + 1) * HEAD_DIM, :],
                                    spare.astype(BF16)], axis=0)

    def tiles(chunks):
        rows = [_chunk(kc) for kc, _ in chunks]

        def score_fn(item):
            j, h = item
            s = _dot(k_ref[0, rows[j], h * AUG:(h + 1) * AUG], w_ref[h])
            if chunks[j][1]:
                s = jnp.where(_key_minus_query() <= 0, s, NEG_BIG)
            return s

        def vt_fn(item):
            j, h = item
            return vt_ref[0, h * HEAD_DIM:(h + 1) * HEAD_DIM, rows[j]]

        items = [(j, h) for j in range(len(chunks)) for h in range(B_HEADS)]
        _attend_chunks(items, score_fn, vt_fn, m_ref, l_ref, acc_ref, FOX_LOOKAHEAD)

    def body(pair, _):
        tiles([(2 * pair, False), (2 * pair + 1, False)])
        return 0

    lax.fori_loop(0, qi // 2, body, 0)

    @pl.when(qi % 2 == 1)
    def _():
        tiles([(qi - 1, False), (qi, True)])

    @pl.when(qi % 2 == 0)
    def _():
        tiles([(qi, True)])

    _softmax_finish(o_ref, B_HEADS, l_ref, acc_ref)


def _fox_attn(bqt, kaug, bvt, cumt):
    bsz, w, seq = bqt.shape
    blk_t = lambda b, i: (b, 0, i)
    full = lambda b, i: (b, 0, 0)
    return pl.pallas_call(
        _fox_kernel,
        out_shape=jax.ShapeDtypeStruct((bsz, w, seq), F32),
        grid=(bsz, seq // TQ),
        in_specs=[pl.BlockSpec((1, w, TQ), blk_t),
                  pl.BlockSpec((1, seq, B_HEADS * AUG), full),
                  pl.BlockSpec((1, w, seq), full),
                  pl.BlockSpec((1, B_HEADS, TQ), blk_t)],
        out_specs=pl.BlockSpec((1, w, TQ), blk_t),
        scratch_shapes=[pltpu.VMEM((B_HEADS, AUG, TQ), BF16), pltpu.VMEM((B_HEADS, 1, TQ), F32),
                        pltpu.VMEM((B_HEADS, 1, TQ), F32), pltpu.VMEM((B_HEADS * HEAD_DIM, TQ), F32)],
        compiler_params=pltpu.CompilerParams(dimension_semantics=("arbitrary", "arbitrary"),
                                             vmem_limit_bytes=VMEM_LIMIT),
        name="fox_attn",
    )(bqt, kaug, bvt, cumt)


def _bf16_pieces(value):
    pieces = []
    rest = np.float32(value)
    for _ in range(3):
        piece = np.asarray(rest).astype(BF16).astype(np.float32)
        pieces.append(float(piece))
        rest = np.float32(rest - piece)
    return pieces


def _dsa_kernel(iqt_ref, ik_ref, wft_ref, qt_ref, k_ref, posx_ref, vt_ref, o_ref,
                s_ref, thr_ref, w_ref, m_ref, l_ref, acc_ref):
    qi = pl.program_id(1)
    nch = qi + 1
    kmq = _key_minus_query()

    def score_chunk(kc):
        ik = ik_ref[0, _chunk(kc), :]
        acc = jnp.zeros((KC, TQ), F32)
        for h in range(IDX_HEADS):
            d = _dot(ik, iqt_ref[0, h * IDX_K:(h + 1) * IDX_K, :])
            acc = acc + wft_ref[0, h:h + 1, :] * jnp.maximum(d, 0.0)
        acc = jnp.where(acc == 0.0, 0.0, acc)
        causal = kmq <= (qi - kc) * KC
        s_ref[_chunk(kc), :] = jnp.where(causal, acc, -jnp.inf)

    def score_body(pair, _):
        score_chunk(2 * pair)
        score_chunk(2 * pair + 1)
        return 0

    lax.fori_loop(0, nch // 2, score_body, 0)

    @pl.when(nch % 2 == 1)
    def _():
        score_chunk(nch - 1)

    def scan(fn, init):
        def pair_body(pair, c):
            c = fn(2 * pair, s_ref[_chunk(2 * pair), :], c)
            return fn(2 * pair + 1, s_ref[_chunk(2 * pair + 1), :], c)

        c = lax.fori_loop(0, nch // 2, pair_body, init)
        return lax.cond(nch % 2 == 1, lambda c: fn(nch - 1, s_ref[_chunk(nch - 1), :], c),
                        lambda c: c, c)

    @pl.when(qi == 0)
    def _():
        thr_ref[...] = jnp.full(thr_ref.shape, LOWEST, F32)

    @pl.when(qi > 0)
    def _():
        def cmin(x):
            return jnp.min(x, axis=0, keepdims=True)

        def cmax(x):
            return jnp.max(x, axis=0, keepdims=True)

        def csum(x):
            return jnp.sum(x, axis=0, keepdims=True)

        zeros = jnp.zeros((1, TQ), F32)
        pinf = jnp.full((1, TQ), jnp.inf, F32)
        ninf = jnp.full((1, TQ), -jnp.inf, F32)

        def init_fn(_, s, c):
            lo, hi = c
            lo = jnp.minimum(lo, cmin(jnp.where(s > -jnp.inf, s, jnp.inf)))
            hi = jnp.maximum(hi, cmax(s))
            return lo, hi

        lo, hi = scan(init_fn, (pinf, ninf))

        def coarse_step(_, carry):
            lo, hi = carry
            mid = lo + (hi - lo) * 0.5
            cnt = scan(lambda _, s, c: c + csum(jnp.where(s >= mid, 1.0, 0.0)), zeros)
            enough = cnt >= float(TOPK)
            return jnp.where(enough, mid, lo), jnp.where(enough, hi, mid)

        lo, hi = lax.fori_loop(0, COARSE_STEPS, coarse_step, (lo, hi))

        def cond(carry):
            return carry[2] > 0

        def step(carry):
            lo, hi, _ = carry
            mid = lo + (hi - lo) * 0.5
            mid = jnp.where(mid <= lo, hi, mid)

            def fn(_, s, c):
                cnt, a, b = c
                ge = s >= mid
                cnt = cnt + csum(jnp.where(ge, 1.0, 0.0))
                b = jnp.minimum(b, cmin(jnp.where(ge, s, jnp.inf)))
                a = jnp.maximum(a, cmax(jnp.where(ge, -jnp.inf, s)))
                return cnt, a, b

            cnt, a, b = scan(fn, (zeros, ninf, pinf))
            enough = cnt >= float(TOPK)
            new_lo = jnp.where(enough, b, lo)
            new_hi = jnp.where(enough, jnp.where(cnt == float(TOPK), b, hi), a)
            active = jnp.max(jnp.where(new_lo < new_hi, 1, 0))
            return new_lo, new_hi, active

        first_active = jnp.max(jnp.where(lo < hi, 1, 0))
        thr, _, _ = lax.while_loop(cond, step, (lo, hi, first_active))

        def count_fn(_, s, c):
            n_gt, n_ge = c
            return (n_gt + csum(jnp.where(s > thr, 1.0, 0.0)),
                    n_ge + csum(jnp.where(s >= thr, 1.0, 0.0)))

        n_gt, n_ge = scan(count_fn, (zeros, zeros))
        thr_ref[...] = thr

        @pl.when(jnp.max(jnp.where(n_ge > float(TOPK), 1, 0)) > 0)
        def _():
            need = float(TOPK) - n_gt
            lower = jnp.where(lax.broadcasted_iota(jnp.int32, (KC, KC), 1)
                              < lax.broadcasted_iota(jnp.int32, (KC, KC), 0), 1.0, 0.0).astype(BF16)

            def sel_fn(kc, s, run):
                eq = s == thr
                eqf = jnp.where(eq, 1.0, 0.0)
                before = _dot(lower, eqf.astype(BF16)) + run
                sel = (s > thr) | (eq & (before < need))
                s_ref[_chunk(kc), :] = jnp.where(sel, 0.0, NEG_BIG)
                return run + csum(eqf)

            scan(sel_fn, zeros)
            thr_ref[...] = jnp.full(thr_ref.shape, 0.5 * NEG_BIG, F32)

    _softmax_init(m_ref, l_ref, acc_ref)
    rep = A_HEADS // A_KV_HEADS
    row64 = lax.broadcasted_iota(jnp.int32, (AUG - HEAD_DIM, TQ), 0)
    qpos = (qi * TQ + lax.broadcasted_iota(jnp.int32, (AUG - HEAD_DIM, TQ), 1)).astype(F32)
    for h in range(A_HEADS):
        slope = np.float32(2.0 ** (-8.0 * (h + 1) / A_HEADS) * LOG2E)
        u1, u2, u3 = _split3_f32(-slope * qpos)
        spare = jnp.zeros((AUG - HEAD_DIM, TQ), F32)
        for p, s_p in enumerate(_bf16_pieces(slope)):
            spare = jnp.where(row64 == p, 128.0 * s_p, spare)
            spare = jnp.where(row64 == 3 + p, s_p, spare)
        spare = jnp.where(row64 == 6, u1, spare)
        spare = jnp.where(row64 == 7, u2, spare)
        spare = jnp.where(row64 == 8, u3, spare)
        w_ref[h] = jnp.concatenate([qt_ref[0, h * HEAD_DIM:(h + 1) * HEAD_DIM, :],
                                    spare.astype(BF16)], axis=0)

    def attn_tiles(chunks):
        rows = [_chunk(kc) for kc in chunks]
        lhs = [[k_ref[0, r, g * AUG:(g + 1) * AUG] + posx_ref[r, :] for g in range(A_KV_HEADS)]
               for r in rows]
        bias = [jnp.where(s_ref[r, :] >= thr_ref[...], 0.0, NEG_BIG) for r in rows]

        def score_fn(item):
            j, h = item
            return _dot(lhs[j][h // rep], w_ref[h]) + bias[j]

        def vt_fn(item):
            j, h = item
            g = h // rep
            return vt_ref[0, g * HEAD_DIM:(g + 1) * HEAD_DIM, rows[j]]

        items = [(j, h) for j in range(len(chunks)) for h in range(A_HEADS)]
        _attend_chunks(items, score_fn, vt_fn, m_ref, l_ref, acc_ref, DSA_LOOKAHEAD)

    def attn_body(pair, _):
        attn_tiles([2 * pair, 2 * pair + 1])
        return 0

    lax.fori_loop(0, nch // 2, attn_body, 0)

    @pl.when(nch % 2 == 1)
    def _():
        attn_tiles([nch - 1])

    _softmax_finish(o_ref, A_HEADS, l_ref, acc_ref)


def _dsa_attn(iqt, ik4, wft, aqt, ak, posx, avt):
    bsz, w, seq = aqt.shape
    blk_t = lambda b, i: (b, 0, i)
    full = lambda b, i: (b, 0, 0)
    return pl.pallas_call(
        _dsa_kernel,
        out_shape=jax.ShapeDtypeStruct((bsz, w, seq), F32),
        grid=(bsz, seq // TQ),
        in_specs=[pl.BlockSpec((1, IDX_HEADS * IDX_K, TQ), blk_t),
                  pl.BlockSpec((1, seq, 256), full),
                  pl.BlockSpec((1, 16, TQ), blk_t),
                  pl.BlockSpec((1, w, TQ), blk_t),
                  pl.BlockSpec((1, seq, A_KV_HEADS * AUG), full),
                  pl.BlockSpec((seq, AUG), lambda b, i: (0, 0)),
                  pl.BlockSpec((1, 128, seq), full)],
        out_specs=pl.BlockSpec((1, w, TQ), blk_t),
        scratch_shapes=[pltpu.VMEM((seq, TQ), F32), pltpu.VMEM((1, TQ), F32),
                        pltpu.VMEM((A_HEADS, AUG, TQ), BF16),
                        pltpu.VMEM((A_HEADS, 1, TQ), F32), pltpu.VMEM((A_HEADS, 1, TQ), F32),
                        pltpu.VMEM((A_HEADS * HEAD_DIM, TQ), F32)],
        compiler_params=pltpu.CompilerParams(dimension_semantics=("arbitrary", "arbitrary"),
                                             vmem_limit_bytes=VMEM_LIMIT),
        name="dsa_attn",
    )(iqt, ik4, wft, aqt, ak, posx, avt)


def _rms_cols(xt, g_col):
    return xt * lax.rsqrt(jnp.mean(xt * xt, axis=0, keepdims=True) + EPS) * g_col


MOE_TMG = 256
MOE_TMC = 256
ROW_TILE = 8
ROUTE_COLS = 128


def _first_index_of_max(vals, lane, big):
    m = jnp.max(vals, axis=-1, keepdims=True)
    idx = jnp.min(jnp.where(vals == m, lane, big), axis=-1, keepdims=True)
    return m, idx


def _route(h, wr_hi_ref, wr_lo_ref, br_ref):
    h_hi, h_lo = _split_bf16(h)
    logits = _dot3(h_hi, h_lo, wr_hi_ref[...], wr_lo_ref[...]) + br_ref[...]
    lane = lax.broadcasted_iota(jnp.int32, logits.shape, 1)
    ninf = -jnp.inf
    gl = jnp.where(lane < N_GROUPS, logits, ninf)
    gmax, gsel = _first_index_of_max(gl, lane, 1 << 20)
    p_group = 1.0 / jnp.sum(jnp.exp(gl - gmax), axis=-1, keepdims=True)
    base = 32 + gsel * EXPERTS_PER_GROUP
    el = jnp.where((lane >= base) & (lane < base + EXPERTS_PER_GROUP), logits, ninf)
    v1, i1 = _first_index_of_max(el, lane, 1 << 20)
    el2 = jnp.where(lane == i1, ninf, el)
    v2, i2 = _first_index_of_max(el2, lane, 1 << 20)
    e2 = jnp.exp(v2 - v1)
    w1 = 1.0 / (1.0 + e2)
    w2 = e2 / (1.0 + e2)
    return i1, i2, w1 * p_group, w2 * p_group


def _token_rows(s, n_tokens, first_token=0):
    return pl.ds(first_token * ROW_TILE + s, n_tokens, stride=ROW_TILE)


def _to_token_tiles(ref, x):
    for s in range(ROW_TILE):
        ref[_token_rows(s, x.shape[0]), :] = x[:, s * 128:(s + 1) * 128]


def _token_tile(ref, t):
    start = t * ROW_TILE if isinstance(t, int) else pl.multiple_of(t * ROW_TILE, ROW_TILE)
    return ref.at[pl.ds(start, ROW_TILE), :]


def _moe_route_kernel(oat_ref, obt_ref, x_ref, mod_ref, ga_ref, gb_ref, wo_ref,
                      gf_ref, wr_hi_ref, wr_lo_ref, br_ref,
                      x1_ref, h_ref, info_ref, infot_ref, cnt_ref, run_ref):
    @pl.when(pl.program_id(0) == 0)
    def _():
        run_ref[...] = jnp.zeros_like(run_ref)

    oa = _rms_cols(oat_ref[0], ga_ref[...]).astype(BF16)
    ob = _rms_cols(obt_ref[0], gb_ref[...]).astype(BF16)
    y = _dot_tn(oa, wo_ref[0:512, :]) + _dot_tn(ob, wo_ref[512:1024, :])
    x1 = x_ref[...] + mod_ref[0, 2:3, :] * y
    x1_ref[...] = x1

    h = _rms(x1, gf_ref[...]) * (1.0 + mod_ref[0, 4:5, :]) + mod_ref[0, 3:4, :]
    _to_token_tiles(h_ref, h)
    i1, i2, w1, w2 = _route(h, wr_hi_ref, wr_lo_ref, br_ref)
    tm = h.shape[0]
    lane = lax.broadcasted_iota(jnp.int32, (tm, ROUTE_COLS), 1)
    picked = jnp.where((lane == i1) | (lane == i2), 1.0, 0.0)
    earlier = jnp.where(lax.broadcasted_iota(jnp.int32, (tm, tm), 1)
                        < lax.broadcasted_iota(jnp.int32, (tm, tm), 0), 1.0, 0.0).astype(BF16)
    before = _dot(earlier, picked.astype(BF16)) + run_ref[...]
    rank1 = jnp.sum(jnp.where(lane == i1, before, 0.0), axis=-1, keepdims=True)
    rank2 = jnp.sum(jnp.where(lane == i2, before, 0.0), axis=-1, keepdims=True)
    run_ref[...] += jnp.sum(picked, axis=0, keepdims=True)
    cnt_ref[...] = run_ref[...]
    info = jnp.where(lane == 0, (i1 - 32).astype(F32), 0.0)
    info = jnp.where(lane == 1, (i2 - 32).astype(F32), info)
    info = jnp.where(lane == 2, rank1, info)
    info = jnp.where(lane == 3, rank2, info)
    info = jnp.where(lane == 4, w1, info)
    info = jnp.where(lane == 5, w2, info)
    info_ref[...] = info
    infot_ref[...] = info.T[0:8, :]


def _moe_route(oat, obt, x2, mod3, g_out_a, g_out_b, w_out_bf, g_ffn, wr_hi, wr_lo, b_route, seq):
    n, d = x2.shape
    tm = 512
    per_b = seq // tm
    row = lambda i: (i, 0)
    const = lambda i: (0, 0)
    blk_t = lambda i: (i // per_b, 0, i % per_b)
    return pl.pallas_call(
        _moe_route_kernel,
        out_shape=[jax.ShapeDtypeStruct((n, d), F32),
                   jax.ShapeDtypeStruct((n * ROW_TILE, d // ROW_TILE), F32),
                   jax.ShapeDtypeStruct((n, ROUTE_COLS), F32), jax.ShapeDtypeStruct((8, n), F32),
                   jax.ShapeDtypeStruct((1, ROUTE_COLS), F32)],
        grid=(n // tm,),
        in_specs=[pl.BlockSpec((1, 512, tm), blk_t), pl.BlockSpec((1, 512, tm), blk_t),
                  pl.BlockSpec((tm, d), row),
                  pl.BlockSpec((1, N_MOD, d), lambda i: (i // per_b, 0, 0)),
                  pl.BlockSpec((512, 1), const), pl.BlockSpec((512, 1), const),
                  pl.BlockSpec((d, d), const),
                  pl.BlockSpec((1, d), const),
                  pl.BlockSpec((d, ROUTE_COLS), const), pl.BlockSpec((d, ROUTE_COLS), const),
                  pl.BlockSpec((1, ROUTE_COLS), const)],
        out_specs=[pl.BlockSpec((tm, d), row),
                   pl.BlockSpec((tm * ROW_TILE, d // ROW_TILE), row),
                   pl.BlockSpec((tm, ROUTE_COLS), row), pl.BlockSpec((8, tm), lambda i: (0, i)),
                   pl.BlockSpec((1, ROUTE_COLS), const)],
        scratch_shapes=[pltpu.VMEM((1, ROUTE_COLS), F32)],
        compiler_params=pltpu.CompilerParams(dimension_semantics=("arbitrary",),
                                             vmem_limit_bytes=VMEM_LIMIT),
        name="mix_out_moe_route",
    )(oat, obt, x2, mod3, g_out_a.reshape(-1, 1), g_out_b.reshape(-1, 1), w_out_bf,
      g_ffn.reshape(1, d), wr_hi, wr_lo, b_route)


def _wait_token_copies(src_hbm, dst, sem, n_tokens):
    pltpu.make_async_copy(src_hbm.at[pl.ds(0, n_tokens * ROW_TILE), :], dst, sem).wait()


def _moe_scatter_kernel(dest_ref, pad_start_ref, pad_len_ref, nu_ref, h_ref, xs_hbm, zero_ref, sem):
    i = pl.program_id(0)
    n_pairs = 2 * MOE_TMC

    @pl.when(i == 0)
    def _():
        zero_ref[...] = jnp.zeros_like(zero_ref)

        def pad_copies(e):
            n = pad_len_ref[e]
            size = MOE_TMG // 2
            while size >= 1:
                first = pad_start_ref[e] + (n & ~(2 * size - 1))
                copy = pltpu.make_async_copy(
                    zero_ref.at[pl.ds(0, size * ROW_TILE), :],
                    xs_hbm.at[pl.ds(pl.multiple_of(first * ROW_TILE, ROW_TILE), size * ROW_TILE), :],
                    sem.at[1])
                yield (n & size) != 0, copy
                size //= 2

        for e in range(N_EXPERTS):
            for present, copy in pad_copies(e):
                pl.when(present)(copy.start)
        for e in range(N_EXPERTS):
            for present, copy in pad_copies(e):
                pl.when(present)(copy.wait)

        def row_tile(t):
            return xs_hbm.at[pl.ds(pl.multiple_of(t * (MOE_TMG * ROW_TILE), MOE_TMG * ROW_TILE),
                                   MOE_TMG * ROW_TILE), :]

        n_tiles = xs_hbm.shape[0] // (MOE_TMG * ROW_TILE)

        def fill_tile(t, _):
            pltpu.make_async_copy(zero_ref, row_tile(t), sem.at[1]).start()
            return 0

        def drain_tile(t, _):
            pltpu.make_async_copy(zero_ref, row_tile(0), sem.at[1]).wait()
            return 0

        lax.fori_loop(nu_ref[0], n_tiles, fill_tile, 0)
        lax.fori_loop(nu_ref[0], n_tiles, drain_tile, 0)

    for r in range(MOE_TMC):
        for j in range(2):
            pltpu.make_async_copy(_token_tile(h_ref, r),
                                  _token_tile(xs_hbm, dest_ref[i * n_pairs + j * MOE_TMC + r]),
                                  sem.at[0]).start(priority=j)
    _wait_token_copies(xs_hbm, xs_hbm.at[pl.ds(0, n_pairs * ROW_TILE), :], sem.at[0], n_pairs)


def _moe_scatter(dest, pad_start, pad_len, n_used, h2, n_rows):
    n = h2.shape[0] // ROW_TILE
    grid_spec = pltpu.PrefetchScalarGridSpec(
        num_scalar_prefetch=4,
        grid=(n // MOE_TMC,),
        in_specs=[pl.BlockSpec((MOE_TMC * ROW_TILE, h2.shape[1]), lambda i, *_: (i, 0))],
        out_specs=pl.BlockSpec(memory_space=pl.ANY),
        scratch_shapes=[pltpu.VMEM((MOE_TMG * ROW_TILE, h2.shape[1]), F32),
                        pltpu.SemaphoreType.DMA((2,))])
    return pl.pallas_call(
        _moe_scatter_kernel,
        out_shape=jax.ShapeDtypeStruct((n_rows * ROW_TILE, h2.shape[1]), F32),
        grid_spec=grid_spec,
        compiler_params=pltpu.CompilerParams(dimension_semantics=("arbitrary",),
                                             vmem_limit_bytes=VMEM_LIMIT),
        name="moe_scatter",
    )(dest, pad_start, pad_len, n_used, h2)


def _moe_expert_kernel(te_ref, nu_ref, x_ref, wg_ref, wu_ref, wd_ref, y_ref,
                       xs_ref, wgb_ref, wub_ref, wdb_ref):
    i = pl.program_id(0)
    used = i < nu_ref[0]
    new_expert = jnp.logical_or(i == 0, te_ref[i] != te_ref[jnp.maximum(i - 1, 0)])

    @pl.when(jnp.logical_and(used, new_expert))
    def _():
        wgb_ref[...] = wg_ref[0].astype(BF16)
        wub_ref[...] = wu_ref[0].astype(BF16)
        wdb_ref[...] = wd_ref[0].astype(BF16)

    @pl.when(used)
    def _():
        for s in range(ROW_TILE):
            xs_ref[:, s * 128:(s + 1) * 128] = x_ref[_token_rows(s, MOE_TMG), :].astype(BF16)
        x = xs_ref[...]
        hg = _dot(x, wgb_ref[...])
        hu = _dot(x, wub_ref[...])
        a = hg * jax.nn.sigmoid(hg) * hu
        _to_token_tiles(y_ref, _dot(a.astype(BF16), wdb_ref[...]))

    @pl.when(i >= nu_ref[0])
    def _():
        y_ref[...] = jnp.zeros_like(y_ref)


def _moe_experts(tile_expert, n_used, xsorted, wg, wu, wd):
    n_rows = xsorted.shape[0] // ROW_TILE
    d, ff = wg.shape[1], wg.shape[2]
    tile = (MOE_TMG * ROW_TILE, xsorted.shape[1])
    grid_spec = pltpu.PrefetchScalarGridSpec(
        num_scalar_prefetch=2,
        grid=(n_rows // MOE_TMG,),
        in_specs=[pl.BlockSpec(tile, lambda i, te, nu: (jnp.minimum(i, nu[0] - 1), 0)),
                  pl.BlockSpec((1, d, ff), lambda i, te, nu: (te[i], 0, 0)),
                  pl.BlockSpec((1, d, ff), lambda i, te, nu: (te[i], 0, 0)),
                  pl.BlockSpec((1, ff, d), lambda i, te, nu: (te[i], 0, 0))],
        out_specs=pl.BlockSpec(tile, lambda i, te, nu: (i, 0)),
        scratch_shapes=[pltpu.VMEM((MOE_TMG, d), BF16), pltpu.VMEM((d, ff), BF16),
                        pltpu.VMEM((d, ff), BF16), pltpu.VMEM((ff, d), BF16)])
    return pl.pallas_call(
        _moe_expert_kernel,
        out_shape=jax.ShapeDtypeStruct(xsorted.shape, F32),
        grid_spec=grid_spec,
        compiler_params=pltpu.CompilerParams(dimension_semantics=("arbitrary",),
                                             vmem_limit_bytes=VMEM_LIMIT),
        name="moe_experts",
    )(tile_expert, n_used, xsorted, wg, wu, wd)


def _moe_combine_kernel(dest_ref, y_hbm, x_ref, info_ref, mod_ref, gfin_ref, o_ref, ybuf, x2_ref, sem):
    i = pl.program_id(0)
    nt = pl.num_programs(0)
    slot = lax.rem(i, 2)
    n_pairs = 2 * MOE_TMC

    def start_gather(tile, to_slot):
        for r in range(n_pairs):
            pltpu.make_async_copy(_token_tile(y_hbm, dest_ref[tile * n_pairs + r]),
                                  _token_tile(ybuf.at[to_slot], r), sem.at[to_slot]).start(priority=r % 2)

    @pl.when(i == 0)
    def _():
        start_gather(0, 0)

    _wait_token_copies(y_hbm, ybuf.at[slot], sem.at[slot], n_pairs)

    @pl.when(i + 1 < nt)
    def _():
        start_gather(i + 1, 1 - slot)

    w1 = info_ref[:, 4:5]
    w2 = info_ref[:, 5:6]
    sumsq = jnp.zeros((MOE_TMC, 1), F32)
    for s in range(ROW_TILE):
        cols = slice(s * 128, (s + 1) * 128)
        y = (w1 * ybuf[slot, _token_rows(s, MOE_TMC), :]
             + w2 * ybuf[slot, _token_rows(s, MOE_TMC, first_token=MOE_TMC), :])
        x2 = x_ref[:, cols] + mod_ref[0, 5:6, cols] * y
        x2_ref[:, cols] = x2
        sumsq = sumsq + jnp.sum(x2 * x2, axis=-1, keepdims=True)
    d = x2_ref.shape[1]
    o_ref[...] = x2_ref[...] * lax.rsqrt(sumsq / d + EPS) * gfin_ref[...]


def _moe_combine(dest, ysorted, x1, info, mod3, g_final, seq):
    n, d = x1.shape
    per_b = seq // MOE_TMC
    grid_spec = pltpu.PrefetchScalarGridSpec(
        num_scalar_prefetch=1,
        grid=(n // MOE_TMC,),
        in_specs=[pl.BlockSpec(memory_space=pl.ANY),
                  pl.BlockSpec((MOE_TMC, d), lambda i, ds: (i, 0)),
                  pl.BlockSpec((MOE_TMC, ROUTE_COLS), lambda i, ds: (i, 0)),
                  pl.BlockSpec((1, N_MOD, d), lambda i, ds: (i // per_b, 0, 0)),
                  pl.BlockSpec((1, d), lambda i, ds: (0, 0))],
        out_specs=pl.BlockSpec((MOE_TMC, d), lambda i, ds: (i, 0)),
        scratch_shapes=[pltpu.VMEM((2, 2 * MOE_TMC * ROW_TILE, ysorted.shape[1]), F32),
                        pltpu.VMEM((MOE_TMC, d), F32), pltpu.SemaphoreType.DMA((2,))])
    return pl.pallas_call(
        _moe_combine_kernel,
        out_shape=jax.ShapeDtypeStruct((n, d), F32),
        grid_spec=grid_spec,
        compiler_params=pltpu.CompilerParams(dimension_semantics=("arbitrary",),
                                             vmem_limit_bytes=VMEM_LIMIT),
        name="moe_combine",
    )(dest, ysorted, x1, info, mod3, g_final.reshape(1, d))


def _mix_out_and_moe(oat, obt, x2, mod3, g_out_a, g_out_b, w_out_bf, g_ffn, wr_hi, wr_lo, b_route,
                     wg, wu, wd, g_final, seq):
    n, d = x2.shape
    x1, h2, info, infot, counts = _moe_route(oat, obt, x2, mod3, g_out_a, g_out_b, w_out_bf,
                                             g_ffn, wr_hi, wr_lo, b_route, seq)

    e1, e2, rank1, rank2 = [infot[k].astype(jnp.int32) for k in range(4)]
    cnt = counts[0, 32:32 + N_EXPERTS].astype(jnp.int32)
    padded = ((cnt + MOE_TMG - 1) // MOE_TMG) * MOE_TMG
    seg_end = jnp.cumsum(padded)
    expert_ids = jnp.arange(N_EXPERTS, dtype=jnp.int32)

    def seg_start_of(e):
        return jnp.sum(jnp.where(expert_ids[None, :] < e[:, None], padded[None, :], 0), axis=1)

    dest1 = seg_start_of(e1) + rank1
    dest2 = seg_start_of(e2) + rank2
    n_rows = 2 * n + N_EXPERTS * MOE_TMG
    tile_start = jnp.arange(n_rows // MOE_TMG, dtype=jnp.int32) * MOE_TMG
    tile_expert = jnp.minimum(jnp.sum((tile_start[:, None] >= seg_end[None, :]).astype(jnp.int32), axis=1),
                              N_EXPERTS - 1)
    n_used = (seg_end[N_EXPERTS - 1:] // MOE_TMG).astype(jnp.int32)
    dest = jnp.concatenate([dest1.reshape(-1, MOE_TMC), dest2.reshape(-1, MOE_TMC)], axis=1).reshape(-1)

    seg_start = seg_end - padded
    xsorted = _moe_scatter(dest, seg_start + cnt, padded - cnt, n_used, h2, n_rows)
    ysorted = _moe_experts(tile_expert, n_used, xsorted, wg, wu, wd)
    return _moe_combine(dest, ysorted, x1, info, mod3, g_final, seq)


def _layer(x3, c, w_ada, b_ada, g_mix, w_in, b_forget, g_out_a, g_out_b, w_out,
           g_ffn, w_group, b_group, w_router, b_router, w_gate, w_up, w_down, g_final):
    bsz, seq, d = x3.shape
    mod3 = _ada_mod(c, w_ada, b_ada).reshape(bsz, N_MOD, d)

    w_t = w_in.T

    def pad_heads(w, n_heads):
        w = w.reshape(d, n_heads, HEAD_DIM)
        return jnp.concatenate([w, jnp.zeros_like(w)], axis=-1).reshape(d, n_heads * AUG)

    w_ik2 = jnp.concatenate([w_in[:, 1280:1344], w_in[:, 1280:1344]], axis=1)
    w_wf_t = jnp.concatenate([w_t[1344:1352], w_t[2888:2896]], axis=0)
    weights = [w_t[0:512].astype(BF16), pad_heads(w_in[:, 512:640], A_KV_HEADS).astype(BF16),
               w_t[640:768].astype(BF16), w_t[1352:1864].astype(BF16),
               pad_heads(w_in[:, 1864:2376], B_HEADS).astype(BF16), w_t[2376:2888].astype(BF16),
               *_split_bf16(w_t[768:1280]), *_split_bf16(w_ik2), *_split_bf16(w_wf_t)]
    aqt, ak, avt, bqt, bk, bvt, iqt, ik4, wft = _in_proj(x3, mod3, g_mix, weights)

    cumt, kaug = _fox_cum(wft, b_forget, bk)
    obt = _fox_attn(bqt, kaug, bvt, cumt)

    pos = jnp.arange(seq, dtype=jnp.int32)[:, None]
    lane = jnp.arange(AUG, dtype=jnp.int32)[None, :] - HEAD_DIM
    posx = jnp.where((lane >= 0) & (lane < 3), pos >> 7,
                     jnp.where((lane >= 3) & (lane < 6), pos & 127,
                               jnp.where((lane >= 6) & (lane < 9), 1, 0))).astype(BF16)
    oat = _dsa_attn(iqt, ik4, wft, aqt, ak, posx, avt)

    w_r = jnp.concatenate([w_group, jnp.zeros((d, 32 - N_GROUPS), F32),
                           jnp.transpose(w_router, (1, 0, 2)).reshape(d, N_EXPERTS),
                           jnp.zeros((d, ROUTE_COLS - 64), F32)], axis=1)
    b_r = jnp.concatenate([b_group, jnp.zeros((32 - N_GROUPS,), F32), b_router.reshape(-1),
                           jnp.zeros((ROUTE_COLS - 64,), F32)]).reshape(1, ROUTE_COLS)
    wr_hi, wr_lo = _split_bf16(w_r)
    out = _mix_out_and_moe(oat, obt, x3.reshape(bsz * seq, d), mod3, g_out_a, g_out_b,
                           w_out.astype(BF16), g_ffn, wr_hi, wr_lo, b_r, w_gate, w_up, w_down,
                           g_final, seq)
    return out.reshape(bsz, seq, d)


def kernel(x, c, w_ada, b_ada, g_mix, w_in, b_forget, g_out_a, g_out_b, w_out, g_ffn, w_group,
           b_group, w_router, b_router, w_gate, w_up, w_down, g_final):
    depth = w_ada.shape[0]
    assert depth == 1, "final norm is fused into the single layer's MoE kernel"
    return _layer(x, c, w_ada[0], b_ada[0], g_mix[0], w_in[0], b_forget[0], g_out_a[0], g_out_b[0],
                  w_out[0], g_ffn[0], w_group[0], b_group[0], w_router[0], b_router[0], w_gate[0],
                  w_up[0], w_down[0], g_final)
```

```python
import math

import jax
import jax.numpy as jnp
import numpy as np
from jax import lax
from jax.experimental import pallas as pl
from jax.experimental.pallas import tpu as pltpu

F32 = jnp.float32
BF16 = jnp.bfloat16

EPS = 1e-6
A_HEADS = 8
A_KV_HEADS = 2
HEAD_DIM = 64
IDX_HEADS = 8
IDX_DIM = 64
TOPK = 256
B_HEADS = 8
N_GROUPS = 4
EXPERTS_PER_GROUP = 8
N_EXPERTS = N_GROUPS * EXPERTS_PER_GROUP
N_MOD = 6

NEG_BIG = -1e30
LOWEST = float(np.finfo(np.float32).min)
LOG2E = math.log2(math.e)
Q_SCALE = HEAD_DIM ** -0.5 * LOG2E
VMEM_LIMIT = 48 * 1024 * 1024

TQ = 256
KC = 256
AUG = 128
IDX_K = IDX_DIM
COARSE_STEPS = 12
FOX_LOOKAHEAD = 6
DSA_LOOKAHEAD = 4
assert TQ == KC == TOPK


def _split_bf16(x):
    hi = x.astype(BF16)
    lo = (x - hi.astype(F32)).astype(BF16)
    return hi, lo


def _split3_f32(x):
    p1 = x.astype(BF16).astype(F32)
    r1 = x - p1
    p2 = r1.astype(BF16).astype(F32)
    p3 = (r1 - p2).astype(BF16).astype(F32)
    return p1, p2, p3


def _dot(a, b):
    return jnp.dot(a, b, preferred_element_type=F32)


def _dot_nt(a, b):
    return lax.dot_general(a, b, (((1,), (1,)), ((), ())), preferred_element_type=F32)


def _dot_tn(a, b):
    return lax.dot_general(a, b, (((0,), (0,)), ((), ())), preferred_element_type=F32)


def _dot3(a_hi, a_lo, b_hi, b_lo):
    return _dot(a_hi, b_hi) + _dot(a_lo, b_hi) + _dot(a_hi, b_lo)


def _dot3_nt(a_hi, a_lo, b_hi, b_lo):
    return _dot_nt(a_hi, b_hi) + _dot_nt(a_lo, b_hi) + _dot_nt(a_hi, b_lo)


def _rms(x, g):
    return x * lax.rsqrt(jnp.mean(x * x, axis=-1, keepdims=True) + EPS) * g


def _chunk(kc):
    return pl.ds(pl.multiple_of(kc * KC, KC), KC)


def _key_minus_query():
    return (lax.broadcasted_iota(jnp.int32, (KC, TQ), 0)
            - lax.broadcasted_iota(jnp.int32, (KC, TQ), 1))


def _ada_kernel(c_ref, w_ref, b_ref, o_ref):
    c = c_ref[...]
    s = c * jax.nn.sigmoid(c)
    s_hi, s_lo = _split_bf16(s)
    w_hi, w_lo = _split_bf16(w_ref[...])
    o_ref[...] = _dot3(s_hi, s_lo, w_hi, w_lo) + b_ref[...]


def _ada_mod(c, w_ada, b_ada):
    bsz, d = c.shape
    n = w_ada.shape[1]
    tn = 1024
    return pl.pallas_call(
        _ada_kernel,
        out_shape=jax.ShapeDtypeStruct((bsz, n), F32),
        grid=(n // tn,),
        in_specs=[pl.BlockSpec((bsz, d), lambda j: (0, 0)),
                  pl.BlockSpec((d, tn), lambda j: (0, j)),
                  pl.BlockSpec((1, tn), lambda j: (0, j))],
        out_specs=pl.BlockSpec((bsz, tn), lambda j: (0, j)),
        compiler_params=pltpu.CompilerParams(dimension_semantics=("arbitrary",),
                                             vmem_limit_bytes=VMEM_LIMIT),
        name="ada_mod",
    )(c, w_ada, b_ada.reshape(1, n))


def _in_proj_kernel(x_ref, mod_ref, g_ref,
                    waq_ref, wak_ref, wav_ref, wbq_ref, wbk_ref, wbv_ref,
                    wiq_ref, wik_ref, wwfh_ref, wwfl_ref,
                    aqt_ref, ak_ref, avt_ref, bqt_ref, bk_ref, bvt_ref, iqt_ref, ik_ref, wft_ref):
    x = x_ref[0]
    h = _rms(x, g_ref[...]) * (1.0 + mod_ref[0, 1:2, :]) + mod_ref[0, 0:1, :]
    h_hi, h_lo = _split_bf16(h)
    aqt_ref[0] = (_dot_nt(waq_ref[...], h_hi) * Q_SCALE).astype(BF16)
    ak_ref[0] = _dot(h_hi, wak_ref[...]).astype(BF16)
    avt_ref[0] = _dot_nt(wav_ref[...], h_hi).astype(BF16)
    bqt_ref[0] = (_dot_nt(wbq_ref[...], h_hi) * Q_SCALE).astype(BF16)
    bk_ref[0] = _dot(h_hi, wbk_ref[...]).astype(BF16)
    bvt_ref[0] = _dot_nt(wbv_ref[...], h_hi).astype(BF16)
    iqt_ref[0] = _dot_nt(wiq_ref[...], h_hi).astype(BF16)
    ik_ref[0] = _dot(h_hi, wik_ref[...]).astype(BF16)
    wft_ref[0] = _dot3_nt(wwfh_ref[...], wwfl_ref[...], h_hi, h_lo)


def _in_proj(x3, mod3, g_mix, weights):
    bsz, seq, d = x3.shape
    tm = 512
    blk_t = lambda b, i: (b, 0, i)
    blk_r = lambda b, i: (b, i, 0)
    const = lambda b, i: (0, 0)
    ak_w = A_KV_HEADS * AUG
    bk_w = B_HEADS * AUG
    outs = [jax.ShapeDtypeStruct((bsz, 512, seq), BF16), jax.ShapeDtypeStruct((bsz, seq, ak_w), BF16),
            jax.ShapeDtypeStruct((bsz, 128, seq), BF16), jax.ShapeDtypeStruct((bsz, 512, seq), BF16),
            jax.ShapeDtypeStruct((bsz, seq, bk_w), BF16), jax.ShapeDtypeStruct((bsz, 512, seq), BF16),
            jax.ShapeDtypeStruct((bsz, IDX_HEADS * IDX_K, seq), BF16),
            jax.ShapeDtypeStruct((bsz, seq, IDX_K), BF16), jax.ShapeDtypeStruct((bsz, 16, seq), F32)]
    out_specs = [pl.BlockSpec((1, 512, tm), blk_t), pl.BlockSpec((1, tm, ak_w), blk_r),
                 pl.BlockSpec((1, 128, tm), blk_t), pl.BlockSpec((1, 512, tm), blk_t),
                 pl.BlockSpec((1, tm, bk_w), blk_r), pl.BlockSpec((1, 512, tm), blk_t),
                 pl.BlockSpec((1, IDX_HEADS * IDX_K, tm), blk_t),
                 pl.BlockSpec((1, tm, IDX_K), blk_r), pl.BlockSpec((1, 16, tm), blk_t)]
    return pl.pallas_call(
        _in_proj_kernel,
        out_shape=outs,
        grid=(bsz, seq // tm),
        in_specs=[pl.BlockSpec((1, tm, d), blk_r),
                  pl.BlockSpec((1, N_MOD, d), lambda b, i: (b, 0, 0)),
                  pl.BlockSpec((1, d), const)] + [pl.BlockSpec(w.shape, const) for w in weights],
        out_specs=out_specs,
        compiler_params=pltpu.CompilerParams(dimension_semantics=("arbitrary", "arbitrary"),
                                             vmem_limit_bytes=VMEM_LIMIT),
        name="in_proj",
    )(x3, mod3, g_mix.reshape(1, d), *weights)


CB = 256


def _cum_kernel(wft_ref, bfor_ref, k_ref, cumt_ref, kaug_ref):
    seq = wft_ref.shape[2]
    r = lax.broadcasted_iota(jnp.int32, (CB, CB), 0)
    cidx = lax.broadcasted_iota(jnp.int32, (CB, CB), 1)
    tri = jnp.where(r <= cidx, 1.0, 0.0).astype(BF16)
    row128 = lax.broadcasted_iota(jnp.int32, (AUG, CB), 0)
    ones_rows = jnp.where((row128 >= HEAD_DIM + 3) & (row128 < HEAD_DIM + 6), 1.0, 0.0)
    carry = jnp.zeros((8, 1), F32)
    for blk in range(seq // CB):
        cols = slice(blk * CB, (blk + 1) * CB)
        z = wft_ref[0, 8:16, cols] + bfor_ref[...]
        logf = jnp.minimum(z, 0.0) - jnp.log(1.0 + jnp.exp(-jnp.abs(z)))
        p1, p2, p3 = _split3_f32(logf)
        pieces = jnp.concatenate([p1, p2, p3, jnp.zeros_like(p1)], axis=0).astype(BF16)
        parts = _dot(pieces, tri)
        cum = parts[0:8] + parts[8:16] + parts[16:24] + carry
        carry = cum[:, CB - 1:CB]
        cum2 = cum * LOG2E
        cumt_ref[0, :, cols] = cum2
        c1, c2, c3 = _split3_f32(cum2)
        for h in range(B_HEADS):
            spare = jnp.where(row128 == HEAD_DIM, -c1[h:h + 1], ones_rows)
            spare = jnp.where(row128 == HEAD_DIM + 1, -c2[h:h + 1], spare)
            spare = jnp.where(row128 == HEAD_DIM + 2, -c3[h:h + 1], spare)
            lanes = slice(h * AUG, (h + 1) * AUG)
            kaug_ref[0, cols, lanes] = k_ref[0, cols, lanes] + spare.T.astype(BF16)


def _fox_cum(wft, b_forget, bk):
    bsz, _, seq = wft.shape
    nh = B_HEADS
    kw = bk.shape[-1]
    return pl.pallas_call(
        _cum_kernel,
        out_shape=[jax.ShapeDtypeStruct((bsz, nh, seq), F32), jax.ShapeDtypeStruct((bsz, seq, kw), BF16)],
        grid=(bsz,),
        in_specs=[pl.BlockSpec((1, 16, seq), lambda b: (b, 0, 0)),
                  pl.BlockSpec((nh, 1), lambda b: (0, 0)),
                  pl.BlockSpec((1, seq, kw), lambda b: (b, 0, 0))],
        out_specs=[pl.BlockSpec((1, nh, seq), lambda b: (b, 0, 0)),
                   pl.BlockSpec((1, seq, kw), lambda b: (b, 0, 0))],
        compiler_params=pltpu.CompilerParams(dimension_semantics=("arbitrary",),
                                             vmem_limit_bytes=VMEM_LIMIT),
        name="fox_cum",
    )(wft, b_forget.reshape(nh, 1), bk)


def _softmax_init(m_ref, l_ref, acc_ref):
    m_ref[...] = jnp.full(m_ref.shape, NEG_BIG, F32)
    l_ref[...] = jnp.zeros(l_ref.shape, F32)
    acc_ref[...] = jnp.zeros(acc_ref.shape, F32)


def _attend_chunks(items, score_fn, vt_fn, m_ref, l_ref, acc_ref, lookahead):
    ahead = min(lookahead, len(items))
    scores = {i: score_fn(items[i]) for i in range(ahead)}
    for i, item in enumerate(items):
        if i + ahead < len(items):
            scores[i + ahead] = score_fn(items[i + ahead])
        s = scores.pop(i)
        h = item[1]
        m_old = m_ref[h]
        m_new = jnp.maximum(m_old, jnp.max(s, axis=0, keepdims=True))
        alpha = jnp.exp2(m_old - m_new)
        p = jnp.exp2(s - m_new)
        l_ref[h] = alpha * l_ref[h] + jnp.sum(p, axis=0, keepdims=True)
        m_ref[h] = m_new
        rows = slice(h * HEAD_DIM, (h + 1) * HEAD_DIM)
        acc_ref[rows, :] = alpha * acc_ref[rows, :] + _dot(vt_fn(item), p.astype(BF16))


def _softmax_finish(o_ref, n_heads, l_ref, acc_ref):
    for h in range(n_heads):
        rows = slice(h * HEAD_DIM, (h + 1) * HEAD_DIM)
        o_ref[0, rows, :] = acc_ref[rows, :] / l_ref[h]


def _fox_kernel(qt_ref, k_ref, vt_ref, cumt_ref, o_ref, w_ref, m_ref, l_ref, acc_ref):
    qi = pl.program_id(1)
    _softmax_init(m_ref, l_ref, acc_ref)

    row64 = lax.broadcasted_iota(jnp.int32, (AUG - HEAD_DIM, TQ), 0)
    for h in range(B_HEADS):
        c1, c2, c3 = _split3_f32(cumt_ref[0, h:h + 1, :])
        spare = jnp.where(row64 < 3, 1.0, 0.0)
        spare = jnp.where(row64 == 3, c1, spare)
        spare = jnp.where(row64 == 4, c2, spare)
        spare = jnp.where(row64 == 5, c3, spare)
        w_ref[h] = jnp.concatenate([qt_ref[0, h * HEAD_DIM:(h + 1) * HEAD_DIM, :],
                                    spare.astype(BF16)], axis=0)

    def tiles(chunks):
        rows = [_chunk(kc) for kc, _ in chunks]

        def score_fn(item):
            j, h = item
            s = _dot(k_ref[0, rows[j], h * AUG:(h + 1) * AUG], w_ref[h])
            if chunks[j][1]:
                s = jnp.where(_key_minus_query() <= 0, s, NEG_BIG)
            return s

        def vt_fn(item):
            j, h = item
            return vt_ref[0, h * HEAD_DIM:(h + 1) * HEAD_DIM, rows[j]]

        items = [(j, h) for j in range(len(chunks)) for h in range(B_HEADS)]
        _attend_chunks(items, score_fn, vt_fn, m_ref, l_ref, acc_ref, FOX_LOOKAHEAD)

    def body(pair, _):
        tiles([(2 * pair, False), (2 * pair + 1, False)])
        return 0

    lax.fori_loop(0, qi // 2, body, 0)

    @pl.when(qi % 2 == 1)
    def _():
        tiles([(qi - 1, False), (qi, True)])

    @pl.when(qi % 2 == 0)
    def _():
        tiles([(qi, True)])

    _softmax_finish(o_ref, B_HEADS, l_ref, acc_ref)


def _fox_attn(bqt, kaug, bvt, cumt):
    bsz, w, seq = bqt.shape
    blk_t = lambda b, i: (b, 0, i)
    full = lambda b, i: (b, 0, 0)
    return pl.pallas_call(
        _fox_kernel,
        out_shape=jax.ShapeDtypeStruct((bsz, w, seq), F32),
        grid=(bsz, seq // TQ),
        in_specs=[pl.BlockSpec((1, w, TQ), blk_t),
                  pl.BlockSpec((1, seq, B_HEADS * AUG), full),
                  pl.BlockSpec((1, w, seq), full),
                  pl.BlockSpec((1, B_HEADS, TQ), blk_t)],
        out_specs=pl.BlockSpec((1, w, TQ), blk_t),
        scratch_shapes=[pltpu.VMEM((B_HEADS, AUG, TQ), BF16), pltpu.VMEM((B_HEADS, 1, TQ), F32),
                        pltpu.VMEM((B_HEADS, 1, TQ), F32), pltpu.VMEM((B_HEADS * HEAD_DIM, TQ), F32)],
        compiler_params=pltpu.CompilerParams(dimension_semantics=("arbitrary", "arbitrary"),
                                             vmem_limit_bytes=VMEM_LIMIT),
        name="fox_attn",
    )(bqt, kaug, bvt, cumt)


def _bf16_pieces(value):
    pieces = []
    rest = np.float32(value)
    for _ in range(3):
        piece = np.asarray(rest).astype(BF16).astype(np.float32)
        pieces.append(float(piece))
        rest = np.float32(rest - piece)
    return pieces


def _dsa_kernel(iqt_ref, ik_ref, wft_ref, qt_ref, k_ref, posx_ref, vt_ref, o_ref,
                s_ref, thr_ref, w_ref, m_ref, l_ref, acc_ref):
    qi = pl.program_id(1)
    nch = qi + 1
    kmq = _key_minus_query()

    def score_chunk(kc):
        ik = ik_ref[0, _chunk(kc), :]
        acc = jnp.zeros((KC, TQ), F32)
        for h in range(IDX_HEADS):
            d = _dot(ik, iqt_ref[0, h * IDX_K:(h + 1) * IDX_K, :])
            acc = acc + wft_ref[0, h:h + 1, :] * jnp.maximum(d, 0.0)
        acc = jnp.where(acc == 0.0, 0.0, acc)
        causal = kmq <= (qi - kc) * KC
        s_ref[_chunk(kc), :] = jnp.where(causal, acc, -jnp.inf)

    def score_body(pair, _):
        score_chunk(2 * pair)
        score_chunk(2 * pair + 1)
        return 0

    lax.fori_loop(0, nch // 2, score_body, 0)

    @pl.when(nch % 2 == 1)
    def _():
        score_chunk(nch - 1)

    def scan(fn, init):
        def pair_body(pair, c):
            c = fn(2 * pair, s_ref[_chunk(2 * pair), :], c)
            return fn(2 * pair + 1, s_ref[_chunk(2 * pair + 1), :], c)

        c = lax.fori_loop(0, nch // 2, pair_body, init)
        return lax.cond(nch % 2 == 1, lambda c: fn(nch - 1, s_ref[_chunk(nch - 1), :], c),
                        lambda c: c, c)

    @pl.when(qi == 0)
    def _():
        thr_ref[...] = jnp.full(thr_ref.shape, LOWEST, F32)

    @pl.when(qi > 0)
    def _():
        def cmin(x):
            return jnp.min(x, axis=0, keepdims=True)

        def cmax(x):
            return jnp.max(x, axis=0, keepdims=True)

        def csum(x):
            return jnp.sum(x, axis=0, keepdims=True)

        zeros = jnp.zeros((1, TQ), F32)
        pinf = jnp.full((1, TQ), jnp.inf, F32)
        ninf = jnp.full((1, TQ), -jnp.inf, F32)

        def init_fn(_, s, c):
            lo, hi = c
            lo = jnp.minimum(lo, cmin(jnp.where(s > -jnp.inf, s, jnp.inf)))
            hi = jnp.maximum(hi, cmax(s))
            return lo, hi

        lo, hi = scan(init_fn, (pinf, ninf))

        def coarse_step(_, carry):
            lo, hi = carry
            mid = lo + (hi - lo) * 0.5
            cnt = scan(lambda _, s, c: c + csum(jnp.where(s >= mid, 1.0, 0.0)), zeros)
            enough = cnt >= float(TOPK)
            return jnp.where(enough, mid, lo), jnp.where(enough, hi, mid)

        lo, hi = lax.fori_loop(0, COARSE_STEPS, coarse_step, (lo, hi))

        def cond(carry):
            return carry[2] > 0

        def step(carry):
            lo, hi, _ = carry
            mid = lo + (hi - lo) * 0.5
            mid = jnp.where(mid <= lo, hi, mid)

            def fn(_, s, c):
                cnt, a, b = c
                ge = s >= mid
                cnt = cnt + csum(jnp.where(ge, 1.0, 0.0))
                b = jnp.minimum(b, cmin(jnp.where(ge, s, jnp.inf)))
                a = jnp.maximum(a, cmax(jnp.where(ge, -jnp.inf, s)))
                return cnt, a, b

            cnt, a, b = scan(fn, (zeros, ninf, pinf))
            enough = cnt >= float(TOPK)
            new_lo = jnp.where(enough, b, lo)
            new_hi = jnp.where(enough, jnp.where(cnt == float(TOPK), b, hi), a)
            active = jnp.max(jnp.where(new_lo < new_hi, 1, 0))
            return new_lo, new_hi, active

        first_active = jnp.max(jnp.where(lo < hi, 1, 0))
        thr, _, _ = lax.while_loop(cond, step, (lo, hi, first_active))

        def count_fn(_, s, c):
            n_gt, n_ge = c
            return (n_gt + csum(jnp.where(s > thr, 1.0, 0.0)),
                    n_ge + csum(jnp.where(s >= thr, 1.0, 0.0)))

        n_gt, n_ge = scan(count_fn, (zeros, zeros))
        thr_ref[...] = thr

        @pl.when(jnp.max(jnp.where(n_ge > float(TOPK), 1, 0)) > 0)
        def _():
            need = float(TOPK) - n_gt
            lower = jnp.where(lax.broadcasted_iota(jnp.int32, (KC, KC), 1)
                              < lax.broadcasted_iota(jnp.int32, (KC, KC), 0), 1.0, 0.0).astype(BF16)

            def sel_fn(kc, s, run):
                eq = s == thr
                eqf = jnp.where(eq, 1.0, 0.0)
                before = _dot(lower, eqf.astype(BF16)) + run
                sel = (s > thr) | (eq & (before < need))
                s_ref[_chunk(kc), :] = jnp.where(sel, 0.0, NEG_BIG)
                return run + csum(eqf)

            scan(sel_fn, zeros)
            thr_ref[...] = jnp.full(thr_ref.shape, 0.5 * NEG_BIG, F32)

    _softmax_init(m_ref, l_ref, acc_ref)
    rep = A_HEADS // A_KV_HEADS
    row64 = lax.broadcasted_iota(jnp.int32, (AUG - HEAD_DIM, TQ), 0)
    qpos = (qi * TQ + lax.broadcasted_iota(jnp.int32, (AUG - HEAD_DIM, TQ), 1)).astype(F32)
    for h in range(A_HEADS):
        slope = np.float32(2.0 ** (-8.0 * (h + 1) / A_HEADS) * LOG2E)
        u1, u2, u3 = _split3_f32(-slope * qpos)
        spare = jnp.zeros((AUG - HEAD_DIM, TQ), F32)
        for p, s_p in enumerate(_bf16_pieces(slope)):
            spare = jnp.where(row64 == p, 128.0 * s_p, spare)
            spare = jnp.where(row64 == 3 + p, s_p, spare)
        spare = jnp.where(row64 == 6, u1, spare)
        spare = jnp.where(row64 == 7, u2, spare)
        spare = jnp.where(row64 == 8, u3, spare)
        w_ref[h] = jnp.concatenate([qt_ref[0, h * HEAD_DIM:(h + 1) * HEAD_DIM, :],
                                    spare.astype(BF16)], axis=0)

    def attn_tiles(chunks):
        rows = [_chunk(kc) for kc in chunks]
        lhs = [[k_ref[0, r, g * AUG:(g + 1) * AUG] + posx_ref[r, :] for g in range(A_KV_HEADS)]
               for r in rows]
        bias = [jnp.where(s_ref[r, :] >= thr_ref[...], 0.0, NEG_BIG) for r in rows]

        def score_fn(item):
            j, h = item
            return _dot(lhs[j][h // rep], w_ref[h]) + bias[j]

        def vt_fn(item):
            j, h = item
            g = h // rep
            return vt_ref[0, g * HEAD_DIM:(g + 1) * HEAD_DIM, rows[j]]

        items = [(j, h) for j in range(len(chunks)) for h in range(A_HEADS)]
        _attend_chunks(items, score_fn, vt_fn, m_ref, l_ref, acc_ref, DSA_LOOKAHEAD)

    def attn_body(pair, _):
        attn_tiles([2 * pair, 2 * pair + 1])
        return 0

    lax.fori_loop(0, nch // 2, attn_body, 0)

    @pl.when(nch % 2 == 1)
    def _():
        attn_tiles([nch - 1])

    _softmax_finish(o_ref, A_HEADS, l_ref, acc_ref)


def _dsa_attn(iqt, ik4, wft, aqt, ak, posx, avt):
    bsz, w, seq = aqt.shape
    blk_t = lambda b, i: (b, 0, i)
    full = lambda b, i: (b, 0, 0)
    return pl.pallas_call(
        _dsa_kernel,
        out_shape=jax.ShapeDtypeStruct((bsz, w, seq), F32),
        grid=(bsz, seq // TQ),
        in_specs=[pl.BlockSpec((1, IDX_HEADS * IDX_K, TQ), blk_t),
                  pl.BlockSpec((1, seq, IDX_K), full),
                  pl.BlockSpec((1, 16, TQ), blk_t),
                  pl.BlockSpec((1, w, TQ), blk_t),
                  pl.BlockSpec((1, seq, A_KV_HEADS * AUG), full),
                  pl.BlockSpec((seq, AUG), lambda b, i: (0, 0)),
                  pl.BlockSpec((1, 128, seq), full)],
        out_specs=pl.BlockSpec((1, w, TQ), blk_t),
        scratch_shapes=[pltpu.VMEM((seq, TQ), F32), pltpu.VMEM((1, TQ), F32),
                        pltpu.VMEM((A_HEADS, AUG, TQ), BF16),
                        pltpu.VMEM((A_HEADS, 1, TQ), F32), pltpu.VMEM((A_HEADS, 1, TQ), F32),
                        pltpu.VMEM((A_HEADS * HEAD_DIM, TQ), F32)],
        compiler_params=pltpu.CompilerParams(dimension_semantics=("arbitrary", "arbitrary"),
                                             vmem_limit_bytes=VMEM_LIMIT),
        name="dsa_attn",
    )(iqt, ik4, wft, aqt, ak, posx, avt)


def _rms_cols(xt, g_col):
    return xt * lax.rsqrt(jnp.mean(xt * xt, axis=0, keepdims=True) + EPS) * g_col


MOE_TMG = 256
MOE_TMC = 256
ROW_TILE = 8
ROUTE_COLS = 128


def _first_index_of_max(vals, lane, big):
    m = jnp.max(vals, axis=-1, keepdims=True)
    idx = jnp.min(jnp.where(vals == m, lane, big), axis=-1, keepdims=True)
    return m, idx


def _route(h, wr_hi_ref, wr_lo_ref, br_ref):
    h_hi, h_lo = _split_bf16(h)
    logits = _dot3(h_hi, h_lo, wr_hi_ref[...], wr_lo_ref[...]) + br_ref[...]
    lane = lax.broadcasted_iota(jnp.int32, logits.shape, 1)
    ninf = -jnp.inf
    gl = jnp.where(lane < N_GROUPS, logits, ninf)
    gmax, gsel = _first_index_of_max(gl, lane, 1 << 20)
    p_group = 1.0 / jnp.sum(jnp.exp(gl - gmax), axis=-1, keepdims=True)
    base = 32 + gsel * EXPERTS_PER_GROUP
    el = jnp.where((lane >= base) & (lane < base + EXPERTS_PER_GROUP), logits, ninf)
    v1, i1 = _first_index_of_max(el, lane, 1 << 20)
    el2 = jnp.where(lane == i1, ninf, el)
    v2, i2 = _first_index_of_max(el2, lane, 1 << 20)
    e2 = jnp.exp(v2 - v1)
    w1 = 1.0 / (1.0 + e2)
    w2 = e2 / (1.0 + e2)
    return i1, i2, w1 * p_group, w2 * p_group


def _token_rows(s, n_tokens, first_token=0):
    return pl.ds(first_token * ROW_TILE + s, n_tokens, stride=ROW_TILE)


def _to_token_tiles(ref, x):
    for s in range(ROW_TILE):
        ref[_token_rows(s, x.shape[0]), :] = x[:, s * 128:(s + 1) * 128]


def _token_tile(ref, t):
    start = t * ROW_TILE if isinstance(t, int) else pl.multiple_of(t * ROW_TILE, ROW_TILE)
    return ref.at[pl.ds(start, ROW_TILE), :]


def _moe_route_kernel(oat_ref, obt_ref, x_ref, mod_ref, ga_ref, gb_ref, wo_ref,
                      gf_ref, wr_hi_ref, wr_lo_ref, br_ref,
                      x1_ref, h_ref, info_ref, infot_ref, cnt_ref, run_ref):
    @pl.when(pl.program_id(0) == 0)
    def _():
        run_ref[...] = jnp.zeros_like(run_ref)

    oa = _rms_cols(oat_ref[0], ga_ref[...]).astype(BF16)
    ob = _rms_cols(obt_ref[0], gb_ref[...]).astype(BF16)
    y = _dot_tn(oa, wo_ref[0:512, :]) + _dot_tn(ob, wo_ref[512:1024, :])
    x1 = x_ref[...] + mod_ref[0, 2:3, :] * y
    x1_ref[...] = x1

    h = _rms(x1, gf_ref[...]) * (1.0 + mod_ref[0, 4:5, :]) + mod_ref[0, 3:4, :]
    _to_token_tiles(h_ref, h)
    i1, i2, w1, w2 = _route(h, wr_hi_ref, wr_lo_ref, br_ref)
    tm = h.shape[0]
    lane = lax.broadcasted_iota(jnp.int32, (tm, ROUTE_COLS), 1)
    picked = jnp.where((lane == i1) | (lane == i2), 1.0, 0.0)
    earlier = jnp.where(lax.broadcasted_iota(jnp.int32, (tm, tm), 1)
                        < lax.broadcasted_iota(jnp.int32, (tm, tm), 0), 1.0, 0.0).astype(BF16)
    before = _dot(earlier, picked.astype(BF16)) + run_ref[...]
    rank1 = jnp.sum(jnp.where(lane == i1, before, 0.0), axis=-1, keepdims=True)
    rank2 = jnp.sum(jnp.where(lane == i2, before, 0.0), axis=-1, keepdims=True)
    run_ref[...] += jnp.sum(picked, axis=0, keepdims=True)
    cnt_ref[...] = run_ref[...]
    info = jnp.where(lane == 0, (i1 - 32).astype(F32), 0.0)
    info = jnp.where(lane == 1, (i2 - 32).astype(F32), info)
    info = jnp.where(lane == 2, rank1, info)
    info = jnp.where(lane == 3, rank2, info)
    info = jnp.where(lane == 4, w1, info)
    info = jnp.where(lane == 5, w2, info)
    info_ref[...] = info
    infot_ref[...] = info.T[0:8, :]


def _moe_route(oat, obt, x2, mod3, g_out_a, g_out_b, w_out_bf, g_ffn, wr_hi, wr_lo, b_route, seq):
    n, d = x2.shape
    tm = 512
    per_b = seq // tm
    row = lambda i: (i, 0)
    const = lambda i: (0, 0)
    blk_t = lambda i: (i // per_b, 0, i % per_b)
    return pl.pallas_call(
        _moe_route_kernel,
        out_shape=[jax.ShapeDtypeStruct((n, d), F32),
                   jax.ShapeDtypeStruct((n * ROW_TILE, d // ROW_TILE), F32),
                   jax.ShapeDtypeStruct((n, ROUTE_COLS), F32), jax.ShapeDtypeStruct((8, n), F32),
                   jax.ShapeDtypeStruct((1, ROUTE_COLS), F32)],
        grid=(n // tm,),
        in_specs=[pl.BlockSpec((1, 512, tm), blk_t), pl.BlockSpec((1, 512, tm), blk_t),
                  pl.BlockSpec((tm, d), row),
                  pl.BlockSpec((1, N_MOD, d), lambda i: (i // per_b, 0, 0)),
                  pl.BlockSpec((512, 1), const), pl.BlockSpec((512, 1), const),
                  pl.BlockSpec((d, d), const),
                  pl.BlockSpec((1, d), const),
                  pl.BlockSpec((d, ROUTE_COLS), const), pl.BlockSpec((d, ROUTE_COLS), const),
                  pl.BlockSpec((1, ROUTE_COLS), const)],
        out_specs=[pl.BlockSpec((tm, d), row),
                   pl.BlockSpec((tm * ROW_TILE, d // ROW_TILE), row),
                   pl.BlockSpec((tm, ROUTE_COLS), row), pl.BlockSpec((8, tm), lambda i: (0, i)),
                   pl.BlockSpec((1, ROUTE_COLS), const)],
        scratch_shapes=[pltpu.VMEM((1, ROUTE_COLS), F32)],
        compiler_params=pltpu.CompilerParams(dimension_semantics=("arbitrary",),
                                             vmem_limit_bytes=VMEM_LIMIT),
        name="mix_out_moe_route",
    )(oat, obt, x2, mod3, g_out_a.reshape(-1, 1), g_out_b.reshape(-1, 1), w_out_bf,
      g_ffn.reshape(1, d), wr_hi, wr_lo, b_route)


def _wait_token_copies(src_hbm, dst, sem, n_tokens):
    pltpu.make_async_copy(src_hbm.at[pl.ds(0, n_tokens * ROW_TILE), :], dst, sem).wait()


def _moe_scatter_kernel(dest_ref, pad_start_ref, pad_len_ref, nu_ref, h_ref, xs_hbm, zero_ref, sem):
    i = pl.program_id(0)
    n_pairs = 2 * MOE_TMC

    @pl.when(i == 0)
    def _():
        zero_ref[...] = jnp.zeros_like(zero_ref)

        def pad_copies(e):
            n = pad_len_ref[e]
            size = MOE_TMG // 2
            while size >= 1:
                first = pad_start_ref[e] + (n & ~(2 * size - 1))
                copy = pltpu.make_async_copy(
                    zero_ref.at[pl.ds(0, size * ROW_TILE), :],
                    xs_hbm.at[pl.ds(pl.multiple_of(first * ROW_TILE, ROW_TILE), size * ROW_TILE), :],
                    sem.at[1])
                yield (n & size) != 0, copy
                size //= 2

        for e in range(N_EXPERTS):
            for present, copy in pad_copies(e):
                pl.when(present)(copy.start)
        for e in range(N_EXPERTS):
            for present, copy in pad_copies(e):
                pl.when(present)(copy.wait)

        def row_tile(t):
            return xs_hbm.at[pl.ds(pl.multiple_of(t * (MOE_TMG * ROW_TILE), MOE_TMG * ROW_TILE),
                                   MOE_TMG * ROW_TILE), :]

        n_tiles = xs_hbm.shape[0] // (MOE_TMG * ROW_TILE)

        def fill_tile(t, _):
            pltpu.make_async_copy(zero_ref, row_tile(t), sem.at[1]).start()
            return 0

        def drain_tile(t, _):
            pltpu.make_async_copy(zero_ref, row_tile(0), sem.at[1]).wait()
            return 0

        lax.fori_loop(nu_ref[0], n_tiles, fill_tile, 0)
        lax.fori_loop(nu_ref[0], n_tiles, drain_tile, 0)

    for r in range(MOE_TMC):
        for j in range(2):
            pltpu.make_async_copy(_token_tile(h_ref, r),
                                  _token_tile(xs_hbm, dest_ref[i * n_pairs + j * MOE_TMC + r]),
                                  sem.at[0]).start(priority=j)
    _wait_token_copies(xs_hbm, xs_hbm.at[pl.ds(0, n_pairs * ROW_TILE), :], sem.at[0], n_pairs)


def _moe_scatter(dest, pad_start, pad_len, n_used, h2, n_rows):
    n = h2.shape[0] // ROW_TILE
    grid_spec = pltpu.PrefetchScalarGridSpec(
        num_scalar_prefetch=4,
        grid=(n // MOE_TMC,),
        in_specs=[pl.BlockSpec((MOE_TMC * ROW_TILE, h2.shape[1]), lambda i, *_: (i, 0))],
        out_specs=pl.BlockSpec(memory_space=pl.ANY),
        scratch_shapes=[pltpu.VMEM((MOE_TMG * ROW_TILE, h2.shape[1]), F32),
                        pltpu.SemaphoreType.DMA((2,))])
    return pl.pallas_call(
        _moe_scatter_kernel,
        out_shape=jax.ShapeDtypeStruct((n_rows * ROW_TILE, h2.shape[1]), F32),
        grid_spec=grid_spec,
        compiler_params=pltpu.CompilerParams(dimension_semantics=("arbitrary",),
                                             vmem_limit_bytes=VMEM_LIMIT),
        name="moe_scatter",
    )(dest, pad_start, pad_len, n_used, h2)


def _moe_expert_kernel(te_ref, nu_ref, x_ref, wg_ref, wu_ref, wd_ref, y_ref,
                       xs_ref, wgb_ref, wub_ref, wdb_ref):
    i = pl.program_id(0)
    used = i < nu_ref[0]
    new_expert = jnp.logical_or(i == 0, te_ref[i] != te_ref[jnp.maximum(i - 1, 0)])

    @pl.when(jnp.logical_and(used, new_expert))
    def _():
        wgb_ref[...] = wg_ref[0].astype(BF16)
        wub_ref[...] = wu_ref[0].astype(BF16)
        wdb_ref[...] = wd_ref[0].astype(BF16)

    @pl.when(used)
    def _():
        for s in range(ROW_TILE):
            xs_ref[:, s * 128:(s + 1) * 128] = x_ref[_token_rows(s, MOE_TMG), :].astype(BF16)
        x = xs_ref[...]
        hg = _dot(x, wgb_ref[...])
        hu = _dot(x, wub_ref[...])
        a = hg * jax.nn.sigmoid(hg) * hu
        _to_token_tiles(y_ref, _dot(a.astype(BF16), wdb_ref[...]))

    @pl.when(i >= nu_ref[0])
    def _():
        y_ref[...] = jnp.zeros_like(y_ref)


def _moe_experts(tile_expert, n_used, xsorted, wg, wu, wd):
    n_rows = xsorted.shape[0] // ROW_TILE
    d, ff = wg.shape[1], wg.shape[2]
    tile = (MOE_TMG * ROW_TILE, xsorted.shape[1])
    grid_spec = pltpu.PrefetchScalarGridSpec(
        num_scalar_prefetch=2,
        grid=(n_rows // MOE_TMG,),
        in_specs=[pl.BlockSpec(tile, lambda i, te, nu: (jnp.minimum(i, nu[0] - 1), 0)),
                  pl.BlockSpec((1, d, ff), lambda i, te, nu: (te[i], 0, 0)),
                  pl.BlockSpec((1, d, ff), lambda i, te, nu: (te[i], 0, 0)),
                  pl.BlockSpec((1, ff, d), lambda i, te, nu: (te[i], 0, 0))],
        out_specs=pl.BlockSpec(tile, lambda i, te, nu: (i, 0)),
        scratch_shapes=[pltpu.VMEM((MOE_TMG, d), BF16), pltpu.VMEM((d, ff), BF16),
                        pltpu.VMEM((d, ff), BF16), pltpu.VMEM((ff, d), BF16)])
    return pl.pallas_call(
        _moe_expert_kernel,
        out_shape=jax.ShapeDtypeStruct(xsorted.shape, F32),
        grid_spec=grid_spec,
        compiler_params=pltpu.CompilerParams(dimension_semantics=("arbitrary",),
                                             vmem_limit_bytes=VMEM_LIMIT),
        name="moe_experts",
    )(tile_expert, n_used, xsorted, wg, wu, wd)


def _moe_combine_kernel(dest_ref, y_hbm, x_ref, info_ref, mod_ref, gfin_ref, o_ref, ybuf, x2_ref, sem):
    i = pl.program_id(0)
    nt = pl.num_programs(0)
    slot = lax.rem(i, 2)
    n_pairs = 2 * MOE_TMC

    def start_gather(tile, to_slot):
        for r in range(n_pairs):
            pltpu.make_async_copy(_token_tile(y_hbm, dest_ref[tile * n_pairs + r]),
                                  _token_tile(ybuf.at[to_slot], r), sem.at[to_slot]).start(priority=r % 2)

    @pl.when(i == 0)
    def _():
        start_gather(0, 0)

    _wait_token_copies(y_hbm, ybuf.at[slot], sem.at[slot], n_pairs)

    @pl.when(i + 1 < nt)
    def _():
        start_gather(i + 1, 1 - slot)

    w1 = info_ref[:, 4:5]
    w2 = info_ref[:, 5:6]
    sumsq = jnp.zeros((MOE_TMC, 1), F32)
    for s in range(ROW_TILE):
        cols = slice(s * 128, (s + 1) * 128)
        y = (w1 * ybuf[slot, _token_rows(s, MOE_TMC), :]
             + w2 * ybuf[slot, _token_rows(s, MOE_TMC, first_token=MOE_TMC), :])
        x2 = x_ref[:, cols] + mod_ref[0, 5:6, cols] * y
        x2_ref[:, cols] = x2
        sumsq = sumsq + jnp.sum(x2 * x2, axis=-1, keepdims=True)
    d = x2_ref.shape[1]
    o_ref[...] = x2_ref[...] * lax.rsqrt(sumsq / d + EPS) * gfin_ref[...]


def _moe_combine(dest, ysorted, x1, info, mod3, g_final, seq):
    n, d = x1.shape
    per_b = seq // MOE_TMC
    grid_spec = pltpu.PrefetchScalarGridSpec(
        num_scalar_prefetch=1,
        grid=(n // MOE_TMC,),
        in_specs=[pl.BlockSpec(memory_space=pl.ANY),
                  pl.BlockSpec((MOE_TMC, d), lambda i, ds: (i, 0)),
                  pl.BlockSpec((MOE_TMC, ROUTE_COLS), lambda i, ds: (i, 0)),
                  pl.BlockSpec((1, N_MOD, d), lambda i, ds: (i // per_b, 0, 0)),
                  pl.BlockSpec((1, d), lambda i, ds: (0, 0))],
        out_specs=pl.BlockSpec((MOE_TMC, d), lambda i, ds: (i, 0)),
        scratch_shapes=[pltpu.VMEM((2, 2 * MOE_TMC * ROW_TILE, ysorted.shape[1]), F32),
                        pltpu.VMEM((MOE_TMC, d), F32), pltpu.SemaphoreType.DMA((2,))])
    return pl.pallas_call(
        _moe_combine_kernel,
        out_shape=jax.ShapeDtypeStruct((n, d), F32),
        grid_spec=grid_spec,
        compiler_params=pltpu.CompilerParams(dimension_semantics=("arbitrary",),
                                             vmem_limit_bytes=VMEM_LIMIT),
        name="moe_combine",
    )(dest, ysorted, x1, info, mod3, g_final.reshape(1, d))


def _mix_out_and_moe(oat, obt, x2, mod3, g_out_a, g_out_b, w_out_bf, g_ffn, wr_hi, wr_lo, b_route,
                     wg, wu, wd, g_final, seq):
    n, d = x2.shape
    x1, h2, info, infot, counts = _moe_route(oat, obt, x2, mod3, g_out_a, g_out_b, w_out_bf,
                                             g_ffn, wr_hi, wr_lo, b_route, seq)

    e1, e2, rank1, rank2 = [infot[k].astype(jnp.int32) for k in range(4)]
    cnt = counts[0, 32:32 + N_EXPERTS].astype(jnp.int32)
    padded = ((cnt + MOE_TMG - 1) // MOE_TMG) * MOE_TMG
    seg_end = jnp.cumsum(padded)
    expert_ids = jnp.arange(N_EXPERTS, dtype=jnp.int32)

    def seg_start_of(e):
        return jnp.sum(jnp.where(expert_ids[None, :] < e[:, None], padded[None, :], 0), axis=1)

    dest1 = seg_start_of(e1) + rank1
    dest2 = seg_start_of(e2) + rank2
    n_rows = 2 * n + N_EXPERTS * MOE_TMG
    tile_start = jnp.arange(n_rows // MOE_TMG, dtype=jnp.int32) * MOE_TMG
    tile_expert = jnp.minimum(jnp.sum((tile_start[:, None] >= seg_end[None, :]).astype(jnp.int32), axis=1),
                              N_EXPERTS - 1)
    n_used = (seg_end[N_EXPERTS - 1:] // MOE_TMG).astype(jnp.int32)
    dest = jnp.concatenate([dest1.reshape(-1, MOE_TMC), dest2.reshape(-1, MOE_TMC)], axis=1).reshape(-1)

    seg_start = seg_end - padded
    xsorted = _moe_scatter(dest, seg_start + cnt, padded - cnt, n_used, h2, n_rows)
    ysorted = _moe_experts(tile_expert, n_used, xsorted, wg, wu, wd)
    return _moe_combine(dest, ysorted, x1, info, mod3, g_final, seq)


def _layer(x3, c, w_ada, b_ada, g_mix, w_in, b_forget, g_out_a, g_out_b, w_out,
           g_ffn, w_group, b_group, w_router, b_router, w_gate, w_up, w_down, g_final):
    bsz, seq, d = x3.shape
    mod3 = _ada_mod(c, w_ada, b_ada).reshape(bsz, N_MOD, d)

    w_t = w_in.T

    def pad_heads(w, n_heads):
        w = w.reshape(d, n_heads, HEAD_DIM)
        return jnp.concatenate([w, jnp.zeros_like(w)], axis=-1).reshape(d, n_heads * AUG)

    w_wf_t = jnp.concatenate([w_t[1344:1352], w_t[2888:2896]], axis=0)
    weights = [w_t[0:512].astype(BF16), pad_heads(w_in[:, 512:640], A_KV_HEADS).astype(BF16),
               w_t[640:768].astype(BF16), w_t[1352:1864].astype(BF16),
               pad_heads(w_in[:, 1864:2376], B_HEADS).astype(BF16), w_t[2376:2888].astype(BF16),
               w_t[768:1280].astype(BF16), w_in[:, 1280:1344].astype(BF16), *_split_bf16(w_wf_t)]
    aqt, ak, avt, bqt, bk, bvt, iqt, ik4, wft = _in_proj(x3, mod3, g_mix, weights)

    cumt, kaug = _fox_cum(wft, b_forget, bk)
    obt = _fox_attn(bqt, kaug, bvt, cumt)

    pos = jnp.arange(seq, dtype=jnp.int32)[:, None]
    lane = jnp.arange(AUG, dtype=jnp.int32)[None, :] - HEAD_DIM
    posx = jnp.where((lane >= 0) & (lane < 3), pos >> 7,
                     jnp.where((lane >= 3) & (lane < 6), pos & 127,
                               jnp.where((lane >= 6) & (lane < 9), 1, 0))).astype(BF16)
    oat = _dsa_attn(iqt, ik4, wft, aqt, ak, posx, avt)

    w_r = jnp.concatenate([w_group, jnp.zeros((d, 32 - N_GROUPS), F32),
                           jnp.transpose(w_router, (1, 0, 2)).reshape(d, N_EXPERTS),
                           jnp.zeros((d, ROUTE_COLS - 64), F32)], axis=1)
    b_r = jnp.concatenate([b_group, jnp.zeros((32 - N_GROUPS,), F32), b_router.reshape(-1),
                           jnp.zeros((ROUTE_COLS - 64,), F32)]).reshape(1, ROUTE_COLS)
    wr_hi, wr_lo = _split_bf16(w_r)
    out = _mix_out_and_moe(oat, obt, x3.reshape(bsz * seq, d), mod3, g_out_a, g_out_b,
                           w_out.astype(BF16), g_ffn, wr_hi, wr_lo, b_r, w_gate, w_up, w_down,
                           g_final, seq)
    return out.reshape(bsz, seq, d)


def kernel(x, c, w_ada, b_ada, g_mix, w_in, b_forget, g_out_a, g_out_b, w_out, g_ffn, w_group,
           b_group, w_router, b_router, w_gate, w_up, w_down, g_final):
    depth = w_ada.shape[0]
    assert depth == 1, "final norm is fused into the single layer's MoE kernel"
    return _layer(x, c, w_ada[0], b_ada[0], g_mix[0], w_in[0], b_forget[0], g_out_a[0], g_out_b[0],
                  w_out[0], g_ffn[0], w_group[0], b_group[0], w_router[0], b_router[0], w_gate[0],
                  w_up[0], w_down[0], g_final)
```

```python
import math

import jax
import jax.numpy as jnp
import numpy as np
from jax import lax
from jax.experimental import pallas as pl
from jax.experimental.pallas import tpu as pltpu

F32 = jnp.float32
BF16 = jnp.bfloat16

EPS = 1e-6
A_HEADS = 8
A_KV_HEADS = 2
HEAD_DIM = 64
IDX_HEADS = 8
IDX_DIM = 64
TOPK = 256
B_HEADS = 8
N_GROUPS = 4
EXPERTS_PER_GROUP = 8
N_EXPERTS = N_GROUPS * EXPERTS_PER_GROUP
N_MOD = 6

NEG_BIG = -1e30
LOWEST = float(np.finfo(np.float32).min)
LOG2E = math.log2(math.e)
Q_SCALE = HEAD_DIM ** -0.5 * LOG2E
VMEM_LIMIT = 48 * 1024 * 1024

TQ = 256
KC = 256
AUG = 128
IDX_K = IDX_DIM
ONES_ROWS = 16
COARSE_STEPS = 12
FOX_LOOKAHEAD = 6
DSA_LOOKAHEAD = 4
assert TQ == KC == TOPK


def _split_bf16(x):
    hi = x.astype(BF16)
    lo = (x - hi.astype(F32)).astype(BF16)
    return hi, lo


def _split3_f32(x):
    p1 = x.astype(BF16).astype(F32)
    r1 = x - p1
    p2 = r1.astype(BF16).astype(F32)
    p3 = (r1 - p2).astype(BF16).astype(F32)
    return p1, p2, p3


def _dot(a, b):
    return jnp.dot(a, b, preferred_element_type=F32)


def _dot_nt(a, b):
    return lax.dot_general(a, b, (((1,), (1,)), ((), ())), preferred_element_type=F32)


def _dot_tn(a, b):
    return lax.dot_general(a, b, (((0,), (0,)), ((), ())), preferred_element_type=F32)


def _dot3(a_hi, a_lo, b_hi, b_lo):
    return _dot(a_hi, b_hi) + _dot(a_lo, b_hi) + _dot(a_hi, b_lo)


def _dot3_nt(a_hi, a_lo, b_hi, b_lo):
    return _dot_nt(a_hi, b_hi) + _dot_nt(a_lo, b_hi) + _dot_nt(a_hi, b_lo)


def _rms(x, g):
    return x * lax.rsqrt(jnp.mean(x * x, axis=-1, keepdims=True) + EPS) * g


def _chunk(kc):
    return pl.ds(pl.multiple_of(kc * KC, KC), KC)


def _key_minus_query():
    return (lax.broadcasted_iota(jnp.int32, (KC, TQ), 0)
            - lax.broadcasted_iota(jnp.int32, (KC, TQ), 1))


def _ada_kernel(c_ref, w_ref, b_ref, o_ref):
    c = c_ref[...]
    s = c * jax.nn.sigmoid(c)
    s_hi, s_lo = _split_bf16(s)
    w_hi, w_lo = _split_bf16(w_ref[...])
    o_ref[...] = _dot3(s_hi, s_lo, w_hi, w_lo) + b_ref[...]


def _ada_mod(c, w_ada, b_ada):
    bsz, d = c.shape
    n = w_ada.shape[1]
    tn = 1024
    return pl.pallas_call(
        _ada_kernel,
        out_shape=jax.ShapeDtypeStruct((bsz, n), F32),
        grid=(n // tn,),
        in_specs=[pl.BlockSpec((bsz, d), lambda j: (0, 0)),
                  pl.BlockSpec((d, tn), lambda j: (0, j)),
                  pl.BlockSpec((1, tn), lambda j: (0, j))],
        out_specs=pl.BlockSpec((bsz, tn), lambda j: (0, j)),
        compiler_params=pltpu.CompilerParams(dimension_semantics=("arbitrary",),
                                             vmem_limit_bytes=VMEM_LIMIT),
        name="ada_mod",
    )(c, w_ada, b_ada.reshape(1, n))


def _in_proj_kernel(x_ref, mod_ref, g_ref,
                    waq_ref, wak_ref, wav_ref, wbq_ref, wbk_ref, wbv_ref,
                    wiq_ref, wik_ref, wwfh_ref, wwfl_ref,
                    aqt_ref, ak_ref, avt_ref, bqt_ref, bk_ref, bvt_ref, iqt_ref, ik_ref, wft_ref):
    x = x_ref[0]
    h = _rms(x, g_ref[...]) * (1.0 + mod_ref[0, 1:2, :]) + mod_ref[0, 0:1, :]
    h_hi, h_lo = _split_bf16(h)
    aqt_ref[0] = (_dot_nt(waq_ref[...], h_hi) * Q_SCALE).astype(BF16)
    ak_ref[0] = _dot(h_hi, wak_ref[...]).astype(BF16)
    avt_ref[0] = _dot_nt(wav_ref[...], h_hi).astype(BF16)
    bqt_ref[0] = (_dot_nt(wbq_ref[...], h_hi) * Q_SCALE).astype(BF16)
    bk_ref[0] = _dot(h_hi, wbk_ref[...]).astype(BF16)
    bvt_ref[0] = _dot_nt(wbv_ref[...], h_hi).astype(BF16)
    iqt_ref[0] = _dot_nt(wiq_ref[...], h_hi).astype(BF16)
    ik_ref[0] = _dot(h_hi, wik_ref[...]).astype(BF16)
    wft_ref[0] = _dot3_nt(wwfh_ref[...], wwfl_ref[...], h_hi, h_lo)


def _in_proj(x3, mod3, g_mix, weights):
    bsz, seq, d = x3.shape
    tm = 512
    blk_t = lambda b, i: (b, 0, i)
    blk_r = lambda b, i: (b, i, 0)
    const = lambda b, i: (0, 0)
    ak_w = A_KV_HEADS * AUG
    bk_w = B_HEADS * AUG
    outs = [jax.ShapeDtypeStruct((bsz, 512, seq), BF16), jax.ShapeDtypeStruct((bsz, seq, ak_w), BF16),
            jax.ShapeDtypeStruct((bsz, 128, seq), BF16), jax.ShapeDtypeStruct((bsz, 512, seq), BF16),
            jax.ShapeDtypeStruct((bsz, seq, bk_w), BF16), jax.ShapeDtypeStruct((bsz, 512, seq), BF16),
            jax.ShapeDtypeStruct((bsz, IDX_HEADS * IDX_K, seq), BF16),
            jax.ShapeDtypeStruct((bsz, seq, IDX_K), BF16), jax.ShapeDtypeStruct((bsz, 16, seq), F32)]
    out_specs = [pl.BlockSpec((1, 512, tm), blk_t), pl.BlockSpec((1, tm, ak_w), blk_r),
                 pl.BlockSpec((1, 128, tm), blk_t), pl.BlockSpec((1, 512, tm), blk_t),
                 pl.BlockSpec((1, tm, bk_w), blk_r), pl.BlockSpec((1, 512, tm), blk_t),
                 pl.BlockSpec((1, IDX_HEADS * IDX_K, tm), blk_t),
                 pl.BlockSpec((1, tm, IDX_K), blk_r), pl.BlockSpec((1, 16, tm), blk_t)]
    return pl.pallas_call(
        _in_proj_kernel,
        out_shape=outs,
        grid=(bsz, seq // tm),
        in_specs=[pl.BlockSpec((1, tm, d), blk_r),
                  pl.BlockSpec((1, N_MOD, d), lambda b, i: (b, 0, 0)),
                  pl.BlockSpec((1, d), const)] + [pl.BlockSpec(w.shape, const) for w in weights],
        out_specs=out_specs,
        compiler_params=pltpu.CompilerParams(dimension_semantics=("arbitrary", "arbitrary"),
                                             vmem_limit_bytes=VMEM_LIMIT),
        name="in_proj",
    )(x3, mod3, g_mix.reshape(1, d), *weights)


CB = 256


def _cum_kernel(wft_ref, bfor_ref, k_ref, cumt_ref, kaug_ref):
    seq = wft_ref.shape[2]
    r = lax.broadcasted_iota(jnp.int32, (CB, CB), 0)
    cidx = lax.broadcasted_iota(jnp.int32, (CB, CB), 1)
    tri = jnp.where(r <= cidx, 1.0, 0.0).astype(BF16)
    row128 = lax.broadcasted_iota(jnp.int32, (AUG, CB), 0)
    ones_rows = jnp.where((row128 >= HEAD_DIM + 3) & (row128 < HEAD_DIM + 6), 1.0, 0.0)
    carry = jnp.zeros((8, 1), F32)
    for blk in range(seq // CB):
        cols = slice(blk * CB, (blk + 1) * CB)
        z = wft_ref[0, 8:16, cols] + bfor_ref[...]
        logf = jnp.minimum(z, 0.0) - jnp.log(1.0 + jnp.exp(-jnp.abs(z)))
        p1, p2, p3 = _split3_f32(logf)
        pieces = jnp.concatenate([p1, p2, p3, jnp.zeros_like(p1)], axis=0).astype(BF16)
        parts = _dot(pieces, tri)
        cum = parts[0:8] + parts[8:16] + parts[16:24] + carry
        carry = cum[:, CB - 1:CB]
        cum2 = cum * LOG2E
        cumt_ref[0, :, cols] = cum2
        c1, c2, c3 = _split3_f32(cum2)
        for h in range(B_HEADS):
            spare = jnp.where(row128 == HEAD_DIM, -c1[h:h + 1], ones_rows)
            spare = jnp.where(row128 == HEAD_DIM + 1, -c2[h:h + 1], spare)
            spare = jnp.where(row128 == HEAD_DIM + 2, -c3[h:h + 1], spare)
            lanes = slice(h * AUG, (h + 1) * AUG)
            kaug_ref[0, cols, lanes] = k_ref[0, cols, lanes] + spare.T.astype(BF16)


def _fox_cum(wft, b_forget, bk):
    bsz, _, seq = wft.shape
    nh = B_HEADS
    kw = bk.shape[-1]
    return pl.pallas_call(
        _cum_kernel,
        out_shape=[jax.ShapeDtypeStruct((bsz, nh, seq), F32), jax.ShapeDtypeStruct((bsz, seq, kw), BF16)],
        grid=(bsz,),
        in_specs=[pl.BlockSpec((1, 16, seq), lambda b: (b, 0, 0)),
                  pl.BlockSpec((nh, 1), lambda b: (0, 0)),
                  pl.BlockSpec((1, seq, kw), lambda b: (b, 0, 0))],
        out_specs=[pl.BlockSpec((1, nh, seq), lambda b: (b, 0, 0)),
                   pl.BlockSpec((1, seq, kw), lambda b: (b, 0, 0))],
        compiler_params=pltpu.CompilerParams(dimension_semantics=("arbitrary",),
                                             vmem_limit_bytes=VMEM_LIMIT),
        name="fox_cum",
    )(wft, b_forget.reshape(nh, 1), bk)


def _softmax_init(m_ref, l_ref, acc_ref):
    m_ref[...] = jnp.full(m_ref.shape, NEG_BIG, F32)
    l_ref[...] = jnp.zeros(l_ref.shape, F32)
    acc_ref[...] = jnp.zeros(acc_ref.shape, F32)


def _attend_chunks(items, score_fn, vt_fn, m_ref, l_ref, acc_ref, lookahead):
    ahead = min(lookahead, len(items))
    scores = {i: score_fn(items[i]) for i in range(ahead)}
    ones = jnp.ones((ONES_ROWS, KC), BF16)
    for i, item in enumerate(items):
        if i + ahead < len(items):
            scores[i + ahead] = score_fn(items[i + ahead])
        s = scores.pop(i)
        h = item[1]
        m_old = m_ref[h]
        m_new = jnp.maximum(m_old, jnp.max(s, axis=0, keepdims=True))
        alpha = jnp.exp2(m_old - m_new)
        p = jnp.exp2(s - m_new).astype(BF16)
        pv = _dot(jnp.concatenate([vt_fn(item), ones], axis=0), p)
        m_ref[h] = m_new
        l_ref[h] = alpha * l_ref[h] + pv[HEAD_DIM:HEAD_DIM + 1, :]
        rows = slice(h * HEAD_DIM, (h + 1) * HEAD_DIM)
        acc_ref[rows, :] = alpha * acc_ref[rows, :] + pv[0:HEAD_DIM, :]


def _softmax_finish(o_ref, n_heads, l_ref, acc_ref):
    for h in range(n_heads):
        rows = slice(h * HEAD_DIM, (h + 1) * HEAD_DIM)
        o_ref[0, rows, :] = acc_ref[rows, :] / l_ref[h]


def _fox_kernel(qt_ref, k_ref, vt_ref, cumt_ref, o_ref, w_ref, m_ref, l_ref, acc_ref):
    qi = pl.program_id(1)
    _softmax_init(m_ref, l_ref, acc_ref)

    row64 = lax.broadcasted_iota(jnp.int32, (AUG - HEAD_DIM, TQ), 0)
    for h in range(B_HEADS):
        c1, c2, c3 = _split3_f32(cumt_ref[0, h:h + 1, :])
        spare = jnp.where(row64 < 3, 1.0, 0.0)
        spare = jnp.where(row64 == 3, c1, spare)
        spare = jnp.where(row64 == 4, c2, spare)
        spare = jnp.where(row64 == 5, c3, spare)
        w_ref[h] = jnp.concatenate([qt_ref[0, h * HEAD_DIM:(h + 1) * HEAD_DIM, :],
                                    spare.astype(BF16)], axis=0)

    def tiles(chunks):
        rows = [_chunk(kc) for kc, _ in chunks]

        def score_fn(item):
            j, h = item
            s = _dot(k_ref[0, rows[j], h * AUG:(h + 1) * AUG], w_ref[h])
            if chunks[j][1]:
                s = jnp.where(_key_minus_query() <= 0, s, NEG_BIG)
            return s

        def vt_fn(item):
            j, h = item
            return vt_ref[0, h * HEAD_DIM:(h + 1) * HEAD_DIM, rows[j]]

        items = [(j, h) for j in range(len(chunks)) for h in range(B_HEADS)]
        _attend_chunks(items, score_fn, vt_fn, m_ref, l_ref, acc_ref, FOX_LOOKAHEAD)

    def body(pair, _):
        tiles([(2 * pair, False), (2 * pair + 1, False)])
        return 0

    lax.fori_loop(0, qi // 2, body, 0)

    @pl.when(qi % 2 == 1)
    def _():
        tiles([(qi - 1, False), (qi, True)])

    @pl.when(qi % 2 == 0)
    def _():
        tiles([(qi, True)])

    _softmax_finish(o_ref, B_HEADS, l_ref, acc_ref)


def _fox_attn(bqt, kaug, bvt, cumt):
    bsz, w, seq = bqt.shape
    blk_t = lambda b, i: (b, 0, i)
    full = lambda b, i: (b, 0, 0)
    return pl.pallas_call(
        _fox_kernel,
        out_shape=jax.ShapeDtypeStruct((bsz, w, seq), F32),
        grid=(bsz, seq // TQ),
        in_specs=[pl.BlockSpec((1, w, TQ), blk_t),
                  pl.BlockSpec((1, seq, B_HEADS * AUG), full),
                  pl.BlockSpec((1, w, seq), full),
                  pl.BlockSpec((1, B_HEADS, TQ), blk_t)],
        out_specs=pl.BlockSpec((1, w, TQ), blk_t),
        scratch_shapes=[pltpu.VMEM((B_HEADS, AUG, TQ), BF16), pltpu.VMEM((B_HEADS, 1, TQ), F32),
                        pltpu.VMEM((B_HEADS, 1, TQ), F32), pltpu.VMEM((B_HEADS * HEAD_DIM, TQ), F32)],
        compiler_params=pltpu.CompilerParams(dimension_semantics=("arbitrary", "arbitrary"),
                                             vmem_limit_bytes=VMEM_LIMIT),
        name="fox_attn",
    )(bqt, kaug, bvt, cumt)


def _bf16_pieces(value):
    pieces = []
    rest = np.float32(value)
    for _ in range(3):
        piece = np.asarray(rest).astype(BF16).astype(np.float32)
        pieces.append(float(piece))
        rest = np.float32(rest - piece)
    return pieces


def _dsa_kernel(iqt_ref, ik_ref, wft_ref, qt_ref, k_ref, posx_ref, vt_ref, o_ref,
                s_ref, thr_ref, w_ref, m_ref, l_ref, acc_ref):
    qi = pl.program_id(1)
    nch = qi + 1
    kmq = _key_minus_query()

    def score_chunk(kc):
        ik = ik_ref[0, _chunk(kc), :]
        acc = jnp.zeros((KC, TQ), F32)
        for h in range(IDX_HEADS):
            d = _dot(ik, iqt_ref[0, h * IDX_K:(h + 1) * IDX_K, :])
            acc = acc + wft_ref[0, h:h + 1, :] * jnp.maximum(d, 0.0)
        acc = jnp.where(acc == 0.0, 0.0, acc)
        causal = kmq <= (qi - kc) * KC
        s_ref[_chunk(kc), :] = jnp.where(causal, acc, -jnp.inf)

    def score_body(pair, _):
        score_chunk(2 * pair)
        score_chunk(2 * pair + 1)
        return 0

    lax.fori_loop(0, nch // 2, score_body, 0)

    @pl.when(nch % 2 == 1)
    def _():
        score_chunk(nch - 1)

    def scan(fn, init):
        def pair_body(pair, c):
            c = fn(2 * pair, s_ref[_chunk(2 * pair), :], c)
            return fn(2 * pair + 1, s_ref[_chunk(2 * pair + 1), :], c)

        c = lax.fori_loop(0, nch // 2, pair_body, init)
        return lax.cond(nch % 2 == 1, lambda c: fn(nch - 1, s_ref[_chunk(nch - 1), :], c),
                        lambda c: c, c)

    @pl.when(qi == 0)
    def _():
        thr_ref[...] = jnp.full(thr_ref.shape, LOWEST, F32)

    @pl.when(qi > 0)
    def _():
        def cmin(x):
            return jnp.min(x, axis=0, keepdims=True)

        def cmax(x):
            return jnp.max(x, axis=0, keepdims=True)

        def csum(x):
            return jnp.sum(x, axis=0, keepdims=True)

        zeros = jnp.zeros((1, TQ), F32)
        pinf = jnp.full((1, TQ), jnp.inf, F32)
        ninf = jnp.full((1, TQ), -jnp.inf, F32)

        def init_fn(_, s, c):
            lo, hi = c
            lo = jnp.minimum(lo, cmin(jnp.where(s > -jnp.inf, s, jnp.inf)))
            hi = jnp.maximum(hi, cmax(s))
            return lo, hi

        lo, hi = scan(init_fn, (pinf, ninf))

        def coarse_step(_, carry):
            lo, hi = carry
            mid = lo + (hi - lo) * 0.5
            cnt = scan(lambda _, s, c: c + csum(jnp.where(s >= mid, 1.0, 0.0)), zeros)
            enough = cnt >= float(TOPK)
            return jnp.where(enough, mid, lo), jnp.where(enough, hi, mid)

        lo, hi = lax.fori_loop(0, COARSE_STEPS, coarse_step, (lo, hi))

        def cond(carry):
            return carry[2] > 0

        def step(carry):
            lo, hi, _ = carry
            mid = lo + (hi - lo) * 0.5
            mid = jnp.where(mid <= lo, hi, mid)

            def fn(_, s, c):
                cnt, a, b = c
                ge = s >= mid
                cnt = cnt + csum(jnp.where(ge, 1.0, 0.0))
                b = jnp.minimum(b, cmin(jnp.where(ge, s, jnp.inf)))
                a = jnp.maximum(a, cmax(jnp.where(ge, -jnp.inf, s)))
                return cnt, a, b

            cnt, a, b = scan(fn, (zeros, ninf, pinf))
            enough = cnt >= float(TOPK)
            new_lo = jnp.where(enough, b, lo)
            new_hi = jnp.where(enough, jnp.where(cnt == float(TOPK), b, hi), a)
            active = jnp.max(jnp.where(new_lo < new_hi, 1, 0))
            return new_lo, new_hi, active

        first_active = jnp.max(jnp.where(lo < hi, 1, 0))
        thr, _, _ = lax.while_loop(cond, step, (lo, hi, first_active))

        def count_fn(_, s, c):
            n_gt, n_ge = c
            return (n_gt + csum(jnp.where(s > thr, 1.0, 0.0)),
                    n_ge + csum(jnp.where(s >= thr, 1.0, 0.0)))

        n_gt, n_ge = scan(count_fn, (zeros, zeros))
        thr_ref[...] = thr

        @pl.when(jnp.max(jnp.where(n_ge > float(TOPK), 1, 0)) > 0)
        def _():
            need = float(TOPK) - n_gt
            lower = jnp.where(lax.broadcasted_iota(jnp.int32, (KC, KC), 1)
                              < lax.broadcasted_iota(jnp.int32, (KC, KC), 0), 1.0, 0.0).astype(BF16)

            def sel_fn(kc, s, run):
                eq = s == thr
                eqf = jnp.where(eq, 1.0, 0.0)
                before = _dot(lower, eqf.astype(BF16)) + run
                sel = (s > thr) | (eq & (before < need))
                s_ref[_chunk(kc), :] = jnp.where(sel, 0.0, NEG_BIG)
                return run + csum(eqf)

            scan(sel_fn, zeros)
            thr_ref[...] = jnp.full(thr_ref.shape, 0.5 * NEG_BIG, F32)

    _softmax_init(m_ref, l_ref, acc_ref)
    rep = A_HEADS // A_KV_HEADS
    row64 = lax.broadcasted_iota(jnp.int32, (AUG - HEAD_DIM, TQ), 0)
    qpos = (qi * TQ + lax.broadcasted_iota(jnp.int32, (AUG - HEAD_DIM, TQ), 1)).astype(F32)
    for h in range(A_HEADS):
        slope = np.float32(2.0 ** (-8.0 * (h + 1) / A_HEADS) * LOG2E)
        u1, u2, u3 = _split3_f32(-slope * qpos)
        spare = jnp.zeros((AUG - HEAD_DIM, TQ), F32)
        for p, s_p in enumerate(_bf16_pieces(slope)):
            spare = jnp.where(row64 == p, 128.0 * s_p, spare)
            spare = jnp.where(row64 == 3 + p, s_p, spare)
        spare = jnp.where(row64 == 6, u1, spare)
        spare = jnp.where(row64 == 7, u2, spare)
        spare = jnp.where(row64 == 8, u3, spare)
        w_ref[h] = jnp.concatenate([qt_ref[0, h * HEAD_DIM:(h + 1) * HEAD_DIM, :],
                                    spare.astype(BF16)], axis=0)

    def attn_tiles(chunks):
        rows = [_chunk(kc) for kc in chunks]
        lhs = [[k_ref[0, r, g * AUG:(g + 1) * AUG] + posx_ref[r, :] for g in range(A_KV_HEADS)]
               for r in rows]
        bias = [jnp.where(s_ref[r, :] >= thr_ref[...], 0.0, NEG_BIG) for r in rows]

        def score_fn(item):
            j, h = item
            return _dot(lhs[j][h // rep], w_ref[h]) + bias[j]

        def vt_fn(item):
            j, h = item
            g = h // rep
            return vt_ref[0, g * HEAD_DIM:(g + 1) * HEAD_DIM, rows[j]]

        items = [(j, h) for j in range(len(chunks)) for h in range(A_HEADS)]
        _attend_chunks(items, score_fn, vt_fn, m_ref, l_ref, acc_ref, DSA_LOOKAHEAD)

    def attn_body(pair, _):
        attn_tiles([2 * pair, 2 * pair + 1])
        return 0

    lax.fori_loop(0, nch // 2, attn_body, 0)

    @pl.when(nch % 2 == 1)
    def _():
        attn_tiles([nch - 1])

    _softmax_finish(o_ref, A_HEADS, l_ref, acc_ref)


def _dsa_attn(iqt, ik4, wft, aqt, ak, posx, avt):
    bsz, w, seq = aqt.shape
    blk_t = lambda b, i: (b, 0, i)
    full = lambda b, i: (b, 0, 0)
    return pl.pallas_call(
        _dsa_kernel,
        out_shape=jax.ShapeDtypeStruct((bsz, w, seq), F32),
        grid=(bsz, seq // TQ),
        in_specs=[pl.BlockSpec((1, IDX_HEADS * IDX_K, TQ), blk_t),
                  pl.BlockSpec((1, seq, IDX_K), full),
                  pl.BlockSpec((1, 16, TQ), blk_t),
                  pl.BlockSpec((1, w, TQ), blk_t),
                  pl.BlockSpec((1, seq, A_KV_HEADS * AUG), full),
                  pl.BlockSpec((seq, AUG), lambda b, i: (0, 0)),
                  pl.BlockSpec((1, 128, seq), full)],
        out_specs=pl.BlockSpec((1, w, TQ), blk_t),
        scratch_shapes=[pltpu.VMEM((seq, TQ), F32), pltpu.VMEM((1, TQ), F32),
                        pltpu.VMEM((A_HEADS, AUG, TQ), BF16),
                        pltpu.VMEM((A_HEADS, 1, TQ), F32), pltpu.VMEM((A_HEADS, 1, TQ), F32),
                        pltpu.VMEM((A_HEADS * HEAD_DIM, TQ), F32)],
        compiler_params=pltpu.CompilerParams(dimension_semantics=("arbitrary", "arbitrary"),
                                             vmem_limit_bytes=VMEM_LIMIT),
        name="dsa_attn",
    )(iqt, ik4, wft, aqt, ak, posx, avt)


def _rms_cols(xt, g_col):
    return xt * lax.rsqrt(jnp.mean(xt * xt, axis=0, keepdims=True) + EPS) * g_col


MOE_TMG = 256
MOE_TMC = 256
ROW_TILE = 8
ROUTE_COLS = 128


def _first_index_of_max(vals, lane, big):
    m = jnp.max(vals, axis=-1, keepdims=True)
    idx = jnp.min(jnp.where(vals == m, lane, big), axis=-1, keepdims=True)
    return m, idx


def _route(h, wr_hi_ref, wr_lo_ref, br_ref):
    h_hi, h_lo = _split_bf16(h)
    logits = _dot3(h_hi, h_lo, wr_hi_ref[...], wr_lo_ref[...]) + br_ref[...]
    lane = lax.broadcasted_iota(jnp.int32, logits.shape, 1)
    ninf = -jnp.inf
    gl = jnp.where(lane < N_GROUPS, logits, ninf)
    gmax, gsel = _first_index_of_max(gl, lane, 1 << 20)
    p_group = 1.0 / jnp.sum(jnp.exp(gl - gmax), axis=-1, keepdims=True)
    base = 32 + gsel * EXPERTS_PER_GROUP
    el = jnp.where((lane >= base) & (lane < base + EXPERTS_PER_GROUP), logits, ninf)
    v1, i1 = _first_index_of_max(el, lane, 1 << 20)
    el2 = jnp.where(lane == i1, ninf, el)
    v2, i2 = _first_index_of_max(el2, lane, 1 << 20)
    e2 = jnp.exp(v2 - v1)
    w1 = 1.0 / (1.0 + e2)
    w2 = e2 / (1.0 + e2)
    return i1, i2, w1 * p_group, w2 * p_group


def _token_rows(s, n_tokens, first_token=0):
    return pl.ds(first_token * ROW_TILE + s, n_tokens, stride=ROW_TILE)


def _to_token_tiles(ref, x):
    for s in range(ROW_TILE):
        ref[_token_rows(s, x.shape[0]), :] = x[:, s * 128:(s + 1) * 128]


def _token_tile(ref, t):
    start = t * ROW_TILE if isinstance(t, int) else pl.multiple_of(t * ROW_TILE, ROW_TILE)
    return ref.at[pl.ds(start, ROW_TILE), :]


def _moe_route_kernel(oat_ref, obt_ref, x_ref, mod_ref, ga_ref, gb_ref, wo_ref,
                      gf_ref, wr_hi_ref, wr_lo_ref, br_ref,
                      x1_ref, h_ref, info_ref, infot_ref, cnt_ref, run_ref):
    @pl.when(pl.program_id(0) == 0)
    def _():
        run_ref[...] = jnp.zeros_like(run_ref)

    oa = _rms_cols(oat_ref[0], ga_ref[...]).astype(BF16)
    ob = _rms_cols(obt_ref[0], gb_ref[...]).astype(BF16)
    y = _dot_tn(oa, wo_ref[0:512, :]) + _dot_tn(ob, wo_ref[512:1024, :])
    x1 = x_ref[...] + mod_ref[0, 2:3, :] * y
    x1_ref[...] = x1

    h = _rms(x1, gf_ref[...]) * (1.0 + mod_ref[0, 4:5, :]) + mod_ref[0, 3:4, :]
    _to_token_tiles(h_ref, h)
    i1, i2, w1, w2 = _route(h, wr_hi_ref, wr_lo_ref, br_ref)
    tm = h.shape[0]
    lane = lax.broadcasted_iota(jnp.int32, (tm, ROUTE_COLS), 1)
    picked = jnp.where((lane == i1) | (lane == i2), 1.0, 0.0)
    earlier = jnp.where(lax.broadcasted_iota(jnp.int32, (tm, tm), 1)
                        < lax.broadcasted_iota(jnp.int32, (tm, tm), 0), 1.0, 0.0).astype(BF16)
    before = _dot(earlier, picked.astype(BF16)) + run_ref[...]
    rank1 = jnp.sum(jnp.where(lane == i1, before, 0.0), axis=-1, keepdims=True)
    rank2 = jnp.sum(jnp.where(lane == i2, before, 0.0), axis=-1, keepdims=True)
    run_ref[...] += jnp.sum(picked, axis=0, keepdims=True)
    cnt_ref[...] = run_ref[...]
    info = jnp.where(lane == 0, (i1 - 32).astype(F32), 0.0)
    info = jnp.where(lane == 1, (i2 - 32).astype(F32), info)
    info = jnp.where(lane == 2, rank1, info)
    info = jnp.where(lane == 3, rank2, info)
    info = jnp.where(lane == 4, w1, info)
    info = jnp.where(lane == 5, w2, info)
    info_ref[...] = info
    infot_ref[...] = info.T[0:8, :]


def _moe_route(oat, obt, x2, mod3, g_out_a, g_out_b, w_out_bf, g_ffn, wr_hi, wr_lo, b_route, seq):
    n, d = x2.shape
    tm = 512
    per_b = seq // tm
    row = lambda i: (i, 0)
    const = lambda i: (0, 0)
    blk_t = lambda i: (i // per_b, 0, i % per_b)
    return pl.pallas_call(
        _moe_route_kernel,
        out_shape=[jax.ShapeDtypeStruct((n, d), F32),
                   jax.ShapeDtypeStruct((n * ROW_TILE, d // ROW_TILE), F32),
                   jax.ShapeDtypeStruct((n, ROUTE_COLS), F32), jax.ShapeDtypeStruct((8, n), F32),
                   jax.ShapeDtypeStruct((1, ROUTE_COLS), F32)],
        grid=(n // tm,),
        in_specs=[pl.BlockSpec((1, 512, tm), blk_t), pl.BlockSpec((1, 512, tm), blk_t),
                  pl.BlockSpec((tm, d), row),
                  pl.BlockSpec((1, N_MOD, d), lambda i: (i // per_b, 0, 0)),
                  pl.BlockSpec((512, 1), const), pl.BlockSpec((512, 1), const),
                  pl.BlockSpec((d, d), const),
                  pl.BlockSpec((1, d), const),
                  pl.BlockSpec((d, ROUTE_COLS), const), pl.BlockSpec((d, ROUTE_COLS), const),
                  pl.BlockSpec((1, ROUTE_COLS), const)],
        out_specs=[pl.BlockSpec((tm, d), row),
                   pl.BlockSpec((tm * ROW_TILE, d // ROW_TILE), row),
                   pl.BlockSpec((tm, ROUTE_COLS), row), pl.BlockSpec((8, tm), lambda i: (0, i)),
                   pl.BlockSpec((1, ROUTE_COLS), const)],
        scratch_shapes=[pltpu.VMEM((1, ROUTE_COLS), F32)],
        compiler_params=pltpu.CompilerParams(dimension_semantics=("arbitrary",),
                                             vmem_limit_bytes=VMEM_LIMIT),
        name="mix_out_moe_route",
    )(oat, obt, x2, mod3, g_out_a.reshape(-1, 1), g_out_b.reshape(-1, 1), w_out_bf,
      g_ffn.reshape(1, d), wr_hi, wr_lo, b_route)


def _wait_token_copies(src_hbm, dst, sem, n_tokens):
    pltpu.make_async_copy(src_hbm.at[pl.ds(0, n_tokens * ROW_TILE), :], dst, sem).wait()


def _moe_scatter_kernel(dest_ref, pad_start_ref, pad_len_ref, nu_ref, h_ref, xs_hbm, zero_ref, sem):
    i = pl.program_id(0)
    n_pairs = 2 * MOE_TMC

    @pl.when(i == 0)
    def _():
        zero_ref[...] = jnp.zeros_like(zero_ref)

        def pad_copies(e):
            n = pad_len_ref[e]
            size = MOE_TMG // 2
            while size >= 1:
                first = pad_start_ref[e] + (n & ~(2 * size - 1))
                copy = pltpu.make_async_copy(
                    zero_ref.at[pl.ds(0, size * ROW_TILE), :],
                    xs_hbm.at[pl.ds(pl.multiple_of(first * ROW_TILE, ROW_TILE), size * ROW_TILE), :],
                    sem.at[1])
                yield (n & size) != 0, copy
                size //= 2

        for e in range(N_EXPERTS):
            for present, copy in pad_copies(e):
                pl.when(present)(copy.start)
        for e in range(N_EXPERTS):
            for present, copy in pad_copies(e):
                pl.when(present)(copy.wait)

        def row_tile(t):
            return xs_hbm.at[pl.ds(pl.multiple_of(t * (MOE_TMG * ROW_TILE), MOE_TMG * ROW_TILE),
                                   MOE_TMG * ROW_TILE), :]

        n_tiles = xs_hbm.shape[0] // (MOE_TMG * ROW_TILE)

        def fill_tile(t, _):
            pltpu.make_async_copy(zero_ref, row_tile(t), sem.at[1]).start()
            return 0

        def drain_tile(t, _):
            pltpu.make_async_copy(zero_ref, row_tile(0), sem.at[1]).wait()
            return 0

        lax.fori_loop(nu_ref[0], n_tiles, fill_tile, 0)
        lax.fori_loop(nu_ref[0], n_tiles, drain_tile, 0)

    for r in range(MOE_TMC):
        for j in range(2):
            pltpu.make_async_copy(_token_tile(h_ref, r),
                                  _token_tile(xs_hbm, dest_ref[i * n_pairs + j * MOE_TMC + r]),
                                  sem.at[0]).start(priority=j)
    _wait_token_copies(xs_hbm, xs_hbm.at[pl.ds(0, n_pairs * ROW_TILE), :], sem.at[0], n_pairs)


def _moe_scatter(dest, pad_start, pad_len, n_used, h2, n_rows):
    n = h2.shape[0] // ROW_TILE
    grid_spec = pltpu.PrefetchScalarGridSpec(
        num_scalar_prefetch=4,
        grid=(n // MOE_TMC,),
        in_specs=[pl.BlockSpec((MOE_TMC * ROW_TILE, h2.shape[1]), lambda i, *_: (i, 0))],
        out_specs=pl.BlockSpec(memory_space=pl.ANY),
        scratch_shapes=[pltpu.VMEM((MOE_TMG * ROW_TILE, h2.shape[1]), F32),
                        pltpu.SemaphoreType.DMA((2,))])
    return pl.pallas_call(
        _moe_scatter_kernel,
        out_shape=jax.ShapeDtypeStruct((n_rows * ROW_TILE, h2.shape[1]), F32),
        grid_spec=grid_spec,
        compiler_params=pltpu.CompilerParams(dimension_semantics=("arbitrary",),
                                             vmem_limit_bytes=VMEM_LIMIT),
        name="moe_scatter",
    )(dest, pad_start, pad_len, n_used, h2)


def _moe_expert_kernel(te_ref, nu_ref, x_ref, wg_ref, wu_ref, wd_ref, y_ref,
                       xs_ref, wgb_ref, wub_ref, wdb_ref):
    i = pl.program_id(0)
    used = i < nu_ref[0]
    new_expert = jnp.logical_or(i == 0, te_ref[i] != te_ref[jnp.maximum(i - 1, 0)])

    @pl.when(jnp.logical_and(used, new_expert))
    def _():
        wgb_ref[...] = wg_ref[0].astype(BF16)
        wub_ref[...] = wu_ref[0].astype(BF16)
        wdb_ref[...] = wd_ref[0].astype(BF16)

    @pl.when(used)
    def _():
        for s in range(ROW_TILE):
            xs_ref[:, s * 128:(s + 1) * 128] = x_ref[_token_rows(s, MOE_TMG), :].astype(BF16)
        x = xs_ref[...]
        hg = _dot(x, wgb_ref[...])
        hu = _dot(x, wub_ref[...])
        a = hg * jax.nn.sigmoid(hg) * hu
        _to_token_tiles(y_ref, _dot(a.astype(BF16), wdb_ref[...]))

    @pl.when(i >= nu_ref[0])
    def _():
        y_ref[...] = jnp.zeros_like(y_ref)


def _moe_experts(tile_expert, n_used, xsorted, wg, wu, wd):
    n_rows = xsorted.shape[0] // ROW_TILE
    d, ff = wg.shape[1], wg.shape[2]
    tile = (MOE_TMG * ROW_TILE, xsorted.shape[1])
    grid_spec = pltpu.PrefetchScalarGridSpec(
        num_scalar_prefetch=2,
        grid=(n_rows // MOE_TMG,),
        in_specs=[pl.BlockSpec(tile, lambda i, te, nu: (jnp.minimum(i, nu[0] - 1), 0)),
                  pl.BlockSpec((1, d, ff), lambda i, te, nu: (te[i], 0, 0)),
                  pl.BlockSpec((1, d, ff), lambda i, te, nu: (te[i], 0, 0)),
                  pl.BlockSpec((1, ff, d), lambda i, te, nu: (te[i], 0, 0))],
        out_specs=pl.BlockSpec(tile, lambda i, te, nu: (i, 0)),
        scratch_shapes=[pltpu.VMEM((MOE_TMG, d), BF16), pltpu.VMEM((d, ff), BF16),
                        pltpu.VMEM((d, ff), BF16), pltpu.VMEM((ff, d), BF16)])
    return pl.pallas_call(
        _moe_expert_kernel,
        out_shape=jax.ShapeDtypeStruct(xsorted.shape, F32),
        grid_spec=grid_spec,
        compiler_params=pltpu.CompilerParams(dimension_semantics=("arbitrary",),
                                             vmem_limit_bytes=VMEM_LIMIT),
        name="moe_experts",
    )(tile_expert, n_used, xsorted, wg, wu, wd)


def _moe_combine_kernel(dest_ref, y_hbm, x_ref, info_ref, mod_ref, gfin_ref, o_ref, ybuf, x2_ref, sem):
    i = pl.program_id(0)
    nt = pl.num_programs(0)
    slot = lax.rem(i, 2)
    n_pairs = 2 * MOE_TMC

    def start_gather(tile, to_slot):
        for r in range(n_pairs):
            pltpu.make_async_copy(_token_tile(y_hbm, dest_ref[tile * n_pairs + r]),
                                  _token_tile(ybuf.at[to_slot], r), sem.at[to_slot]).start(priority=r % 2)

    @pl.when(i == 0)
    def _():
        start_gather(0, 0)

    _wait_token_copies(y_hbm, ybuf.at[slot], sem.at[slot], n_pairs)

    @pl.when(i + 1 < nt)
    def _():
        start_gather(i + 1, 1 - slot)

    w1 = info_ref[:, 4:5]
    w2 = info_ref[:, 5:6]
    sumsq = jnp.zeros((MOE_TMC, 1), F32)
    for s in range(ROW_TILE):
        cols = slice(s * 128, (s + 1) * 128)
        y = (w1 * ybuf[slot, _token_rows(s, MOE_TMC), :]
             + w2 * ybuf[slot, _token_rows(s, MOE_TMC, first_token=MOE_TMC), :])
        x2 = x_ref[:, cols] + mod_ref[0, 5:6, cols] * y
        x2_ref[:, cols] = x2
        sumsq = sumsq + jnp.sum(x2 * x2, axis=-1, keepdims=True)
    d = x2_ref.shape[1]
    o_ref[...] = x2_ref[...] * lax.rsqrt(sumsq / d + EPS) * gfin_ref[...]


def _moe_combine(dest, ysorted, x1, info, mod3, g_final, seq):
    n, d = x1.shape
    per_b = seq // MOE_TMC
    grid_spec = pltpu.PrefetchScalarGridSpec(
        num_scalar_prefetch=1,
        grid=(n // MOE_TMC,),
        in_specs=[pl.BlockSpec(memory_space=pl.ANY),
                  pl.BlockSpec((MOE_TMC, d), lambda i, ds: (i, 0)),
                  pl.BlockSpec((MOE_TMC, ROUTE_COLS), lambda i, ds: (i, 0)),
                  pl.BlockSpec((1, N_MOD, d), lambda i, ds: (i // per_b, 0, 0)),
                  pl.BlockSpec((1, d), lambda i, ds: (0, 0))],
        out_specs=pl.BlockSpec((MOE_TMC, d), lambda i, ds: (i, 0)),
        scratch_shapes=[pltpu.VMEM((2, 2 * MOE_TMC * ROW_TILE, ysorted.shape[1]), F32),
                        pltpu.VMEM((MOE_TMC, d), F32), pltpu.SemaphoreType.DMA((2,))])
    return pl.pallas_call(
        _moe_combine_kernel,
        out_shape=jax.ShapeDtypeStruct((n, d), F32),
        grid_spec=grid_spec,
        compiler_params=pltpu.CompilerParams(dimension_semantics=("arbitrary",),
                                             vmem_limit_bytes=VMEM_LIMIT),
        name="moe_combine",
    )(dest, ysorted, x1, info, mod3, g_final.reshape(1, d))


def _mix_out_and_moe(oat, obt, x2, mod3, g_out_a, g_out_b, w_out_bf, g_ffn, wr_hi, wr_lo, b_route,
                     wg, wu, wd, g_final, seq):
    n, d = x2.shape
    x1, h2, info, infot, counts = _moe_route(oat, obt, x2, mod3, g_out_a, g_out_b, w_out_bf,
                                             g_ffn, wr_hi, wr_lo, b_route, seq)

    e1, e2, rank1, rank2 = [infot[k].astype(jnp.int32) for k in range(4)]
    cnt = counts[0, 32:32 + N_EXPERTS].astype(jnp.int32)
    padded = ((cnt + MOE_TMG - 1) // MOE_TMG) * MOE_TMG
    seg_end = jnp.cumsum(padded)
    expert_ids = jnp.arange(N_EXPERTS, dtype=jnp.int32)

    def seg_start_of(e):
        return jnp.sum(jnp.where(expert_ids[None, :] < e[:, None], padded[None, :], 0), axis=1)

    dest1 = seg_start_of(e1) + rank1
    dest2 = seg_start_of(e2) + rank2
    n_rows = 2 * n + N_EXPERTS * MOE_TMG
    tile_start = jnp.arange(n_rows // MOE_TMG, dtype=jnp.int32) * MOE_TMG
    tile_expert = jnp.minimum(jnp.sum((tile_start[:, None] >= seg_end[None, :]).astype(jnp.int32), axis=1),
                              N_EXPERTS - 1)
    n_used = (seg_end[N_EXPERTS - 1:] // MOE_TMG).astype(jnp.int32)
    dest = jnp.concatenate([dest1.reshape(-1, MOE_TMC), dest2.reshape(-1, MOE_TMC)], axis=1).reshape(-1)

    seg_start = seg_end - padded
    xsorted = _moe_scatter(dest, seg_start + cnt, padded - cnt, n_used, h2, n_rows)
    ysorted = _moe_experts(tile_expert, n_used, xsorted, wg, wu, wd)
    return _moe_combine(dest, ysorted, x1, info, mod3, g_final, seq)


def _layer(x3, c, w_ada, b_ada, g_mix, w_in, b_forget, g_out_a, g_out_b, w_out,
           g_ffn, w_group, b_group, w_router, b_router, w_gate, w_up, w_down, g_final):
    bsz, seq, d = x3.shape
    mod3 = _ada_mod(c, w_ada, b_ada).reshape(bsz, N_MOD, d)

    w_t = w_in.T

    def pad_heads(w, n_heads):
        w = w.reshape(d, n_heads, HEAD_DIM)
        return jnp.concatenate([w, jnp.zeros_like(w)], axis=-1).reshape(d, n_heads * AUG)

    w_wf_t = jnp.concatenate([w_t[1344:1352], w_t[2888:2896]], axis=0)
    weights = [w_t[0:512].astype(BF16), pad_heads(w_in[:, 512:640], A_KV_HEADS).astype(BF16),
               w_t[640:768].astype(BF16), w_t[1352:1864].astype(BF16),
               pad_heads(w_in[:, 1864:2376], B_HEADS).astype(BF16), w_t[2376:2888].astype(BF16),
               w_t[768:1280].astype(BF16), w_in[:, 1280:1344].astype(BF16), *_split_bf16(w_wf_t)]
    aqt, ak, avt, bqt, bk, bvt, iqt, ik4, wft = _in_proj(x3, mod3, g_mix, weights)

    cumt, kaug = _fox_cum(wft, b_forget, bk)
    obt = _fox_attn(bqt, kaug, bvt, cumt)

    pos = jnp.arange(seq, dtype=jnp.int32)[:, None]
    lane = jnp.arange(AUG, dtype=jnp.int32)[None, :] - HEAD_DIM
    posx = jnp.where((lane >= 0) & (lane < 3), pos >> 7,
                     jnp.where((lane >= 3) & (lane < 6), pos & 127,
                               jnp.where((lane >= 6) & (lane < 9), 1, 0))).astype(BF16)
    oat = _dsa_attn(iqt, ik4, wft, aqt, ak, posx, avt)

    w_r = jnp.concatenate([w_group, jnp.zeros((d, 32 - N_GROUPS), F32),
                           jnp.transpose(w_router, (1, 0, 2)).reshape(d, N_EXPERTS),
                           jnp.zeros((d, ROUTE_COLS - 64), F32)], axis=1)
    b_r = jnp.concatenate([b_group, jnp.zeros((32 - N_GROUPS,), F32), b_router.reshape(-1),
                           jnp.zeros((ROUTE_COLS - 64,), F32)]).reshape(1, ROUTE_COLS)
    wr_hi, wr_lo = _split_bf16(w_r)
    out = _mix_out_and_moe(oat, obt, x3.reshape(bsz * seq, d), mod3, g_out_a, g_out_b,
                           w_out.astype(BF16), g_ffn, wr_hi, wr_lo, b_r, w_gate, w_up, w_down,
                           g_final, seq)
    return out.reshape(bsz, seq, d)


def kernel(x, c, w_ada, b_ada, g_mix, w_in, b_forget, g_out_a, g_out_b, w_out, g_ffn, w_group,
           b_group, w_router, b_router, w_gate, w_up, w_down, g_final):
    depth = w_ada.shape[0]
    assert depth == 1, "final norm is fused into the single layer's MoE kernel"
    return _layer(x, c, w_ada[0], b_ada[0], g_mix[0], w_in[0], b_forget[0], g_out_a[0], g_out_b[0],
                  w_out[0], g_ffn[0], w_group[0], b_group[0], w_router[0], b_router[0], w_gate[0],
                  w_up[0], w_down[0], g_final)
```

```python
import math

import jax
import jax.numpy as jnp
import numpy as np
from jax import lax
from jax.experimental import pallas as pl
from jax.experimental.pallas import tpu as pltpu

F32 = jnp.float32
BF16 = jnp.bfloat16

EPS = 1e-6
A_HEADS = 8
A_KV_HEADS = 2
HEAD_DIM = 64
IDX_HEADS = 8
IDX_DIM = 64
TOPK = 256
B_HEADS = 8
N_GROUPS = 4
EXPERTS_PER_GROUP = 8
N_EXPERTS = N_GROUPS * EXPERTS_PER_GROUP
N_MOD = 6

NEG_BIG = -1e30
LOWEST = float(np.finfo(np.float32).min)
LOG2E = math.log2(math.e)
Q_SCALE = HEAD_DIM ** -0.5 * LOG2E
VMEM_LIMIT = 48 * 1024 * 1024

TQ = 256
KC = 256
AUG = 128
IDX_K = IDX_DIM
ONES_ROWS = 16
COARSE_STEPS = 12
FOX_LOOKAHEAD = 6
DSA_LOOKAHEAD = 4
assert TQ == KC == TOPK


def _split_bf16(x):
    hi = x.astype(BF16)
    lo = (x - hi.astype(F32)).astype(BF16)
    return hi, lo


def _split3_f32(x):
    p1 = x.astype(BF16).astype(F32)
    r1 = x - p1
    p2 = r1.astype(BF16).astype(F32)
    p3 = (r1 - p2).astype(BF16).astype(F32)
    return p1, p2, p3


def _dot(a, b):
    return jnp.dot(a, b, preferred_element_type=F32)


def _dot_nt(a, b):
    return lax.dot_general(a, b, (((1,), (1,)), ((), ())), preferred_element_type=F32)


def _dot_tn(a, b):
    return lax.dot_general(a, b, (((0,), (0,)), ((), ())), preferred_element_type=F32)


def _dot3(a_hi, a_lo, b_hi, b_lo):
    return _dot(a_hi, b_hi) + _dot(a_lo, b_hi) + _dot(a_hi, b_lo)


def _dot3_nt(a_hi, a_lo, b_hi, b_lo):
    return _dot_nt(a_hi, b_hi) + _dot_nt(a_lo, b_hi) + _dot_nt(a_hi, b_lo)


def _rms(x, g):
    return x * lax.rsqrt(jnp.mean(x * x, axis=-1, keepdims=True) + EPS) * g


def _chunk(kc):
    return pl.ds(pl.multiple_of(kc * KC, KC), KC)


def _key_minus_query():
    return (lax.broadcasted_iota(jnp.int32, (KC, TQ), 0)
            - lax.broadcasted_iota(jnp.int32, (KC, TQ), 1))


def _ada_kernel(c_ref, w_ref, b_ref, o_ref):
    c = c_ref[...]
    s = c * jax.nn.sigmoid(c)
    s_hi, s_lo = _split_bf16(s)
    w_hi, w_lo = _split_bf16(w_ref[...])
    o_ref[...] = _dot3(s_hi, s_lo, w_hi, w_lo) + b_ref[...]


def _ada_mod(c, w_ada, b_ada):
    bsz, d = c.shape
    n = w_ada.shape[1]
    tn = 1024
    return pl.pallas_call(
        _ada_kernel,
        out_shape=jax.ShapeDtypeStruct((bsz, n), F32),
        grid=(n // tn,),
        in_specs=[pl.BlockSpec((bsz, d), lambda j: (0, 0)),
                  pl.BlockSpec((d, tn), lambda j: (0, j)),
                  pl.BlockSpec((1, tn), lambda j: (0, j))],
        out_specs=pl.BlockSpec((bsz, tn), lambda j: (0, j)),
        compiler_params=pltpu.CompilerParams(dimension_semantics=("arbitrary",),
                                             vmem_limit_bytes=VMEM_LIMIT),
        name="ada_mod",
    )(c, w_ada, b_ada.reshape(1, n))


def _in_proj_kernel(x_ref, mod_ref, g_ref,
                    waq_ref, wak_ref, wav_ref, wbq_ref, wbk_ref, wbv_ref,
                    wiq_ref, wik_ref, wwfh_ref, wwfl_ref,
                    aqt_ref, ak_ref, avt_ref, bqt_ref, bk_ref, bvt_ref, iqt_ref, ik_ref, wft_ref):
    x = x_ref[0]
    h = _rms(x, g_ref[...]) * (1.0 + mod_ref[0, 1:2, :]) + mod_ref[0, 0:1, :]
    h_hi, h_lo = _split_bf16(h)
    aqt_ref[0] = (_dot_nt(waq_ref[...], h_hi) * Q_SCALE).astype(BF16)
    ak_ref[0] = _dot(h_hi, wak_ref[...]).astype(BF16)
    avt_ref[0] = _dot_nt(wav_ref[...], h_hi).astype(BF16)
    bqt_ref[0] = (_dot_nt(wbq_ref[...], h_hi) * Q_SCALE).astype(BF16)
    bk_ref[0] = _dot(h_hi, wbk_ref[...]).astype(BF16)
    bvt_ref[0] = _dot_nt(wbv_ref[...], h_hi).astype(BF16)
    iqt_ref[0] = _dot_nt(wiq_ref[...], h_hi).astype(BF16)
    ik_ref[0] = _dot(h_hi, wik_ref[...]).astype(BF16)
    wft_ref[0] = _dot3_nt(wwfh_ref[...], wwfl_ref[...], h_hi, h_lo)


def _in_proj(x3, mod3, g_mix, weights):
    bsz, seq, d = x3.shape
    tm = 512
    blk_t = lambda b, i: (b, 0, i)
    blk_r = lambda b, i: (b, i, 0)
    const = lambda b, i: (0, 0)
    ak_w = A_KV_HEADS * AUG
    bk_w = B_HEADS * AUG
    outs = [jax.ShapeDtypeStruct((bsz, 512, seq), BF16), jax.ShapeDtypeStruct((bsz, seq, ak_w), BF16),
            jax.ShapeDtypeStruct((bsz, 128, seq), BF16), jax.ShapeDtypeStruct((bsz, 512, seq), BF16),
            jax.ShapeDtypeStruct((bsz, seq, bk_w), BF16), jax.ShapeDtypeStruct((bsz, 512, seq), BF16),
            jax.ShapeDtypeStruct((bsz, IDX_HEADS * IDX_K, seq), BF16),
            jax.ShapeDtypeStruct((bsz, seq, IDX_K), BF16), jax.ShapeDtypeStruct((bsz, 16, seq), F32)]
    out_specs = [pl.BlockSpec((1, 512, tm), blk_t), pl.BlockSpec((1, tm, ak_w), blk_r),
                 pl.BlockSpec((1, 128, tm), blk_t), pl.BlockSpec((1, 512, tm), blk_t),
                 pl.BlockSpec((1, tm, bk_w), blk_r), pl.BlockSpec((1, 512, tm), blk_t),
                 pl.BlockSpec((1, IDX_HEADS * IDX_K, tm), blk_t),
                 pl.BlockSpec((1, tm, IDX_K), blk_r), pl.BlockSpec((1, 16, tm), blk_t)]
    return pl.pallas_call(
        _in_proj_kernel,
        out_shape=outs,
        grid=(bsz, seq // tm),
        in_specs=[pl.BlockSpec((1, tm, d), blk_r),
                  pl.BlockSpec((1, N_MOD, d), lambda b, i: (b, 0, 0)),
                  pl.BlockSpec((1, d), const)] + [pl.BlockSpec(w.shape, const) for w in weights],
        out_specs=out_specs,
        compiler_params=pltpu.CompilerParams(dimension_semantics=("arbitrary", "arbitrary"),
                                             vmem_limit_bytes=VMEM_LIMIT),
        name="in_proj",
    )(x3, mod3, g_mix.reshape(1, d), *weights)


CB = 256


def _cum_kernel(wft_ref, bfor_ref, k_ref, cumt_ref, kaug_ref):
    seq = wft_ref.shape[2]
    r = lax.broadcasted_iota(jnp.int32, (CB, CB), 0)
    cidx = lax.broadcasted_iota(jnp.int32, (CB, CB), 1)
    tri = jnp.where(r <= cidx, 1.0, 0.0).astype(BF16)
    row128 = lax.broadcasted_iota(jnp.int32, (AUG, CB), 0)
    ones_rows = jnp.where((row128 >= HEAD_DIM + 3) & (row128 < HEAD_DIM + 6), 1.0, 0.0)
    carry = jnp.zeros((8, 1), F32)
    for blk in range(seq // CB):
        cols = slice(blk * CB, (blk + 1) * CB)
        z = wft_ref[0, 8:16, cols] + bfor_ref[...]
        logf = jnp.minimum(z, 0.0) - jnp.log(1.0 + jnp.exp(-jnp.abs(z)))
        p1, p2, p3 = _split3_f32(logf)
        pieces = jnp.concatenate([p1, p2, p3, jnp.zeros_like(p1)], axis=0).astype(BF16)
        parts = _dot(pieces, tri)
        cum = parts[0:8] + parts[8:16] + parts[16:24] + carry
        carry = cum[:, CB - 1:CB]
        cum2 = cum * LOG2E
        cumt_ref[0, :, cols] = cum2
        c1, c2, c3 = _split3_f32(cum2)
        for h in range(B_HEADS):
            spare = jnp.where(row128 == HEAD_DIM, -c1[h:h + 1], ones_rows)
            spare = jnp.where(row128 == HEAD_DIM + 1, -c2[h:h + 1], spare)
            spare = jnp.where(row128 == HEAD_DIM + 2, -c3[h:h + 1], spare)
            lanes = slice(h * AUG, (h + 1) * AUG)
            kaug_ref[0, cols, lanes] = k_ref[0, cols, lanes] + spare.T.astype(BF16)


def _fox_cum(wft, b_forget, bk):
    bsz, _, seq = wft.shape
    nh = B_HEADS
    kw = bk.shape[-1]
    return pl.pallas_call(
        _cum_kernel,
        out_shape=[jax.ShapeDtypeStruct((bsz, nh, seq), F32), jax.ShapeDtypeStruct((bsz, seq, kw), BF16)],
        grid=(bsz,),
        in_specs=[pl.BlockSpec((1, 16, seq), lambda b: (b, 0, 0)),
                  pl.BlockSpec((nh, 1), lambda b: (0, 0)),
                  pl.BlockSpec((1, seq, kw), lambda b: (b, 0, 0))],
        out_specs=[pl.BlockSpec((1, nh, seq), lambda b: (b, 0, 0)),
                   pl.BlockSpec((1, seq, kw), lambda b: (b, 0, 0))],
        compiler_params=pltpu.CompilerParams(dimension_semantics=("arbitrary",),
                                             vmem_limit_bytes=VMEM_LIMIT),
        name="fox_cum",
    )(wft, b_forget.reshape(nh, 1), bk)


def _softmax_init(m_ref, l_ref, acc_ref):
    m_ref[...] = jnp.full(m_ref.shape, NEG_BIG, F32)
    l_ref[...] = jnp.zeros(l_ref.shape, F32)
    acc_ref[...] = jnp.zeros(acc_ref.shape, F32)


def _attend_chunks(items, score_fn, vt_fn, m_ref, l_ref, acc_ref, lookahead):
    ahead = min(lookahead, len(items))
    scores = {i: score_fn(items[i]) for i in range(ahead)}
    ones = jnp.ones((ONES_ROWS, KC), BF16)
    for i, item in enumerate(items):
        if i + ahead < len(items):
            scores[i + ahead] = score_fn(items[i + ahead])
        s = scores.pop(i)
        h = item[1]
        m_old = m_ref[h]
        m_new = jnp.maximum(m_old, jnp.max(s, axis=0, keepdims=True))
        alpha = jnp.exp2(m_old - m_new)
        p = jnp.exp2(s - m_new).astype(BF16)
        pv = _dot(jnp.concatenate([vt_fn(item), ones], axis=0), p)
        m_ref[h] = m_new
        l_ref[h] = alpha * l_ref[h] + pv[HEAD_DIM:HEAD_DIM + 1, :]
        rows = slice(h * HEAD_DIM, (h + 1) * HEAD_DIM)
        acc_ref[rows, :] = alpha * acc_ref[rows, :] + pv[0:HEAD_DIM, :]


def _softmax_finish(o_ref, n_heads, l_ref, acc_ref):
    for h in range(n_heads):
        rows = slice(h * HEAD_DIM, (h + 1) * HEAD_DIM)
        o_ref[0, rows, :] = acc_ref[rows, :] / l_ref[h]


def _fox_kernel(qt_ref, k_ref, vt_ref, cumt_ref, o_ref, w_ref, m_ref, l_ref, acc_ref):
    qi = pl.program_id(1)
    _softmax_init(m_ref, l_ref, acc_ref)

    row64 = lax.broadcasted_iota(jnp.int32, (AUG - HEAD_DIM, TQ), 0)
    for h in range(B_HEADS):
        c1, c2, c3 = _split3_f32(cumt_ref[0, h:h + 1, :])
        spare = jnp.where(row64 < 3, 1.0, 0.0)
        spare = jnp.where(row64 == 3, c1, spare)
        spare = jnp.where(row64 == 4, c2, spare)
        spare = jnp.where(row64 == 5, c3, spare)
        w_ref[h] = jnp.concatenate([qt_ref[0, h * HEAD_DIM:(h + 1) * HEAD_DIM, :],
                                    spare.astype(BF16)], axis=0)

    def tiles(chunks):
        rows = [_chunk(kc) for kc, _ in chunks]

        def score_fn(item):
            j, h = item
            s = _dot(k_ref[0, rows[j], h * AUG:(h + 1) * AUG], w_ref[h])
            if chunks[j][1]:
                s = jnp.where(_key_minus_query() <= 0, s, NEG_BIG)
            return s

        def vt_fn(item):
            j, h = item
            return vt_ref[0, h * HEAD_DIM:(h + 1) * HEAD_DIM, rows[j]]

        items = [(j, h) for j in range(len(chunks)) for h in range(B_HEADS)]
        _attend_chunks(items, score_fn, vt_fn, m_ref, l_ref, acc_ref, FOX_LOOKAHEAD)

    def body(pair, _):
        tiles([(2 * pair, False), (2 * pair + 1, False)])
        return 0

    lax.fori_loop(0, qi // 2, body, 0)

    @pl.when(qi % 2 == 1)
    def _():
        tiles([(qi - 1, False), (qi, True)])

    @pl.when(qi % 2 == 0)
    def _():
        tiles([(qi, True)])

    _softmax_finish(o_ref, B_HEADS, l_ref, acc_ref)


def _fox_attn(bqt, kaug, bvt, cumt):
    bsz, w, seq = bqt.shape
    blk_t = lambda b, i: (b, 0, i)
    full = lambda b, i: (b, 0, 0)
    return pl.pallas_call(
        _fox_kernel,
        out_shape=jax.ShapeDtypeStruct((bsz, w, seq), F32),
        grid=(bsz, seq // TQ),
        in_specs=[pl.BlockSpec((1, w, TQ), blk_t),
                  pl.BlockSpec((1, seq, B_HEADS * AUG), full),
                  pl.BlockSpec((1, w, seq), full),
                  pl.BlockSpec((1, B_HEADS, TQ), blk_t)],
        out_specs=pl.BlockSpec((1, w, TQ), blk_t),
        scratch_shapes=[pltpu.VMEM((B_HEADS, AUG, TQ), BF16), pltpu.VMEM((B_HEADS, 1, TQ), F32),
                        pltpu.VMEM((B_HEADS, 1, TQ), F32), pltpu.VMEM((B_HEADS * HEAD_DIM, TQ), F32)],
        compiler_params=pltpu.CompilerParams(dimension_semantics=("arbitrary", "arbitrary"),
                                             vmem_limit_bytes=VMEM_LIMIT),
        name="fox_attn",
    )(bqt, kaug, bvt, cumt)


def _bf16_pieces(value):
    pieces = []
    rest = np.float32(value)
    for _ in range(3):
        piece = np.asarray(rest).astype(BF16).astype(np.float32)
        pieces.append(float(piece))
        rest = np.float32(rest - piece)
    return pieces


def _dsa_kernel(iqt_ref, ik_ref, wft_ref, qt_ref, k_ref, posx_ref, vt_ref, o_ref,
                s_ref, thr_ref, w_ref, m_ref, l_ref, acc_ref):
    qi = pl.program_id(1)
    nch = qi + 1
    kmq = _key_minus_query()

    def score_chunk(kc):
        ik = ik_ref[0, _chunk(kc), :]
        acc = jnp.zeros((KC, TQ), F32)
        for h in range(IDX_HEADS):
            d = _dot(ik, iqt_ref[0, h * IDX_K:(h + 1) * IDX_K, :])
            acc = acc + wft_ref[0, h:h + 1, :] * jnp.maximum(d, 0.0)
        acc = jnp.where(acc == 0.0, 0.0, acc)
        causal = kmq <= (qi - kc) * KC
        s_ref[_chunk(kc), :] = jnp.where(causal, acc, -jnp.inf)

    def score_body(pair, _):
        score_chunk(2 * pair)
        score_chunk(2 * pair + 1)
        return 0

    lax.fori_loop(0, nch // 2, score_body, 0)

    @pl.when(nch % 2 == 1)
    def _():
        score_chunk(nch - 1)
        s_ref[_chunk(nch), :] = jnp.full((KC, TQ), -jnp.inf, F32)

    def scan(fn, init):
        def pair_body(pair, c):
            c = fn(2 * pair, s_ref[_chunk(2 * pair), :], c)
            return fn(2 * pair + 1, s_ref[_chunk(2 * pair + 1), :], c)

        return lax.fori_loop(0, (nch + 1) // 2, pair_body, init)

    @pl.when(qi == 0)
    def _():
        thr_ref[...] = jnp.full(thr_ref.shape, LOWEST, F32)

    @pl.when(qi > 0)
    def _():
        def part(x, op):
            return op(x.reshape(KC // 8, 8, TQ), axis=0)

        def cmin(x):
            return part(x, jnp.min)

        def cmax(x):
            return part(x, jnp.max)

        def csum(x):
            return part(x, jnp.sum)

        def fold(x, op):
            return op(x, axis=0, keepdims=True)

        zeros = jnp.zeros((8, TQ), F32)
        pinf = jnp.full((8, TQ), jnp.inf, F32)
        ninf = jnp.full((8, TQ), -jnp.inf, F32)

        def init_fn(_, s, c):
            lo, hi = c
            lo = jnp.minimum(lo, cmin(jnp.where(s > -jnp.inf, s, jnp.inf)))
            hi = jnp.maximum(hi, cmax(s))
            return lo, hi

        lo, hi = scan(init_fn, (pinf, ninf))
        lo, hi = fold(lo, jnp.min), fold(hi, jnp.max)

        def coarse_step(_, carry):
            lo, hi = carry
            mid = lo + (hi - lo) * 0.5
            cnt = scan(lambda _, s, c: c + csum(jnp.where(s >= mid, 1.0, 0.0)), zeros)
            enough = fold(cnt, jnp.sum) >= float(TOPK)
            return jnp.where(enough, mid, lo), jnp.where(enough, hi, mid)

        lo, hi = lax.fori_loop(0, COARSE_STEPS, coarse_step, (lo, hi))

        def cond(carry):
            return carry[2] > 0

        def step(carry):
            lo, hi, _ = carry
            mid = lo + (hi - lo) * 0.5
            mid = jnp.where(mid <= lo, hi, mid)

            def fn(_, s, c):
                cnt, a, b = c
                ge = s >= mid
                cnt = cnt + csum(jnp.where(ge, 1.0, 0.0))
                b = jnp.minimum(b, cmin(jnp.where(ge, s, jnp.inf)))
                a = jnp.maximum(a, cmax(jnp.where(ge, -jnp.inf, s)))
                return cnt, a, b

            cnt, a, b = scan(fn, (zeros, ninf, pinf))
            cnt, a, b = fold(cnt, jnp.sum), fold(a, jnp.max), fold(b, jnp.min)
            enough = cnt >= float(TOPK)
            new_lo = jnp.where(enough, b, lo)
            new_hi = jnp.where(enough, jnp.where(cnt == float(TOPK), b, hi), a)
            active = jnp.max(jnp.where(new_lo < new_hi, 1, 0))
            return new_lo, new_hi, active

        first_active = jnp.max(jnp.where(lo < hi, 1, 0))
        thr, _, _ = lax.while_loop(cond, step, (lo, hi, first_active))

        def count_fn(_, s, c):
            n_gt, n_ge = c
            return (n_gt + csum(jnp.where(s > thr, 1.0, 0.0)),
                    n_ge + csum(jnp.where(s >= thr, 1.0, 0.0)))

        n_gt, n_ge = scan(count_fn, (zeros, zeros))
        n_gt, n_ge = fold(n_gt, jnp.sum), fold(n_ge, jnp.sum)
        thr_ref[...] = thr

        @pl.when(jnp.max(jnp.where(n_ge > float(TOPK), 1, 0)) > 0)
        def _():
            need = float(TOPK) - n_gt
            lower = jnp.where(lax.broadcasted_iota(jnp.int32, (KC, KC), 1)
                              < lax.broadcasted_iota(jnp.int32, (KC, KC), 0), 1.0, 0.0).astype(BF16)

            def sel_fn(kc, s, run):
                eq = s == thr
                eqf = jnp.where(eq, 1.0, 0.0)
                before = _dot(lower, eqf.astype(BF16)) + run
                sel = (s > thr) | (eq & (before < need))
                s_ref[_chunk(kc), :] = jnp.where(sel, 0.0, NEG_BIG)
                return run + jnp.sum(eqf, axis=0, keepdims=True)

            scan(sel_fn, jnp.zeros((1, TQ), F32))
            thr_ref[...] = jnp.full(thr_ref.shape, 0.5 * NEG_BIG, F32)

    _softmax_init(m_ref, l_ref, acc_ref)
    rep = A_HEADS // A_KV_HEADS
    row64 = lax.broadcasted_iota(jnp.int32, (AUG - HEAD_DIM, TQ), 0)
    qpos = (qi * TQ + lax.broadcasted_iota(jnp.int32, (AUG - HEAD_DIM, TQ), 1)).astype(F32)
    for h in range(A_HEADS):
        slope = np.float32(2.0 ** (-8.0 * (h + 1) / A_HEADS) * LOG2E)
        u1, u2, u3 = _split3_f32(-slope * qpos)
        spare = jnp.zeros((AUG - HEAD_DIM, TQ), F32)
        for p, s_p in enumerate(_bf16_pieces(slope)):
            spare = jnp.where(row64 == p, 128.0 * s_p, spare)
            spare = jnp.where(row64 == 3 + p, s_p, spare)
        spare = jnp.where(row64 == 6, u1, spare)
        spare = jnp.where(row64 == 7, u2, spare)
        spare = jnp.where(row64 == 8, u3, spare)
        w_ref[h] = jnp.concatenate([qt_ref[0, h * HEAD_DIM:(h + 1) * HEAD_DIM, :],
                                    spare.astype(BF16)], axis=0)

    def attn_tiles(chunks):
        rows = [_chunk(kc) for kc in chunks]
        lhs = [[k_ref[0, r, g * AUG:(g + 1) * AUG] + posx_ref[r, :] for g in range(A_KV_HEADS)]
               for r in rows]
        bias = [jnp.where(s_ref[r, :] >= thr_ref[...], 0.0, NEG_BIG) for r in rows]

        def score_fn(item):
            j, h = item
            return _dot(lhs[j][h // rep], w_ref[h]) + bias[j]

        def vt_fn(item):
            j, h = item
            g = h // rep
            return vt_ref[0, g * HEAD_DIM:(g + 1) * HEAD_DIM, rows[j]]

        items = [(j, h) for j in range(len(chunks)) for h in range(A_HEADS)]
        _attend_chunks(items, score_fn, vt_fn, m_ref, l_ref, acc_ref, DSA_LOOKAHEAD)

    def attn_body(pair, _):
        attn_tiles([2 * pair, 2 * pair + 1])
        return 0

    lax.fori_loop(0, nch // 2, attn_body, 0)

    @pl.when(nch % 2 == 1)
    def _():
        attn_tiles([nch - 1])

    _softmax_finish(o_ref, A_HEADS, l_ref, acc_ref)


def _dsa_attn(iqt, ik4, wft, aqt, ak, posx, avt):
    bsz, w, seq = aqt.shape
    blk_t = lambda b, i: (b, 0, i)
    full = lambda b, i: (b, 0, 0)
    return pl.pallas_call(
        _dsa_kernel,
        out_shape=jax.ShapeDtypeStruct((bsz, w, seq), F32),
        grid=(bsz, seq // TQ),
        in_specs=[pl.BlockSpec((1, IDX_HEADS * IDX_K, TQ), blk_t),
                  pl.BlockSpec((1, seq, IDX_K), full),
                  pl.BlockSpec((1, 16, TQ), blk_t),
                  pl.BlockSpec((1, w, TQ), blk_t),
                  pl.BlockSpec((1, seq, A_KV_HEADS * AUG), full),
                  pl.BlockSpec((seq, AUG), lambda b, i: (0, 0)),
                  pl.BlockSpec((1, 128, seq), full)],
        out_specs=pl.BlockSpec((1, w, TQ), blk_t),
        scratch_shapes=[pltpu.VMEM((seq, TQ), F32), pltpu.VMEM((1, TQ), F32),
                        pltpu.VMEM((A_HEADS, AUG, TQ), BF16),
                        pltpu.VMEM((A_HEADS, 1, TQ), F32), pltpu.VMEM((A_HEADS, 1, TQ), F32),
                        pltpu.VMEM((A_HEADS * HEAD_DIM, TQ), F32)],
        compiler_params=pltpu.CompilerParams(dimension_semantics=("arbitrary", "arbitrary"),
                                             vmem_limit_bytes=VMEM_LIMIT),
        name="dsa_attn",
    )(iqt, ik4, wft, aqt, ak, posx, avt)


def _rms_cols(xt, g_col):
    return xt * lax.rsqrt(jnp.mean(xt * xt, axis=0, keepdims=True) + EPS) * g_col


MOE_TMG = 256
MOE_TMC = 256
ROW_TILE = 8
ROUTE_COLS = 128


def _first_index_of_max(vals, lane, big):
    m = jnp.max(vals, axis=-1, keepdims=True)
    idx = jnp.min(jnp.where(vals == m, lane, big), axis=-1, keepdims=True)
    return m, idx


def _route(h, wr_hi_ref, wr_lo_ref, br_ref):
    h_hi, h_lo = _split_bf16(h)
    logits = _dot3(h_hi, h_lo, wr_hi_ref[...], wr_lo_ref[...]) + br_ref[...]
    lane = lax.broadcasted_iota(jnp.int32, logits.shape, 1)
    ninf = -jnp.inf
    gl = jnp.where(lane < N_GROUPS, logits, ninf)
    gmax, gsel = _first_index_of_max(gl, lane, 1 << 20)
    p_group = 1.0 / jnp.sum(jnp.exp(gl - gmax), axis=-1, keepdims=True)
    base = 32 + gsel * EXPERTS_PER_GROUP
    el = jnp.where((lane >= base) & (lane < base + EXPERTS_PER_GROUP), logits, ninf)
    v1, i1 = _first_index_of_max(el, lane, 1 << 20)
    el2 = jnp.where(lane == i1, ninf, el)
    v2, i2 = _first_index_of_max(el2, lane, 1 << 20)
    e2 = jnp.exp(v2 - v1)
    w1 = 1.0 / (1.0 + e2)
    w2 = e2 / (1.0 + e2)
    return i1, i2, w1 * p_group, w2 * p_group


def _token_rows(s, n_tokens, first_token=0):
    return pl.ds(first_token * ROW_TILE + s, n_tokens, stride=ROW_TILE)


def _to_token_tiles(ref, x):
    for s in range(ROW_TILE):
        ref[_token_rows(s, x.shape[0]), :] = x[:, s * 128:(s + 1) * 128]


def _token_tile(ref, t):
    start = t * ROW_TILE if isinstance(t, int) else pl.multiple_of(t * ROW_TILE, ROW_TILE)
    return ref.at[pl.ds(start, ROW_TILE), :]


def _moe_route_kernel(oat_ref, obt_ref, x_ref, mod_ref, ga_ref, gb_ref, wo_ref,
                      gf_ref, wr_hi_ref, wr_lo_ref, br_ref,
                      x1_ref, h_ref, info_ref, infot_ref, cnt_ref, run_ref):
    @pl.when(pl.program_id(0) == 0)
    def _():
        run_ref[...] = jnp.zeros_like(run_ref)

    oa = _rms_cols(oat_ref[0], ga_ref[...]).astype(BF16)
    ob = _rms_cols(obt_ref[0], gb_ref[...]).astype(BF16)
    y = _dot_tn(oa, wo_ref[0:512, :]) + _dot_tn(ob, wo_ref[512:1024, :])
    x1 = x_ref[...] + mod_ref[0, 2:3, :] * y
    x1_ref[...] = x1

    h = _rms(x1, gf_ref[...]) * (1.0 + mod_ref[0, 4:5, :]) + mod_ref[0, 3:4, :]
    _to_token_tiles(h_ref, h)
    i1, i2, w1, w2 = _route(h, wr_hi_ref, wr_lo_ref, br_ref)
    tm = h.shape[0]
    lane = lax.broadcasted_iota(jnp.int32, (tm, ROUTE_COLS), 1)
    picked = jnp.where((lane == i1) | (lane == i2), 1.0, 0.0)
    earlier = jnp.where(lax.broadcasted_iota(jnp.int32, (tm, tm), 1)
                        < lax.broadcasted_iota(jnp.int32, (tm, tm), 0), 1.0, 0.0).astype(BF16)
    before = _dot(earlier, picked.astype(BF16)) + run_ref[...]
    rank1 = jnp.sum(jnp.where(lane == i1, before, 0.0), axis=-1, keepdims=True)
    rank2 = jnp.sum(jnp.where(lane == i2, before, 0.0), axis=-1, keepdims=True)
    run_ref[...] += jnp.sum(picked, axis=0, keepdims=True)
    cnt_ref[...] = run_ref[...]
    info = jnp.where(lane == 0, (i1 - 32).astype(F32), 0.0)
    info = jnp.where(lane == 1, (i2 - 32).astype(F32), info)
    info = jnp.where(lane == 2, rank1, info)
    info = jnp.where(lane == 3, rank2, info)
    info = jnp.where(lane == 4, w1, info)
    info = jnp.where(lane == 5, w2, info)
    info_ref[...] = info
    infot_ref[...] = info.T[0:8, :]


def _moe_route(oat, obt, x2, mod3, g_out_a, g_out_b, w_out_bf, g_ffn, wr_hi, wr_lo, b_route, seq):
    n, d = x2.shape
    tm = 512
    per_b = seq // tm
    row = lambda i: (i, 0)
    const = lambda i: (0, 0)
    blk_t = lambda i: (i // per_b, 0, i % per_b)
    return pl.pallas_call(
        _moe_route_kernel,
        out_shape=[jax.ShapeDtypeStruct((n, d), F32),
                   jax.ShapeDtypeStruct((n * ROW_TILE, d // ROW_TILE), F32),
                   jax.ShapeDtypeStruct((n, ROUTE_COLS), F32), jax.ShapeDtypeStruct((8, n), F32),
                   jax.ShapeDtypeStruct((1, ROUTE_COLS), F32)],
        grid=(n // tm,),
        in_specs=[pl.BlockSpec((1, 512, tm), blk_t), pl.BlockSpec((1, 512, tm), blk_t),
                  pl.BlockSpec((tm, d), row),
                  pl.BlockSpec((1, N_MOD, d), lambda i: (i // per_b, 0, 0)),
                  pl.BlockSpec((512, 1), const), pl.BlockSpec((512, 1), const),
                  pl.BlockSpec((d, d), const),
                  pl.BlockSpec((1, d), const),
                  pl.BlockSpec((d, ROUTE_COLS), const), pl.BlockSpec((d, ROUTE_COLS), const),
                  pl.BlockSpec((1, ROUTE_COLS), const)],
        out_specs=[pl.BlockSpec((tm, d), row),
                   pl.BlockSpec((tm * ROW_TILE, d // ROW_TILE), row),
                   pl.BlockSpec((tm, ROUTE_COLS), row), pl.BlockSpec((8, tm), lambda i: (0, i)),
                   pl.BlockSpec((1, ROUTE_COLS), const)],
        scratch_shapes=[pltpu.VMEM((1, ROUTE_COLS), F32)],
        compiler_params=pltpu.CompilerParams(dimension_semantics=("arbitrary",),
                                             vmem_limit_bytes=VMEM_LIMIT),
        name="mix_out_moe_route",
    )(oat, obt, x2, mod3, g_out_a.reshape(-1, 1), g_out_b.reshape(-1, 1), w_out_bf,
      g_ffn.reshape(1, d), wr_hi, wr_lo, b_route)


def _wait_token_copies(src_hbm, dst, sem, n_tokens):
    pltpu.make_async_copy(src_hbm.at[pl.ds(0, n_tokens * ROW_TILE), :], dst, sem).wait()


def _moe_scatter_kernel(dest_ref, pad_start_ref, pad_len_ref, nu_ref, h_ref, xs_hbm, zero_ref, sem):
    i = pl.program_id(0)
    n_pairs = 2 * MOE_TMC

    @pl.when(i == 0)
    def _():
        zero_ref[...] = jnp.zeros_like(zero_ref)

        def pad_copies(e):
            n = pad_len_ref[e]
            size = MOE_TMG // 2
            while size >= 1:
                first = pad_start_ref[e] + (n & ~(2 * size - 1))
                copy = pltpu.make_async_copy(
                    zero_ref.at[pl.ds(0, size * ROW_TILE), :],
                    xs_hbm.at[pl.ds(pl.multiple_of(first * ROW_TILE, ROW_TILE), size * ROW_TILE), :],
                    sem.at[1])
                yield (n & size) != 0, copy
                size //= 2

        for e in range(N_EXPERTS):
            for present, copy in pad_copies(e):
                pl.when(present)(copy.start)
        for e in range(N_EXPERTS):
            for present, copy in pad_copies(e):
                pl.when(present)(copy.wait)

        def row_tile(t):
            return xs_hbm.at[pl.ds(pl.multiple_of(t * (MOE_TMG * ROW_TILE), MOE_TMG * ROW_TILE),
                                   MOE_TMG * ROW_TILE), :]

        n_tiles = xs_hbm.shape[0] // (MOE_TMG * ROW_TILE)

        def fill_tile(t, _):
            pltpu.make_async_copy(zero_ref, row_tile(t), sem.at[1]).start()
            return 0

        def drain_tile(t, _):
            pltpu.make_async_copy(zero_ref, row_tile(0), sem.at[1]).wait()
            return 0

        lax.fori_loop(nu_ref[0], n_tiles, fill_tile, 0)
        lax.fori_loop(nu_ref[0], n_tiles, drain_tile, 0)

    for r in range(MOE_TMC):
        for j in range(2):
            pltpu.make_async_copy(_token_tile(h_ref, r),
                                  _token_tile(xs_hbm, dest_ref[i * n_pairs + j * MOE_TMC + r]),
                                  sem.at[0]).start(priority=j)
    _wait_token_copies(xs_hbm, xs_hbm.at[pl.ds(0, n_pairs * ROW_TILE), :], sem.at[0], n_pairs)


def _moe_scatter(dest, pad_start, pad_len, n_used, h2, n_rows):
    n = h2.shape[0] // ROW_TILE
    grid_spec = pltpu.PrefetchScalarGridSpec(
        num_scalar_prefetch=4,
        grid=(n // MOE_TMC,),
        in_specs=[pl.BlockSpec((MOE_TMC * ROW_TILE, h2.shape[1]), lambda i, *_: (i, 0))],
        out_specs=pl.BlockSpec(memory_space=pl.ANY),
        scratch_shapes=[pltpu.VMEM((MOE_TMG * ROW_TILE, h2.shape[1]), F32),
                        pltpu.SemaphoreType.DMA((2,))])
    return pl.pallas_call(
        _moe_scatter_kernel,
        out_shape=jax.ShapeDtypeStruct((n_rows * ROW_TILE, h2.shape[1]), F32),
        grid_spec=grid_spec,
        compiler_params=pltpu.CompilerParams(dimension_semantics=("arbitrary",),
                                             vmem_limit_bytes=VMEM_LIMIT),
        name="moe_scatter",
    )(dest, pad_start, pad_len, n_used, h2)


def _moe_expert_kernel(te_ref, nu_ref, x_ref, wg_ref, wu_ref, wd_ref, y_ref,
                       xs_ref, wgb_ref, wub_ref, wdb_ref):
    i = pl.program_id(0)
    used = i < nu_ref[0]
    new_expert = jnp.logical_or(i == 0, te_ref[i] != te_ref[jnp.maximum(i - 1, 0)])

    @pl.when(jnp.logical_and(used, new_expert))
    def _():
        wgb_ref[...] = wg_ref[0].astype(BF16)
        wub_ref[...] = wu_ref[0].astype(BF16)
        wdb_ref[...] = wd_ref[0].astype(BF16)

    @pl.when(used)
    def _():
        for s in range(ROW_TILE):
            xs_ref[:, s * 128:(s + 1) * 128] = x_ref[_token_rows(s, MOE_TMG), :].astype(BF16)
        x = xs_ref[...]
        hg = _dot(x, wgb_ref[...])
        hu = _dot(x, wub_ref[...])
        a = hg * jax.nn.sigmoid(hg) * hu
        _to_token_tiles(y_ref, _dot(a.astype(BF16), wdb_ref[...]))

    @pl.when(i >= nu_ref[0])
    def _():
        y_ref[...] = jnp.zeros_like(y_ref)


def _moe_experts(tile_expert, n_used, xsorted, wg, wu, wd):
    n_rows = xsorted.shape[0] // ROW_TILE
    d, ff = wg.shape[1], wg.shape[2]
    tile = (MOE_TMG * ROW_TILE, xsorted.shape[1])
    grid_spec = pltpu.PrefetchScalarGridSpec(
        num_scalar_prefetch=2,
        grid=(n_rows // MOE_TMG,),
        in_specs=[pl.BlockSpec(tile, lambda i, te, nu: (jnp.minimum(i, nu[0] - 1), 0)),
                  pl.BlockSpec((1, d, ff), lambda i, te, nu: (te[i], 0, 0)),
                  pl.BlockSpec((1, d, ff), lambda i, te, nu: (te[i], 0, 0)),
                  pl.BlockSpec((1, ff, d), lambda i, te, nu: (te[i], 0, 0))],
        out_specs=pl.BlockSpec(tile, lambda i, te, nu: (i, 0)),
        scratch_shapes=[pltpu.VMEM((MOE_TMG, d), BF16), pltpu.VMEM((d, ff), BF16),
                        pltpu.VMEM((d, ff), BF16), pltpu.VMEM((ff, d), BF16)])
    return pl.pallas_call(
        _moe_expert_kernel,
        out_shape=jax.ShapeDtypeStruct(xsorted.shape, F32),
        grid_spec=grid_spec,
        compiler_params=pltpu.CompilerParams(dimension_semantics=("arbitrary",),
                                             vmem_limit_bytes=VMEM_LIMIT),
        name="moe_experts",
    )(tile_expert, n_used, xsorted, wg, wu, wd)


def _moe_combine_kernel(dest_ref, y_hbm, x_ref, info_ref, mod_ref, gfin_ref, o_ref, ybuf, x2_ref, sem):
    i = pl.program_id(0)
    nt = pl.num_programs(0)
    slot = lax.rem(i, 2)
    n_pairs = 2 * MOE_TMC

    def start_gather(tile, to_slot):
        for r in range(n_pairs):
            pltpu.make_async_copy(_token_tile(y_hbm, dest_ref[tile * n_pairs + r]),
                                  _token_tile(ybuf.at[to_slot], r), sem.at[to_slot]).start(priority=r % 2)

    @pl.when(i == 0)
    def _():
        start_gather(0, 0)

    _wait_token_copies(y_hbm, ybuf.at[slot], sem.at[slot], n_pairs)

    @pl.when(i + 1 < nt)
    def _():
        start_gather(i + 1, 1 - slot)

    w1 = info_ref[:, 4:5]
    w2 = info_ref[:, 5:6]
    sumsq = jnp.zeros((MOE_TMC, 1), F32)
    for s in range(ROW_TILE):
        cols = slice(s * 128, (s + 1) * 128)
        y = (w1 * ybuf[slot, _token_rows(s, MOE_TMC), :]
             + w2 * ybuf[slot, _token_rows(s, MOE_TMC, first_token=MOE_TMC), :])
        x2 = x_ref[:, cols] + mod_ref[0, 5:6, cols] * y
        x2_ref[:, cols] = x2
        sumsq = sumsq + jnp.sum(x2 * x2, axis=-1, keepdims=True)
    d = x2_ref.shape[1]
    o_ref[...] = x2_ref[...] * lax.rsqrt(sumsq / d + EPS) * gfin_ref[...]


def _moe_combine(dest, ysorted, x1, info, mod3, g_final, seq):
    n, d = x1.shape
    per_b = seq // MOE_TMC
    grid_spec = pltpu.PrefetchScalarGridSpec(
        num_scalar_prefetch=1,
        grid=(n // MOE_TMC,),
        in_specs=[pl.BlockSpec(memory_space=pl.ANY),
                  pl.BlockSpec((MOE_TMC, d), lambda i, ds: (i, 0)),
                  pl.BlockSpec((MOE_TMC, ROUTE_COLS), lambda i, ds: (i, 0)),
                  pl.BlockSpec((1, N_MOD, d), lambda i, ds: (i // per_b, 0, 0)),
                  pl.BlockSpec((1, d), lambda i, ds: (0, 0))],
        out_specs=pl.BlockSpec((MOE_TMC, d), lambda i, ds: (i, 0)),
        scratch_shapes=[pltpu.VMEM((2, 2 * MOE_TMC * ROW_TILE, ysorted.shape[1]), F32),
                        pltpu.VMEM((MOE_TMC, d), F32), pltpu.SemaphoreType.DMA((2,))])
    return pl.pallas_call(
        _moe_combine_kernel,
        out_shape=jax.ShapeDtypeStruct((n, d), F32),
        grid_spec=grid_spec,
        compiler_params=pltpu.CompilerParams(dimension_semantics=("arbitrary",),
                                             vmem_limit_bytes=VMEM_LIMIT),
        name="moe_combine",
    )(dest, ysorted, x1, info, mod3, g_final.reshape(1, d))


def _mix_out_and_moe(oat, obt, x2, mod3, g_out_a, g_out_b, w_out_bf, g_ffn, wr_hi, wr_lo, b_route,
                     wg, wu, wd, g_final, seq):
    n, d = x2.shape
    x1, h2, info, infot, counts = _moe_route(oat, obt, x2, mod3, g_out_a, g_out_b, w_out_bf,
                                             g_ffn, wr_hi, wr_lo, b_route, seq)

    e1, e2, rank1, rank2 = [infot[k].astype(jnp.int32) for k in range(4)]
    cnt = counts[0, 32:32 + N_EXPERTS].astype(jnp.int32)
    padded = ((cnt + MOE_TMG - 1) // MOE_TMG) * MOE_TMG
    seg_end = jnp.cumsum(padded)
    expert_ids = jnp.arange(N_EXPERTS, dtype=jnp.int32)

    def seg_start_of(e):
        return jnp.sum(jnp.where(expert_ids[None, :] < e[:, None], padded[None, :], 0), axis=1)

    dest1 = seg_start_of(e1) + rank1
    dest2 = seg_start_of(e2) + rank2
    n_rows = 2 * n + N_EXPERTS * MOE_TMG
    tile_start = jnp.arange(n_rows // MOE_TMG, dtype=jnp.int32) * MOE_TMG
    tile_expert = jnp.minimum(jnp.sum((tile_start[:, None] >= seg_end[None, :]).astype(jnp.int32), axis=1),
                              N_EXPERTS - 1)
    n_used = (seg_end[N_EXPERTS - 1:] // MOE_TMG).astype(jnp.int32)
    dest = jnp.concatenate([dest1.reshape(-1, MOE_TMC), dest2.reshape(-1, MOE_TMC)], axis=1).reshape(-1)

    seg_start = seg_end - padded
    xsorted = _moe_scatter(dest, seg_start + cnt, padded - cnt, n_used, h2, n_rows)
    ysorted = _moe_experts(tile_expert, n_used, xsorted, wg, wu, wd)
    return _moe_combine(dest, ysorted, x1, info, mod3, g_final, seq)


def _layer(x3, c, w_ada, b_ada, g_mix, w_in, b_forget, g_out_a, g_out_b, w_out,
           g_ffn, w_group, b_group, w_router, b_router, w_gate, w_up, w_down, g_final):
    bsz, seq, d = x3.shape
    mod3 = _ada_mod(c, w_ada, b_ada).reshape(bsz, N_MOD, d)

    w_t = w_in.T

    def pad_heads(w, n_heads):
        w = w.reshape(d, n_heads, HEAD_DIM)
        return jnp.concatenate([w, jnp.zeros_like(w)], axis=-1).reshape(d, n_heads * AUG)

    w_wf_t = jnp.concatenate([w_t[1344:1352], w_t[2888:2896]], axis=0)
    weights = [w_t[0:512].astype(BF16), pad_heads(w_in[:, 512:640], A_KV_HEADS).astype(BF16),
               w_t[640:768].astype(BF16), w_t[1352:1864].astype(BF16),
               pad_heads(w_in[:, 1864:2376], B_HEADS).astype(BF16), w_t[2376:2888].astype(BF16),
               w_t[768:1280].astype(BF16), w_in[:, 1280:1344].astype(BF16), *_split_bf16(w_wf_t)]
    aqt, ak, avt, bqt, bk, bvt, iqt, ik4, wft = _in_proj(x3, mod3, g_mix, weights)

    cumt, kaug = _fox_cum(wft, b_forget, bk)
    obt = _fox_attn(bqt, kaug, bvt, cumt)

    pos = jnp.arange(seq, dtype=jnp.int32)[:, None]
    lane = jnp.arange(AUG, dtype=jnp.int32)[None, :] - HEAD_DIM
    posx = jnp.where((lane >= 0) & (lane < 3), pos >> 7,
                     jnp.where((lane >= 3) & (lane < 6), pos & 127,
                               jnp.where((lane >= 6) & (lane < 9), 1, 0))).astype(BF16)
    oat = _dsa_attn(iqt, ik4, wft, aqt, ak, posx, avt)

    w_r = jnp.concatenate([w_group, jnp.zeros((d, 32 - N_GROUPS), F32),
                           jnp.transpose(w_router, (1, 0, 2)).reshape(d, N_EXPERTS),
                           jnp.zeros((d, ROUTE_COLS - 64), F32)], axis=1)
    b_r = jnp.concatenate([b_group, jnp.zeros((32 - N_GROUPS,), F32), b_router.reshape(-1),
                           jnp.zeros((ROUTE_COLS - 64,), F32)]).reshape(1, ROUTE_COLS)
    wr_hi, wr_lo = _split_bf16(w_r)
    out = _mix_out_and_moe(oat, obt, x3.reshape(bsz * seq, d), mod3, g_out_a, g_out_b,
                           w_out.astype(BF16), g_ffn, wr_hi, wr_lo, b_r, w_gate, w_up, w_down,
                           g_final, seq)
    return out.reshape(bsz, seq, d)


def kernel(x, c, w_ada, b_ada, g_mix, w_in, b_forget, g_out_a, g_out_b, w_out, g_ffn, w_group,
           b_group, w_router, b_router, w_gate, w_up, w_down, g_final):
    depth = w_ada.shape[0]
    assert depth == 1, "final norm is fused into the single layer's MoE kernel"
    return _layer(x, c, w_ada[0], b_ada[0], g_mix[0], w_in[0], b_forget[0], g_out_a[0], g_out_b[0],
                  w_out[0], g_ffn[0], w_group[0], b_group[0], w_router[0], b_router[0], w_gate[0],
                  w_up[0], w_down[0], g_final)
```

```python
import math

import jax
import jax.numpy as jnp
import numpy as np
from jax import lax
from jax.experimental import pallas as pl
from jax.experimental.pallas import tpu as pltpu

F32 = jnp.float32
BF16 = jnp.bfloat16

EPS = 1e-6
A_HEADS = 8
A_KV_HEADS = 2
HEAD_DIM = 64
IDX_HEADS = 8
IDX_DIM = 64
TOPK = 256
B_HEADS = 8
N_GROUPS = 4
EXPERTS_PER_GROUP = 8
N_EXPERTS = N_GROUPS * EXPERTS_PER_GROUP
N_MOD = 6

NEG_BIG = -1e30
LOWEST = float(np.finfo(np.float32).min)
LOG2E = math.log2(math.e)
Q_SCALE = HEAD_DIM ** -0.5 * LOG2E
VMEM_LIMIT = 48 * 1024 * 1024

TQ = 256
KC = 256
AUG = 128
IDX_K = IDX_DIM
ONES_ROWS = 16
COARSE_STEPS = 14
FOX_LOOKAHEAD = 6
DSA_LOOKAHEAD = 4
assert TQ == KC == TOPK


def _split_bf16(x):
    hi = x.astype(BF16)
    lo = (x - hi.astype(F32)).astype(BF16)
    return hi, lo


def _split3_f32(x):
    p1 = x.astype(BF16).astype(F32)
    r1 = x - p1
    p2 = r1.astype(BF16).astype(F32)
    p3 = (r1 - p2).astype(BF16).astype(F32)
    return p1, p2, p3


def _dot(a, b):
    return jnp.dot(a, b, preferred_element_type=F32)


def _dot_nt(a, b):
    return lax.dot_general(a, b, (((1,), (1,)), ((), ())), preferred_element_type=F32)


def _dot_tn(a, b):
    return lax.dot_general(a, b, (((0,), (0,)), ((), ())), preferred_element_type=F32)


def _dot3(a_hi, a_lo, b_hi, b_lo):
    return _dot(a_hi, b_hi) + _dot(a_lo, b_hi) + _dot(a_hi, b_lo)


def _dot3_nt(a_hi, a_lo, b_hi, b_lo):
    return _dot_nt(a_hi, b_hi) + _dot_nt(a_lo, b_hi) + _dot_nt(a_hi, b_lo)


def _rms(x, g):
    return x * lax.rsqrt(jnp.mean(x * x, axis=-1, keepdims=True) + EPS) * g


def _chunk(kc):
    return pl.ds(pl.multiple_of(kc * KC, KC), KC)


def _key_minus_query():
    return (lax.broadcasted_iota(jnp.int32, (KC, TQ), 0)
            - lax.broadcasted_iota(jnp.int32, (KC, TQ), 1))


def _ada_kernel(c_ref, w_ref, b_ref, o_ref):
    c = c_ref[...]
    s = c * jax.nn.sigmoid(c)
    s_hi, s_lo = _split_bf16(s)
    w_hi, w_lo = _split_bf16(w_ref[...])
    o_ref[...] = _dot3(s_hi, s_lo, w_hi, w_lo) + b_ref[...]


def _ada_mod(c, w_ada, b_ada):
    bsz, d = c.shape
    n = w_ada.shape[1]
    tn = 1024
    return pl.pallas_call(
        _ada_kernel,
        out_shape=jax.ShapeDtypeStruct((bsz, n), F32),
        grid=(n // tn,),
        in_specs=[pl.BlockSpec((bsz, d), lambda j: (0, 0)),
                  pl.BlockSpec((d, tn), lambda j: (0, j)),
                  pl.BlockSpec((1, tn), lambda j: (0, j))],
        out_specs=pl.BlockSpec((bsz, tn), lambda j: (0, j)),
        compiler_params=pltpu.CompilerParams(dimension_semantics=("arbitrary",),
                                             vmem_limit_bytes=VMEM_LIMIT),
        name="ada_mod",
    )(c, w_ada, b_ada.reshape(1, n))


def _in_proj_kernel(x_ref, mod_ref, g_ref,
                    waq_ref, wak_ref, wav_ref, wbq_ref, wbk_ref, wbv_ref,
                    wiq_ref, wik_ref, wwfh_ref, wwfl_ref,
                    aqt_ref, ak_ref, avt_ref, bqt_ref, bk_ref, bvt_ref, iqt_ref, ik_ref, wft_ref):
    x = x_ref[0]
    h = _rms(x, g_ref[...]) * (1.0 + mod_ref[0, 1:2, :]) + mod_ref[0, 0:1, :]
    h_hi, h_lo = _split_bf16(h)
    aqt_ref[0] = (_dot_nt(waq_ref[...], h_hi) * Q_SCALE).astype(BF16)
    ak_ref[0] = _dot(h_hi, wak_ref[...]).astype(BF16)
    avt_ref[0] = _dot_nt(wav_ref[...], h_hi).astype(BF16)
    bqt_ref[0] = (_dot_nt(wbq_ref[...], h_hi) * Q_SCALE).astype(BF16)
    bk_ref[0] = _dot(h_hi, wbk_ref[...]).astype(BF16)
    bvt_ref[0] = _dot_nt(wbv_ref[...], h_hi).astype(BF16)
    iqt_ref[0] = _dot_nt(wiq_ref[...], h_hi).astype(BF16)
    ik_ref[0] = _dot(h_hi, wik_ref[...]).astype(BF16)
    wft_ref[0] = _dot3_nt(wwfh_ref[...], wwfl_ref[...], h_hi, h_lo)


def _in_proj(x3, mod3, g_mix, weights):
    bsz, seq, d = x3.shape
    tm = 512
    blk_t = lambda b, i: (b, 0, i)
    blk_r = lambda b, i: (b, i, 0)
    const = lambda b, i: (0, 0)
    ak_w = A_KV_HEADS * AUG
    bk_w = B_HEADS * AUG
    outs = [jax.ShapeDtypeStruct((bsz, 512, seq), BF16), jax.ShapeDtypeStruct((bsz, seq, ak_w), BF16),
            jax.ShapeDtypeStruct((bsz, 128, seq), BF16), jax.ShapeDtypeStruct((bsz, 512, seq), BF16),
            jax.ShapeDtypeStruct((bsz, seq, bk_w), BF16), jax.ShapeDtypeStruct((bsz, 512, seq), BF16),
            jax.ShapeDtypeStruct((bsz, IDX_HEADS * IDX_K, seq), BF16),
            jax.ShapeDtypeStruct((bsz, seq, IDX_K), BF16), jax.ShapeDtypeStruct((bsz, 16, seq), F32)]
    out_specs = [pl.BlockSpec((1, 512, tm), blk_t), pl.BlockSpec((1, tm, ak_w), blk_r),
                 pl.BlockSpec((1, 128, tm), blk_t), pl.BlockSpec((1, 512, tm), blk_t),
                 pl.BlockSpec((1, tm, bk_w), blk_r), pl.BlockSpec((1, 512, tm), blk_t),
                 pl.BlockSpec((1, IDX_HEADS * IDX_K, tm), blk_t),
                 pl.BlockSpec((1, tm, IDX_K), blk_r), pl.BlockSpec((1, 16, tm), blk_t)]
    return pl.pallas_call(
        _in_proj_kernel,
        out_shape=outs,
        grid=(bsz, seq // tm),
        in_specs=[pl.BlockSpec((1, tm, d), blk_r),
                  pl.BlockSpec((1, N_MOD, d), lambda b, i: (b, 0, 0)),
                  pl.BlockSpec((1, d), const)] + [pl.BlockSpec(w.shape, const) for w in weights],
        out_specs=out_specs,
        compiler_params=pltpu.CompilerParams(dimension_semantics=("arbitrary", "arbitrary"),
                                             vmem_limit_bytes=VMEM_LIMIT),
        name="in_proj",
    )(x3, mod3, g_mix.reshape(1, d), *weights)


CB = 256


def _cum_kernel(wft_ref, bfor_ref, k_ref, cumt_ref, kaug_ref):
    seq = wft_ref.shape[2]
    r = lax.broadcasted_iota(jnp.int32, (CB, CB), 0)
    cidx = lax.broadcasted_iota(jnp.int32, (CB, CB), 1)
    tri = jnp.where(r <= cidx, 1.0, 0.0).astype(BF16)
    row128 = lax.broadcasted_iota(jnp.int32, (AUG, CB), 0)
    ones_rows = jnp.where((row128 >= HEAD_DIM + 3) & (row128 < HEAD_DIM + 6), 1.0, 0.0)
    carry = jnp.zeros((8, 1), F32)
    for blk in range(seq // CB):
        cols = slice(blk * CB, (blk + 1) * CB)
        z = wft_ref[0, 8:16, cols] + bfor_ref[...]
        logf = jnp.minimum(z, 0.0) - jnp.log(1.0 + jnp.exp(-jnp.abs(z)))
        p1, p2, p3 = _split3_f32(logf)
        pieces = jnp.concatenate([p1, p2, p3, jnp.zeros_like(p1)], axis=0).astype(BF16)
        parts = _dot(pieces, tri)
        cum = parts[0:8] + parts[8:16] + parts[16:24] + carry
        carry = cum[:, CB - 1:CB]
        cum2 = cum * LOG2E
        cumt_ref[0, :, cols] = cum2
        c1, c2, c3 = _split3_f32(cum2)
        for h in range(B_HEADS):
            spare = jnp.where(row128 == HEAD_DIM, -c1[h:h + 1], ones_rows)
            spare = jnp.where(row128 == HEAD_DIM + 1, -c2[h:h + 1], spare)
            spare = jnp.where(row128 == HEAD_DIM + 2, -c3[h:h + 1], spare)
            lanes = slice(h * AUG, (h + 1) * AUG)
            kaug_ref[0, cols, lanes] = k_ref[0, cols, lanes] + spare.T.astype(BF16)


def _fox_cum(wft, b_forget, bk):
    bsz, _, seq = wft.shape
    nh = B_HEADS
    kw = bk.shape[-1]
    return pl.pallas_call(
        _cum_kernel,
        out_shape=[jax.ShapeDtypeStruct((bsz, nh, seq), F32), jax.ShapeDtypeStruct((bsz, seq, kw), BF16)],
        grid=(bsz,),
        in_specs=[pl.BlockSpec((1, 16, seq), lambda b: (b, 0, 0)),
                  pl.BlockSpec((nh, 1), lambda b: (0, 0)),
                  pl.BlockSpec((1, seq, kw), lambda b: (b, 0, 0))],
        out_specs=[pl.BlockSpec((1, nh, seq), lambda b: (b, 0, 0)),
                   pl.BlockSpec((1, seq, kw), lambda b: (b, 0, 0))],
        compiler_params=pltpu.CompilerParams(dimension_semantics=("arbitrary",),
                                             vmem_limit_bytes=VMEM_LIMIT),
        name="fox_cum",
    )(wft, b_forget.reshape(nh, 1), bk)


def _softmax_init(m_ref, l_ref, acc_ref):
    m_ref[...] = jnp.full(m_ref.shape, NEG_BIG, F32)
    l_ref[...] = jnp.zeros(l_ref.shape, F32)
    acc_ref[...] = jnp.zeros(acc_ref.shape, F32)


def _attend_chunks(items, score_fn, vt_fn, m_ref, l_ref, acc_ref, lookahead):
    ahead = min(lookahead, len(items))
    scores = {i: score_fn(items[i]) for i in range(ahead)}
    ones = jnp.ones((ONES_ROWS, KC), BF16)
    for i, item in enumerate(items):
        if i + ahead < len(items):
            scores[i + ahead] = score_fn(items[i + ahead])
        s = scores.pop(i)
        h = item[1]
        m_old = m_ref[h]
        m_new = jnp.maximum(m_old, jnp.max(s, axis=0, keepdims=True))
        alpha = jnp.exp2(m_old - m_new)
        p = jnp.exp2(s - m_new).astype(BF16)
        pv = _dot(jnp.concatenate([vt_fn(item), ones], axis=0), p)
        m_ref[h] = m_new
        l_ref[h] = alpha * l_ref[h] + pv[HEAD_DIM:HEAD_DIM + 1, :]
        rows = slice(h * HEAD_DIM, (h + 1) * HEAD_DIM)
        acc_ref[rows, :] = alpha * acc_ref[rows, :] + pv[0:HEAD_DIM, :]


def _softmax_finish(o_ref, n_heads, l_ref, acc_ref):
    for h in range(n_heads):
        rows = slice(h * HEAD_DIM, (h + 1) * HEAD_DIM)
        o_ref[0, rows, :] = acc_ref[rows, :] / l_ref[h]


def _fox_kernel(qt_ref, k_ref, vt_ref, cumt_ref, o_ref, w_ref, m_ref, l_ref, acc_ref):
    qi = pl.program_id(1)
    _softmax_init(m_ref, l_ref, acc_ref)

    row64 = lax.broadcasted_iota(jnp.int32, (AUG - HEAD_DIM, TQ), 0)
    for h in range(B_HEADS):
        c1, c2, c3 = _split3_f32(cumt_ref[0, h:h + 1, :])
        spare = jnp.where(row64 < 3, 1.0, 0.0)
        spare = jnp.where(row64 == 3, c1, spare)
        spare = jnp.where(row64 == 4, c2, spare)
        spare = jnp.where(row64 == 5, c3, spare)
        w_ref[h] = jnp.concatenate([qt_ref[0, h * HEAD_DIM:(h + 1) * HEAD_DIM, :],
                                    spare.astype(BF16)], axis=0)

    def tiles(chunks):
        rows = [_chunk(kc) for kc, _ in chunks]

        def score_fn(item):
            j, h = item
            s = _dot(k_ref[0, rows[j], h * AUG:(h + 1) * AUG], w_ref[h])
            if chunks[j][1]:
                s = jnp.where(_key_minus_query() <= 0, s, NEG_BIG)
            return s

        def vt_fn(item):
            j, h = item
            return vt_ref[0, h * HEAD_DIM:(h + 1) * HEAD_DIM, rows[j]]

        items = [(j, h) for j in range(len(chunks)) for h in range(B_HEADS)]
        _attend_chunks(items, score_fn, vt_fn, m_ref, l_ref, acc_ref, FOX_LOOKAHEAD)

    def body(pair, _):
        tiles([(2 * pair, False), (2 * pair + 1, False)])
        return 0

    lax.fori_loop(0, qi // 2, body, 0)

    @pl.when(qi % 2 == 1)
    def _():
        tiles([(qi - 1, False), (qi, True)])

    @pl.when(qi % 2 == 0)
    def _():
        tiles([(qi, True)])

    _softmax_finish(o_ref, B_HEADS, l_ref, acc_ref)


def _fox_attn(bqt, kaug, bvt, cumt):
    bsz, w, seq = bqt.shape
    blk_t = lambda b, i: (b, 0, i)
    full = lambda b, i: (b, 0, 0)
    return pl.pallas_call(
        _fox_kernel,
        out_shape=jax.ShapeDtypeStruct((bsz, w, seq), F32),
        grid=(bsz, seq // TQ),
        in_specs=[pl.BlockSpec((1, w, TQ), blk_t),
                  pl.BlockSpec((1, seq, B_HEADS * AUG), full),
                  pl.BlockSpec((1, w, seq), full),
                  pl.BlockSpec((1, B_HEADS, TQ), blk_t)],
        out_specs=pl.BlockSpec((1, w, TQ), blk_t),
        scratch_shapes=[pltpu.VMEM((B_HEADS, AUG, TQ), BF16), pltpu.VMEM((B_HEADS, 1, TQ), F32),
                        pltpu.VMEM((B_HEADS, 1, TQ), F32), pltpu.VMEM((B_HEADS * HEAD_DIM, TQ), F32)],
        compiler_params=pltpu.CompilerParams(dimension_semantics=("arbitrary", "arbitrary"),
                                             vmem_limit_bytes=VMEM_LIMIT),
        name="fox_attn",
    )(bqt, kaug, bvt, cumt)


def _bf16_pieces(value):
    pieces = []
    rest = np.float32(value)
    for _ in range(3):
        piece = np.asarray(rest).astype(BF16).astype(np.float32)
        pieces.append(float(piece))
        rest = np.float32(rest - piece)
    return pieces


def _dsa_kernel(iqt_ref, ik_ref, wft_ref, qt_ref, k_ref, posx_ref, vt_ref, o_ref,
                s_ref, thr_ref, w_ref, m_ref, l_ref, acc_ref):
    qi = pl.program_id(1)
    nch = qi + 1
    kmq = _key_minus_query()

    def score_chunk(kc):
        ik = ik_ref[0, _chunk(kc), :]
        acc = jnp.zeros((KC, TQ), F32)
        for h in range(IDX_HEADS):
            d = _dot(ik, iqt_ref[0, h * IDX_K:(h + 1) * IDX_K, :])
            acc = acc + wft_ref[0, h:h + 1, :] * jnp.maximum(d, 0.0)
        acc = jnp.where(acc == 0.0, 0.0, acc)
        causal = kmq <= (qi - kc) * KC
        s_ref[_chunk(kc), :] = jnp.where(causal, acc, -jnp.inf)

    def score_body(pair, _):
        score_chunk(2 * pair)
        score_chunk(2 * pair + 1)
        return 0

    lax.fori_loop(0, nch // 2, score_body, 0)

    @pl.when(nch % 2 == 1)
    def _():
        score_chunk(nch - 1)
        s_ref[_chunk(nch), :] = jnp.full((KC, TQ), -jnp.inf, F32)

    def scan(fn, init):
        def pair_body(pair, c):
            c = fn(2 * pair, s_ref[_chunk(2 * pair), :], c)
            return fn(2 * pair + 1, s_ref[_chunk(2 * pair + 1), :], c)

        return lax.fori_loop(0, (nch + 1) // 2, pair_body, init)

    @pl.when(qi == 0)
    def _():
        thr_ref[...] = jnp.full(thr_ref.shape, LOWEST, F32)

    @pl.when(qi > 0)
    def _():
        def part(x, op):
            return op(x.reshape(KC // 8, 8, TQ), axis=0)

        def cmin(x):
            return part(x, jnp.min)

        def cmax(x):
            return part(x, jnp.max)

        def csum(x):
            return part(x, jnp.sum)

        def fold(x, op):
            return op(x, axis=0, keepdims=True)

        zeros = jnp.zeros((8, TQ), F32)
        pinf = jnp.full((8, TQ), jnp.inf, F32)
        ninf = jnp.full((8, TQ), -jnp.inf, F32)

        def init_fn(_, s, c):
            lo, hi = c
            lo = jnp.minimum(lo, cmin(jnp.where(s > -jnp.inf, s, jnp.inf)))
            hi = jnp.maximum(hi, cmax(s))
            return lo, hi

        lo, hi = scan(init_fn, (pinf, ninf))
        lo, hi = fold(lo, jnp.min), fold(hi, jnp.max)

        def coarse_step(_, carry):
            lo, hi = carry
            mid = lo + (hi - lo) * 0.5
            cnt = scan(lambda _, s, c: c + csum(jnp.where(s >= mid, 1.0, 0.0)), zeros)
            enough = fold(cnt, jnp.sum) >= float(TOPK)
            return jnp.where(enough, mid, lo), jnp.where(enough, hi, mid)

        lo, hi = lax.fori_loop(0, COARSE_STEPS, coarse_step, (lo, hi))

        def cond(carry):
            return carry[2] > 0

        def step(carry):
            lo, hi, _ = carry
            mid = lo + (hi - lo) * 0.5
            mid = jnp.where(mid <= lo, hi, mid)

            def fn(_, s, c):
                cnt, a, b = c
                ge = s >= mid
                cnt = cnt + csum(jnp.where(ge, 1.0, 0.0))
                b = jnp.minimum(b, cmin(jnp.where(ge, s, jnp.inf)))
                a = jnp.maximum(a, cmax(jnp.where(ge, -jnp.inf, s)))
                return cnt, a, b

            cnt, a, b = scan(fn, (zeros, ninf, pinf))
            cnt, a, b = fold(cnt, jnp.sum), fold(a, jnp.max), fold(b, jnp.min)
            enough = cnt >= float(TOPK)
            new_lo = jnp.where(enough, b, jnp.where(cnt == float(TOPK - 1), a, lo))
            new_hi = jnp.where(enough, jnp.where(cnt == float(TOPK), b, hi), a)
            active = jnp.max(jnp.where(new_lo < new_hi, 1, 0))
            return new_lo, new_hi, active

        first_active = jnp.max(jnp.where(lo < hi, 1, 0))
        thr, _, _ = lax.while_loop(cond, step, (lo, hi, first_active))

        def count_fn(_, s, c):
            n_gt, n_ge = c
            return (n_gt + csum(jnp.where(s > thr, 1.0, 0.0)),
                    n_ge + csum(jnp.where(s >= thr, 1.0, 0.0)))

        n_gt, n_ge = scan(count_fn, (zeros, zeros))
        n_gt, n_ge = fold(n_gt, jnp.sum), fold(n_ge, jnp.sum)
        thr_ref[...] = thr

        @pl.when(jnp.max(jnp.where(n_ge > float(TOPK), 1, 0)) > 0)
        def _():
            need = float(TOPK) - n_gt
            lower = jnp.where(lax.broadcasted_iota(jnp.int32, (KC, KC), 1)
                              < lax.broadcasted_iota(jnp.int32, (KC, KC), 0), 1.0, 0.0).astype(BF16)

            def sel_fn(kc, s, run):
                eq = s == thr
                eqf = jnp.where(eq, 1.0, 0.0)
                before = _dot(lower, eqf.astype(BF16)) + run
                sel = (s > thr) | (eq & (before < need))
                s_ref[_chunk(kc), :] = jnp.where(sel, 0.0, NEG_BIG)
                return run + jnp.sum(eqf, axis=0, keepdims=True)

            scan(sel_fn, jnp.zeros((1, TQ), F32))
            thr_ref[...] = jnp.full(thr_ref.shape, 0.5 * NEG_BIG, F32)

    _softmax_init(m_ref, l_ref, acc_ref)
    rep = A_HEADS // A_KV_HEADS
    row64 = lax.broadcasted_iota(jnp.int32, (AUG - HEAD_DIM, TQ), 0)
    qpos = (qi * TQ + lax.broadcasted_iota(jnp.int32, (1, TQ), 1)).astype(F32)
    for h in range(A_HEADS):
        slope = np.float32(2.0 ** (-8.0 * (h + 1) / A_HEADS) * LOG2E)
        u1, u2, u3 = _split3_f32(-slope * qpos)
        spare = jnp.zeros((AUG - HEAD_DIM, TQ), F32)
        for p, s_p in enumerate(_bf16_pieces(slope)):
            spare = jnp.where(row64 == p, 128.0 * s_p, spare)
            spare = jnp.where(row64 == 3 + p, s_p, spare)
        spare = jnp.where(row64 == 6, u1, spare)
        spare = jnp.where(row64 == 7, u2, spare)
        spare = jnp.where(row64 == 8, u3, spare)
        w_ref[h] = jnp.concatenate([qt_ref[0, h * HEAD_DIM:(h + 1) * HEAD_DIM, :],
                                    spare.astype(BF16)], axis=0)

    def attn_tiles(chunks):
        rows = [_chunk(kc) for kc in chunks]
        lhs = [[k_ref[0, r, g * AUG:(g + 1) * AUG] + posx_ref[r, :] for g in range(A_KV_HEADS)]
               for r in rows]
        bias = [jnp.where(s_ref[r, :] >= thr_ref[...], 0.0, NEG_BIG) for r in rows]

        def score_fn(item):
            j, h = item
            return _dot(lhs[j][h // rep], w_ref[h]) + bias[j]

        def vt_fn(item):
            j, h = item
            g = h // rep
            return vt_ref[0, g * HEAD_DIM:(g + 1) * HEAD_DIM, rows[j]]

        items = [(j, h) for j in range(len(chunks)) for h in range(A_HEADS)]
        _attend_chunks(items, score_fn, vt_fn, m_ref, l_ref, acc_ref, DSA_LOOKAHEAD)

    def attn_body(pair, _):
        attn_tiles([2 * pair, 2 * pair + 1])
        return 0

    lax.fori_loop(0, nch // 2, attn_body, 0)

    @pl.when(nch % 2 == 1)
    def _():
        attn_tiles([nch - 1])

    _softmax_finish(o_ref, A_HEADS, l_ref, acc_ref)


def _dsa_attn(iqt, ik4, wft, aqt, ak, posx, avt):
    bsz, w, seq = aqt.shape
    blk_t = lambda b, i: (b, 0, i)
    full = lambda b, i: (b, 0, 0)
    return pl.pallas_call(
        _dsa_kernel,
        out_shape=jax.ShapeDtypeStruct((bsz, w, seq), F32),
        grid=(bsz, seq // TQ),
        in_specs=[pl.BlockSpec((1, IDX_HEADS * IDX_K, TQ), blk_t),
                  pl.BlockSpec((1, seq, IDX_K), full),
                  pl.BlockSpec((1, 16, TQ), blk_t),
                  pl.BlockSpec((1, w, TQ), blk_t),
                  pl.BlockSpec((1, seq, A_KV_HEADS * AUG), full),
                  pl.BlockSpec((seq, AUG), lambda b, i: (0, 0)),
                  pl.BlockSpec((1, 128, seq), full)],
        out_specs=pl.BlockSpec((1, w, TQ), blk_t),
        scratch_shapes=[pltpu.VMEM((seq, TQ), F32), pltpu.VMEM((1, TQ), F32),
                        pltpu.VMEM((A_HEADS, AUG, TQ), BF16),
                        pltpu.VMEM((A_HEADS, 1, TQ), F32), pltpu.VMEM((A_HEADS, 1, TQ), F32),
                        pltpu.VMEM((A_HEADS * HEAD_DIM, TQ), F32)],
        compiler_params=pltpu.CompilerParams(dimension_semantics=("arbitrary", "arbitrary"),
                                             vmem_limit_bytes=VMEM_LIMIT),
        name="dsa_attn",
    )(iqt, ik4, wft, aqt, ak, posx, avt)


def _rms_cols(xt, g_col):
    return xt * lax.rsqrt(jnp.mean(xt * xt, axis=0, keepdims=True) + EPS) * g_col


MOE_TMG = 256
MOE_TMC = 256
ROW_TILE = 8
ROUTE_COLS = 128


def _first_index_of_max(vals, lane, big):
    m = jnp.max(vals, axis=-1, keepdims=True)
    idx = jnp.min(jnp.where(vals == m, lane, big), axis=-1, keepdims=True)
    return m, idx


def _route(h, wr_hi_ref, wr_lo_ref, br_ref):
    h_hi, h_lo = _split_bf16(h)
    logits = _dot3(h_hi, h_lo, wr_hi_ref[...], wr_lo_ref[...]) + br_ref[...]
    lane = lax.broadcasted_iota(jnp.int32, logits.shape, 1)
    ninf = -jnp.inf
    gl = jnp.where(lane < N_GROUPS, logits, ninf)
    gmax, gsel = _first_index_of_max(gl, lane, 1 << 20)
    p_group = 1.0 / jnp.sum(jnp.exp(gl - gmax), axis=-1, keepdims=True)
    base = 32 + gsel * EXPERTS_PER_GROUP
    el = jnp.where((lane >= base) & (lane < base + EXPERTS_PER_GROUP), logits, ninf)
    v1, i1 = _first_index_of_max(el, lane, 1 << 20)
    el2 = jnp.where(lane == i1, ninf, el)
    v2, i2 = _first_index_of_max(el2, lane, 1 << 20)
    e2 = jnp.exp(v2 - v1)
    w1 = 1.0 / (1.0 + e2)
    w2 = e2 / (1.0 + e2)
    return i1, i2, w1 * p_group, w2 * p_group


def _token_rows(s, n_tokens, first_token=0):
    return pl.ds(first_token * ROW_TILE + s, n_tokens, stride=ROW_TILE)


def _to_token_tiles(ref, x):
    for s in range(ROW_TILE):
        ref[_token_rows(s, x.shape[0]), :] = x[:, s * 128:(s + 1) * 128]


def _token_tile(ref, t):
    start = t * ROW_TILE if isinstance(t, int) else pl.multiple_of(t * ROW_TILE, ROW_TILE)
    return ref.at[pl.ds(start, ROW_TILE), :]


def _moe_route_kernel(oat_ref, obt_ref, x_ref, mod_ref, ga_ref, gb_ref, wo_ref,
                      gf_ref, wr_hi_ref, wr_lo_ref, br_ref,
                      x1_ref, h_ref, info_ref, infot_ref, cnt_ref, run_ref):
    @pl.when(pl.program_id(0) == 0)
    def _():
        run_ref[...] = jnp.zeros_like(run_ref)

    oa = _rms_cols(oat_ref[0], ga_ref[...]).astype(BF16)
    ob = _rms_cols(obt_ref[0], gb_ref[...]).astype(BF16)
    y = _dot_tn(oa, wo_ref[0:512, :]) + _dot_tn(ob, wo_ref[512:1024, :])
    x1 = x_ref[...] + mod_ref[0, 2:3, :] * y
    x1_ref[...] = x1

    h = _rms(x1, gf_ref[...]) * (1.0 + mod_ref[0, 4:5, :]) + mod_ref[0, 3:4, :]
    _to_token_tiles(h_ref, h)
    i1, i2, w1, w2 = _route(h, wr_hi_ref, wr_lo_ref, br_ref)
    tm = h.shape[0]
    lane = lax.broadcasted_iota(jnp.int32, (tm, ROUTE_COLS), 1)
    picked = jnp.where((lane == i1) | (lane == i2), 1.0, 0.0)
    earlier = jnp.where(lax.broadcasted_iota(jnp.int32, (tm, tm), 1)
                        < lax.broadcasted_iota(jnp.int32, (tm, tm), 0), 1.0, 0.0).astype(BF16)
    before = _dot(earlier, picked.astype(BF16)) + run_ref[...]
    rank1 = jnp.sum(jnp.where(lane == i1, before, 0.0), axis=-1, keepdims=True)
    rank2 = jnp.sum(jnp.where(lane == i2, before, 0.0), axis=-1, keepdims=True)
    run_ref[...] += jnp.sum(picked, axis=0, keepdims=True)
    cnt_ref[...] = run_ref[...]
    info = jnp.where(lane == 0, (i1 - 32).astype(F32), 0.0)
    info = jnp.where(lane == 1, (i2 - 32).astype(F32), info)
    info = jnp.where(lane == 2, rank1, info)
    info = jnp.where(lane == 3, rank2, info)
    info = jnp.where(lane == 4, w1, info)
    info = jnp.where(lane == 5, w2, info)
    info_ref[...] = info
    infot_ref[...] = info.T[0:8, :]


def _moe_route(oat, obt, x2, mod3, g_out_a, g_out_b, w_out_bf, g_ffn, wr_hi, wr_lo, b_route, seq):
    n, d = x2.shape
    tm = 512
    per_b = seq // tm
    row = lambda i: (i, 0)
    const = lambda i: (0, 0)
    blk_t = lambda i: (i // per_b, 0, i % per_b)
    return pl.pallas_call(
        _moe_route_kernel,
        out_shape=[jax.ShapeDtypeStruct((n, d), F32),
                   jax.ShapeDtypeStruct((n * ROW_TILE, d // ROW_TILE), F32),
                   jax.ShapeDtypeStruct((n, ROUTE_COLS), F32), jax.ShapeDtypeStruct((8, n), F32),
                   jax.ShapeDtypeStruct((1, ROUTE_COLS), F32)],
        grid=(n // tm,),
        in_specs=[pl.BlockSpec((1, 512, tm), blk_t), pl.BlockSpec((1, 512, tm), blk_t),
                  pl.BlockSpec((tm, d), row),
                  pl.BlockSpec((1, N_MOD, d), lambda i: (i // per_b, 0, 0)),
                  pl.BlockSpec((512, 1), const), pl.BlockSpec((512, 1), const),
                  pl.BlockSpec((d, d), const),
                  pl.BlockSpec((1, d), const),
                  pl.BlockSpec((d, ROUTE_COLS), const), pl.BlockSpec((d, ROUTE_COLS), const),
                  pl.BlockSpec((1, ROUTE_COLS), const)],
        out_specs=[pl.BlockSpec((tm, d), row),
                   pl.BlockSpec((tm * ROW_TILE, d // ROW_TILE), row),
                   pl.BlockSpec((tm, ROUTE_COLS), row), pl.BlockSpec((8, tm), lambda i: (0, i)),
                   pl.BlockSpec((1, ROUTE_COLS), const)],
        scratch_shapes=[pltpu.VMEM((1, ROUTE_COLS), F32)],
        compiler_params=pltpu.CompilerParams(dimension_semantics=("arbitrary",),
                                             vmem_limit_bytes=VMEM_LIMIT),
        name="mix_out_moe_route",
    )(oat, obt, x2, mod3, g_out_a.reshape(-1, 1), g_out_b.reshape(-1, 1), w_out_bf,
      g_ffn.reshape(1, d), wr_hi, wr_lo, b_route)


def _wait_token_copies(src_hbm, dst, sem, n_tokens):
    pltpu.make_async_copy(src_hbm.at[pl.ds(0, n_tokens * ROW_TILE), :], dst, sem).wait()


def _moe_scatter_kernel(dest_ref, pad_start_ref, pad_len_ref, nu_ref, h_ref, xs_hbm, zero_ref, sem):
    i = pl.program_id(0)
    n_pairs = 2 * MOE_TMC

    @pl.when(i == 0)
    def _():
        zero_ref[...] = jnp.zeros_like(zero_ref)

        def pad_copies(e):
            n = pad_len_ref[e]
            size = MOE_TMG // 2
            while size >= 1:
                first = pad_start_ref[e] + (n & ~(2 * size - 1))
                copy = pltpu.make_async_copy(
                    zero_ref.at[pl.ds(0, size * ROW_TILE), :],
                    xs_hbm.at[pl.ds(pl.multiple_of(first * ROW_TILE, ROW_TILE), size * ROW_TILE), :],
                    sem.at[1])
                yield (n & size) != 0, copy
                size //= 2

        for e in range(N_EXPERTS):
            for present, copy in pad_copies(e):
                pl.when(present)(copy.start)
        for e in range(N_EXPERTS):
            for present, copy in pad_copies(e):
                pl.when(present)(copy.wait)

        def row_tile(t):
            return xs_hbm.at[pl.ds(pl.multiple_of(t * (MOE_TMG * ROW_TILE), MOE_TMG * ROW_TILE),
                                   MOE_TMG * ROW_TILE), :]

        n_tiles = xs_hbm.shape[0] // (MOE_TMG * ROW_TILE)

        def fill_tile(t, _):
            pltpu.make_async_copy(zero_ref, row_tile(t), sem.at[1]).start()
            return 0

        def drain_tile(t, _):
            pltpu.make_async_copy(zero_ref, row_tile(0), sem.at[1]).wait()
            return 0

        lax.fori_loop(nu_ref[0], n_tiles, fill_tile, 0)
        lax.fori_loop(nu_ref[0], n_tiles, drain_tile, 0)

    for r in range(MOE_TMC):
        for j in range(2):
            pltpu.make_async_copy(_token_tile(h_ref, r),
                                  _token_tile(xs_hbm, dest_ref[i * n_pairs + j * MOE_TMC + r]),
                                  sem.at[0]).start(priority=j)
    _wait_token_copies(xs_hbm, xs_hbm.at[pl.ds(0, n_pairs * ROW_TILE), :], sem.at[0], n_pairs)


def _moe_scatter(dest, pad_start, pad_len, n_used, h2, n_rows):
    n = h2.shape[0] // ROW_TILE
    grid_spec = pltpu.PrefetchScalarGridSpec(
        num_scalar_prefetch=4,
        grid=(n // MOE_TMC,),
        in_specs=[pl.BlockSpec((MOE_TMC * ROW_TILE, h2.shape[1]), lambda i, *_: (i, 0))],
        out_specs=pl.BlockSpec(memory_space=pl.ANY),
        scratch_shapes=[pltpu.VMEM((MOE_TMG * ROW_TILE, h2.shape[1]), F32),
                        pltpu.SemaphoreType.DMA((2,))])
    return pl.pallas_call(
        _moe_scatter_kernel,
        out_shape=jax.ShapeDtypeStruct((n_rows * ROW_TILE, h2.shape[1]), F32),
        grid_spec=grid_spec,
        compiler_params=pltpu.CompilerParams(dimension_semantics=("arbitrary",),
                                             vmem_limit_bytes=VMEM_LIMIT),
        name="moe_scatter",
    )(dest, pad_start, pad_len, n_used, h2)


def _moe_expert_kernel(te_ref, nu_ref, x_ref, wg_ref, wu_ref, wd_ref, y_ref,
                       xs_ref, wgb_ref, wub_ref, wdb_ref):
    i = pl.program_id(0)
    used = i < nu_ref[0]
    new_expert = jnp.logical_or(i == 0, te_ref[i] != te_ref[jnp.maximum(i - 1, 0)])

    @pl.when(jnp.logical_and(used, new_expert))
    def _():
        wgb_ref[...] = wg_ref[0].astype(BF16)
        wub_ref[...] = wu_ref[0].astype(BF16)
        wdb_ref[...] = wd_ref[0].astype(BF16)

    @pl.when(used)
    def _():
        for s in range(ROW_TILE):
            xs_ref[:, s * 128:(s + 1) * 128] = x_ref[_token_rows(s, MOE_TMG), :].astype(BF16)
        x = xs_ref[...]
        hg = _dot(x, wgb_ref[...])
        hu = _dot(x, wub_ref[...])
        a = hg * jax.nn.sigmoid(hg) * hu
        _to_token_tiles(y_ref, _dot(a.astype(BF16), wdb_ref[...]))

    @pl.when(i >= nu_ref[0])
    def _():
        y_ref[...] = jnp.zeros_like(y_ref)


def _moe_experts(tile_expert, n_used, xsorted, wg, wu, wd):
    n_rows = xsorted.shape[0] // ROW_TILE
    d, ff = wg.shape[1], wg.shape[2]
    tile = (MOE_TMG * ROW_TILE, xsorted.shape[1])
    grid_spec = pltpu.PrefetchScalarGridSpec(
        num_scalar_prefetch=2,
        grid=(n_rows // MOE_TMG,),
        in_specs=[pl.BlockSpec(tile, lambda i, te, nu: (jnp.minimum(i, nu[0] - 1), 0)),
                  pl.BlockSpec((1, d, ff), lambda i, te, nu: (te[i], 0, 0)),
                  pl.BlockSpec((1, d, ff), lambda i, te, nu: (te[i], 0, 0)),
                  pl.BlockSpec((1, ff, d), lambda i, te, nu: (te[i], 0, 0))],
        out_specs=pl.BlockSpec(tile, lambda i, te, nu: (i, 0)),
        scratch_shapes=[pltpu.VMEM((MOE_TMG, d), BF16), pltpu.VMEM((d, ff), BF16),
                        pltpu.VMEM((d, ff), BF16), pltpu.VMEM((ff, d), BF16)])
    return pl.pallas_call(
        _moe_expert_kernel,
        out_shape=jax.ShapeDtypeStruct(xsorted.shape, F32),
        grid_spec=grid_spec,
        compiler_params=pltpu.CompilerParams(dimension_semantics=("arbitrary",),
                                             vmem_limit_bytes=VMEM_LIMIT),
        name="moe_experts",
    )(tile_expert, n_used, xsorted, wg, wu, wd)


def _moe_combine_kernel(dest_ref, y_hbm, x_ref, info_ref, mod_ref, gfin_ref, o_ref, ybuf, x2_ref, sem):
    i = pl.program_id(0)
    nt = pl.num_programs(0)
    slot = lax.rem(i, 2)
    n_pairs = 2 * MOE_TMC

    def start_gather(tile, to_slot):
        for r in range(n_pairs):
            pltpu.make_async_copy(_token_tile(y_hbm, dest_ref[tile * n_pairs + r]),
                                  _token_tile(ybuf.at[to_slot], r), sem.at[to_slot]).start(priority=r % 2)

    @pl.when(i == 0)
    def _():
        start_gather(0, 0)

    _wait_token_copies(y_hbm, ybuf.at[slot], sem.at[slot], n_pairs)

    @pl.when(i + 1 < nt)
    def _():
        start_gather(i + 1, 1 - slot)

    w1 = info_ref[:, 4:5]
    w2 = info_ref[:, 5:6]
    sumsq = jnp.zeros((MOE_TMC, 1), F32)
    for s in range(ROW_TILE):
        cols = slice(s * 128, (s + 1) * 128)
        y = (w1 * ybuf[slot, _token_rows(s, MOE_TMC), :]
             + w2 * ybuf[slot, _token_rows(s, MOE_TMC, first_token=MOE_TMC), :])
        x2 = x_ref[:, cols] + mod_ref[0, 5:6, cols] * y
        x2_ref[:, cols] = x2
        sumsq = sumsq + jnp.sum(x2 * x2, axis=-1, keepdims=True)
    d = x2_ref.shape[1]
    o_ref[...] = x2_ref[...] * lax.rsqrt(sumsq / d + EPS) * gfin_ref[...]


def _moe_combine(dest, ysorted, x1, info, mod3, g_final, seq):
    n, d = x1.shape
    per_b = seq // MOE_TMC
    grid_spec = pltpu.PrefetchScalarGridSpec(
        num_scalar_prefetch=1,
        grid=(n // MOE_TMC,),
        in_specs=[pl.BlockSpec(memory_space=pl.ANY),
                  pl.BlockSpec((MOE_TMC, d), lambda i, ds: (i, 0)),
                  pl.BlockSpec((MOE_TMC, ROUTE_COLS), lambda i, ds: (i, 0)),
                  pl.BlockSpec((1, N_MOD, d), lambda i, ds: (i // per_b, 0, 0)),
                  pl.BlockSpec((1, d), lambda i, ds: (0, 0))],
        out_specs=pl.BlockSpec((MOE_TMC, d), lambda i, ds: (i, 0)),
        scratch_shapes=[pltpu.VMEM((2, 2 * MOE_TMC * ROW_TILE, ysorted.shape[1]), F32),
                        pltpu.VMEM((MOE_TMC, d), F32), pltpu.SemaphoreType.DMA((2,))])
    return pl.pallas_call(
        _moe_combine_kernel,
        out_shape=jax.ShapeDtypeStruct((n, d), F32),
        grid_spec=grid_spec,
        compiler_params=pltpu.CompilerParams(dimension_semantics=("arbitrary",),
                                             vmem_limit_bytes=VMEM_LIMIT),
        name="moe_combine",
    )(dest, ysorted, x1, info, mod3, g_final.reshape(1, d))


def _mix_out_and_moe(oat, obt, x2, mod3, g_out_a, g_out_b, w_out_bf, g_ffn, wr_hi, wr_lo, b_route,
                     wg, wu, wd, g_final, seq):
    n, d = x2.shape
    x1, h2, info, infot, counts = _moe_route(oat, obt, x2, mod3, g_out_a, g_out_b, w_out_bf,
                                             g_ffn, wr_hi, wr_lo, b_route, seq)

    e1, e2, rank1, rank2 = [infot[k].astype(jnp.int32) for k in range(4)]
    cnt = counts[0, 32:32 + N_EXPERTS].astype(jnp.int32)
    padded = ((cnt + MOE_TMG - 1) // MOE_TMG) * MOE_TMG
    seg_end = jnp.cumsum(padded)
    expert_ids = jnp.arange(N_EXPERTS, dtype=jnp.int32)

    def seg_start_of(e):
        return jnp.sum(jnp.where(expert_ids[None, :] < e[:, None], padded[None, :], 0), axis=1)

    dest1 = seg_start_of(e1) + rank1
    dest2 = seg_start_of(e2) + rank2
    n_rows = 2 * n + N_EXPERTS * MOE_TMG
    tile_start = jnp.arange(n_rows // MOE_TMG, dtype=jnp.int32) * MOE_TMG
    tile_expert = jnp.minimum(jnp.sum((tile_start[:, None] >= seg_end[None, :]).astype(jnp.int32), axis=1),
                              N_EXPERTS - 1)
    n_used = (seg_end[N_EXPERTS - 1:] // MOE_TMG).astype(jnp.int32)
    dest = jnp.concatenate([dest1.reshape(-1, MOE_TMC), dest2.reshape(-1, MOE_TMC)], axis=1).reshape(-1)

    seg_start = seg_end - padded
    xsorted = _moe_scatter(dest, seg_start + cnt, padded - cnt, n_used, h2, n_rows)
    ysorted = _moe_experts(tile_expert, n_used, xsorted, wg, wu, wd)
    return _moe_combine(dest, ysorted, x1, info, mod3, g_final, seq)


def _layer(x3, c, w_ada, b_ada, g_mix, w_in, b_forget, g_out_a, g_out_b, w_out,
           g_ffn, w_group, b_group, w_router, b_router, w_gate, w_up, w_down, g_final):
    bsz, seq, d = x3.shape
    mod3 = _ada_mod(c, w_ada, b_ada).reshape(bsz, N_MOD, d)

    w_t = w_in.T

    def pad_heads(w, n_heads):
        w = w.reshape(d, n_heads, HEAD_DIM)
        return jnp.concatenate([w, jnp.zeros_like(w)], axis=-1).reshape(d, n_heads * AUG)

    w_wf_t = jnp.concatenate([w_t[1344:1352], w_t[2888:2896]], axis=0)
    weights = [w_t[0:512].astype(BF16), pad_heads(w_in[:, 512:640], A_KV_HEADS).astype(BF16),
               w_t[640:768].astype(BF16), w_t[1352:1864].astype(BF16),
               pad_heads(w_in[:, 1864:2376], B_HEADS).astype(BF16), w_t[2376:2888].astype(BF16),
               w_t[768:1280].astype(BF16), w_in[:, 1280:1344].astype(BF16), *_split_bf16(w_wf_t)]
    aqt, ak, avt, bqt, bk, bvt, iqt, ik4, wft = _in_proj(x3, mod3, g_mix, weights)

    cumt, kaug = _fox_cum(wft, b_forget, bk)
    obt = _fox_attn(bqt, kaug, bvt, cumt)

    pos = jnp.arange(seq, dtype=jnp.int32)[:, None]
    lane = jnp.arange(AUG, dtype=jnp.int32)[None, :] - HEAD_DIM
    posx = jnp.where((lane >= 0) & (lane < 3), pos >> 7,
                     jnp.where((lane >= 3) & (lane < 6), pos & 127,
                               jnp.where((lane >= 6) & (lane < 9), 1, 0))).astype(BF16)
    oat = _dsa_attn(iqt, ik4, wft, aqt, ak, posx, avt)

    w_r = jnp.concatenate([w_group, jnp.zeros((d, 32 - N_GROUPS), F32),
                           jnp.transpose(w_router, (1, 0, 2)).reshape(d, N_EXPERTS),
                           jnp.zeros((d, ROUTE_COLS - 64), F32)], axis=1)
    b_r = jnp.concatenate([b_group, jnp.zeros((32 - N_GROUPS,), F32), b_router.reshape(-1),
                           jnp.zeros((ROUTE_COLS - 64,), F32)]).reshape(1, ROUTE_COLS)
    wr_hi, wr_lo = _split_bf16(w_r)
    out = _mix_out_and_moe(oat, obt, x3.reshape(bsz * seq, d), mod3, g_out_a, g_out_b,
                           w_out.astype(BF16), g_ffn, wr_hi, wr_lo, b_r, w_gate, w_up, w_down,
                           g_final, seq)
    return out.reshape(bsz, seq, d)


def kernel(x, c, w_ada, b_ada, g_mix, w_in, b_forget, g_out_a, g_out_b, w_out, g_ffn, w_group,
           b_group, w_router, b_router, w_gate, w_up, w_down, g_final):
    depth = w_ada.shape[0]
    assert depth == 1, "final norm is fused into the single layer's MoE kernel"
    return _layer(x, c, w_ada[0], b_ada[0], g_mix[0], w_in[0], b_forget[0], g_out_a[0], g_out_b[0],
                  w_out[0], g_ffn[0], w_group[0], b_group[0], w_router[0], b_router[0], w_gate[0],
                  w_up[0], w_down[0], g_final)
```

```python
import math

import jax
import jax.numpy as jnp
import numpy as np
from jax import lax
from jax.experimental import pallas as pl
from jax.experimental.pallas import tpu as pltpu

F32 = jnp.float32
BF16 = jnp.bfloat16

EPS = 1e-6
A_HEADS = 8
A_KV_HEADS = 2
HEAD_DIM = 64
IDX_HEADS = 8
IDX_DIM = 64
TOPK = 256
B_HEADS = 8
N_GROUPS = 4
EXPERTS_PER_GROUP = 8
N_EXPERTS = N_GROUPS * EXPERTS_PER_GROUP
N_MOD = 6

NEG_BIG = -1e30
LOWEST = float(np.finfo(np.float32).min)
LOG2E = math.log2(math.e)
Q_SCALE = HEAD_DIM ** -0.5 * LOG2E
VMEM_LIMIT = 48 * 1024 * 1024

TQ = 256
KC = 256
AUG = 128
IDX_K = IDX_DIM
ONES_ROWS = 16
COARSE_STEPS = 14
FOX_LOOKAHEAD = 6
DSA_LOOKAHEAD = 4
assert TQ == KC == TOPK


def _split_bf16(x):
    hi = x.astype(BF16)
    lo = (x - hi.astype(F32)).astype(BF16)
    return hi, lo


def _split3_f32(x):
    p1 = x.astype(BF16).astype(F32)
    r1 = x - p1
    p2 = r1.astype(BF16).astype(F32)
    p3 = (r1 - p2).astype(BF16).astype(F32)
    return p1, p2, p3


def _dot(a, b):
    return jnp.dot(a, b, preferred_element_type=F32)


def _dot_nt(a, b):
    return lax.dot_general(a, b, (((1,), (1,)), ((), ())), preferred_element_type=F32)


def _dot_tn(a, b):
    return lax.dot_general(a, b, (((0,), (0,)), ((), ())), preferred_element_type=F32)


def _dot3(a_hi, a_lo, b_hi, b_lo):
    return _dot(a_hi, b_hi) + _dot(a_lo, b_hi) + _dot(a_hi, b_lo)


def _dot3_nt(a_hi, a_lo, b_hi, b_lo):
    return _dot_nt(a_hi, b_hi) + _dot_nt(a_lo, b_hi) + _dot_nt(a_hi, b_lo)


def _rms(x, g):
    return x * lax.rsqrt(jnp.mean(x * x, axis=-1, keepdims=True) + EPS) * g


def _chunk(kc):
    return pl.ds(pl.multiple_of(kc * KC, KC), KC)


def _key_minus_query():
    return (lax.broadcasted_iota(jnp.int32, (KC, TQ), 0)
            - lax.broadcasted_iota(jnp.int32, (KC, TQ), 1))


def _ada_kernel(c_ref, w_ref, b_ref, o_ref):
    c = c_ref[...]
    s = c * jax.nn.sigmoid(c)
    s_hi, s_lo = _split_bf16(s)
    w_hi, w_lo = _split_bf16(w_ref[...])
    o_ref[...] = _dot3(s_hi, s_lo, w_hi, w_lo) + b_ref[...]


def _ada_mod(c, w_ada, b_ada):
    bsz, d = c.shape
    n = w_ada.shape[1]
    tn = 1024
    return pl.pallas_call(
        _ada_kernel,
        out_shape=jax.ShapeDtypeStruct((bsz, n), F32),
        grid=(n // tn,),
        in_specs=[pl.BlockSpec((bsz, d), lambda j: (0, 0)),
                  pl.BlockSpec((d, tn), lambda j: (0, j)),
                  pl.BlockSpec((1, tn), lambda j: (0, j))],
        out_specs=pl.BlockSpec((bsz, tn), lambda j: (0, j)),
        compiler_params=pltpu.CompilerParams(dimension_semantics=("arbitrary",),
                                             vmem_limit_bytes=VMEM_LIMIT),
        name="ada_mod",
    )(c, w_ada, b_ada.reshape(1, n))


def _in_proj_kernel(x_ref, mod_ref, g_ref,
                    waq_ref, wak_ref, wav_ref, wbq_ref, wbk_ref, wbv_ref,
                    wiq_ref, wik_ref, wwfh_ref, wwfl_ref,
                    aqt_ref, ak_ref, avt_ref, bqt_ref, bk_ref, bvt_ref, iqt_ref, ik_ref, wft_ref):
    x = x_ref[0]
    h = _rms(x, g_ref[...]) * (1.0 + mod_ref[0, 1:2, :]) + mod_ref[0, 0:1, :]
    h_hi, h_lo = _split_bf16(h)
    aqt_ref[0] = (_dot_nt(waq_ref[...], h_hi) * Q_SCALE).astype(BF16)
    ak_ref[0] = _dot(h_hi, wak_ref[...]).astype(BF16)
    avt_ref[0] = _dot_nt(wav_ref[...], h_hi).astype(BF16)
    bqt_ref[0] = (_dot_nt(wbq_ref[...], h_hi) * Q_SCALE).astype(BF16)
    bk_ref[0] = _dot(h_hi, wbk_ref[...]).astype(BF16)
    bvt_ref[0] = _dot_nt(wbv_ref[...], h_hi).astype(BF16)
    iqt_ref[0] = _dot_nt(wiq_ref[...], h_hi).astype(BF16)
    ik_ref[0] = _dot(h_hi, wik_ref[...]).astype(BF16)
    wft_ref[0] = _dot3_nt(wwfh_ref[...], wwfl_ref[...], h_hi, h_lo)


def _in_proj(x3, mod3, g_mix, weights):
    bsz, seq, d = x3.shape
    tm = 512
    blk_t = lambda b, i: (b, 0, i)
    blk_r = lambda b, i: (b, i, 0)
    const = lambda b, i: (0, 0)
    ak_w = A_KV_HEADS * AUG
    bk_w = B_HEADS * HEAD_DIM
    outs = [jax.ShapeDtypeStruct((bsz, 512, seq), BF16), jax.ShapeDtypeStruct((bsz, seq, ak_w), BF16),
            jax.ShapeDtypeStruct((bsz, 128, seq), BF16), jax.ShapeDtypeStruct((bsz, 512, seq), BF16),
            jax.ShapeDtypeStruct((bsz, seq, bk_w), BF16), jax.ShapeDtypeStruct((bsz, 512, seq), BF16),
            jax.ShapeDtypeStruct((bsz, IDX_HEADS * IDX_K, seq), BF16),
            jax.ShapeDtypeStruct((bsz, seq, IDX_K), BF16), jax.ShapeDtypeStruct((bsz, 16, seq), F32)]
    out_specs = [pl.BlockSpec((1, 512, tm), blk_t), pl.BlockSpec((1, tm, ak_w), blk_r),
                 pl.BlockSpec((1, 128, tm), blk_t), pl.BlockSpec((1, 512, tm), blk_t),
                 pl.BlockSpec((1, tm, bk_w), blk_r), pl.BlockSpec((1, 512, tm), blk_t),
                 pl.BlockSpec((1, IDX_HEADS * IDX_K, tm), blk_t),
                 pl.BlockSpec((1, tm, IDX_K), blk_r), pl.BlockSpec((1, 16, tm), blk_t)]
    return pl.pallas_call(
        _in_proj_kernel,
        out_shape=outs,
        grid=(bsz, seq // tm),
        in_specs=[pl.BlockSpec((1, tm, d), blk_r),
                  pl.BlockSpec((1, N_MOD, d), lambda b, i: (b, 0, 0)),
                  pl.BlockSpec((1, d), const)] + [pl.BlockSpec(w.shape, const) for w in weights],
        out_specs=out_specs,
        compiler_params=pltpu.CompilerParams(dimension_semantics=("arbitrary", "arbitrary"),
                                             vmem_limit_bytes=VMEM_LIMIT),
        name="in_proj",
    )(x3, mod3, g_mix.reshape(1, d), *weights)


CB = 256


def _cum_kernel(wft_ref, bfor_ref, k_ref, cumt_ref, kaug_ref):
    seq = wft_ref.shape[2]
    r = lax.broadcasted_iota(jnp.int32, (CB, CB), 0)
    cidx = lax.broadcasted_iota(jnp.int32, (CB, CB), 1)
    tri = jnp.where(r <= cidx, 1.0, 0.0).astype(BF16)
    row128 = lax.broadcasted_iota(jnp.int32, (AUG, CB), 0)
    ones_rows = jnp.where((row128 >= HEAD_DIM + 3) & (row128 < HEAD_DIM + 6), 1.0, 0.0)
    src = lax.broadcasted_iota(jnp.int32, (AUG, AUG), 0)
    dst = lax.broadcasted_iota(jnp.int32, (AUG, AUG), 1)
    place = [jnp.where((dst < HEAD_DIM) & (src == dst + half * HEAD_DIM), 1.0, 0.0).astype(BF16)
             for half in range(2)]
    carry = jnp.zeros((8, 1), F32)
    for blk in range(seq // CB):
        cols = slice(blk * CB, (blk + 1) * CB)
        z = wft_ref[0, 8:16, cols] + bfor_ref[...]
        logf = jnp.minimum(z, 0.0) - jnp.log(1.0 + jnp.exp(-jnp.abs(z)))
        p1, p2, p3 = _split3_f32(logf)
        pieces = jnp.concatenate([p1, p2, p3, jnp.zeros_like(p1)], axis=0).astype(BF16)
        parts = _dot(pieces, tri)
        cum = parts[0:8] + parts[8:16] + parts[16:24] + carry
        carry = cum[:, CB - 1:CB]
        cum2 = cum * LOG2E
        cumt_ref[0, :, cols] = cum2
        c1, c2, c3 = _split3_f32(cum2)
        for h in range(B_HEADS):
            spare = jnp.where(row128 == HEAD_DIM, -c1[h:h + 1], ones_rows)
            spare = jnp.where(row128 == HEAD_DIM + 1, -c2[h:h + 1], spare)
            spare = jnp.where(row128 == HEAD_DIM + 2, -c3[h:h + 1], spare)
            pair = k_ref[0, cols, (h // 2) * AUG:(h // 2 + 1) * AUG]
            k_h = _dot(pair, place[h % 2]).astype(BF16)
            kaug_ref[0, cols, h * AUG:(h + 1) * AUG] = k_h + spare.T.astype(BF16)


def _fox_cum(wft, b_forget, bk):
    bsz, _, seq = wft.shape
    nh = B_HEADS
    kw = bk.shape[-1]
    kaug_w = nh * AUG
    return pl.pallas_call(
        _cum_kernel,
        out_shape=[jax.ShapeDtypeStruct((bsz, nh, seq), F32),
                   jax.ShapeDtypeStruct((bsz, seq, kaug_w), BF16)],
        grid=(bsz,),
        in_specs=[pl.BlockSpec((1, 16, seq), lambda b: (b, 0, 0)),
                  pl.BlockSpec((nh, 1), lambda b: (0, 0)),
                  pl.BlockSpec((1, seq, kw), lambda b: (b, 0, 0))],
        out_specs=[pl.BlockSpec((1, nh, seq), lambda b: (b, 0, 0)),
                   pl.BlockSpec((1, seq, kaug_w), lambda b: (b, 0, 0))],
        compiler_params=pltpu.CompilerParams(dimension_semantics=("arbitrary",),
                                             vmem_limit_bytes=VMEM_LIMIT),
        name="fox_cum",
    )(wft, b_forget.reshape(nh, 1), bk)


def _softmax_init(m_ref, l_ref, acc_ref):
    m_ref[...] = jnp.full(m_ref.shape, NEG_BIG, F32)
    l_ref[...] = jnp.zeros(l_ref.shape, F32)
    acc_ref[...] = jnp.zeros(acc_ref.shape, F32)


def _attend_chunks(items, score_fn, vt_fn, m_ref, l_ref, acc_ref, lookahead):
    ahead = min(lookahead, len(items))
    scores = {i: score_fn(items[i]) for i in range(ahead)}
    ones = jnp.ones((ONES_ROWS, KC), BF16)
    for i, item in enumerate(items):
        if i + ahead < len(items):
            scores[i + ahead] = score_fn(items[i + ahead])
        s = scores.pop(i)
        h = item[1]
        m_old = m_ref[h]
        m_new = jnp.maximum(m_old, jnp.max(s, axis=0, keepdims=True))
        alpha = jnp.exp2(m_old - m_new)
        p = jnp.exp2(s - m_new).astype(BF16)
        pv = _dot(jnp.concatenate([vt_fn(item), ones], axis=0), p)
        m_ref[h] = m_new
        l_ref[h] = alpha * l_ref[h] + pv[HEAD_DIM:HEAD_DIM + 1, :]
        rows = slice(h * HEAD_DIM, (h + 1) * HEAD_DIM)
        acc_ref[rows, :] = alpha * acc_ref[rows, :] + pv[0:HEAD_DIM, :]


def _softmax_finish(o_ref, n_heads, l_ref, acc_ref):
    for h in range(n_heads):
        rows = slice(h * HEAD_DIM, (h + 1) * HEAD_DIM)
        o_ref[0, rows, :] = acc_ref[rows, :] / l_ref[h]


def _fox_kernel(qt_ref, k_ref, vt_ref, cumt_ref, o_ref, w_ref, m_ref, l_ref, acc_ref):
    qi = pl.program_id(1)
    _softmax_init(m_ref, l_ref, acc_ref)

    row64 = lax.broadcasted_iota(jnp.int32, (AUG - HEAD_DIM, TQ), 0)
    for h in range(B_HEADS):
        c1, c2, c3 = _split3_f32(cumt_ref[0, h:h + 1, :])
        spare = jnp.where(row64 < 3, 1.0, 0.0)
        spare = jnp.where(row64 == 3, c1, spare)
        spare = jnp.where(row64 == 4, c2, spare)
        spare = jnp.where(row64 == 5, c3, spare)
        w_ref[h] = jnp.concatenate([qt_ref[0, h * HEAD_DIM:(h + 1) * HEAD_DIM, :],
                                    spare.astype(BF16)], axis=0)

    def tiles(chunks):
        rows = [_chunk(kc) for kc, _ in chunks]

        def score_fn(item):
            j, h = item
            s = _dot(k_ref[0, rows[j], h * AUG:(h + 1) * AUG], w_ref[h])
            if chunks[j][1]:
                s = jnp.where(_key_minus_query() <= 0, s, NEG_BIG)
            return s

        def vt_fn(item):
            j, h = item
            return vt_ref[0, h * HEAD_DIM:(h + 1) * HEAD_DIM, rows[j]]

        items = [(j, h) for j in range(len(chunks)) for h in range(B_HEADS)]
        _attend_chunks(items, score_fn, vt_fn, m_ref, l_ref, acc_ref, FOX_LOOKAHEAD)

    def body(pair, _):
        tiles([(2 * pair, False), (2 * pair + 1, False)])
        return 0

    lax.fori_loop(0, qi // 2, body, 0)

    @pl.when(qi % 2 == 1)
    def _():
        tiles([(qi - 1, False), (qi, True)])

    @pl.when(qi % 2 == 0)
    def _():
        tiles([(qi, True)])

    _softmax_finish(o_ref, B_HEADS, l_ref, acc_ref)


def _fox_attn(bqt, kaug, bvt, cumt):
    bsz, w, seq = bqt.shape
    blk_t = lambda b, i: (b, 0, i)
    full = lambda b, i: (b, 0, 0)
    return pl.pallas_call(
        _fox_kernel,
        out_shape=jax.ShapeDtypeStruct((bsz, w, seq), F32),
        grid=(bsz, seq // TQ),
        in_specs=[pl.BlockSpec((1, w, TQ), blk_t),
                  pl.BlockSpec((1, seq, B_HEADS * AUG), full),
                  pl.BlockSpec((1, w, seq), full),
                  pl.BlockSpec((1, B_HEADS, TQ), blk_t)],
        out_specs=pl.BlockSpec((1, w, TQ), blk_t),
        scratch_shapes=[pltpu.VMEM((B_HEADS, AUG, TQ), BF16), pltpu.VMEM((B_HEADS, 1, TQ), F32),
                        pltpu.VMEM((B_HEADS, 1, TQ), F32), pltpu.VMEM((B_HEADS * HEAD_DIM, TQ), F32)],
        compiler_params=pltpu.CompilerParams(dimension_semantics=("arbitrary", "arbitrary"),
                                             vmem_limit_bytes=VMEM_LIMIT),
        name="fox_attn",
    )(bqt, kaug, bvt, cumt)


def _bf16_pieces(value):
    pieces = []
    rest = np.float32(value)
    for _ in range(3):
        piece = np.asarray(rest).astype(BF16).astype(np.float32)
        pieces.append(float(piece))
        rest = np.float32(rest - piece)
    return pieces


def _dsa_kernel(iqt_ref, ik_ref, wft_ref, qt_ref, k_ref, posx_ref, vt_ref, o_ref,
                s_ref, thr_ref, w_ref, m_ref, l_ref, acc_ref):
    qi = pl.program_id(1)
    nch = qi + 1
    kmq = _key_minus_query()

    def score_chunk(kc):
        ik = ik_ref[0, _chunk(kc), :]
        acc = jnp.zeros((KC, TQ), F32)
        for h in range(IDX_HEADS):
            d = _dot(ik, iqt_ref[0, h * IDX_K:(h + 1) * IDX_K, :])
            acc = acc + wft_ref[0, h:h + 1, :] * jnp.maximum(d, 0.0)
        acc = jnp.where(acc == 0.0, 0.0, acc)
        causal = kmq <= (qi - kc) * KC
        s_ref[_chunk(kc), :] = jnp.where(causal, acc, -jnp.inf)

    def score_body(pair, _):
        score_chunk(2 * pair)
        score_chunk(2 * pair + 1)
        return 0

    lax.fori_loop(0, nch // 2, score_body, 0)

    @pl.when(nch % 2 == 1)
    def _():
        score_chunk(nch - 1)
        s_ref[_chunk(nch), :] = jnp.full((KC, TQ), -jnp.inf, F32)

    def scan(fn, init):
        def pair_body(pair, c):
            c = fn(2 * pair, s_ref[_chunk(2 * pair), :], c)
            return fn(2 * pair + 1, s_ref[_chunk(2 * pair + 1), :], c)

        return lax.fori_loop(0, (nch + 1) // 2, pair_body, init)

    @pl.when(qi == 0)
    def _():
        thr_ref[...] = jnp.full(thr_ref.shape, LOWEST, F32)

    @pl.when(qi > 0)
    def _():
        def part(x, op):
            return op(x.reshape(KC // 8, 8, TQ), axis=0)

        def cmin(x):
            return part(x, jnp.min)

        def cmax(x):
            return part(x, jnp.max)

        def csum(x):
            return part(x, jnp.sum)

        def fold(x, op):
            return op(x, axis=0, keepdims=True)

        zeros = jnp.zeros((8, TQ), F32)
        pinf = jnp.full((8, TQ), jnp.inf, F32)
        ninf = jnp.full((8, TQ), -jnp.inf, F32)

        def init_fn(_, s, c):
            lo, hi = c
            lo = jnp.minimum(lo, cmin(jnp.where(s > -jnp.inf, s, jnp.inf)))
            hi = jnp.maximum(hi, cmax(s))
            return lo, hi

        lo, hi = scan(init_fn, (pinf, ninf))
        lo, hi = fold(lo, jnp.min), fold(hi, jnp.max)

        def coarse_step(_, carry):
            lo, hi = carry
            mid = lo + (hi - lo) * 0.5
            cnt = scan(lambda _, s, c: c + csum(jnp.where(s >= mid, 1.0, 0.0)), zeros)
            enough = fold(cnt, jnp.sum) >= float(TOPK)
            return jnp.where(enough, mid, lo), jnp.where(enough, hi, mid)

        lo, hi = lax.fori_loop(0, COARSE_STEPS, coarse_step, (lo, hi))

        def cond(carry):
            return carry[2] > 0

        def step(carry):
            lo, hi, _ = carry
            mid = lo + (hi - lo) * 0.5
            mid = jnp.where(mid <= lo, hi, mid)

            def fn(_, s, c):
                cnt, a, b = c
                ge = s >= mid
                cnt = cnt + csum(jnp.where(ge, 1.0, 0.0))
                b = jnp.minimum(b, cmin(jnp.where(ge, s, jnp.inf)))
                a = jnp.maximum(a, cmax(jnp.where(ge, -jnp.inf, s)))
                return cnt, a, b

            cnt, a, b = scan(fn, (zeros, ninf, pinf))
            cnt, a, b = fold(cnt, jnp.sum), fold(a, jnp.max), fold(b, jnp.min)
            enough = cnt >= float(TOPK)
            new_lo = jnp.where(enough, b, jnp.where(cnt == float(TOPK - 1), a, lo))
            new_hi = jnp.where(enough, jnp.where(cnt == float(TOPK), b, hi), a)
            active = jnp.max(jnp.where(new_lo < new_hi, 1, 0))
            return new_lo, new_hi, active

        first_active = jnp.max(jnp.where(lo < hi, 1, 0))
        thr, _, _ = lax.while_loop(cond, step, (lo, hi, first_active))

        def count_fn(_, s, c):
            n_gt, n_ge = c
            return (n_gt + csum(jnp.where(s > thr, 1.0, 0.0)),
                    n_ge + csum(jnp.where(s >= thr, 1.0, 0.0)))

        n_gt, n_ge = scan(count_fn, (zeros, zeros))
        n_gt, n_ge = fold(n_gt, jnp.sum), fold(n_ge, jnp.sum)
        thr_ref[...] = thr

        @pl.when(jnp.max(jnp.where(n_ge > float(TOPK), 1, 0)) > 0)
        def _():
            need = float(TOPK) - n_gt
            lower = jnp.where(lax.broadcasted_iota(jnp.int32, (KC, KC), 1)
                              < lax.broadcasted_iota(jnp.int32, (KC, KC), 0), 1.0, 0.0).astype(BF16)

            def sel_fn(kc, s, run):
                eq = s == thr
                eqf = jnp.where(eq, 1.0, 0.0)
                before = _dot(lower, eqf.astype(BF16)) + run
                sel = (s > thr) | (eq & (before < need))
                s_ref[_chunk(kc), :] = jnp.where(sel, 0.0, NEG_BIG)
                return run + jnp.sum(eqf, axis=0, keepdims=True)

            scan(sel_fn, jnp.zeros((1, TQ), F32))
            thr_ref[...] = jnp.full(thr_ref.shape, 0.5 * NEG_BIG, F32)

    _softmax_init(m_ref, l_ref, acc_ref)
    rep = A_HEADS // A_KV_HEADS
    row64 = lax.broadcasted_iota(jnp.int32, (AUG - HEAD_DIM, TQ), 0)
    qpos = (qi * TQ + lax.broadcasted_iota(jnp.int32, (1, TQ), 1)).astype(F32)
    for h in range(A_HEADS):
        slope = np.float32(2.0 ** (-8.0 * (h + 1) / A_HEADS) * LOG2E)
        u1, u2, u3 = _split3_f32(-slope * qpos)
        spare = jnp.zeros((AUG - HEAD_DIM, TQ), F32)
        for p, s_p in enumerate(_bf16_pieces(slope)):
            spare = jnp.where(row64 == p, 128.0 * s_p, spare)
            spare = jnp.where(row64 == 3 + p, s_p, spare)
        spare = jnp.where(row64 == 6, u1, spare)
        spare = jnp.where(row64 == 7, u2, spare)
        spare = jnp.where(row64 == 8, u3, spare)
        w_ref[h] = jnp.concatenate([qt_ref[0, h * HEAD_DIM:(h + 1) * HEAD_DIM, :],
                                    spare.astype(BF16)], axis=0)

    def attn_tiles(chunks):
        rows = [_chunk(kc) for kc in chunks]
        lhs = [[k_ref[0, r, g * AUG:(g + 1) * AUG] + posx_ref[r, :] for g in range(A_KV_HEADS)]
               for r in rows]
        bias = [jnp.where(s_ref[r, :] >= thr_ref[...], 0.0, NEG_BIG) for r in rows]

        def score_fn(item):
            j, h = item
            return _dot(lhs[j][h // rep], w_ref[h]) + bias[j]

        def vt_fn(item):
            j, h = item
            g = h // rep
            return vt_ref[0, g * HEAD_DIM:(g + 1) * HEAD_DIM, rows[j]]

        items = [(j, h) for j in range(len(chunks)) for h in range(A_HEADS)]
        _attend_chunks(items, score_fn, vt_fn, m_ref, l_ref, acc_ref, DSA_LOOKAHEAD)

    def attn_body(pair, _):
        attn_tiles([2 * pair, 2 * pair + 1])
        return 0

    lax.fori_loop(0, nch // 2, attn_body, 0)

    @pl.when(nch % 2 == 1)
    def _():
        attn_tiles([nch - 1])

    _softmax_finish(o_ref, A_HEADS, l_ref, acc_ref)


def _dsa_attn(iqt, ik4, wft, aqt, ak, posx, avt):
    bsz, w, seq = aqt.shape
    blk_t = lambda b, i: (b, 0, i)
    full = lambda b, i: (b, 0, 0)
    return pl.pallas_call(
        _dsa_kernel,
        out_shape=jax.ShapeDtypeStruct((bsz, w, seq), F32),
        grid=(bsz, seq // TQ),
        in_specs=[pl.BlockSpec((1, IDX_HEADS * IDX_K, TQ), blk_t),
                  pl.BlockSpec((1, seq, IDX_K), full),
                  pl.BlockSpec((1, 16, TQ), blk_t),
                  pl.BlockSpec((1, w, TQ), blk_t),
                  pl.BlockSpec((1, seq, A_KV_HEADS * AUG), full),
                  pl.BlockSpec((seq, AUG), lambda b, i: (0, 0)),
                  pl.BlockSpec((1, 128, seq), full)],
        out_specs=pl.BlockSpec((1, w, TQ), blk_t),
        scratch_shapes=[pltpu.VMEM((seq, TQ), F32), pltpu.VMEM((1, TQ), F32),
                        pltpu.VMEM((A_HEADS, AUG, TQ), BF16),
                        pltpu.VMEM((A_HEADS, 1, TQ), F32), pltpu.VMEM((A_HEADS, 1, TQ), F32),
                        pltpu.VMEM((A_HEADS * HEAD_DIM, TQ), F32)],
        compiler_params=pltpu.CompilerParams(dimension_semantics=("arbitrary", "arbitrary"),
                                             vmem_limit_bytes=VMEM_LIMIT),
        name="dsa_attn",
    )(iqt, ik4, wft, aqt, ak, posx, avt)


def _rms_cols(xt, g_col):
    return xt * lax.rsqrt(jnp.mean(xt * xt, axis=0, keepdims=True) + EPS) * g_col


MOE_TMG = 256
MOE_TMC = 256
ROW_TILE = 8
ROUTE_COLS = 128


def _first_index_of_max(vals, lane, big):
    m = jnp.max(vals, axis=-1, keepdims=True)
    idx = jnp.min(jnp.where(vals == m, lane, big), axis=-1, keepdims=True)
    return m, idx


def _route(h, wr_hi_ref, wr_lo_ref, br_ref):
    h_hi, h_lo = _split_bf16(h)
    logits = _dot3(h_hi, h_lo, wr_hi_ref[...], wr_lo_ref[...]) + br_ref[...]
    lane = lax.broadcasted_iota(jnp.int32, logits.shape, 1)
    ninf = -jnp.inf
    gl = jnp.where(lane < N_GROUPS, logits, ninf)
    gmax, gsel = _first_index_of_max(gl, lane, 1 << 20)
    p_group = 1.0 / jnp.sum(jnp.exp(gl - gmax), axis=-1, keepdims=True)
    base = 32 + gsel * EXPERTS_PER_GROUP
    el = jnp.where((lane >= base) & (lane < base + EXPERTS_PER_GROUP), logits, ninf)
    v1, i1 = _first_index_of_max(el, lane, 1 << 20)
    el2 = jnp.where(lane == i1, ninf, el)
    v2, i2 = _first_index_of_max(el2, lane, 1 << 20)
    e2 = jnp.exp(v2 - v1)
    w1 = 1.0 / (1.0 + e2)
    w2 = e2 / (1.0 + e2)
    return i1, i2, w1 * p_group, w2 * p_group


def _token_rows(s, n_tokens, first_token=0):
    return pl.ds(first_token * ROW_TILE + s, n_tokens, stride=ROW_TILE)


def _to_token_tiles(ref, x):
    for s in range(ROW_TILE):
        ref[_token_rows(s, x.shape[0]), :] = x[:, s * 128:(s + 1) * 128]


def _token_tile(ref, t):
    start = t * ROW_TILE if isinstance(t, int) else pl.multiple_of(t * ROW_TILE, ROW_TILE)
    return ref.at[pl.ds(start, ROW_TILE), :]


def _moe_route_kernel(oat_ref, obt_ref, x_ref, mod_ref, ga_ref, gb_ref, wo_ref,
                      gf_ref, wr_hi_ref, wr_lo_ref, br_ref,
                      x1_ref, h_ref, info_ref, infot_ref, cnt_ref, run_ref):
    @pl.when(pl.program_id(0) == 0)
    def _():
        run_ref[...] = jnp.zeros_like(run_ref)

    oa = _rms_cols(oat_ref[0], ga_ref[...]).astype(BF16)
    ob = _rms_cols(obt_ref[0], gb_ref[...]).astype(BF16)
    y = _dot_tn(oa, wo_ref[0:512, :]) + _dot_tn(ob, wo_ref[512:1024, :])
    x1 = x_ref[...] + mod_ref[0, 2:3, :] * y
    x1_ref[...] = x1

    h = _rms(x1, gf_ref[...]) * (1.0 + mod_ref[0, 4:5, :]) + mod_ref[0, 3:4, :]
    _to_token_tiles(h_ref, h)
    i1, i2, w1, w2 = _route(h, wr_hi_ref, wr_lo_ref, br_ref)
    tm = h.shape[0]
    lane = lax.broadcasted_iota(jnp.int32, (tm, ROUTE_COLS), 1)
    picked = jnp.where((lane == i1) | (lane == i2), 1.0, 0.0)
    earlier = jnp.where(lax.broadcasted_iota(jnp.int32, (tm, tm), 1)
                        < lax.broadcasted_iota(jnp.int32, (tm, tm), 0), 1.0, 0.0).astype(BF16)
    before = _dot(earlier, picked.astype(BF16)) + run_ref[...]
    rank1 = jnp.sum(jnp.where(lane == i1, before, 0.0), axis=-1, keepdims=True)
    rank2 = jnp.sum(jnp.where(lane == i2, before, 0.0), axis=-1, keepdims=True)
    run_ref[...] += jnp.sum(picked, axis=0, keepdims=True)
    cnt_ref[...] = run_ref[...]
    info = jnp.where(lane == 0, (i1 - 32).astype(F32), 0.0)
    info = jnp.where(lane == 1, (i2 - 32).astype(F32), info)
    info = jnp.where(lane == 2, rank1, info)
    info = jnp.where(lane == 3, rank2, info)
    info = jnp.where(lane == 4, w1, info)
    info = jnp.where(lane == 5, w2, info)
    info_ref[...] = info
    infot_ref[...] = info.T[0:8, :]


def _moe_route(oat, obt, x2, mod3, g_out_a, g_out_b, w_out_bf, g_ffn, wr_hi, wr_lo, b_route, seq):
    n, d = x2.shape
    tm = 512
    per_b = seq // tm
    row = lambda i: (i, 0)
    const = lambda i: (0, 0)
    blk_t = lambda i: (i // per_b, 0, i % per_b)
    return pl.pallas_call(
        _moe_route_kernel,
        out_shape=[jax.ShapeDtypeStruct((n, d), F32),
                   jax.ShapeDtypeStruct((n * ROW_TILE, d // ROW_TILE), F32),
                   jax.ShapeDtypeStruct((n, ROUTE_COLS), F32), jax.ShapeDtypeStruct((8, n), F32),
                   jax.ShapeDtypeStruct((1, ROUTE_COLS), F32)],
        grid=(n // tm,),
        in_specs=[pl.BlockSpec((1, 512, tm), blk_t), pl.BlockSpec((1, 512, tm), blk_t),
                  pl.BlockSpec((tm, d), row),
                  pl.BlockSpec((1, N_MOD, d), lambda i: (i // per_b, 0, 0)),
                  pl.BlockSpec((512, 1), const), pl.BlockSpec((512, 1), const),
                  pl.BlockSpec((d, d), const),
                  pl.BlockSpec((1, d), const),
                  pl.BlockSpec((d, ROUTE_COLS), const), pl.BlockSpec((d, ROUTE_COLS), const),
                  pl.BlockSpec((1, ROUTE_COLS), const)],
        out_specs=[pl.BlockSpec((tm, d), row),
                   pl.BlockSpec((tm * ROW_TILE, d // ROW_TILE), row),
                   pl.BlockSpec((tm, ROUTE_COLS), row), pl.BlockSpec((8, tm), lambda i: (0, i)),
                   pl.BlockSpec((1, ROUTE_COLS), const)],
        scratch_shapes=[pltpu.VMEM((1, ROUTE_COLS), F32)],
        compiler_params=pltpu.CompilerParams(dimension_semantics=("arbitrary",),
                                             vmem_limit_bytes=VMEM_LIMIT),
        name="mix_out_moe_route",
    )(oat, obt, x2, mod3, g_out_a.reshape(-1, 1), g_out_b.reshape(-1, 1), w_out_bf,
      g_ffn.reshape(1, d), wr_hi, wr_lo, b_route)


def _wait_token_copies(src_hbm, dst, sem, n_tokens):
    pltpu.make_async_copy(src_hbm.at[pl.ds(0, n_tokens * ROW_TILE), :], dst, sem).wait()


def _moe_scatter_kernel(dest_ref, pad_start_ref, pad_len_ref, nu_ref, h_ref, xs_hbm, zero_ref, sem):
    i = pl.program_id(0)
    n_pairs = 2 * MOE_TMC

    @pl.when(i == 0)
    def _():
        zero_ref[...] = jnp.zeros_like(zero_ref)

        def pad_copies(e):
            n = pad_len_ref[e]
            size = MOE_TMG // 2
            while size >= 1:
                first = pad_start_ref[e] + (n & ~(2 * size - 1))
                copy = pltpu.make_async_copy(
                    zero_ref.at[pl.ds(0, size * ROW_TILE), :],
                    xs_hbm.at[pl.ds(pl.multiple_of(first * ROW_TILE, ROW_TILE), size * ROW_TILE), :],
                    sem.at[1])
                yield (n & size) != 0, copy
                size //= 2

        for e in range(N_EXPERTS):
            for present, copy in pad_copies(e):
                pl.when(present)(copy.start)
        for e in range(N_EXPERTS):
            for present, copy in pad_copies(e):
                pl.when(present)(copy.wait)

        def row_tile(t):
            return xs_hbm.at[pl.ds(pl.multiple_of(t * (MOE_TMG * ROW_TILE), MOE_TMG * ROW_TILE),
                                   MOE_TMG * ROW_TILE), :]

        n_tiles = xs_hbm.shape[0] // (MOE_TMG * ROW_TILE)

        def fill_tile(t, _):
            pltpu.make_async_copy(zero_ref, row_tile(t), sem.at[1]).start()
            return 0

        def drain_tile(t, _):
            pltpu.make_async_copy(zero_ref, row_tile(0), sem.at[1]).wait()
            return 0

        lax.fori_loop(nu_ref[0], n_tiles, fill_tile, 0)
        lax.fori_loop(nu_ref[0], n_tiles, drain_tile, 0)

    for r in range(MOE_TMC):
        for j in range(2):
            pltpu.make_async_copy(_token_tile(h_ref, r),
                                  _token_tile(xs_hbm, dest_ref[i * n_pairs + j * MOE_TMC + r]),
                                  sem.at[0]).start(priority=j)
    _wait_token_copies(xs_hbm, xs_hbm.at[pl.ds(0, n_pairs * ROW_TILE), :], sem.at[0], n_pairs)


def _moe_scatter(dest, pad_start, pad_len, n_used, h2, n_rows):
    n = h2.shape[0] // ROW_TILE
    grid_spec = pltpu.PrefetchScalarGridSpec(
        num_scalar_prefetch=4,
        grid=(n // MOE_TMC,),
        in_specs=[pl.BlockSpec((MOE_TMC * ROW_TILE, h2.shape[1]), lambda i, *_: (i, 0))],
        out_specs=pl.BlockSpec(memory_space=pl.ANY),
        scratch_shapes=[pltpu.VMEM((MOE_TMG * ROW_TILE, h2.shape[1]), F32),
                        pltpu.SemaphoreType.DMA((2,))])
    return pl.pallas_call(
        _moe_scatter_kernel,
        out_shape=jax.ShapeDtypeStruct((n_rows * ROW_TILE, h2.shape[1]), F32),
        grid_spec=grid_spec,
        compiler_params=pltpu.CompilerParams(dimension_semantics=("arbitrary",),
                                             vmem_limit_bytes=VMEM_LIMIT),
        name="moe_scatter",
    )(dest, pad_start, pad_len, n_used, h2)


def _moe_expert_kernel(te_ref, nu_ref, x_ref, wg_ref, wu_ref, wd_ref, y_ref,
                       xs_ref, wgb_ref, wub_ref, wdb_ref):
    i = pl.program_id(0)
    used = i < nu_ref[0]
    new_expert = jnp.logical_or(i == 0, te_ref[i] != te_ref[jnp.maximum(i - 1, 0)])

    @pl.when(jnp.logical_and(used, new_expert))
    def _():
        wgb_ref[...] = wg_ref[0].astype(BF16)
        wub_ref[...] = wu_ref[0].astype(BF16)
        wdb_ref[...] = wd_ref[0].astype(BF16)

    @pl.when(used)
    def _():
        for s in range(ROW_TILE):
            xs_ref[:, s * 128:(s + 1) * 128] = x_ref[_token_rows(s, MOE_TMG), :].astype(BF16)
        x = xs_ref[...]
        hg = _dot(x, wgb_ref[...])
        hu = _dot(x, wub_ref[...])
        a = hg * jax.nn.sigmoid(hg) * hu
        _to_token_tiles(y_ref, _dot(a.astype(BF16), wdb_ref[...]))

    @pl.when(i >= nu_ref[0])
    def _():
        y_ref[...] = jnp.zeros_like(y_ref)


def _moe_experts(tile_expert, n_used, xsorted, wg, wu, wd):
    n_rows = xsorted.shape[0] // ROW_TILE
    d, ff = wg.shape[1], wg.shape[2]
    tile = (MOE_TMG * ROW_TILE, xsorted.shape[1])
    grid_spec = pltpu.PrefetchScalarGridSpec(
        num_scalar_prefetch=2,
        grid=(n_rows // MOE_TMG,),
        in_specs=[pl.BlockSpec(tile, lambda i, te, nu: (jnp.minimum(i, nu[0] - 1), 0)),
                  pl.BlockSpec((1, d, ff), lambda i, te, nu: (te[i], 0, 0)),
                  pl.BlockSpec((1, d, ff), lambda i, te, nu: (te[i], 0, 0)),
                  pl.BlockSpec((1, ff, d), lambda i, te, nu: (te[i], 0, 0))],
        out_specs=pl.BlockSpec(tile, lambda i, te, nu: (i, 0)),
        scratch_shapes=[pltpu.VMEM((MOE_TMG, d), BF16), pltpu.VMEM((d, ff), BF16),
                        pltpu.VMEM((d, ff), BF16), pltpu.VMEM((ff, d), BF16)])
    return pl.pallas_call(
        _moe_expert_kernel,
        out_shape=jax.ShapeDtypeStruct(xsorted.shape, F32),
        grid_spec=grid_spec,
        compiler_params=pltpu.CompilerParams(dimension_semantics=("arbitrary",),
                                             vmem_limit_bytes=VMEM_LIMIT),
        name="moe_experts",
    )(tile_expert, n_used, xsorted, wg, wu, wd)


def _moe_combine_kernel(dest_ref, y_hbm, x_ref, info_ref, mod_ref, gfin_ref, o_ref, ybuf, x2_ref, sem):
    i = pl.program_id(0)
    nt = pl.num_programs(0)
    slot = lax.rem(i, 2)
    n_pairs = 2 * MOE_TMC

    def start_gather(tile, to_slot):
        for r in range(n_pairs):
            pltpu.make_async_copy(_token_tile(y_hbm, dest_ref[tile * n_pairs + r]),
                                  _token_tile(ybuf.at[to_slot], r), sem.at[to_slot]).start(priority=r % 2)

    @pl.when(i == 0)
    def _():
        start_gather(0, 0)

    _wait_token_copies(y_hbm, ybuf.at[slot], sem.at[slot], n_pairs)

    @pl.when(i + 1 < nt)
    def _():
        start_gather(i + 1, 1 - slot)

    w1 = info_ref[:, 4:5]
    w2 = info_ref[:, 5:6]
    sumsq = jnp.zeros((MOE_TMC, 1), F32)
    for s in range(ROW_TILE):
        cols = slice(s * 128, (s + 1) * 128)
        y = (w1 * ybuf[slot, _token_rows(s, MOE_TMC), :]
             + w2 * ybuf[slot, _token_rows(s, MOE_TMC, first_token=MOE_TMC), :])
        x2 = x_ref[:, cols] + mod_ref[0, 5:6, cols] * y
        x2_ref[:, cols] = x2
        sumsq = sumsq + jnp.sum(x2 * x2, axis=-1, keepdims=True)
    d = x2_ref.shape[1]
    o_ref[...] = x2_ref[...] * lax.rsqrt(sumsq / d + EPS) * gfin_ref[...]


def _moe_combine(dest, ysorted, x1, info, mod3, g_final, seq):
    n, d = x1.shape
    per_b = seq // MOE_TMC
    grid_spec = pltpu.PrefetchScalarGridSpec(
        num_scalar_prefetch=1,
        grid=(n // MOE_TMC,),
        in_specs=[pl.BlockSpec(memory_space=pl.ANY),
                  pl.BlockSpec((MOE_TMC, d), lambda i, ds: (i, 0)),
                  pl.BlockSpec((MOE_TMC, ROUTE_COLS), lambda i, ds: (i, 0)),
                  pl.BlockSpec((1, N_MOD, d), lambda i, ds: (i // per_b, 0, 0)),
                  pl.BlockSpec((1, d), lambda i, ds: (0, 0))],
        out_specs=pl.BlockSpec((MOE_TMC, d), lambda i, ds: (i, 0)),
        scratch_shapes=[pltpu.VMEM((2, 2 * MOE_TMC * ROW_TILE, ysorted.shape[1]), F32),
                        pltpu.VMEM((MOE_TMC, d), F32), pltpu.SemaphoreType.DMA((2,))])
    return pl.pallas_call(
        _moe_combine_kernel,
        out_shape=jax.ShapeDtypeStruct((n, d), F32),
        grid_spec=grid_spec,
        compiler_params=pltpu.CompilerParams(dimension_semantics=("arbitrary",),
                                             vmem_limit_bytes=VMEM_LIMIT),
        name="moe_combine",
    )(dest, ysorted, x1, info, mod3, g_final.reshape(1, d))


def _mix_out_and_moe(oat, obt, x2, mod3, g_out_a, g_out_b, w_out_bf, g_ffn, wr_hi, wr_lo, b_route,
                     wg, wu, wd, g_final, seq):
    n, d = x2.shape
    x1, h2, info, infot, counts = _moe_route(oat, obt, x2, mod3, g_out_a, g_out_b, w_out_bf,
                                             g_ffn, wr_hi, wr_lo, b_route, seq)

    e1, e2, rank1, rank2 = [infot[k].astype(jnp.int32) for k in range(4)]
    cnt = counts[0, 32:32 + N_EXPERTS].astype(jnp.int32)
    padded = ((cnt + MOE_TMG - 1) // MOE_TMG) * MOE_TMG
    seg_end = jnp.cumsum(padded)
    expert_ids = jnp.arange(N_EXPERTS, dtype=jnp.int32)

    def seg_start_of(e):
        return jnp.sum(jnp.where(expert_ids[None, :] < e[:, None], padded[None, :], 0), axis=1)

    dest1 = seg_start_of(e1) + rank1
    dest2 = seg_start_of(e2) + rank2
    n_rows = 2 * n + N_EXPERTS * MOE_TMG
    tile_start = jnp.arange(n_rows // MOE_TMG, dtype=jnp.int32) * MOE_TMG
    tile_expert = jnp.minimum(jnp.sum((tile_start[:, None] >= seg_end[None, :]).astype(jnp.int32), axis=1),
                              N_EXPERTS - 1)
    n_used = (seg_end[N_EXPERTS - 1:] // MOE_TMG).astype(jnp.int32)
    dest = jnp.concatenate([dest1.reshape(-1, MOE_TMC), dest2.reshape(-1, MOE_TMC)], axis=1).reshape(-1)

    seg_start = seg_end - padded
    xsorted = _moe_scatter(dest, seg_start + cnt, padded - cnt, n_used, h2, n_rows)
    ysorted = _moe_experts(tile_expert, n_used, xsorted, wg, wu, wd)
    return _moe_combine(dest, ysorted, x1, info, mod3, g_final, seq)


def _layer(x3, c, w_ada, b_ada, g_mix, w_in, b_forget, g_out_a, g_out_b, w_out,
           g_ffn, w_group, b_group, w_router, b_router, w_gate, w_up, w_down, g_final):
    bsz, seq, d = x3.shape
    mod3 = _ada_mod(c, w_ada, b_ada).reshape(bsz, N_MOD, d)

    w_t = w_in.T

    def pad_heads(w, n_heads):
        w = w.reshape(d, n_heads, HEAD_DIM)
        return jnp.concatenate([w, jnp.zeros_like(w)], axis=-1).reshape(d, n_heads * AUG)

    w_wf_t = jnp.concatenate([w_t[1344:1352], w_t[2888:2896]], axis=0)
    weights = [w_t[0:512].astype(BF16), pad_heads(w_in[:, 512:640], A_KV_HEADS).astype(BF16),
               w_t[640:768].astype(BF16), w_t[1352:1864].astype(BF16),
               w_in[:, 1864:2376].astype(BF16), w_t[2376:2888].astype(BF16),
               w_t[768:1280].astype(BF16), w_in[:, 1280:1344].astype(BF16), *_split_bf16(w_wf_t)]
    aqt, ak, avt, bqt, bk, bvt, iqt, ik4, wft = _in_proj(x3, mod3, g_mix, weights)

    cumt, kaug = _fox_cum(wft, b_forget, bk)
    obt = _fox_attn(bqt, kaug, bvt, cumt)

    pos = jnp.arange(seq, dtype=jnp.int32)[:, None]
    lane = jnp.arange(AUG, dtype=jnp.int32)[None, :] - HEAD_DIM
    posx = jnp.where((lane >= 0) & (lane < 3), pos >> 7,
                     jnp.where((lane >= 3) & (lane < 6), pos & 127,
                               jnp.where((lane >= 6) & (lane < 9), 1, 0))).astype(BF16)
    oat = _dsa_attn(iqt, ik4, wft, aqt, ak, posx, avt)

    w_r = jnp.concatenate([w_group, jnp.zeros((d, 32 - N_GROUPS), F32),
                           jnp.transpose(w_router, (1, 0, 2)).reshape(d, N_EXPERTS),
                           jnp.zeros((d, ROUTE_COLS - 64), F32)], axis=1)
    b_r = jnp.concatenate([b_group, jnp.zeros((32 - N_GROUPS,), F32), b_router.reshape(-1),
                           jnp.zeros((ROUTE_COLS - 64,), F32)]).reshape(1, ROUTE_COLS)
    wr_hi, wr_lo = _split_bf16(w_r)
    out = _mix_out_and_moe(oat, obt, x3.reshape(bsz * seq, d), mod3, g_out_a, g_out_b,
                           w_out.astype(BF16), g_ffn, wr_hi, wr_lo, b_r, w_gate, w_up, w_down,
                           g_final, seq)
    return out.reshape(bsz, seq, d)


def kernel(x, c, w_ada, b_ada, g_mix, w_in, b_forget, g_out_a, g_out_b, w_out, g_ffn, w_group,
           b_group, w_router, b_router, w_gate, w_up, w_down, g_final):
    depth = w_ada.shape[0]
    assert depth == 1, "final norm is fused into the single layer's MoE kernel"
    return _layer(x, c, w_ada[0], b_ada[0], g_mix[0], w_in[0], b_forget[0], g_out_a[0], g_out_b[0],
                  w_out[0], g_ffn[0], w_group[0], b_group[0], w_router[0], b_router[0], w_gate[0],
                  w_up[0], w_down[0], g_final)
```

```python
import math

import jax
import jax.numpy as jnp
import numpy as np
from jax import lax
from jax.experimental import pallas as pl
from jax.experimental.pallas import tpu as pltpu

F32 = jnp.float32
BF16 = jnp.bfloat16

EPS = 1e-6
A_HEADS = 8
A_KV_HEADS = 2
HEAD_DIM = 64
IDX_HEADS = 8
IDX_DIM = 64
TOPK = 256
B_HEADS = 8
N_GROUPS = 4
EXPERTS_PER_GROUP = 8
N_EXPERTS = N_GROUPS * EXPERTS_PER_GROUP
N_MOD = 6

NEG_BIG = -1e30
LOWEST = float(np.finfo(np.float32).min)
LOG2E = math.log2(math.e)
Q_SCALE = HEAD_DIM ** -0.5 * LOG2E
VMEM_LIMIT = 48 * 1024 * 1024

TQ = 256
KC = 256
AUG = 128
IDX_K = IDX_DIM
ONES_ROWS = 16
COARSE_STEPS = 14
FOX_LOOKAHEAD = 6
DSA_LOOKAHEAD = 4
assert TQ == KC == TOPK


def _split_bf16(x):
    hi = x.astype(BF16)
    lo = (x - hi.astype(F32)).astype(BF16)
    return hi, lo


def _split3_f32(x):
    p1 = x.astype(BF16).astype(F32)
    r1 = x - p1
    p2 = r1.astype(BF16).astype(F32)
    p3 = (r1 - p2).astype(BF16).astype(F32)
    return p1, p2, p3


def _dot(a, b):
    return jnp.dot(a, b, preferred_element_type=F32)


def _dot_nt(a, b):
    return lax.dot_general(a, b, (((1,), (1,)), ((), ())), preferred_element_type=F32)


def _dot_tn(a, b):
    return lax.dot_general(a, b, (((0,), (0,)), ((), ())), preferred_element_type=F32)


def _dot3(a_hi, a_lo, b_hi, b_lo):
    return _dot(a_hi, b_hi) + _dot(a_lo, b_hi) + _dot(a_hi, b_lo)


def _dot3_nt(a_hi, a_lo, b_hi, b_lo):
    return _dot_nt(a_hi, b_hi) + _dot_nt(a_lo, b_hi) + _dot_nt(a_hi, b_lo)


def _rms(x, g):
    return x * lax.rsqrt(jnp.mean(x * x, axis=-1, keepdims=True) + EPS) * g


def _chunk(kc):
    return pl.ds(pl.multiple_of(kc * KC, KC), KC)


def _key_minus_query():
    return (lax.broadcasted_iota(jnp.int32, (KC, TQ), 0)
            - lax.broadcasted_iota(jnp.int32, (KC, TQ), 1))


def _ada_kernel(c_ref, w_ref, b_ref, o_ref):
    c = c_ref[...]
    s = c * jax.nn.sigmoid(c)
    s_hi, s_lo = _split_bf16(s)
    w_hi, w_lo = _split_bf16(w_ref[...])
    o_ref[...] = _dot3(s_hi, s_lo, w_hi, w_lo) + b_ref[...]


def _ada_mod(c, w_ada, b_ada):
    bsz, d = c.shape
    n = w_ada.shape[1]
    tn = 1024
    return pl.pallas_call(
        _ada_kernel,
        out_shape=jax.ShapeDtypeStruct((bsz, n), F32),
        grid=(n // tn,),
        in_specs=[pl.BlockSpec((bsz, d), lambda j: (0, 0)),
                  pl.BlockSpec((d, tn), lambda j: (0, j)),
                  pl.BlockSpec((1, tn), lambda j: (0, j))],
        out_specs=pl.BlockSpec((bsz, tn), lambda j: (0, j)),
        compiler_params=pltpu.CompilerParams(dimension_semantics=("arbitrary",),
                                             vmem_limit_bytes=VMEM_LIMIT),
        name="ada_mod",
    )(c, w_ada, b_ada.reshape(1, n))


def _in_proj_kernel(x_ref, mod_ref, g_ref,
                    waq_ref, wak_ref, wav_ref, wbq_ref, wbk_ref, wbv_ref,
                    wiq_ref, wik_ref, wwfh_ref, wwfl_ref,
                    aqt_ref, ak_ref, avt_ref, bqt_ref, bk_ref, bvt_ref, iqt_ref, ik_ref, wft_ref):
    x = x_ref[0]
    h = _rms(x, g_ref[...]) * (1.0 + mod_ref[0, 1:2, :]) + mod_ref[0, 0:1, :]
    h_hi, h_lo = _split_bf16(h)
    aqt_ref[0] = (_dot_nt(waq_ref[...], h_hi) * Q_SCALE).astype(BF16)
    ak_ref[0] = _dot(h_hi, wak_ref[...]).astype(BF16)
    avt_ref[0] = _dot_nt(wav_ref[...], h_hi).astype(BF16)
    bqt_ref[0] = (_dot_nt(wbq_ref[...], h_hi) * Q_SCALE).astype(BF16)
    bk_ref[0] = _dot(h_hi, wbk_ref[...]).astype(BF16)
    bvt_ref[0] = _dot_nt(wbv_ref[...], h_hi).astype(BF16)
    iqt_ref[0] = _dot_nt(wiq_ref[...], h_hi).astype(BF16)
    ik_ref[0] = _dot(h_hi, wik_ref[...]).astype(BF16)
    wft_ref[0] = _dot3_nt(wwfh_ref[...], wwfl_ref[...], h_hi, h_lo)


def _in_proj(x3, mod3, g_mix, weights):
    bsz, seq, d = x3.shape
    tm = 512
    blk_t = lambda b, i: (b, 0, i)
    blk_r = lambda b, i: (b, i, 0)
    const = lambda b, i: (0, 0)
    ak_w = A_KV_HEADS * AUG
    bk_w = B_HEADS * HEAD_DIM
    outs = [jax.ShapeDtypeStruct((bsz, 512, seq), BF16), jax.ShapeDtypeStruct((bsz, seq, ak_w), BF16),
            jax.ShapeDtypeStruct((bsz, 128, seq), BF16), jax.ShapeDtypeStruct((bsz, 512, seq), BF16),
            jax.ShapeDtypeStruct((bsz, seq, bk_w), BF16), jax.ShapeDtypeStruct((bsz, 512, seq), BF16),
            jax.ShapeDtypeStruct((bsz, IDX_HEADS * IDX_K, seq), BF16),
            jax.ShapeDtypeStruct((bsz, seq, IDX_K), BF16), jax.ShapeDtypeStruct((bsz, 16, seq), F32)]
    out_specs = [pl.BlockSpec((1, 512, tm), blk_t), pl.BlockSpec((1, tm, ak_w), blk_r),
                 pl.BlockSpec((1, 128, tm), blk_t), pl.BlockSpec((1, 512, tm), blk_t),
                 pl.BlockSpec((1, tm, bk_w), blk_r), pl.BlockSpec((1, 512, tm), blk_t),
                 pl.BlockSpec((1, IDX_HEADS * IDX_K, tm), blk_t),
                 pl.BlockSpec((1, tm, IDX_K), blk_r), pl.BlockSpec((1, 16, tm), blk_t)]
    return pl.pallas_call(
        _in_proj_kernel,
        out_shape=outs,
        grid=(bsz, seq // tm),
        in_specs=[pl.BlockSpec((1, tm, d), blk_r),
                  pl.BlockSpec((1, N_MOD, d), lambda b, i: (b, 0, 0)),
                  pl.BlockSpec((1, d), const)] + [pl.BlockSpec(w.shape, const) for w in weights],
        out_specs=out_specs,
        compiler_params=pltpu.CompilerParams(dimension_semantics=("arbitrary", "arbitrary"),
                                             vmem_limit_bytes=VMEM_LIMIT),
        name="in_proj",
    )(x3, mod3, g_mix.reshape(1, d), *weights)


CB = 256


def _cum_kernel(wft_ref, bfor_ref, k_ref, cumt_ref, kaug_ref):
    seq = wft_ref.shape[2]
    r = lax.broadcasted_iota(jnp.int32, (CB, CB), 0)
    cidx = lax.broadcasted_iota(jnp.int32, (CB, CB), 1)
    tri = jnp.where(r <= cidx, 1.0, 0.0).astype(BF16)
    row128 = lax.broadcasted_iota(jnp.int32, (AUG, CB), 0)
    ones_rows = jnp.where((row128 >= HEAD_DIM + 3) & (row128 < HEAD_DIM + 6), 1.0, 0.0)
    src = lax.broadcasted_iota(jnp.int32, (AUG, AUG), 0)
    dst = lax.broadcasted_iota(jnp.int32, (AUG, AUG), 1)
    place = [jnp.where((dst < HEAD_DIM) & (src == dst + half * HEAD_DIM), 1.0, 0.0).astype(BF16)
             for half in range(2)]
    carry = jnp.zeros((8, 1), F32)
    for blk in range(seq // CB):
        cols = slice(blk * CB, (blk + 1) * CB)
        z = wft_ref[0, 8:16, cols] + bfor_ref[...]
        logf = jnp.minimum(z, 0.0) - jnp.log(1.0 + jnp.exp(-jnp.abs(z)))
        p1, p2, p3 = _split3_f32(logf)
        pieces = jnp.concatenate([p1, p2, p3, jnp.zeros_like(p1)], axis=0).astype(BF16)
        parts = _dot(pieces, tri)
        cum = parts[0:8] + parts[8:16] + parts[16:24] + carry
        carry = cum[:, CB - 1:CB]
        cum2 = cum * LOG2E
        cumt_ref[0, :, cols] = cum2
        c1, c2, c3 = _split3_f32(cum2)
        for h in range(B_HEADS):
            spare = jnp.where(row128 == HEAD_DIM, -c1[h:h + 1], ones_rows)
            spare = jnp.where(row128 == HEAD_DIM + 1, -c2[h:h + 1], spare)
            spare = jnp.where(row128 == HEAD_DIM + 2, -c3[h:h + 1], spare)
            pair = k_ref[0, cols, (h // 2) * AUG:(h // 2 + 1) * AUG]
            k_h = _dot(pair, place[h % 2]).astype(BF16)
            kaug_ref[0, cols, h * AUG:(h + 1) * AUG] = k_h + spare.T.astype(BF16)


def _fox_cum(wft, b_forget, bk):
    bsz, _, seq = wft.shape
    nh = B_HEADS
    kw = bk.shape[-1]
    kaug_w = nh * AUG
    return pl.pallas_call(
        _cum_kernel,
        out_shape=[jax.ShapeDtypeStruct((bsz, nh, seq), F32),
                   jax.ShapeDtypeStruct((bsz, seq, kaug_w), BF16)],
        grid=(bsz,),
        in_specs=[pl.BlockSpec((1, 16, seq), lambda b: (b, 0, 0)),
                  pl.BlockSpec((nh, 1), lambda b: (0, 0)),
                  pl.BlockSpec((1, seq, kw), lambda b: (b, 0, 0))],
        out_specs=[pl.BlockSpec((1, nh, seq), lambda b: (b, 0, 0)),
                   pl.BlockSpec((1, seq, kaug_w), lambda b: (b, 0, 0))],
        compiler_params=pltpu.CompilerParams(dimension_semantics=("arbitrary",),
                                             vmem_limit_bytes=VMEM_LIMIT),
        name="fox_cum",
    )(wft, b_forget.reshape(nh, 1), bk)


def _softmax_init(m_ref, l_ref, acc_ref):
    m_ref[...] = jnp.full(m_ref.shape, NEG_BIG, F32)
    l_ref[...] = jnp.zeros(l_ref.shape, F32)
    acc_ref[...] = jnp.zeros(acc_ref.shape, F32)


def _attend_chunks(items, score_fn, vt_fn, m_ref, l_ref, acc_ref, lookahead):
    ahead = min(lookahead, len(items))
    scores = {i: score_fn(items[i]) for i in range(ahead)}
    ones = jnp.ones((ONES_ROWS, KC), BF16)
    for i, item in enumerate(items):
        if i + ahead < len(items):
            scores[i + ahead] = score_fn(items[i + ahead])
        s = scores.pop(i)
        h = item[1]
        m_old = m_ref[h]
        m_new = jnp.maximum(m_old, jnp.max(s, axis=0, keepdims=True))
        alpha = jnp.exp2(m_old - m_new)
        p = jnp.exp2(s - m_new).astype(BF16)
        pv = _dot(jnp.concatenate([vt_fn(item), ones], axis=0), p)
        m_ref[h] = m_new
        l_ref[h] = alpha * l_ref[h] + pv[HEAD_DIM:HEAD_DIM + 1, :]
        rows = slice(h * HEAD_DIM, (h + 1) * HEAD_DIM)
        acc_ref[rows, :] = alpha * acc_ref[rows, :] + pv[0:HEAD_DIM, :]


def _softmax_finish(o_ref, n_heads, l_ref, acc_ref):
    for h in range(n_heads):
        rows = slice(h * HEAD_DIM, (h + 1) * HEAD_DIM)
        o_ref[0, rows, :] = acc_ref[rows, :] / l_ref[h]


def _fox_kernel(qt_ref, k_ref, vt_ref, cumt_ref, o_ref, w_ref, m_ref, l_ref, acc_ref):
    qi = pl.program_id(1)
    _softmax_init(m_ref, l_ref, acc_ref)

    row64 = lax.broadcasted_iota(jnp.int32, (AUG - HEAD_DIM, TQ), 0)
    for h in range(B_HEADS):
        c1, c2, c3 = _split3_f32(cumt_ref[0, h:h + 1, :])
        spare = jnp.where(row64 < 3, 1.0, 0.0)
        spare = jnp.where(row64 == 3, c1, spare)
        spare = jnp.where(row64 == 4, c2, spare)
        spare = jnp.where(row64 == 5, c3, spare)
        w_ref[h] = jnp.concatenate([qt_ref[0, h * HEAD_DIM:(h + 1) * HEAD_DIM, :],
                                    spare.astype(BF16)], axis=0)

    def tiles(chunks):
        rows = [_chunk(kc) for kc, _ in chunks]

        def score_fn(item):
            j, h = item
            s = _dot(k_ref[0, rows[j], h * AUG:(h + 1) * AUG], w_ref[h])
            if chunks[j][1]:
                s = jnp.where(_key_minus_query() <= 0, s, NEG_BIG)
            return s

        def vt_fn(item):
            j, h = item
            return vt_ref[0, h * HEAD_DIM:(h + 1) * HEAD_DIM, rows[j]]

        items = [(j, h) for j in range(len(chunks)) for h in range(B_HEADS)]
        _attend_chunks(items, score_fn, vt_fn, m_ref, l_ref, acc_ref, FOX_LOOKAHEAD)

    def body(pair, _):
        tiles([(2 * pair, False), (2 * pair + 1, False)])
        return 0

    lax.fori_loop(0, qi // 2, body, 0)

    @pl.when(qi % 2 == 1)
    def _():
        tiles([(qi - 1, False), (qi, True)])

    @pl.when(qi % 2 == 0)
    def _():
        tiles([(qi, True)])

    _softmax_finish(o_ref, B_HEADS, l_ref, acc_ref)


def _fox_attn(bqt, kaug, bvt, cumt):
    bsz, w, seq = bqt.shape
    blk_t = lambda b, i: (b, 0, i)
    full = lambda b, i: (b, 0, 0)
    return pl.pallas_call(
        _fox_kernel,
        out_shape=jax.ShapeDtypeStruct((bsz, w, seq), F32),
        grid=(bsz, seq // TQ),
        in_specs=[pl.BlockSpec((1, w, TQ), blk_t),
                  pl.BlockSpec((1, seq, B_HEADS * AUG), full),
                  pl.BlockSpec((1, w, seq), full),
                  pl.BlockSpec((1, B_HEADS, TQ), blk_t)],
        out_specs=pl.BlockSpec((1, w, TQ), blk_t),
        scratch_shapes=[pltpu.VMEM((B_HEADS, AUG, TQ), BF16), pltpu.VMEM((B_HEADS, 1, TQ), F32),
                        pltpu.VMEM((B_HEADS, 1, TQ), F32), pltpu.VMEM((B_HEADS * HEAD_DIM, TQ), F32)],
        compiler_params=pltpu.CompilerParams(dimension_semantics=("arbitrary", "arbitrary"),
                                             vmem_limit_bytes=VMEM_LIMIT),
        name="fox_attn",
    )(bqt, kaug, bvt, cumt)


def _bf16_pieces(value):
    pieces = []
    rest = np.float32(value)
    for _ in range(3):
        piece = np.asarray(rest).astype(BF16).astype(np.float32)
        pieces.append(float(piece))
        rest = np.float32(rest - piece)
    return pieces


def _dsa_kernel(iqt_ref, ik_ref, wft_ref, qt_ref, k_ref, posx_ref, vt_ref, o_ref,
                s_ref, thr_ref, w_ref, m_ref, l_ref, acc_ref):
    qi = pl.program_id(1)
    nch = qi + 1
    kmq = _key_minus_query()

    def score_chunk(kc):
        ik = ik_ref[0, _chunk(kc), :]
        acc = jnp.zeros((KC, TQ), F32)
        for h in range(IDX_HEADS):
            d = _dot(ik, iqt_ref[0, h * IDX_K:(h + 1) * IDX_K, :])
            acc = acc + wft_ref[0, h:h + 1, :] * jnp.maximum(d, 0.0)
        acc = jnp.where(acc == 0.0, 0.0, acc)
        causal = kmq <= (qi - kc) * KC
        s_ref[_chunk(kc), :] = jnp.where(causal, acc, -jnp.inf)

    def score_body(pair, _):
        score_chunk(2 * pair)
        score_chunk(2 * pair + 1)
        return 0

    lax.fori_loop(0, nch // 2, score_body, 0)

    @pl.when(nch % 2 == 1)
    def _():
        score_chunk(nch - 1)
        s_ref[_chunk(nch), :] = jnp.full((KC, TQ), -jnp.inf, F32)

    def scan(fn, init):
        def pair_body(pair, c):
            c = fn(2 * pair, s_ref[_chunk(2 * pair), :], c)
            return fn(2 * pair + 1, s_ref[_chunk(2 * pair + 1), :], c)

        return lax.fori_loop(0, (nch + 1) // 2, pair_body, init)

    @pl.when(qi == 0)
    def _():
        thr_ref[...] = jnp.full(thr_ref.shape, LOWEST, F32)

    @pl.when(qi > 0)
    def _():
        def part(x, op):
            return op(x.reshape(KC // 8, 8, TQ), axis=0)

        def cmin(x):
            return part(x, jnp.min)

        def cmax(x):
            return part(x, jnp.max)

        def csum(x):
            return part(x, jnp.sum)

        def fold(x, op):
            return op(x, axis=0, keepdims=True)

        zeros = jnp.zeros((8, TQ), F32)
        pinf = jnp.full((8, TQ), jnp.inf, F32)
        ninf = jnp.full((8, TQ), -jnp.inf, F32)

        def init_fn(_, s, c):
            lo, hi = c
            lo = jnp.minimum(lo, cmin(jnp.where(s > -jnp.inf, s, jnp.inf)))
            hi = jnp.maximum(hi, cmax(s))
            return lo, hi

        lo, hi = scan(init_fn, (pinf, ninf))
        lo, hi = fold(lo, jnp.min), fold(hi, jnp.max)

        def coarse_step(_, carry):
            lo, hi = carry
            mid = lo + (hi - lo) * 0.5
            cnt = scan(lambda _, s, c: c + csum(jnp.where(s >= mid, 1.0, 0.0)), zeros)
            enough = fold(cnt, jnp.sum) >= float(TOPK)
            return jnp.where(enough, mid, lo), jnp.where(enough, hi, mid)

        lo, hi = lax.fori_loop(0, COARSE_STEPS, coarse_step, (lo, hi))

        def cond(carry):
            return carry[2] > 0

        def step(carry):
            lo, hi, _ = carry
            mid = lo + (hi - lo) * 0.5
            mid = jnp.where(mid <= lo, hi, mid)

            def fn(_, s, c):
                cnt, a, b = c
                ge = s >= mid
                cnt = cnt + csum(jnp.where(ge, 1.0, 0.0))
                b = jnp.minimum(b, cmin(jnp.where(ge, s, jnp.inf)))
                a = jnp.maximum(a, cmax(jnp.where(ge, -jnp.inf, s)))
                return cnt, a, b

            cnt, a, b = scan(fn, (zeros, ninf, pinf))
            cnt, a, b = fold(cnt, jnp.sum), fold(a, jnp.max), fold(b, jnp.min)
            enough = cnt >= float(TOPK)
            new_lo = jnp.where(enough, b, jnp.where(cnt == float(TOPK - 1), a, lo))
            new_hi = jnp.where(enough, jnp.where(cnt == float(TOPK), b, hi), a)
            active = jnp.max(jnp.where(new_lo < new_hi, 1, 0))
            return new_lo, new_hi, active

        first_active = jnp.max(jnp.where(lo < hi, 1, 0))
        thr, _, _ = lax.while_loop(cond, step, (lo, hi, first_active))

        def count_fn(_, s, c):
            n_gt, n_ge = c
            return (n_gt + csum(jnp.where(s > thr, 1.0, 0.0)),
                    n_ge + csum(jnp.where(s >= thr, 1.0, 0.0)))

        n_gt, n_ge = scan(count_fn, (zeros, zeros))
        n_gt, n_ge = fold(n_gt, jnp.sum), fold(n_ge, jnp.sum)
        thr_ref[...] = thr

        @pl.when(jnp.max(jnp.where(n_ge > float(TOPK), 1, 0)) > 0)
        def _():
            need = float(TOPK) - n_gt
            lower = jnp.where(lax.broadcasted_iota(jnp.int32, (KC, KC), 1)
                              < lax.broadcasted_iota(jnp.int32, (KC, KC), 0), 1.0, 0.0).astype(BF16)

            def sel_fn(kc, s, run):
                eq = s == thr
                eqf = jnp.where(eq, 1.0, 0.0)
                before = _dot(lower, eqf.astype(BF16)) + run
                sel = (s > thr) | (eq & (before < need))
                s_ref[_chunk(kc), :] = jnp.where(sel, 0.0, NEG_BIG)
                return run + jnp.sum(eqf, axis=0, keepdims=True)

            scan(sel_fn, jnp.zeros((1, TQ), F32))
            thr_ref[...] = jnp.full(thr_ref.shape, 0.5 * NEG_BIG, F32)

    _softmax_init(m_ref, l_ref, acc_ref)
    rep = A_HEADS // A_KV_HEADS
    row64 = lax.broadcasted_iota(jnp.int32, (AUG - HEAD_DIM, TQ), 0)
    qpos = (qi * TQ + lax.broadcasted_iota(jnp.int32, (1, TQ), 1)).astype(F32)
    for h in range(A_HEADS):
        slope = np.float32(2.0 ** (-8.0 * (h + 1) / A_HEADS) * LOG2E)
        u1, u2, u3 = _split3_f32(-slope * qpos)
        spare = jnp.zeros((AUG - HEAD_DIM, TQ), F32)
        for p, s_p in enumerate(_bf16_pieces(slope)):
            spare = jnp.where(row64 == p, 128.0 * s_p, spare)
            spare = jnp.where(row64 == 3 + p, s_p, spare)
        spare = jnp.where(row64 == 6, u1, spare)
        spare = jnp.where(row64 == 7, u2, spare)
        spare = jnp.where(row64 == 8, u3, spare)
        w_ref[h] = jnp.concatenate([qt_ref[0, h * HEAD_DIM:(h + 1) * HEAD_DIM, :],
                                    spare.astype(BF16)], axis=0)

    def attn_tiles(chunks):
        rows = [_chunk(kc) for kc in chunks]
        lhs = [[k_ref[0, r, g * AUG:(g + 1) * AUG] + posx_ref[r, :] for g in range(A_KV_HEADS)]
               for r in rows]
        bias = [jnp.where(s_ref[r, :] >= thr_ref[...], 0.0, NEG_BIG) for r in rows]

        def score_fn(item):
            j, h = item
            return _dot(lhs[j][h // rep], w_ref[h]) + bias[j]

        def vt_fn(item):
            j, h = item
            g = h // rep
            return vt_ref[0, g * HEAD_DIM:(g + 1) * HEAD_DIM, rows[j]]

        items = [(j, h) for j in range(len(chunks)) for h in range(A_HEADS)]
        _attend_chunks(items, score_fn, vt_fn, m_ref, l_ref, acc_ref, DSA_LOOKAHEAD)

    def attn_body(pair, _):
        attn_tiles([2 * pair, 2 * pair + 1])
        return 0

    lax.fori_loop(0, nch // 2, attn_body, 0)

    @pl.when(nch % 2 == 1)
    def _():
        attn_tiles([nch - 1])

    _softmax_finish(o_ref, A_HEADS, l_ref, acc_ref)


def _dsa_attn(iqt, ik4, wft, aqt, ak, posx, avt):
    bsz, w, seq = aqt.shape
    blk_t = lambda b, i: (b, 0, i)
    full = lambda b, i: (b, 0, 0)
    return pl.pallas_call(
        _dsa_kernel,
        out_shape=jax.ShapeDtypeStruct((bsz, w, seq), F32),
        grid=(bsz, seq // TQ),
        in_specs=[pl.BlockSpec((1, IDX_HEADS * IDX_K, TQ), blk_t),
                  pl.BlockSpec((1, seq, IDX_K), full),
                  pl.BlockSpec((1, 16, TQ), blk_t),
                  pl.BlockSpec((1, w, TQ), blk_t),
                  pl.BlockSpec((1, seq, A_KV_HEADS * AUG), full),
                  pl.BlockSpec((seq, AUG), lambda b, i: (0, 0)),
                  pl.BlockSpec((1, 128, seq), full)],
        out_specs=pl.BlockSpec((1, w, TQ), blk_t),
        scratch_shapes=[pltpu.VMEM((seq, TQ), F32), pltpu.VMEM((1, TQ), F32),
                        pltpu.VMEM((A_HEADS, AUG, TQ), BF16),
                        pltpu.VMEM((A_HEADS, 1, TQ), F32), pltpu.VMEM((A_HEADS, 1, TQ), F32),
                        pltpu.VMEM((A_HEADS * HEAD_DIM, TQ), F32)],
        compiler_params=pltpu.CompilerParams(dimension_semantics=("arbitrary", "arbitrary"),
                                             vmem_limit_bytes=VMEM_LIMIT),
        name="dsa_attn",
    )(iqt, ik4, wft, aqt, ak, posx, avt)


def _rms_cols(xt, g_col):
    return xt * lax.rsqrt(jnp.mean(xt * xt, axis=0, keepdims=True) + EPS) * g_col


MOE_TMG = 256
MOE_TMC = 256
ROW_TILE = 8
ROUTE_COLS = 128


def _first_index_of_max(vals, lane, big):
    m = jnp.max(vals, axis=-1, keepdims=True)
    idx = jnp.min(jnp.where(vals == m, lane, big), axis=-1, keepdims=True)
    return m, idx


def _route(h, wr_hi_ref, wr_lo_ref, br_ref):
    h_hi, h_lo = _split_bf16(h)
    logits = _dot3(h_hi, h_lo, wr_hi_ref[...], wr_lo_ref[...]) + br_ref[...]
    lane = lax.broadcasted_iota(jnp.int32, logits.shape, 1)
    ninf = -jnp.inf
    gl = jnp.where(lane < N_GROUPS, logits, ninf)
    gmax, gsel = _first_index_of_max(gl, lane, 1 << 20)
    p_group = 1.0 / jnp.sum(jnp.exp(gl - gmax), axis=-1, keepdims=True)
    base = 32 + gsel * EXPERTS_PER_GROUP
    el = jnp.where((lane >= base) & (lane < base + EXPERTS_PER_GROUP), logits, ninf)
    v1, i1 = _first_index_of_max(el, lane, 1 << 20)
    el2 = jnp.where(lane == i1, ninf, el)
    v2, i2 = _first_index_of_max(el2, lane, 1 << 20)
    e2 = jnp.exp(v2 - v1)
    w1 = 1.0 / (1.0 + e2)
    w2 = e2 / (1.0 + e2)
    return i1, i2, w1 * p_group, w2 * p_group


def _token_rows(s, n_tokens, first_token=0):
    return pl.ds(first_token * ROW_TILE + s, n_tokens, stride=ROW_TILE)


def _to_token_tiles(ref, x):
    for s in range(ROW_TILE):
        ref[_token_rows(s, x.shape[0]), :] = x[:, s * 128:(s + 1) * 128]


def _token_tile(ref, t):
    start = t * ROW_TILE if isinstance(t, int) else pl.multiple_of(t * ROW_TILE, ROW_TILE)
    return ref.at[pl.ds(start, ROW_TILE), :]


def _moe_route_kernel(oat_ref, obt_ref, x_ref, mod_ref, ga_ref, gb_ref, wo_ref,
                      gf_ref, wr_hi_ref, wr_lo_ref, br_ref,
                      x1_ref, h_ref, info_ref, infot_ref, cnt_ref, run_ref):
    @pl.when(pl.program_id(0) == 0)
    def _():
        run_ref[...] = jnp.zeros_like(run_ref)

    oa = _rms_cols(oat_ref[0], ga_ref[...]).astype(BF16)
    ob = _rms_cols(obt_ref[0], gb_ref[...]).astype(BF16)
    y = _dot_tn(oa, wo_ref[0:512, :]) + _dot_tn(ob, wo_ref[512:1024, :])
    x1 = x_ref[...] + mod_ref[0, 2:3, :] * y
    x1_ref[...] = x1

    h = _rms(x1, gf_ref[...]) * (1.0 + mod_ref[0, 4:5, :]) + mod_ref[0, 3:4, :]
    _to_token_tiles(h_ref, h)
    i1, i2, w1, w2 = _route(h, wr_hi_ref, wr_lo_ref, br_ref)
    tm = h.shape[0]
    lane = lax.broadcasted_iota(jnp.int32, (tm, ROUTE_COLS), 1)
    picked = jnp.where((lane == i1) | (lane == i2), 1.0, 0.0)
    earlier = jnp.where(lax.broadcasted_iota(jnp.int32, (tm, tm), 1)
                        < lax.broadcasted_iota(jnp.int32, (tm, tm), 0), 1.0, 0.0).astype(BF16)
    before = _dot(earlier, picked.astype(BF16)) + run_ref[...]
    rank1 = jnp.sum(jnp.where(lane == i1, before, 0.0), axis=-1, keepdims=True)
    rank2 = jnp.sum(jnp.where(lane == i2, before, 0.0), axis=-1, keepdims=True)
    run_ref[...] += jnp.sum(picked, axis=0, keepdims=True)
    cnt_ref[...] = run_ref[...]
    info = jnp.where(lane == 0, (i1 - 32).astype(F32), 0.0)
    info = jnp.where(lane == 1, (i2 - 32).astype(F32), info)
    info = jnp.where(lane == 2, rank1, info)
    info = jnp.where(lane == 3, rank2, info)
    info = jnp.where(lane == 4, w1, info)
    info = jnp.where(lane == 5, w2, info)
    info_ref[...] = info
    infot_ref[...] = info.T[0:8, :]


def _moe_route(oat, obt, x2, mod3, g_out_a, g_out_b, w_out_bf, g_ffn, wr_hi, wr_lo, b_route, seq):
    n, d = x2.shape
    tm = 512
    per_b = seq // tm
    row = lambda i: (i, 0)
    const = lambda i: (0, 0)
    blk_t = lambda i: (i // per_b, 0, i % per_b)
    return pl.pallas_call(
        _moe_route_kernel,
        out_shape=[jax.ShapeDtypeStruct((n, d), F32),
                   jax.ShapeDtypeStruct((n * ROW_TILE, d // ROW_TILE), F32),
                   jax.ShapeDtypeStruct((n, ROUTE_COLS), F32), jax.ShapeDtypeStruct((8, n), F32),
                   jax.ShapeDtypeStruct((1, ROUTE_COLS), F32)],
        grid=(n // tm,),
        in_specs=[pl.BlockSpec((1, 512, tm), blk_t), pl.BlockSpec((1, 512, tm), blk_t),
                  pl.BlockSpec((tm, d), row),
                  pl.BlockSpec((1, N_MOD, d), lambda i: (i // per_b, 0, 0)),
                  pl.BlockSpec((512, 1), const), pl.BlockSpec((512, 1), const),
                  pl.BlockSpec((d, d), const),
                  pl.BlockSpec((1, d), const),
                  pl.BlockSpec((d, ROUTE_COLS), const), pl.BlockSpec((d, ROUTE_COLS), const),
                  pl.BlockSpec((1, ROUTE_COLS), const)],
        out_specs=[pl.BlockSpec((tm, d), row),
                   pl.BlockSpec((tm * ROW_TILE, d // ROW_TILE), row),
                   pl.BlockSpec((tm, ROUTE_COLS), row), pl.BlockSpec((8, tm), lambda i: (0, i)),
                   pl.BlockSpec((1, ROUTE_COLS), const)],
        scratch_shapes=[pltpu.VMEM((1, ROUTE_COLS), F32)],
        compiler_params=pltpu.CompilerParams(dimension_semantics=("arbitrary",),
                                             vmem_limit_bytes=VMEM_LIMIT),
        name="mix_out_moe_route",
    )(oat, obt, x2, mod3, g_out_a.reshape(-1, 1), g_out_b.reshape(-1, 1), w_out_bf,
      g_ffn.reshape(1, d), wr_hi, wr_lo, b_route)


def _wait_token_copies(src_hbm, dst, sem, n_tokens):
    pltpu.make_async_copy(src_hbm.at[pl.ds(0, n_tokens * ROW_TILE), :], dst, sem).wait()


def _moe_scatter_kernel(dest_ref, pad_start_ref, pad_len_ref, nu_ref, h_ref, xs_hbm, zero_ref, sem):
    i = pl.program_id(0)
    n_pairs = 2 * MOE_TMC

    @pl.when(i == 0)
    def _():
        zero_ref[...] = jnp.zeros_like(zero_ref)

        def pad_copies(e):
            n = pad_len_ref[e]
            size = MOE_TMG // 2
            while size >= 1:
                first = pad_start_ref[e] + (n & ~(2 * size - 1))
                copy = pltpu.make_async_copy(
                    zero_ref.at[pl.ds(0, size * ROW_TILE), :],
                    xs_hbm.at[pl.ds(pl.multiple_of(first * ROW_TILE, ROW_TILE), size * ROW_TILE), :],
                    sem.at[1])
                yield (n & size) != 0, copy
                size //= 2

        for e in range(N_EXPERTS):
            for present, copy in pad_copies(e):
                pl.when(present)(copy.start)
        for e in range(N_EXPERTS):
            for present, copy in pad_copies(e):
                pl.when(present)(copy.wait)

        def row_tile(t):
            return xs_hbm.at[pl.ds(pl.multiple_of(t * (MOE_TMG * ROW_TILE), MOE_TMG * ROW_TILE),
                                   MOE_TMG * ROW_TILE), :]

        n_tiles = xs_hbm.shape[0] // (MOE_TMG * ROW_TILE)

        def fill_tile(t, _):
            pltpu.make_async_copy(zero_ref, row_tile(t), sem.at[1]).start()
            return 0

        def drain_tile(t, _):
            pltpu.make_async_copy(zero_ref, row_tile(0), sem.at[1]).wait()
            return 0

        lax.fori_loop(nu_ref[0], n_tiles, fill_tile, 0)
        lax.fori_loop(nu_ref[0], n_tiles, drain_tile, 0)

    for r in range(MOE_TMC):
        for j in range(2):
            pltpu.make_async_copy(_token_tile(h_ref, r),
                                  _token_tile(xs_hbm, dest_ref[i * n_pairs + j * MOE_TMC + r]),
                                  sem.at[0]).start(priority=j)
    _wait_token_copies(xs_hbm, xs_hbm.at[pl.ds(0, n_pairs * ROW_TILE), :], sem.at[0], n_pairs)


def _moe_scatter(dest, pad_start, pad_len, n_used, h2, n_rows):
    n = h2.shape[0] // ROW_TILE
    grid_spec = pltpu.PrefetchScalarGridSpec(
        num_scalar_prefetch=4,
        grid=(n // MOE_TMC,),
        in_specs=[pl.BlockSpec((MOE_TMC * ROW_TILE, h2.shape[1]), lambda i, *_: (i, 0))],
        out_specs=pl.BlockSpec(memory_space=pl.ANY),
        scratch_shapes=[pltpu.VMEM((MOE_TMG * ROW_TILE, h2.shape[1]), F32),
                        pltpu.SemaphoreType.DMA((2,))])
    return pl.pallas_call(
        _moe_scatter_kernel,
        out_shape=jax.ShapeDtypeStruct((n_rows * ROW_TILE, h2.shape[1]), F32),
        grid_spec=grid_spec,
        compiler_params=pltpu.CompilerParams(dimension_semantics=("arbitrary",),
                                             vmem_limit_bytes=VMEM_LIMIT),
        name="moe_scatter",
    )(dest, pad_start, pad_len, n_used, h2)


TILES_PER_STEP = 2


def _moe_expert_kernel(te_ref, nu_ref, x_ref, *refs):
    n = TILES_PER_STEP
    wg_refs, wu_refs, wd_refs = (refs[k * n:(k + 1) * n] for k in range(3))
    y_ref, xs_ref, wgb_ref, wub_ref, wdb_ref = refs[3 * n:]
    del nu_ref
    i = pl.program_id(0)

    for slot in range(n):
        tile = n * i + slot
        new_expert = jnp.logical_or(i == 0, te_ref[tile] != te_ref[jnp.maximum(tile - n, 0)])

        @pl.when(new_expert)
        def _(slot=slot):
            wgb_ref[slot] = wg_refs[slot][0].astype(BF16)
            wub_ref[slot] = wu_refs[slot][0].astype(BF16)
            wdb_ref[slot] = wd_refs[slot][0].astype(BF16)

    for slot in range(n):
        for s in range(ROW_TILE):
            rows = _token_rows(s, MOE_TMG, first_token=slot * MOE_TMG)
            xs_ref[slot, :, s * 128:(s + 1) * 128] = x_ref[rows, :].astype(BF16)
    acts = []
    for slot in range(n):
        x = xs_ref[slot]
        hg = _dot(x, wgb_ref[slot])
        hu = _dot(x, wub_ref[slot])
        acts.append((hg * jax.nn.sigmoid(hg) * hu).astype(BF16))
    for slot in range(n):
        y = _dot(acts[slot], wdb_ref[slot])
        for s in range(ROW_TILE):
            y_ref[_token_rows(s, MOE_TMG, first_token=slot * MOE_TMG), :] = y[:, s * 128:(s + 1) * 128]


def _moe_experts(tile_expert, n_used, xsorted, wg, wu, wd):
    n_rows = xsorted.shape[0] // ROW_TILE
    d, ff = wg.shape[1], wg.shape[2]
    n = TILES_PER_STEP
    block = (n * MOE_TMG * ROW_TILE, xsorted.shape[1])
    slots = range(n)
    w_in_specs = [pl.BlockSpec((1, d, ff), lambda i, te, nu, s=s: (te[n * i + s], 0, 0)) for s in slots]
    w_out_specs = [pl.BlockSpec((1, ff, d), lambda i, te, nu, s=s: (te[n * i + s], 0, 0)) for s in slots]
    grid_spec = pltpu.PrefetchScalarGridSpec(
        num_scalar_prefetch=2,
        grid=(n_rows // (MOE_TMG * n),),
        in_specs=[pl.BlockSpec(block, lambda i, te, nu: (jnp.minimum(i, (nu[0] - 1) // n), 0))]
        + w_in_specs + w_in_specs + w_out_specs,
        out_specs=pl.BlockSpec(block, lambda i, te, nu: (i, 0)),
        scratch_shapes=[pltpu.VMEM((n, MOE_TMG, d), BF16), pltpu.VMEM((n, d, ff), BF16),
                        pltpu.VMEM((n, d, ff), BF16), pltpu.VMEM((n, ff, d), BF16)])
    return pl.pallas_call(
        _moe_expert_kernel,
        out_shape=jax.ShapeDtypeStruct(xsorted.shape, F32),
        grid_spec=grid_spec,
        compiler_params=pltpu.CompilerParams(dimension_semantics=("arbitrary",),
                                             vmem_limit_bytes=VMEM_LIMIT),
        name="moe_experts",
    )(tile_expert, n_used, xsorted, *([wg] * n), *([wu] * n), *([wd] * n))


def _moe_combine_kernel(dest_ref, y_hbm, x_ref, info_ref, mod_ref, gfin_ref, o_ref, ybuf, x2_ref, sem):
    i = pl.program_id(0)
    nt = pl.num_programs(0)
    slot = lax.rem(i, 2)
    n_pairs = 2 * MOE_TMC

    def start_gather(tile, to_slot):
        for r in range(n_pairs):
            pltpu.make_async_copy(_token_tile(y_hbm, dest_ref[tile * n_pairs + r]),
                                  _token_tile(ybuf.at[to_slot], r), sem.at[to_slot]).start(priority=r % 2)

    @pl.when(i == 0)
    def _():
        start_gather(0, 0)

    _wait_token_copies(y_hbm, ybuf.at[slot], sem.at[slot], n_pairs)

    @pl.when(i + 1 < nt)
    def _():
        start_gather(i + 1, 1 - slot)

    w1 = info_ref[:, 4:5]
    w2 = info_ref[:, 5:6]
    sumsq = jnp.zeros((MOE_TMC, 1), F32)
    for s in range(ROW_TILE):
        cols = slice(s * 128, (s + 1) * 128)
        y = (w1 * ybuf[slot, _token_rows(s, MOE_TMC), :]
             + w2 * ybuf[slot, _token_rows(s, MOE_TMC, first_token=MOE_TMC), :])
        x2 = x_ref[:, cols] + mod_ref[0, 5:6, cols] * y
        x2_ref[:, cols] = x2
        sumsq = sumsq + jnp.sum(x2 * x2, axis=-1, keepdims=True)
    d = x2_ref.shape[1]
    o_ref[...] = x2_ref[...] * lax.rsqrt(sumsq / d + EPS) * gfin_ref[...]


def _moe_combine(dest, ysorted, x1, info, mod3, g_final, seq):
    n, d = x1.shape
    per_b = seq // MOE_TMC
    grid_spec = pltpu.PrefetchScalarGridSpec(
        num_scalar_prefetch=1,
        grid=(n // MOE_TMC,),
        in_specs=[pl.BlockSpec(memory_space=pl.ANY),
                  pl.BlockSpec((MOE_TMC, d), lambda i, ds: (i, 0)),
                  pl.BlockSpec((MOE_TMC, ROUTE_COLS), lambda i, ds: (i, 0)),
                  pl.BlockSpec((1, N_MOD, d), lambda i, ds: (i // per_b, 0, 0)),
                  pl.BlockSpec((1, d), lambda i, ds: (0, 0))],
        out_specs=pl.BlockSpec((MOE_TMC, d), lambda i, ds: (i, 0)),
        scratch_shapes=[pltpu.VMEM((2, 2 * MOE_TMC * ROW_TILE, ysorted.shape[1]), F32),
                        pltpu.VMEM((MOE_TMC, d), F32), pltpu.SemaphoreType.DMA((2,))])
    return pl.pallas_call(
        _moe_combine_kernel,
        out_shape=jax.ShapeDtypeStruct((n, d), F32),
        grid_spec=grid_spec,
        compiler_params=pltpu.CompilerParams(dimension_semantics=("arbitrary",),
                                             vmem_limit_bytes=VMEM_LIMIT),
        name="moe_combine",
    )(dest, ysorted, x1, info, mod3, g_final.reshape(1, d))


def _mix_out_and_moe(oat, obt, x2, mod3, g_out_a, g_out_b, w_out_bf, g_ffn, wr_hi, wr_lo, b_route,
                     wg, wu, wd, g_final, seq):
    n, d = x2.shape
    x1, h2, info, infot, counts = _moe_route(oat, obt, x2, mod3, g_out_a, g_out_b, w_out_bf,
                                             g_ffn, wr_hi, wr_lo, b_route, seq)

    e1, e2, rank1, rank2 = [infot[k].astype(jnp.int32) for k in range(4)]
    cnt = counts[0, 32:32 + N_EXPERTS].astype(jnp.int32)
    padded = ((cnt + MOE_TMG - 1) // MOE_TMG) * MOE_TMG
    seg_end = jnp.cumsum(padded)
    expert_ids = jnp.arange(N_EXPERTS, dtype=jnp.int32)

    def seg_start_of(e):
        return jnp.sum(jnp.where(expert_ids[None, :] < e[:, None], padded[None, :], 0), axis=1)

    dest1 = seg_start_of(e1) + rank1
    dest2 = seg_start_of(e2) + rank2
    n_rows = 2 * n + N_EXPERTS * MOE_TMG
    tile_start = jnp.arange(n_rows // MOE_TMG, dtype=jnp.int32) * MOE_TMG
    tile_expert = jnp.minimum(jnp.sum((tile_start[:, None] >= seg_end[None, :]).astype(jnp.int32), axis=1),
                              N_EXPERTS - 1)
    n_used = (seg_end[N_EXPERTS - 1:] // MOE_TMG).astype(jnp.int32)
    dest = jnp.concatenate([dest1.reshape(-1, MOE_TMC), dest2.reshape(-1, MOE_TMC)], axis=1).reshape(-1)

    seg_start = seg_end - padded
    xsorted = _moe_scatter(dest, seg_start + cnt, padded - cnt, n_used, h2, n_rows)
    ysorted = _moe_experts(tile_expert, n_used, xsorted, wg, wu, wd)
    return _moe_combine(dest, ysorted, x1, info, mod3, g_final, seq)


def _layer(x3, c, w_ada, b_ada, g_mix, w_in, b_forget, g_out_a, g_out_b, w_out,
           g_ffn, w_group, b_group, w_router, b_router, w_gate, w_up, w_down, g_final):
    bsz, seq, d = x3.shape
    mod3 = _ada_mod(c, w_ada, b_ada).reshape(bsz, N_MOD, d)

    w_t = w_in.T

    def pad_heads(w, n_heads):
        w = w.reshape(d, n_heads, HEAD_DIM)
        return jnp.concatenate([w, jnp.zeros_like(w)], axis=-1).reshape(d, n_heads * AUG)

    w_wf_t = jnp.concatenate([w_t[1344:1352], w_t[2888:2896]], axis=0)
    weights = [w_t[0:512].astype(BF16), pad_heads(w_in[:, 512:640], A_KV_HEADS).astype(BF16),
               w_t[640:768].astype(BF16), w_t[1352:1864].astype(BF16),
               w_in[:, 1864:2376].astype(BF16), w_t[2376:2888].astype(BF16),
               w_t[768:1280].astype(BF16), w_in[:, 1280:1344].astype(BF16), *_split_bf16(w_wf_t)]
    aqt, ak, avt, bqt, bk, bvt, iqt, ik4, wft = _in_proj(x3, mod3, g_mix, weights)

    cumt, kaug = _fox_cum(wft, b_forget, bk)
    obt = _fox_attn(bqt, kaug, bvt, cumt)

    pos = jnp.arange(seq, dtype=jnp.int32)[:, None]
    lane = jnp.arange(AUG, dtype=jnp.int32)[None, :] - HEAD_DIM
    posx = jnp.where((lane >= 0) & (lane < 3), pos >> 7,
                     jnp.where((lane >= 3) & (lane < 6), pos & 127,
                               jnp.where((lane >= 6) & (lane < 9), 1, 0))).astype(BF16)
    oat = _dsa_attn(iqt, ik4, wft, aqt, ak, posx, avt)

    w_r = jnp.concatenate([w_group, jnp.zeros((d, 32 - N_GROUPS), F32),
                           jnp.transpose(w_router, (1, 0, 2)).reshape(d, N_EXPERTS),
                           jnp.zeros((d, ROUTE_COLS - 64), F32)], axis=1)
    b_r = jnp.concatenate([b_group, jnp.zeros((32 - N_GROUPS,), F32), b_router.reshape(-1),
                           jnp.zeros((ROUTE_COLS - 64,), F32)]).reshape(1, ROUTE_COLS)
    wr_hi, wr_lo = _split_bf16(w_r)
    out = _mix_out_and_moe(oat, obt, x3.reshape(bsz * seq, d), mod3, g_out_a, g_out_b,
                           w_out.astype(BF16), g_ffn, wr_hi, wr_lo, b_r, w_gate, w_up, w_down,
                           g_final, seq)
    return out.reshape(bsz, seq, d)


def kernel(x, c, w_ada, b_ada, g_mix, w_in, b_forget, g_out_a, g_out_b, w_out, g_ffn, w_group,
           b_group, w_router, b_router, w_gate, w_up, w_down, g_final):
    depth = w_ada.shape[0]
    assert depth == 1, "final norm is fused into the single layer's MoE kernel"
    return _layer(x, c, w_ada[0], b_ada[0], g_mix[0], w_in[0], b_forget[0], g_out_a[0], g_out_b[0],
                  w_out[0], g_ffn[0], w_group[0], b_group[0], w_router[0], b_router[0], w_gate[0],
                  w_up[0], w_down[0], g_final)
```

```python
import math

import jax
import jax.numpy as jnp
import numpy as np
from jax import lax
from jax.experimental import pallas as pl
from jax.experimental.pallas import tpu as pltpu

F32 = jnp.float32
BF16 = jnp.bfloat16

EPS = 1e-6
A_HEADS = 8
A_KV_HEADS = 2
HEAD_DIM = 64
IDX_HEADS = 8
IDX_DIM = 64
TOPK = 256
B_HEADS = 8
N_GROUPS = 4
EXPERTS_PER_GROUP = 8
N_EXPERTS = N_GROUPS * EXPERTS_PER_GROUP
N_MOD = 6

NEG_BIG = -1e30
LOWEST = float(np.finfo(np.float32).min)
LOG2E = math.log2(math.e)
Q_SCALE = HEAD_DIM ** -0.5 * LOG2E
VMEM_LIMIT = 48 * 1024 * 1024

TQ = 256
KC = 256
AUG = 128
IDX_K = IDX_DIM
ONES_ROWS = 16
COARSE_STEPS = 14
FOX_LOOKAHEAD = 6
DSA_LOOKAHEAD = 4
assert TQ == KC == TOPK


def _split_bf16(x):
    hi = x.astype(BF16)
    lo = (x - hi.astype(F32)).astype(BF16)
    return hi, lo


def _split3_f32(x):
    p1 = x.astype(BF16).astype(F32)
    r1 = x - p1
    p2 = r1.astype(BF16).astype(F32)
    p3 = (r1 - p2).astype(BF16).astype(F32)
    return p1, p2, p3


def _dot(a, b):
    return jnp.dot(a, b, preferred_element_type=F32)


def _dot_nt(a, b):
    return lax.dot_general(a, b, (((1,), (1,)), ((), ())), preferred_element_type=F32)


def _dot_tn(a, b):
    return lax.dot_general(a, b, (((0,), (0,)), ((), ())), preferred_element_type=F32)


def _dot3(a_hi, a_lo, b_hi, b_lo):
    return _dot(a_hi, b_hi) + _dot(a_lo, b_hi) + _dot(a_hi, b_lo)


def _dot3_nt(a_hi, a_lo, b_hi, b_lo):
    return _dot_nt(a_hi, b_hi) + _dot_nt(a_lo, b_hi) + _dot_nt(a_hi, b_lo)


def _rms(x, g):
    return x * lax.rsqrt(jnp.mean(x * x, axis=-1, keepdims=True) + EPS) * g


def _chunk(kc):
    return pl.ds(pl.multiple_of(kc * KC, KC), KC)


def _key_minus_query():
    return (lax.broadcasted_iota(jnp.int32, (KC, TQ), 0)
            - lax.broadcasted_iota(jnp.int32, (KC, TQ), 1))


def _ada_kernel(c_ref, w_ref, b_ref, o_ref):
    c = c_ref[...]
    s = c * jax.nn.sigmoid(c)
    s_hi, s_lo = _split_bf16(s)
    w_hi, w_lo = _split_bf16(w_ref[...])
    o_ref[...] = _dot3(s_hi, s_lo, w_hi, w_lo) + b_ref[...]


def _ada_mod(c, w_ada, b_ada):
    bsz, d = c.shape
    n = w_ada.shape[1]
    tn = 1024
    return pl.pallas_call(
        _ada_kernel,
        out_shape=jax.ShapeDtypeStruct((bsz, n), F32),
        grid=(n // tn,),
        in_specs=[pl.BlockSpec((bsz, d), lambda j: (0, 0)),
                  pl.BlockSpec((d, tn), lambda j: (0, j)),
                  pl.BlockSpec((1, tn), lambda j: (0, j))],
        out_specs=pl.BlockSpec((bsz, tn), lambda j: (0, j)),
        compiler_params=pltpu.CompilerParams(dimension_semantics=("arbitrary",),
                                             vmem_limit_bytes=VMEM_LIMIT),
        name="ada_mod",
    )(c, w_ada, b_ada.reshape(1, n))


def _in_proj_kernel(x_ref, mod_ref, g_ref,
                    waq_ref, wak_ref, wav_ref, wbq_ref, wbk_ref, wbv_ref,
                    wiq_ref, wik_ref, wwfh_ref, wwfl_ref,
                    aqt_ref, ak_ref, avt_ref, bqt_ref, bk_ref, bvt_ref, iqt_ref, ik_ref, wft_ref):
    x = x_ref[0]
    h = _rms(x, g_ref[...]) * (1.0 + mod_ref[0, 1:2, :]) + mod_ref[0, 0:1, :]
    h_hi, h_lo = _split_bf16(h)
    aqt_ref[0] = (_dot_nt(waq_ref[...], h_hi) * Q_SCALE).astype(BF16)
    ak_ref[0] = _dot(h_hi, wak_ref[...]).astype(BF16)
    avt_ref[0] = _dot_nt(wav_ref[...], h_hi).astype(BF16)
    bqt_ref[0] = (_dot_nt(wbq_ref[...], h_hi) * Q_SCALE).astype(BF16)
    bk_ref[0] = _dot(h_hi, wbk_ref[...]).astype(BF16)
    bvt_ref[0] = _dot_nt(wbv_ref[...], h_hi).astype(BF16)
    iqt_ref[0] = _dot_nt(wiq_ref[...], h_hi).astype(BF16)
    ik_ref[0] = _dot(h_hi, wik_ref[...]).astype(BF16)
    wft_ref[0] = _dot3_nt(wwfh_ref[...], wwfl_ref[...], h_hi, h_lo)


def _in_proj(x3, mod3, g_mix, weights):
    bsz, seq, d = x3.shape
    tm = 512
    blk_t = lambda b, i: (b, 0, i)
    blk_r = lambda b, i: (b, i, 0)
    const = lambda b, i: (0, 0)
    ak_w = A_KV_HEADS * AUG
    bk_w = B_HEADS * HEAD_DIM
    outs = [jax.ShapeDtypeStruct((bsz, 512, seq), BF16), jax.ShapeDtypeStruct((bsz, seq, ak_w), BF16),
            jax.ShapeDtypeStruct((bsz, 128, seq), BF16), jax.ShapeDtypeStruct((bsz, 512, seq), BF16),
            jax.ShapeDtypeStruct((bsz, seq, bk_w), BF16), jax.ShapeDtypeStruct((bsz, 512, seq), BF16),
            jax.ShapeDtypeStruct((bsz, IDX_HEADS * IDX_K, seq), BF16),
            jax.ShapeDtypeStruct((bsz, seq, IDX_K), BF16), jax.ShapeDtypeStruct((bsz, 16, seq), F32)]
    out_specs = [pl.BlockSpec((1, 512, tm), blk_t), pl.BlockSpec((1, tm, ak_w), blk_r),
                 pl.BlockSpec((1, 128, tm), blk_t), pl.BlockSpec((1, 512, tm), blk_t),
                 pl.BlockSpec((1, tm, bk_w), blk_r), pl.BlockSpec((1, 512, tm), blk_t),
                 pl.BlockSpec((1, IDX_HEADS * IDX_K, tm), blk_t),
                 pl.BlockSpec((1, tm, IDX_K), blk_r), pl.BlockSpec((1, 16, tm), blk_t)]
    return pl.pallas_call(
        _in_proj_kernel,
        out_shape=outs,
        grid=(bsz, seq // tm),
        in_specs=[pl.BlockSpec((1, tm, d), blk_r),
                  pl.BlockSpec((1, N_MOD, d), lambda b, i: (b, 0, 0)),
                  pl.BlockSpec((1, d), const)] + [pl.BlockSpec(w.shape, const) for w in weights],
        out_specs=out_specs,
        compiler_params=pltpu.CompilerParams(dimension_semantics=("arbitrary", "arbitrary"),
                                             vmem_limit_bytes=VMEM_LIMIT),
        name="in_proj",
    )(x3, mod3, g_mix.reshape(1, d), *weights)


CB = 256


def _cum_kernel(wft_ref, bfor_ref, k_ref, cumt_ref, kaug_ref):
    seq = wft_ref.shape[2]
    r = lax.broadcasted_iota(jnp.int32, (CB, CB), 0)
    cidx = lax.broadcasted_iota(jnp.int32, (CB, CB), 1)
    tri = jnp.where(r <= cidx, 1.0, 0.0).astype(BF16)
    row128 = lax.broadcasted_iota(jnp.int32, (AUG, CB), 0)
    ones_rows = jnp.where((row128 >= HEAD_DIM + 3) & (row128 < HEAD_DIM + 6), 1.0, 0.0)
    src = lax.broadcasted_iota(jnp.int32, (AUG, AUG), 0)
    dst = lax.broadcasted_iota(jnp.int32, (AUG, AUG), 1)
    place = [jnp.where((dst < HEAD_DIM) & (src == dst + half * HEAD_DIM), 1.0, 0.0).astype(BF16)
             for half in range(2)]
    carry = jnp.zeros((8, 1), F32)
    for blk in range(seq // CB):
        cols = slice(blk * CB, (blk + 1) * CB)
        z = wft_ref[0, 8:16, cols] + bfor_ref[...]
        logf = jnp.minimum(z, 0.0) - jnp.log(1.0 + jnp.exp(-jnp.abs(z)))
        p1, p2, p3 = _split3_f32(logf)
        pieces = jnp.concatenate([p1, p2, p3, jnp.zeros_like(p1)], axis=0).astype(BF16)
        parts = _dot(pieces, tri)
        cum = parts[0:8] + parts[8:16] + parts[16:24] + carry
        carry = cum[:, CB - 1:CB]
        cum2 = cum * LOG2E
        cumt_ref[0, :, cols] = cum2
        c1, c2, c3 = _split3_f32(cum2)
        for h in range(B_HEADS):
            spare = jnp.where(row128 == HEAD_DIM, -c1[h:h + 1], ones_rows)
            spare = jnp.where(row128 == HEAD_DIM + 1, -c2[h:h + 1], spare)
            spare = jnp.where(row128 == HEAD_DIM + 2, -c3[h:h + 1], spare)
            pair = k_ref[0, cols, (h // 2) * AUG:(h // 2 + 1) * AUG]
            k_h = _dot(pair, place[h % 2]).astype(BF16)
            kaug_ref[0, cols, h * AUG:(h + 1) * AUG] = k_h + spare.T.astype(BF16)


def _fox_cum(wft, b_forget, bk):
    bsz, _, seq = wft.shape
    nh = B_HEADS
    kw = bk.shape[-1]
    kaug_w = nh * AUG
    return pl.pallas_call(
        _cum_kernel,
        out_shape=[jax.ShapeDtypeStruct((bsz, nh, seq), F32),
                   jax.ShapeDtypeStruct((bsz, seq, kaug_w), BF16)],
        grid=(bsz,),
        in_specs=[pl.BlockSpec((1, 16, seq), lambda b: (b, 0, 0)),
                  pl.BlockSpec((nh, 1), lambda b: (0, 0)),
                  pl.BlockSpec((1, seq, kw), lambda b: (b, 0, 0))],
        out_specs=[pl.BlockSpec((1, nh, seq), lambda b: (b, 0, 0)),
                   pl.BlockSpec((1, seq, kaug_w), lambda b: (b, 0, 0))],
        compiler_params=pltpu.CompilerParams(dimension_semantics=("arbitrary",),
                                             vmem_limit_bytes=VMEM_LIMIT),
        name="fox_cum",
    )(wft, b_forget.reshape(nh, 1), bk)


def _softmax_init(m_ref, l_ref, acc_ref):
    m_ref[...] = jnp.full(m_ref.shape, NEG_BIG, F32)
    l_ref[...] = jnp.zeros(l_ref.shape, F32)
    acc_ref[...] = jnp.zeros(acc_ref.shape, F32)


def _attend_chunks(items, score_fn, vt_fn, m_ref, l_ref, acc_ref, lookahead):
    ahead = min(lookahead, len(items))
    scores = {i: score_fn(items[i]) for i in range(ahead)}
    ones = jnp.ones((ONES_ROWS, KC), BF16)
    for i, item in enumerate(items):
        if i + ahead < len(items):
            scores[i + ahead] = score_fn(items[i + ahead])
        s = scores.pop(i)
        h = item[1]
        m_old = m_ref[h]
        m_new = jnp.maximum(m_old, jnp.max(s, axis=0, keepdims=True))
        alpha = jnp.exp2(m_old - m_new)
        p = jnp.exp2(s - m_new).astype(BF16)
        pv = _dot(jnp.concatenate([vt_fn(item), ones], axis=0), p)
        m_ref[h] = m_new
        l_ref[h] = alpha * l_ref[h] + pv[HEAD_DIM:HEAD_DIM + 1, :]
        rows = slice(h * HEAD_DIM, (h + 1) * HEAD_DIM)
        acc_ref[rows, :] = alpha * acc_ref[rows, :] + pv[0:HEAD_DIM, :]


def _softmax_finish(o_ref, n_heads, l_ref, acc_ref):
    for h in range(n_heads):
        rows = slice(h * HEAD_DIM, (h + 1) * HEAD_DIM)
        o_ref[0, rows, :] = acc_ref[rows, :] / l_ref[h]


def _fox_kernel(qt_ref, k_ref, vt_ref, cumt_ref, o_ref, w_ref, m_ref, l_ref, acc_ref):
    qi = pl.program_id(1)
    _softmax_init(m_ref, l_ref, acc_ref)

    row64 = lax.broadcasted_iota(jnp.int32, (AUG - HEAD_DIM, TQ), 0)
    for h in range(B_HEADS):
        c1, c2, c3 = _split3_f32(cumt_ref[0, h:h + 1, :])
        spare = jnp.where(row64 < 3, 1.0, 0.0)
        spare = jnp.where(row64 == 3, c1, spare)
        spare = jnp.where(row64 == 4, c2, spare)
        spare = jnp.where(row64 == 5, c3, spare)
        w_ref[h] = jnp.concatenate([qt_ref[0, h * HEAD_DIM:(h + 1) * HEAD_DIM, :],
                                    spare.astype(BF16)], axis=0)

    def tiles(chunks):
        rows = [_chunk(kc) for kc, _ in chunks]

        def score_fn(item):
            j, h = item
            s = _dot(k_ref[0, rows[j], h * AUG:(h + 1) * AUG], w_ref[h])
            if chunks[j][1]:
                s = jnp.where(_key_minus_query() <= 0, s, NEG_BIG)
            return s

        def vt_fn(item):
            j, h = item
            return vt_ref[0, h * HEAD_DIM:(h + 1) * HEAD_DIM, rows[j]]

        items = [(j, h) for j in range(len(chunks)) for h in range(B_HEADS)]
        _attend_chunks(items, score_fn, vt_fn, m_ref, l_ref, acc_ref, FOX_LOOKAHEAD)

    def body(pair, _):
        tiles([(2 * pair, False), (2 * pair + 1, False)])
        return 0

    lax.fori_loop(0, qi // 2, body, 0)

    @pl.when(qi % 2 == 1)
    def _():
        tiles([(qi - 1, False), (qi, True)])

    @pl.when(qi % 2 == 0)
    def _():
        tiles([(qi, True)])

    _softmax_finish(o_ref, B_HEADS, l_ref, acc_ref)


def _fox_attn(bqt, kaug, bvt, cumt):
    bsz, w, seq = bqt.shape
    blk_t = lambda b, i: (b, 0, i)
    full = lambda b, i: (b, 0, 0)
    return pl.pallas_call(
        _fox_kernel,
        out_shape=jax.ShapeDtypeStruct((bsz, w, seq), F32),
        grid=(bsz, seq // TQ),
        in_specs=[pl.BlockSpec((1, w, TQ), blk_t),
                  pl.BlockSpec((1, seq, B_HEADS * AUG), full),
                  pl.BlockSpec((1, w, seq), full),
                  pl.BlockSpec((1, B_HEADS, TQ), blk_t)],
        out_specs=pl.BlockSpec((1, w, TQ), blk_t),
        scratch_shapes=[pltpu.VMEM((B_HEADS, AUG, TQ), BF16), pltpu.VMEM((B_HEADS, 1, TQ), F32),
                        pltpu.VMEM((B_HEADS, 1, TQ), F32), pltpu.VMEM((B_HEADS * HEAD_DIM, TQ), F32)],
        compiler_params=pltpu.CompilerParams(dimension_semantics=("arbitrary", "arbitrary"),
                                             vmem_limit_bytes=VMEM_LIMIT),
        name="fox_attn",
    )(bqt, kaug, bvt, cumt)


def _bf16_pieces(value):
    pieces = []
    rest = np.float32(value)
    for _ in range(3):
        piece = np.asarray(rest).astype(BF16).astype(np.float32)
        pieces.append(float(piece))
        rest = np.float32(rest - piece)
    return pieces


def _dsa_kernel(iqt_ref, ik_ref, wft_ref, qt_ref, k_ref, posx_ref, vt_ref, o_ref,
                s_ref, thr_ref, lohi_ref, w_ref, m_ref, l_ref, acc_ref):
    qi = pl.program_id(1)
    nch = qi + 1
    kmq = _key_minus_query()

    def part(x, op):
        return op(x.reshape(KC // 8, 8, TQ), axis=0)

    def fold(x, op):
        return op(x, axis=0, keepdims=True)

    lohi_ref[0] = jnp.full((8, TQ), jnp.inf, F32)
    lohi_ref[1] = jnp.full((8, TQ), -jnp.inf, F32)

    def score_chunk(kc):
        ik = ik_ref[0, _chunk(kc), :]
        acc = jnp.zeros((KC, TQ), F32)
        for h in range(IDX_HEADS):
            d = _dot(ik, iqt_ref[0, h * IDX_K:(h + 1) * IDX_K, :])
            acc = acc + wft_ref[0, h:h + 1, :] * jnp.maximum(d, 0.0)
        causal = kmq <= (qi - kc) * KC
        s_ref[_chunk(kc), :] = jnp.where(causal, acc, -jnp.inf)
        lohi_ref[0] = jnp.minimum(lohi_ref[0], part(jnp.where(causal, acc, jnp.inf), jnp.min))
        lohi_ref[1] = jnp.maximum(lohi_ref[1], part(jnp.where(causal, acc, -jnp.inf), jnp.max))

    def score_body(pair, _):
        score_chunk(2 * pair)
        score_chunk(2 * pair + 1)
        return 0

    lax.fori_loop(0, nch // 2, score_body, 0)

    @pl.when(nch % 2 == 1)
    def _():
        score_chunk(nch - 1)
        s_ref[_chunk(nch), :] = jnp.full((KC, TQ), -jnp.inf, F32)

    def scan(fn, init):
        def pair_body(pair, c):
            c = fn(2 * pair, s_ref[_chunk(2 * pair), :], c)
            return fn(2 * pair + 1, s_ref[_chunk(2 * pair + 1), :], c)

        return lax.fori_loop(0, (nch + 1) // 2, pair_body, init)

    @pl.when(qi == 0)
    def _():
        thr_ref[...] = jnp.full(thr_ref.shape, LOWEST, F32)

    @pl.when(qi > 0)
    def _():
        def cmin(x):
            return part(x, jnp.min)

        def cmax(x):
            return part(x, jnp.max)

        def csum(x):
            return part(x, jnp.sum)

        zeros = jnp.zeros((8, TQ), F32)
        pinf = jnp.full((8, TQ), jnp.inf, F32)
        ninf = jnp.full((8, TQ), -jnp.inf, F32)
        lo, hi = fold(lohi_ref[0], jnp.min), fold(lohi_ref[1], jnp.max)

        def coarse_step(_, carry):
            lo, hi = carry
            mid = lo + (hi - lo) * 0.5
            cnt = scan(lambda _, s, c: c + csum(jnp.where(s >= mid, 1.0, 0.0)), zeros)
            enough = fold(cnt, jnp.sum) >= float(TOPK)
            return jnp.where(enough, mid, lo), jnp.where(enough, hi, mid)

        lo, hi = lax.fori_loop(0, COARSE_STEPS, coarse_step, (lo, hi))

        def cond(carry):
            return carry[2] > 0

        def step(carry):
            lo, hi, _ = carry
            mid = lo + (hi - lo) * 0.5
            mid = jnp.where(mid <= lo, hi, mid)

            def fn(_, s, c):
                cnt, a, b = c
                ge = s >= mid
                cnt = cnt + csum(jnp.where(ge, 1.0, 0.0))
                b = jnp.minimum(b, cmin(jnp.where(ge, s, jnp.inf)))
                a = jnp.maximum(a, cmax(jnp.where(ge, -jnp.inf, s)))
                return cnt, a, b

            cnt, a, b = scan(fn, (zeros, ninf, pinf))
            cnt, a, b = fold(cnt, jnp.sum), fold(a, jnp.max), fold(b, jnp.min)
            enough = cnt >= float(TOPK)
            new_lo = jnp.where(enough, b, jnp.where(cnt == float(TOPK - 1), a, lo))
            new_hi = jnp.where(enough, jnp.where(cnt == float(TOPK), b, hi), a)
            active = jnp.max(jnp.where(new_lo < new_hi, 1, 0))
            return new_lo, new_hi, active

        first_active = jnp.max(jnp.where(lo < hi, 1, 0))
        thr, _, _ = lax.while_loop(cond, step, (lo, hi, first_active))

        def count_fn(_, s, c):
            n_gt, n_ge = c
            return (n_gt + csum(jnp.where(s > thr, 1.0, 0.0)),
                    n_ge + csum(jnp.where(s >= thr, 1.0, 0.0)))

        n_gt, n_ge = scan(count_fn, (zeros, zeros))
        n_gt, n_ge = fold(n_gt, jnp.sum), fold(n_ge, jnp.sum)
        thr_ref[...] = thr

        @pl.when(jnp.max(jnp.where(n_ge > float(TOPK), 1, 0)) > 0)
        def _():
            need = float(TOPK) - n_gt
            lower = jnp.where(lax.broadcasted_iota(jnp.int32, (KC, KC), 1)
                              < lax.broadcasted_iota(jnp.int32, (KC, KC), 0), 1.0, 0.0).astype(BF16)

            def sel_fn(kc, s, run):
                eq = s == thr
                eqf = jnp.where(eq, 1.0, 0.0)
                before = _dot(lower, eqf.astype(BF16)) + run
                sel = (s > thr) | (eq & (before < need))
                s_ref[_chunk(kc), :] = jnp.where(sel, 0.0, NEG_BIG)
                return run + jnp.sum(eqf, axis=0, keepdims=True)

            scan(sel_fn, jnp.zeros((1, TQ), F32))
            thr_ref[...] = jnp.full(thr_ref.shape, 0.5 * NEG_BIG, F32)

    _softmax_init(m_ref, l_ref, acc_ref)
    rep = A_HEADS // A_KV_HEADS
    row64 = lax.broadcasted_iota(jnp.int32, (AUG - HEAD_DIM, TQ), 0)
    qpos = (qi * TQ + lax.broadcasted_iota(jnp.int32, (1, TQ), 1)).astype(F32)
    for h in range(A_HEADS):
        slope = np.float32(2.0 ** (-8.0 * (h + 1) / A_HEADS) * LOG2E)
        u1, u2, u3 = _split3_f32(-slope * qpos)
        spare = jnp.zeros((AUG - HEAD_DIM, TQ), F32)
        for p, s_p in enumerate(_bf16_pieces(slope)):
            spare = jnp.where(row64 == p, 128.0 * s_p, spare)
            spare = jnp.where(row64 == 3 + p, s_p, spare)
        spare = jnp.where(row64 == 6, u1, spare)
        spare = jnp.where(row64 == 7, u2, spare)
        spare = jnp.where(row64 == 8, u3, spare)
        w_ref[h] = jnp.concatenate([qt_ref[0, h * HEAD_DIM:(h + 1) * HEAD_DIM, :],
                                    spare.astype(BF16)], axis=0)

    def attn_tiles(chunks):
        rows = [_chunk(kc) for kc in chunks]
        lhs = [[k_ref[0, r, g * AUG:(g + 1) * AUG] + posx_ref[r, :] for g in range(A_KV_HEADS)]
               for r in rows]
        bias = [jnp.where(s_ref[r, :] >= thr_ref[...], 0.0, NEG_BIG) for r in rows]

        def score_fn(item):
            j, h = item
            return _dot(lhs[j][h // rep], w_ref[h]) + bias[j]

        def vt_fn(item):
            j, h = item
            g = h // rep
            return vt_ref[0, g * HEAD_DIM:(g + 1) * HEAD_DIM, rows[j]]

        items = [(j, h) for j in range(len(chunks)) for h in range(A_HEADS)]
        _attend_chunks(items, score_fn, vt_fn, m_ref, l_ref, acc_ref, DSA_LOOKAHEAD)

    def attn_body(pair, _):
        attn_tiles([2 * pair, 2 * pair + 1])
        return 0

    lax.fori_loop(0, nch // 2, attn_body, 0)

    @pl.when(nch % 2 == 1)
    def _():
        attn_tiles([nch - 1])

    _softmax_finish(o_ref, A_HEADS, l_ref, acc_ref)


def _dsa_attn(iqt, ik4, wft, aqt, ak, posx, avt):
    bsz, w, seq = aqt.shape
    blk_t = lambda b, i: (b, 0, i)
    full = lambda b, i: (b, 0, 0)
    return pl.pallas_call(
        _dsa_kernel,
        out_shape=jax.ShapeDtypeStruct((bsz, w, seq), F32),
        grid=(bsz, seq // TQ),
        in_specs=[pl.BlockSpec((1, IDX_HEADS * IDX_K, TQ), blk_t),
                  pl.BlockSpec((1, seq, IDX_K), full),
                  pl.BlockSpec((1, 16, TQ), blk_t),
                  pl.BlockSpec((1, w, TQ), blk_t),
                  pl.BlockSpec((1, seq, A_KV_HEADS * AUG), full),
                  pl.BlockSpec((seq, AUG), lambda b, i: (0, 0)),
                  pl.BlockSpec((1, 128, seq), full)],
        out_specs=pl.BlockSpec((1, w, TQ), blk_t),
        scratch_shapes=[pltpu.VMEM((seq, TQ), F32), pltpu.VMEM((1, TQ), F32),
                        pltpu.VMEM((2, 8, TQ), F32), pltpu.VMEM((A_HEADS, AUG, TQ), BF16),
                        pltpu.VMEM((A_HEADS, 1, TQ), F32), pltpu.VMEM((A_HEADS, 1, TQ), F32),
                        pltpu.VMEM((A_HEADS * HEAD_DIM, TQ), F32)],
        compiler_params=pltpu.CompilerParams(dimension_semantics=("arbitrary", "arbitrary"),
                                             vmem_limit_bytes=VMEM_LIMIT),
        name="dsa_attn",
    )(iqt, ik4, wft, aqt, ak, posx, avt)


def _rms_cols(xt, g_col):
    return xt * lax.rsqrt(jnp.mean(xt * xt, axis=0, keepdims=True) + EPS) * g_col


MOE_TMG = 256
MOE_TMC = 256
ROW_TILE = 8
ROUTE_COLS = 128


def _first_index_of_max(vals, row, big):
    m = jnp.max(vals, axis=0, keepdims=True)
    idx = jnp.min(jnp.where(vals == m, row, big), axis=0, keepdims=True)
    return m, idx


def _route(h, wr_hi_ref, wr_lo_ref, br_ref):
    h_hi, h_lo = _split_bf16(h)
    logits_t = (_dot3(h_hi, h_lo, wr_hi_ref[...], wr_lo_ref[...]) + br_ref[...]).T
    tm = h.shape[0]
    ninf = -jnp.inf
    row8 = lax.broadcasted_iota(jnp.int32, (8, tm), 0)
    gl = jnp.where(row8 < N_GROUPS, logits_t[0:8], ninf)
    gmax, gsel = _first_index_of_max(gl, row8, 1 << 20)
    p_group = 1.0 / jnp.sum(jnp.exp(gl - gmax), axis=0, keepdims=True)
    expert = lax.broadcasted_iota(jnp.int32, (N_EXPERTS, tm), 0)
    assert EXPERTS_PER_GROUP == 8
    el = jnp.where((expert >> 3) == gsel, logits_t[32:32 + N_EXPERTS], ninf)
    v1, i1 = _first_index_of_max(el, expert, 1 << 20)
    el2 = jnp.where(expert == i1, ninf, el)
    v2, i2 = _first_index_of_max(el2, expert, 1 << 20)
    e2 = jnp.exp(v2 - v1)
    w1 = 1.0 / (1.0 + e2)
    w2 = e2 / (1.0 + e2)
    return i1, i2, w1 * p_group, w2 * p_group


def _token_rows(s, n_tokens, first_token=0):
    return pl.ds(first_token * ROW_TILE + s, n_tokens, stride=ROW_TILE)


def _to_token_tiles(ref, x):
    for s in range(ROW_TILE):
        ref[_token_rows(s, x.shape[0]), :] = x[:, s * 128:(s + 1) * 128]


def _token_tile(ref, t):
    start = t * ROW_TILE if isinstance(t, int) else pl.multiple_of(t * ROW_TILE, ROW_TILE)
    return ref.at[pl.ds(start, ROW_TILE), :]


def _moe_route_kernel(oat_ref, obt_ref, x_ref, mod_ref, ga_ref, gb_ref, wo_ref,
                      gf_ref, wr_hi_ref, wr_lo_ref, br_ref,
                      x1_ref, h_ref, info_ref, infot_ref, cnt_ref, run_ref):
    @pl.when(pl.program_id(0) == 0)
    def _():
        run_ref[...] = jnp.zeros_like(run_ref)

    oa = _rms_cols(oat_ref[0], ga_ref[...]).astype(BF16)
    ob = _rms_cols(obt_ref[0], gb_ref[...]).astype(BF16)
    y = _dot_tn(oa, wo_ref[0:512, :]) + _dot_tn(ob, wo_ref[512:1024, :])
    x1 = x_ref[...] + mod_ref[0, 2:3, :] * y
    x1_ref[...] = x1

    h = _rms(x1, gf_ref[...]) * (1.0 + mod_ref[0, 4:5, :]) + mod_ref[0, 3:4, :]
    _to_token_tiles(h_ref, h)
    i1, i2, w1, w2 = _route(h, wr_hi_ref, wr_lo_ref, br_ref)
    tm = h.shape[0]
    expert = lax.broadcasted_iota(jnp.int32, (N_EXPERTS, tm), 0)
    picked = jnp.where((expert == i1) | (expert == i2), 1.0, 0.0)
    earlier = jnp.where(lax.broadcasted_iota(jnp.int32, (tm, tm), 0)
                        < lax.broadcasted_iota(jnp.int32, (tm, tm), 1), 1.0, 0.0).astype(BF16)
    before = _dot(picked.astype(BF16), earlier) + run_ref[...]
    rank1 = jnp.sum(jnp.where(expert == i1, before, 0.0), axis=0, keepdims=True)
    rank2 = jnp.sum(jnp.where(expert == i2, before, 0.0), axis=0, keepdims=True)
    run_ref[...] += jnp.sum(picked, axis=1, keepdims=True)
    cnt_ref[...] = run_ref[...]
    row8 = lax.broadcasted_iota(jnp.int32, (8, tm), 0)
    info_t = jnp.where(row8 == 0, i1.astype(F32), 0.0)
    info_t = jnp.where(row8 == 1, i2.astype(F32), info_t)
    info_t = jnp.where(row8 == 2, rank1, info_t)
    info_t = jnp.where(row8 == 3, rank2, info_t)
    info_t = jnp.where(row8 == 4, w1, info_t)
    info_t = jnp.where(row8 == 5, w2, info_t)
    infot_ref[...] = info_t
    info_ref[...] = jnp.concatenate([info_t, jnp.zeros((ROUTE_COLS - 8, tm), F32)], axis=0).T


def _moe_route(oat, obt, x2, mod3, g_out_a, g_out_b, w_out_bf, g_ffn, wr_hi, wr_lo, b_route, seq):
    n, d = x2.shape
    tm = 512
    per_b = seq // tm
    row = lambda i: (i, 0)
    const = lambda i: (0, 0)
    blk_t = lambda i: (i // per_b, 0, i % per_b)
    return pl.pallas_call(
        _moe_route_kernel,
        out_shape=[jax.ShapeDtypeStruct((n, d), F32),
                   jax.ShapeDtypeStruct((n * ROW_TILE, d // ROW_TILE), F32),
                   jax.ShapeDtypeStruct((n, ROUTE_COLS), F32), jax.ShapeDtypeStruct((8, n), F32),
                   jax.ShapeDtypeStruct((N_EXPERTS, 1), F32)],
        grid=(n // tm,),
        in_specs=[pl.BlockSpec((1, 512, tm), blk_t), pl.BlockSpec((1, 512, tm), blk_t),
                  pl.BlockSpec((tm, d), row),
                  pl.BlockSpec((1, N_MOD, d), lambda i: (i // per_b, 0, 0)),
                  pl.BlockSpec((512, 1), const), pl.BlockSpec((512, 1), const),
                  pl.BlockSpec((d, d), const),
                  pl.BlockSpec((1, d), const),
                  pl.BlockSpec((d, ROUTE_COLS), const), pl.BlockSpec((d, ROUTE_COLS), const),
                  pl.BlockSpec((1, ROUTE_COLS), const)],
        out_specs=[pl.BlockSpec((tm, d), row),
                   pl.BlockSpec((tm * ROW_TILE, d // ROW_TILE), row),
                   pl.BlockSpec((tm, ROUTE_COLS), row), pl.BlockSpec((8, tm), lambda i: (0, i)),
                   pl.BlockSpec((N_EXPERTS, 1), const)],
        scratch_shapes=[pltpu.VMEM((N_EXPERTS, 1), F32)],
        compiler_params=pltpu.CompilerParams(dimension_semantics=("arbitrary",),
                                             vmem_limit_bytes=VMEM_LIMIT),
        name="mix_out_moe_route",
    )(oat, obt, x2, mod3, g_out_a.reshape(-1, 1), g_out_b.reshape(-1, 1), w_out_bf,
      g_ffn.reshape(1, d), wr_hi, wr_lo, b_route)


def _wait_token_copies(src_hbm, dst, sem, n_tokens):
    pltpu.make_async_copy(src_hbm.at[pl.ds(0, n_tokens * ROW_TILE), :], dst, sem).wait()


def _moe_scatter_kernel(dest_ref, pad_start_ref, pad_len_ref, nu_ref, h_ref, xs_hbm, zero_ref, sem):
    i = pl.program_id(0)
    n_pairs = 2 * MOE_TMC

    @pl.when(i == 0)
    def _():
        zero_ref[...] = jnp.zeros_like(zero_ref)

        def pad_copies(e):
            n = pad_len_ref[e]
            size = MOE_TMG // 2
            while size >= 1:
                first = pad_start_ref[e] + (n & ~(2 * size - 1))
                copy = pltpu.make_async_copy(
                    zero_ref.at[pl.ds(0, size * ROW_TILE), :],
                    xs_hbm.at[pl.ds(pl.multiple_of(first * ROW_TILE, ROW_TILE), size * ROW_TILE), :],
                    sem.at[1])
                yield (n & size) != 0, copy
                size //= 2

        for e in range(N_EXPERTS):
            for present, copy in pad_copies(e):
                pl.when(present)(copy.start)
        for e in range(N_EXPERTS):
            for present, copy in pad_copies(e):
                pl.when(present)(copy.wait)

        def row_tile(t):
            return xs_hbm.at[pl.ds(pl.multiple_of(t * (MOE_TMG * ROW_TILE), MOE_TMG * ROW_TILE),
                                   MOE_TMG * ROW_TILE), :]

        n_tiles = xs_hbm.shape[0] // (MOE_TMG * ROW_TILE)

        def fill_tile(t, _):
            pltpu.make_async_copy(zero_ref, row_tile(t), sem.at[1]).start()
            return 0

        def drain_tile(t, _):
            pltpu.make_async_copy(zero_ref, row_tile(0), sem.at[1]).wait()
            return 0

        lax.fori_loop(nu_ref[0], n_tiles, fill_tile, 0)
        lax.fori_loop(nu_ref[0], n_tiles, drain_tile, 0)

    for r in range(MOE_TMC):
        for j in range(2):
            pltpu.make_async_copy(_token_tile(h_ref, r),
                                  _token_tile(xs_hbm, dest_ref[i * n_pairs + j * MOE_TMC + r]),
                                  sem.at[0]).start(priority=j)
    _wait_token_copies(xs_hbm, xs_hbm.at[pl.ds(0, n_pairs * ROW_TILE), :], sem.at[0], n_pairs)


def _moe_scatter(dest, pad_start, pad_len, n_used, h2, n_rows):
    n = h2.shape[0] // ROW_TILE
    grid_spec = pltpu.PrefetchScalarGridSpec(
        num_scalar_prefetch=4,
        grid=(n // MOE_TMC,),
        in_specs=[pl.BlockSpec((MOE_TMC * ROW_TILE, h2.shape[1]), lambda i, *_: (i, 0))],
        out_specs=pl.BlockSpec(memory_space=pl.ANY),
        scratch_shapes=[pltpu.VMEM((MOE_TMG * ROW_TILE, h2.shape[1]), F32),
                        pltpu.SemaphoreType.DMA((2,))])
    return pl.pallas_call(
        _moe_scatter_kernel,
        out_shape=jax.ShapeDtypeStruct((n_rows * ROW_TILE, h2.shape[1]), F32),
        grid_spec=grid_spec,
        compiler_params=pltpu.CompilerParams(dimension_semantics=("arbitrary",),
                                             vmem_limit_bytes=VMEM_LIMIT),
        name="moe_scatter",
    )(dest, pad_start, pad_len, n_used, h2)


TILES_PER_STEP = 2


def _moe_expert_kernel(te_ref, nu_ref, x_ref, *refs):
    n = TILES_PER_STEP
    wg_refs, wu_refs, wd_refs = (refs[k * n:(k + 1) * n] for k in range(3))
    y_ref, xs_ref, wgb_ref, wub_ref, wdb_ref = refs[3 * n:]
    i = pl.program_id(0)
    step_used = n * i < nu_ref[0]

    @pl.when(jnp.logical_not(step_used))
    def _():
        y_ref[...] = jnp.zeros_like(y_ref)

    for slot in range(n):
        tile = n * i + slot
        new_expert = jnp.logical_or(i == 0, te_ref[tile] != te_ref[jnp.maximum(tile - n, 0)])

        @pl.when(jnp.logical_and(step_used, new_expert))
        def _(slot=slot):
            wgb_ref[slot] = wg_refs[slot][0].astype(BF16)
            wub_ref[slot] = wu_refs[slot][0].astype(BF16)
            wdb_ref[slot] = wd_refs[slot][0].astype(BF16)

    @pl.when(step_used)
    def _():
        for slot in range(n):
            for s in range(ROW_TILE):
                rows = _token_rows(s, MOE_TMG, first_token=slot * MOE_TMG)
                xs_ref[slot, :, s * 128:(s + 1) * 128] = x_ref[rows, :].astype(BF16)
        acts = []
        for slot in range(n):
            x = xs_ref[slot]
            hg = _dot(x, wgb_ref[slot])
            hu = _dot(x, wub_ref[slot])
            acts.append((hg * jax.nn.sigmoid(hg) * hu).astype(BF16))
        for slot in range(n):
            y = _dot(acts[slot], wdb_ref[slot])
            for s in range(ROW_TILE):
                rows = _token_rows(s, MOE_TMG, first_token=slot * MOE_TMG)
                y_ref[rows, :] = y[:, s * 128:(s + 1) * 128]


def _moe_experts(tile_expert, n_used, xsorted, wg, wu, wd):
    n_rows = xsorted.shape[0] // ROW_TILE
    d, ff = wg.shape[1], wg.shape[2]
    n = TILES_PER_STEP
    block = (n * MOE_TMG * ROW_TILE, xsorted.shape[1])
    slots = range(n)
    w_in_specs = [pl.BlockSpec((1, d, ff), lambda i, te, nu, s=s: (te[n * i + s], 0, 0)) for s in slots]
    w_out_specs = [pl.BlockSpec((1, ff, d), lambda i, te, nu, s=s: (te[n * i + s], 0, 0)) for s in slots]
    grid_spec = pltpu.PrefetchScalarGridSpec(
        num_scalar_prefetch=2,
        grid=(n_rows // (MOE_TMG * n),),
        in_specs=[pl.BlockSpec(block, lambda i, te, nu: (jnp.minimum(i, (nu[0] - 1) // n), 0))]
        + w_in_specs + w_in_specs + w_out_specs,
        out_specs=pl.BlockSpec(block, lambda i, te, nu: (i, 0)),
        scratch_shapes=[pltpu.VMEM((n, MOE_TMG, d), BF16), pltpu.VMEM((n, d, ff), BF16),
                        pltpu.VMEM((n, d, ff), BF16), pltpu.VMEM((n, ff, d), BF16)])
    return pl.pallas_call(
        _moe_expert_kernel,
        out_shape=jax.ShapeDtypeStruct(xsorted.shape, F32),
        grid_spec=grid_spec,
        compiler_params=pltpu.CompilerParams(dimension_semantics=("arbitrary",),
                                             vmem_limit_bytes=VMEM_LIMIT),
        name="moe_experts",
    )(tile_expert, n_used, xsorted, *([wg] * n), *([wu] * n), *([wd] * n))


def _moe_combine_kernel(dest_ref, y_hbm, x_ref, info_ref, mod_ref, gfin_ref, o_ref, ybuf, x2_ref, sem):
    i = pl.program_id(0)
    nt = pl.num_programs(0)
    slot = lax.rem(i, 2)
    n_pairs = 2 * MOE_TMC

    def start_gather(tile, to_slot):
        for r in range(n_pairs):
            pltpu.make_async_copy(_token_tile(y_hbm, dest_ref[tile * n_pairs + r]),
                                  _token_tile(ybuf.at[to_slot], r), sem.at[to_slot]).start(priority=r % 2)

    @pl.when(i == 0)
    def _():
        start_gather(0, 0)

    _wait_token_copies(y_hbm, ybuf.at[slot], sem.at[slot], n_pairs)

    @pl.when(i + 1 < nt)
    def _():
        start_gather(i + 1, 1 - slot)

    w1 = info_ref[:, 4:5]
    w2 = info_ref[:, 5:6]
    sumsq = jnp.zeros((MOE_TMC, 1), F32)
    for s in range(ROW_TILE):
        cols = slice(s * 128, (s + 1) * 128)
        y = (w1 * ybuf[slot, _token_rows(s, MOE_TMC), :]
             + w2 * ybuf[slot, _token_rows(s, MOE_TMC, first_token=MOE_TMC), :])
        x2 = x_ref[:, cols] + mod_ref[0, 5:6, cols] * y
        x2_ref[:, cols] = x2
        sumsq = sumsq + jnp.sum(x2 * x2, axis=-1, keepdims=True)
    d = x2_ref.shape[1]
    o_ref[...] = x2_ref[...] * lax.rsqrt(sumsq / d + EPS) * gfin_ref[...]


def _moe_combine(dest, ysorted, x1, info, mod3, g_final, seq):
    n, d = x1.shape
    per_b = seq // MOE_TMC
    grid_spec = pltpu.PrefetchScalarGridSpec(
        num_scalar_prefetch=1,
        grid=(n // MOE_TMC,),
        in_specs=[pl.BlockSpec(memory_space=pl.ANY),
                  pl.BlockSpec((MOE_TMC, d), lambda i, ds: (i, 0)),
                  pl.BlockSpec((MOE_TMC, ROUTE_COLS), lambda i, ds: (i, 0)),
                  pl.BlockSpec((1, N_MOD, d), lambda i, ds: (i // per_b, 0, 0)),
                  pl.BlockSpec((1, d), lambda i, ds: (0, 0))],
        out_specs=pl.BlockSpec((MOE_TMC, d), lambda i, ds: (i, 0)),
        scratch_shapes=[pltpu.VMEM((2, 2 * MOE_TMC * ROW_TILE, ysorted.shape[1]), F32),
                        pltpu.VMEM((MOE_TMC, d), F32), pltpu.SemaphoreType.DMA((2,))])
    return pl.pallas_call(
        _moe_combine_kernel,
        out_shape=jax.ShapeDtypeStruct((n, d), F32),
        grid_spec=grid_spec,
        compiler_params=pltpu.CompilerParams(dimension_semantics=("arbitrary",),
                                             vmem_limit_bytes=VMEM_LIMIT),
        name="moe_combine",
    )(dest, ysorted, x1, info, mod3, g_final.reshape(1, d))


def _mix_out_and_moe(oat, obt, x2, mod3, g_out_a, g_out_b, w_out_bf, g_ffn, wr_hi, wr_lo, b_route,
                     wg, wu, wd, g_final, seq):
    n, d = x2.shape
    x1, h2, info, infot, counts = _moe_route(oat, obt, x2, mod3, g_out_a, g_out_b, w_out_bf,
                                             g_ffn, wr_hi, wr_lo, b_route, seq)

    e1, e2, rank1, rank2 = [infot[k].astype(jnp.int32) for k in range(4)]
    cnt = counts[:, 0].astype(jnp.int32)
    padded = ((cnt + MOE_TMG - 1) // MOE_TMG) * MOE_TMG
    seg_end = jnp.cumsum(padded)
    expert_ids = jnp.arange(N_EXPERTS, dtype=jnp.int32)

    def seg_start_of(e):
        return jnp.sum(jnp.where(expert_ids[None, :] < e[:, None], padded[None, :], 0), axis=1)

    dest1 = seg_start_of(e1) + rank1
    dest2 = seg_start_of(e2) + rank2
    n_rows = 2 * n + N_EXPERTS * MOE_TMG
    tile_start = jnp.arange(n_rows // MOE_TMG, dtype=jnp.int32) * MOE_TMG
    tile_expert = jnp.minimum(jnp.sum((tile_start[:, None] >= seg_end[None, :]).astype(jnp.int32), axis=1),
                              N_EXPERTS - 1)
    n_used = (seg_end[N_EXPERTS - 1:] // MOE_TMG).astype(jnp.int32)
    dest = jnp.concatenate([dest1.reshape(-1, MOE_TMC), dest2.reshape(-1, MOE_TMC)], axis=1).reshape(-1)

    seg_start = seg_end - padded
    xsorted = _moe_scatter(dest, seg_start + cnt, padded - cnt, n_used, h2, n_rows)
    ysorted = _moe_experts(tile_expert, n_used, xsorted, wg, wu, wd)
    return _moe_combine(dest, ysorted, x1, info, mod3, g_final, seq)


def _layer(x3, c, w_ada, b_ada, g_mix, w_in, b_forget, g_out_a, g_out_b, w_out,
           g_ffn, w_group, b_group, w_router, b_router, w_gate, w_up, w_down, g_final):
    bsz, seq, d = x3.shape
    mod3 = _ada_mod(c, w_ada, b_ada).reshape(bsz, N_MOD, d)

    w_t = w_in.T

    def pad_heads(w, n_heads):
        w = w.reshape(d, n_heads, HEAD_DIM)
        return jnp.concatenate([w, jnp.zeros_like(w)], axis=-1).reshape(d, n_heads * AUG)

    w_wf_t = jnp.concatenate([w_t[1344:1352], w_t[2888:2896]], axis=0)
    weights = [w_t[0:512].astype(BF16), pad_heads(w_in[:, 512:640], A_KV_HEADS).astype(BF16),
               w_t[640:768].astype(BF16), w_t[1352:1864].astype(BF16),
               w_in[:, 1864:2376].astype(BF16), w_t[2376:2888].astype(BF16),
               w_t[768:1280].astype(BF16), w_in[:, 1280:1344].astype(BF16), *_split_bf16(w_wf_t)]
    aqt, ak, avt, bqt, bk, bvt, iqt, ik4, wft = _in_proj(x3, mod3, g_mix, weights)

    cumt, kaug = _fox_cum(wft, b_forget, bk)
    obt = _fox_attn(bqt, kaug, bvt, cumt)

    pos = jnp.arange(seq, dtype=jnp.int32)[:, None]
    lane = jnp.arange(AUG, dtype=jnp.int32)[None, :] - HEAD_DIM
    posx = jnp.where((lane >= 0) & (lane < 3), pos >> 7,
                     jnp.where((lane >= 3) & (lane < 6), pos & 127,
                               jnp.where((lane >= 6) & (lane < 9), 1, 0))).astype(BF16)
    oat = _dsa_attn(iqt, ik4, wft, aqt, ak, posx, avt)

    w_r = jnp.concatenate([w_group, jnp.zeros((d, 32 - N_GROUPS), F32),
                           jnp.transpose(w_router, (1, 0, 2)).reshape(d, N_EXPERTS),
                           jnp.zeros((d, ROUTE_COLS - 64), F32)], axis=1)
    b_r = jnp.concatenate([b_group, jnp.zeros((32 - N_GROUPS,), F32), b_router.reshape(-1),
                           jnp.zeros((ROUTE_COLS - 64,), F32)]).reshape(1, ROUTE_COLS)
    wr_hi, wr_lo = _split_bf16(w_r)
    out = _mix_out_and_moe(oat, obt, x3.reshape(bsz * seq, d), mod3, g_out_a, g_out_b,
                           w_out.astype(BF16), g_ffn, wr_hi, wr_lo, b_r, w_gate, w_up, w_down,
                           g_final, seq)
    return out.reshape(bsz, seq, d)


def kernel(x, c, w_ada, b_ada, g_mix, w_in, b_forget, g_out_a, g_out_b, w_out, g_ffn, w_group,
           b_group, w_router, b_router, w_gate, w_up, w_down, g_final):
    depth = w_ada.shape[0]
    assert depth == 1, "final norm is fused into the single layer's MoE kernel"
    return _layer(x, c, w_ada[0], b_ada[0], g_mix[0], w_in[0], b_forget[0], g_out_a[0], g_out_b[0],
                  w_out[0], g_ffn[0], w_group[0], b_group[0], w_router[0], b_router[0], w_gate[0],
                  w_up[0], w_down[0], g_final)
```

```python
import math

import jax
import jax.numpy as jnp
import numpy as np
from jax import lax
from jax.experimental import pallas as pl
from jax.experimental.pallas import tpu as pltpu

F32 = jnp.float32
BF16 = jnp.bfloat16

EPS = 1e-6
A_HEADS = 8
A_KV_HEADS = 2
HEAD_DIM = 64
IDX_HEADS = 8
IDX_DIM = 64
TOPK = 256
B_HEADS = 8
N_GROUPS = 4
EXPERTS_PER_GROUP = 8
N_EXPERTS = N_GROUPS * EXPERTS_PER_GROUP
N_MOD = 6

NEG_BIG = -1e30
LOWEST = float(np.finfo(np.float32).min)
LOG2E = math.log2(math.e)
Q_SCALE = HEAD_DIM ** -0.5 * LOG2E
VMEM_LIMIT = 48 * 1024 * 1024

TQ = 256
KC = 256
AUG = 128
IDX_K = IDX_DIM
ONES_ROWS = 16
COARSE_STEPS = 14
FOX_LOOKAHEAD = 6
DSA_LOOKAHEAD = 4
assert TQ == KC == TOPK


def _split_bf16(x):
    hi = x.astype(BF16)
    lo = (x - hi.astype(F32)).astype(BF16)
    return hi, lo


def _split3_f32(x):
    p1 = x.astype(BF16).astype(F32)
    r1 = x - p1
    p2 = r1.astype(BF16).astype(F32)
    p3 = (r1 - p2).astype(BF16).astype(F32)
    return p1, p2, p3


def _dot(a, b):
    return jnp.dot(a, b, preferred_element_type=F32)


def _dot_nt(a, b):
    return lax.dot_general(a, b, (((1,), (1,)), ((), ())), preferred_element_type=F32)


def _dot_tn(a, b):
    return lax.dot_general(a, b, (((0,), (0,)), ((), ())), preferred_element_type=F32)


def _dot3(a_hi, a_lo, b_hi, b_lo):
    return _dot(a_hi, b_hi) + _dot(a_lo, b_hi) + _dot(a_hi, b_lo)


def _dot3_nt(a_hi, a_lo, b_hi, b_lo):
    return _dot_nt(a_hi, b_hi) + _dot_nt(a_lo, b_hi) + _dot_nt(a_hi, b_lo)


def _rms(x, g):
    return x * lax.rsqrt(jnp.mean(x * x, axis=-1, keepdims=True) + EPS) * g


def _chunk(kc):
    return pl.ds(pl.multiple_of(kc * KC, KC), KC)


def _key_minus_query():
    return (lax.broadcasted_iota(jnp.int32, (KC, TQ), 0)
            - lax.broadcasted_iota(jnp.int32, (KC, TQ), 1))


def _ada_kernel(c_ref, w_ref, b_ref, o_ref):
    c = c_ref[...]
    s = c * jax.nn.sigmoid(c)
    s_hi, s_lo = _split_bf16(s)
    w_hi, w_lo = _split_bf16(w_ref[...])
    o_ref[...] = _dot3(s_hi, s_lo, w_hi, w_lo) + b_ref[...]


def _ada_mod(c, w_ada, b_ada):
    bsz, d = c.shape
    n = w_ada.shape[1]
    tn = 1024
    return pl.pallas_call(
        _ada_kernel,
        out_shape=jax.ShapeDtypeStruct((bsz, n), F32),
        grid=(n // tn,),
        in_specs=[pl.BlockSpec((bsz, d), lambda j: (0, 0)),
                  pl.BlockSpec((d, tn), lambda j: (0, j)),
                  pl.BlockSpec((1, tn), lambda j: (0, j))],
        out_specs=pl.BlockSpec((bsz, tn), lambda j: (0, j)),
        compiler_params=pltpu.CompilerParams(dimension_semantics=("arbitrary",),
                                             vmem_limit_bytes=VMEM_LIMIT),
        name="ada_mod",
    )(c, w_ada, b_ada.reshape(1, n))


def _in_proj_kernel(x_ref, mod_ref, g_ref,
                    waq_ref, wak_ref, wav_ref, wbq_ref, wbk_ref, wbv_ref,
                    wiq_ref, wik_ref, wwfh_ref, wwfl_ref,
                    aqt_ref, ak_ref, avt_ref, bqt_ref, bk_ref, bvt_ref, iqt_ref, ik_ref, wft_ref):
    x = x_ref[0]
    h = _rms(x, g_ref[...]) * (1.0 + mod_ref[0, 1:2, :]) + mod_ref[0, 0:1, :]
    h_hi, h_lo = _split_bf16(h)
    aqt_ref[0] = (_dot_nt(waq_ref[...], h_hi) * Q_SCALE).astype(BF16)
    ak_ref[0] = _dot(h_hi, wak_ref[...]).astype(BF16)
    avt_ref[0] = _dot_nt(wav_ref[...], h_hi).astype(BF16)
    bqt_ref[0] = (_dot_nt(wbq_ref[...], h_hi) * Q_SCALE).astype(BF16)
    bk_ref[0] = _dot(h_hi, wbk_ref[...]).astype(BF16)
    bvt_ref[0] = _dot_nt(wbv_ref[...], h_hi).astype(BF16)
    iqt_ref[0] = _dot_nt(wiq_ref[...], h_hi).astype(BF16)
    ik_ref[0] = _dot(h_hi, wik_ref[...]).astype(BF16)
    wft_ref[0] = _dot3_nt(wwfh_ref[...], wwfl_ref[...], h_hi, h_lo)


def _in_proj(x3, mod3, g_mix, weights):
    bsz, seq, d = x3.shape
    tm = 512
    blk_t = lambda b, i: (b, 0, i)
    blk_r = lambda b, i: (b, i, 0)
    const = lambda b, i: (0, 0)
    ak_w = A_KV_HEADS * AUG
    bk_w = B_HEADS * HEAD_DIM
    outs = [jax.ShapeDtypeStruct((bsz, 512, seq), BF16), jax.ShapeDtypeStruct((bsz, seq, ak_w), BF16),
            jax.ShapeDtypeStruct((bsz, 128, seq), BF16), jax.ShapeDtypeStruct((bsz, 512, seq), BF16),
            jax.ShapeDtypeStruct((bsz, seq, bk_w), BF16), jax.ShapeDtypeStruct((bsz, 512, seq), BF16),
            jax.ShapeDtypeStruct((bsz, IDX_HEADS * IDX_K, seq), BF16),
            jax.ShapeDtypeStruct((bsz, seq, IDX_K), BF16), jax.ShapeDtypeStruct((bsz, 16, seq), F32)]
    out_specs = [pl.BlockSpec((1, 512, tm), blk_t), pl.BlockSpec((1, tm, ak_w), blk_r),
                 pl.BlockSpec((1, 128, tm), blk_t), pl.BlockSpec((1, 512, tm), blk_t),
                 pl.BlockSpec((1, tm, bk_w), blk_r), pl.BlockSpec((1, 512, tm), blk_t),
                 pl.BlockSpec((1, IDX_HEADS * IDX_K, tm), blk_t),
                 pl.BlockSpec((1, tm, IDX_K), blk_r), pl.BlockSpec((1, 16, tm), blk_t)]
    return pl.pallas_call(
        _in_proj_kernel,
        out_shape=outs,
        grid=(bsz, seq // tm),
        in_specs=[pl.BlockSpec((1, tm, d), blk_r),
                  pl.BlockSpec((1, N_MOD, d), lambda b, i: (b, 0, 0)),
                  pl.BlockSpec((1, d), const)] + [pl.BlockSpec(w.shape, const) for w in weights],
        out_specs=out_specs,
        compiler_params=pltpu.CompilerParams(dimension_semantics=("arbitrary", "arbitrary"),
                                             vmem_limit_bytes=VMEM_LIMIT),
        name="in_proj",
    )(x3, mod3, g_mix.reshape(1, d), *weights)


CB = 256


def _cum_kernel(wft_ref, bfor_ref, k_ref, cumt_ref, kaug_ref):
    seq = wft_ref.shape[2]
    r = lax.broadcasted_iota(jnp.int32, (CB, CB), 0)
    cidx = lax.broadcasted_iota(jnp.int32, (CB, CB), 1)
    tri = jnp.where(r <= cidx, 1.0, 0.0).astype(BF16)
    row128 = lax.broadcasted_iota(jnp.int32, (AUG, CB), 0)
    ones_rows = jnp.where((row128 >= HEAD_DIM + 3) & (row128 < HEAD_DIM + 6), 1.0, 0.0)
    src = lax.broadcasted_iota(jnp.int32, (AUG, AUG), 0)
    dst = lax.broadcasted_iota(jnp.int32, (AUG, AUG), 1)
    place = [jnp.where((dst < HEAD_DIM) & (src == dst + half * HEAD_DIM), 1.0, 0.0).astype(BF16)
             for half in range(2)]
    carry = jnp.zeros((8, 1), F32)
    for blk in range(seq // CB):
        cols = slice(blk * CB, (blk + 1) * CB)
        z = wft_ref[0, 8:16, cols] + bfor_ref[...]
        logf = jnp.minimum(z, 0.0) - jnp.log(1.0 + jnp.exp(-jnp.abs(z)))
        p1, p2, p3 = _split3_f32(logf)
        pieces = jnp.concatenate([p1, p2, p3, jnp.zeros_like(p1)], axis=0).astype(BF16)
        parts = _dot(pieces, tri)
        cum = parts[0:8] + parts[8:16] + parts[16:24] + carry
        carry = cum[:, CB - 1:CB]
        cum2 = cum * LOG2E
        cumt_ref[0, :, cols] = cum2
        c1, c2, c3 = _split3_f32(cum2)
        for h in range(B_HEADS):
            spare = jnp.where(row128 == HEAD_DIM, -c1[h:h + 1], ones_rows)
            spare = jnp.where(row128 == HEAD_DIM + 1, -c2[h:h + 1], spare)
            spare = jnp.where(row128 == HEAD_DIM + 2, -c3[h:h + 1], spare)
            pair = k_ref[0, cols, (h // 2) * AUG:(h // 2 + 1) * AUG]
            k_h = _dot(pair, place[h % 2]).astype(BF16)
            kaug_ref[0, cols, h * AUG:(h + 1) * AUG] = k_h + spare.T.astype(BF16)


def _fox_cum(wft, b_forget, bk):
    bsz, _, seq = wft.shape
    nh = B_HEADS
    kw = bk.shape[-1]
    kaug_w = nh * AUG
    return pl.pallas_call(
        _cum_kernel,
        out_shape=[jax.ShapeDtypeStruct((bsz, nh, seq), F32),
                   jax.ShapeDtypeStruct((bsz, seq, kaug_w), BF16)],
        grid=(bsz,),
        in_specs=[pl.BlockSpec((1, 16, seq), lambda b: (b, 0, 0)),
                  pl.BlockSpec((nh, 1), lambda b: (0, 0)),
                  pl.BlockSpec((1, seq, kw), lambda b: (b, 0, 0))],
        out_specs=[pl.BlockSpec((1, nh, seq), lambda b: (b, 0, 0)),
                   pl.BlockSpec((1, seq, kaug_w), lambda b: (b, 0, 0))],
        compiler_params=pltpu.CompilerParams(dimension_semantics=("arbitrary",),
                                             vmem_limit_bytes=VMEM_LIMIT),
        name="fox_cum",
    )(wft, b_forget.reshape(nh, 1), bk)


def _softmax_init(m_ref, l_ref, acc_ref):
    m_ref[...] = jnp.full(m_ref.shape, NEG_BIG, F32)
    l_ref[...] = jnp.zeros(l_ref.shape, F32)
    acc_ref[...] = jnp.zeros(acc_ref.shape, F32)


def _attend_chunks(items, score_fn, vt_fn, m_ref, l_ref, acc_ref, lookahead):
    ahead = min(lookahead, len(items))
    scores = {i: score_fn(items[i]) for i in range(ahead)}
    ones = jnp.ones((ONES_ROWS, KC), BF16)
    for i, item in enumerate(items):
        if i + ahead < len(items):
            scores[i + ahead] = score_fn(items[i + ahead])
        s = scores.pop(i)
        h = item[1]
        m_old = m_ref[h]
        m_new = jnp.maximum(m_old, jnp.max(s, axis=0, keepdims=True))
        alpha = jnp.exp2(m_old - m_new)
        p = jnp.exp2(s - m_new).astype(BF16)
        pv = _dot(jnp.concatenate([vt_fn(item), ones], axis=0), p)
        m_ref[h] = m_new
        l_ref[h] = alpha * l_ref[h] + pv[HEAD_DIM:HEAD_DIM + 1, :]
        rows = slice(h * HEAD_DIM, (h + 1) * HEAD_DIM)
        acc_ref[rows, :] = alpha * acc_ref[rows, :] + pv[0:HEAD_DIM, :]


def _softmax_finish(o_ref, n_heads, l_ref, acc_ref):
    for h in range(n_heads):
        rows = slice(h * HEAD_DIM, (h + 1) * HEAD_DIM)
        o_ref[0, rows, :] = acc_ref[rows, :] / l_ref[h]


def _fox_kernel(qt_ref, k_ref, vt_ref, cumt_ref, o_ref, w_ref, m_ref, l_ref, acc_ref):
    qi = pl.program_id(1)
    _softmax_init(m_ref, l_ref, acc_ref)

    row64 = lax.broadcasted_iota(jnp.int32, (AUG - HEAD_DIM, TQ), 0)
    for h in range(B_HEADS):
        c1, c2, c3 = _split3_f32(cumt_ref[0, h:h + 1, :])
        spare = jnp.where(row64 < 3, 1.0, 0.0)
        spare = jnp.where(row64 == 3, c1, spare)
        spare = jnp.where(row64 == 4, c2, spare)
        spare = jnp.where(row64 == 5, c3, spare)
        w_ref[h] = jnp.concatenate([qt_ref[0, h * HEAD_DIM:(h + 1) * HEAD_DIM, :],
                                    spare.astype(BF16)], axis=0)

    def tiles(chunks):
        rows = [_chunk(kc) for kc, _ in chunks]

        def score_fn(item):
            j, h = item
            s = _dot(k_ref[0, rows[j], h * AUG:(h + 1) * AUG], w_ref[h])
            if chunks[j][1]:
                s = jnp.where(_key_minus_query() <= 0, s, NEG_BIG)
            return s

        def vt_fn(item):
            j, h = item
            return vt_ref[0, h * HEAD_DIM:(h + 1) * HEAD_DIM, rows[j]]

        items = [(j, h) for j in range(len(chunks)) for h in range(B_HEADS)]
        _attend_chunks(items, score_fn, vt_fn, m_ref, l_ref, acc_ref, FOX_LOOKAHEAD)

    def body(pair, _):
        tiles([(2 * pair, False), (2 * pair + 1, False)])
        return 0

    lax.fori_loop(0, qi // 2, body, 0)

    @pl.when(qi % 2 == 1)
    def _():
        tiles([(qi - 1, False), (qi, True)])

    @pl.when(qi % 2 == 0)
    def _():
        tiles([(qi, True)])

    _softmax_finish(o_ref, B_HEADS, l_ref, acc_ref)


def _fox_attn(bqt, kaug, bvt, cumt):
    bsz, w, seq = bqt.shape
    blk_t = lambda b, i: (b, 0, i)
    full = lambda b, i: (b, 0, 0)
    return pl.pallas_call(
        _fox_kernel,
        out_shape=jax.ShapeDtypeStruct((bsz, w, seq), F32),
        grid=(bsz, seq // TQ),
        in_specs=[pl.BlockSpec((1, w, TQ), blk_t),
                  pl.BlockSpec((1, seq, B_HEADS * AUG), full),
                  pl.BlockSpec((1, w, seq), full),
                  pl.BlockSpec((1, B_HEADS, TQ), blk_t)],
        out_specs=pl.BlockSpec((1, w, TQ), blk_t),
        scratch_shapes=[pltpu.VMEM((B_HEADS, AUG, TQ), BF16), pltpu.VMEM((B_HEADS, 1, TQ), F32),
                        pltpu.VMEM((B_HEADS, 1, TQ), F32), pltpu.VMEM((B_HEADS * HEAD_DIM, TQ), F32)],
        compiler_params=pltpu.CompilerParams(dimension_semantics=("arbitrary", "arbitrary"),
                                             vmem_limit_bytes=VMEM_LIMIT),
        name="fox_attn",
    )(bqt, kaug, bvt, cumt)


def _bf16_pieces(value):
    pieces = []
    rest = np.float32(value)
    for _ in range(3):
        piece = np.asarray(rest).astype(BF16).astype(np.float32)
        pieces.append(float(piece))
        rest = np.float32(rest - piece)
    return pieces


def _dsa_kernel(iqt_ref, ik_ref, wft_ref, qt_ref, k_ref, posx_ref, vt_ref, o_ref,
                s_ref, thr_ref, lohi_ref, w_ref, m_ref, l_ref, acc_ref):
    qi = pl.program_id(1)
    nch = qi + 1
    kmq = _key_minus_query()

    def part(x, op):
        return op(x.reshape(KC // 8, 8, TQ), axis=0)

    def fold(x, op):
        return op(x, axis=0, keepdims=True)

    lohi_ref[0] = jnp.full((8, TQ), jnp.inf, F32)
    lohi_ref[1] = jnp.full((8, TQ), -jnp.inf, F32)

    def score_chunk(kc):
        ik = ik_ref[0, _chunk(kc), :]
        acc = jnp.zeros((KC, TQ), F32)
        for h in range(IDX_HEADS):
            d = _dot(ik, iqt_ref[0, h * IDX_K:(h + 1) * IDX_K, :])
            acc = acc + wft_ref[0, h:h + 1, :] * jnp.maximum(d, 0.0)
        causal = kmq <= (qi - kc) * KC
        s_ref[_chunk(kc), :] = jnp.where(causal, acc, -jnp.inf)
        lohi_ref[0] = jnp.minimum(lohi_ref[0], part(jnp.where(causal, acc, jnp.inf), jnp.min))
        lohi_ref[1] = jnp.maximum(lohi_ref[1], part(jnp.where(causal, acc, -jnp.inf), jnp.max))

    def score_body(pair, _):
        score_chunk(2 * pair)
        score_chunk(2 * pair + 1)
        return 0

    lax.fori_loop(0, nch // 2, score_body, 0)

    @pl.when(nch % 2 == 1)
    def _():
        score_chunk(nch - 1)
        s_ref[_chunk(nch), :] = jnp.full((KC, TQ), -jnp.inf, F32)

    def scan(fn, init):
        def pair_body(pair, c):
            c = fn(2 * pair, s_ref[_chunk(2 * pair), :], c)
            return fn(2 * pair + 1, s_ref[_chunk(2 * pair + 1), :], c)

        return lax.fori_loop(0, (nch + 1) // 2, pair_body, init)

    @pl.when(qi == 0)
    def _():
        thr_ref[...] = jnp.full(thr_ref.shape, LOWEST, F32)

    @pl.when(qi > 0)
    def _():
        def cmin(x):
            return part(x, jnp.min)

        def cmax(x):
            return part(x, jnp.max)

        def csum(x):
            return part(x, jnp.sum)

        zeros = jnp.zeros((8, TQ), F32)
        pinf = jnp.full((8, TQ), jnp.inf, F32)
        ninf = jnp.full((8, TQ), -jnp.inf, F32)
        lo, hi = fold(lohi_ref[0], jnp.min), fold(lohi_ref[1], jnp.max)

        def coarse_step(_, carry):
            lo, hi = carry
            mid = lo + (hi - lo) * 0.5
            cnt = scan(lambda _, s, c: c + csum(jnp.where(s >= mid, 1.0, 0.0)), zeros)
            enough = fold(cnt, jnp.sum) >= float(TOPK)
            return jnp.where(enough, mid, lo), jnp.where(enough, hi, mid)

        lo, hi = lax.fori_loop(0, COARSE_STEPS, coarse_step, (lo, hi))

        def cond(carry):
            return carry[2] > 0

        def step(carry):
            lo, hi, _ = carry
            mid = lo + (hi - lo) * 0.5
            mid = jnp.where(mid <= lo, hi, mid)

            def fn(_, s, c):
                cnt, a, b = c
                ge = s >= mid
                cnt = cnt + csum(jnp.where(ge, 1.0, 0.0))
                b = jnp.minimum(b, cmin(jnp.where(ge, s, jnp.inf)))
                a = jnp.maximum(a, cmax(jnp.where(ge, -jnp.inf, s)))
                return cnt, a, b

            cnt, a, b = scan(fn, (zeros, ninf, pinf))
            cnt, a, b = fold(cnt, jnp.sum), fold(a, jnp.max), fold(b, jnp.min)
            enough = cnt >= float(TOPK)
            new_lo = jnp.where(enough, b, jnp.where(cnt == float(TOPK - 1), a, lo))
            new_hi = jnp.where(enough, jnp.where(cnt == float(TOPK), b, hi), a)
            active = jnp.max(jnp.where(new_lo < new_hi, 1, 0))
            return new_lo, new_hi, active

        first_active = jnp.max(jnp.where(lo < hi, 1, 0))
        thr, _, _ = lax.while_loop(cond, step, (lo, hi, first_active))

        n_ge = fold(scan(lambda _, s, c: c + csum(jnp.where(s >= thr, 1.0, 0.0)), zeros), jnp.sum)
        thr_ref[...] = thr

        @pl.when(jnp.max(jnp.where(n_ge > float(TOPK), 1, 0)) > 0)
        def _():
            n_gt = fold(scan(lambda _, s, c: c + csum(jnp.where(s > thr, 1.0, 0.0)), zeros), jnp.sum)
            need = float(TOPK) - n_gt
            lower = jnp.where(lax.broadcasted_iota(jnp.int32, (KC, KC), 1)
                              < lax.broadcasted_iota(jnp.int32, (KC, KC), 0), 1.0, 0.0).astype(BF16)

            def sel_fn(kc, s, run):
                eq = s == thr
                eqf = jnp.where(eq, 1.0, 0.0)
                before = _dot(lower, eqf.astype(BF16)) + run
                sel = (s > thr) | (eq & (before < need))
                s_ref[_chunk(kc), :] = jnp.where(sel, 0.0, NEG_BIG)
                return run + jnp.sum(eqf, axis=0, keepdims=True)

            scan(sel_fn, jnp.zeros((1, TQ), F32))
            thr_ref[...] = jnp.full(thr_ref.shape, 0.5 * NEG_BIG, F32)

    _softmax_init(m_ref, l_ref, acc_ref)
    rep = A_HEADS // A_KV_HEADS
    row64 = lax.broadcasted_iota(jnp.int32, (AUG - HEAD_DIM, TQ), 0)
    qpos = (qi * TQ + lax.broadcasted_iota(jnp.int32, (1, TQ), 1)).astype(F32)
    for h in range(A_HEADS):
        slope = np.float32(2.0 ** (-8.0 * (h + 1) / A_HEADS) * LOG2E)
        u1, u2, u3 = _split3_f32(-slope * qpos)
        spare = jnp.zeros((AUG - HEAD_DIM, TQ), F32)
        for p, s_p in enumerate(_bf16_pieces(slope)):
            spare = jnp.where(row64 == p, 128.0 * s_p, spare)
            spare = jnp.where(row64 == 3 + p, s_p, spare)
        spare = jnp.where(row64 == 6, u1, spare)
        spare = jnp.where(row64 == 7, u2, spare)
        spare = jnp.where(row64 == 8, u3, spare)
        w_ref[h] = jnp.concatenate([qt_ref[0, h * HEAD_DIM:(h + 1) * HEAD_DIM, :],
                                    spare.astype(BF16)], axis=0)

    def attn_tiles(chunks):
        rows = [_chunk(kc) for kc in chunks]
        lhs = [[k_ref[0, r, g * AUG:(g + 1) * AUG] + posx_ref[r, :] for g in range(A_KV_HEADS)]
               for r in rows]
        bias = [jnp.where(s_ref[r, :] >= thr_ref[...], 0.0, NEG_BIG) for r in rows]

        def score_fn(item):
            j, h = item
            return _dot(lhs[j][h // rep], w_ref[h]) + bias[j]

        def vt_fn(item):
            j, h = item
            g = h // rep
            return vt_ref[0, g * HEAD_DIM:(g + 1) * HEAD_DIM, rows[j]]

        items = [(j, h) for j in range(len(chunks)) for h in range(A_HEADS)]
        _attend_chunks(items, score_fn, vt_fn, m_ref, l_ref, acc_ref, DSA_LOOKAHEAD)

    def attn_body(pair, _):
        attn_tiles([2 * pair, 2 * pair + 1])
        return 0

    lax.fori_loop(0, nch // 2, attn_body, 0)

    @pl.when(nch % 2 == 1)
    def _():
        attn_tiles([nch - 1])

    _softmax_finish(o_ref, A_HEADS, l_ref, acc_ref)


def _dsa_attn(iqt, ik4, wft, aqt, ak, posx, avt):
    bsz, w, seq = aqt.shape
    blk_t = lambda b, i: (b, 0, i)
    full = lambda b, i: (b, 0, 0)
    return pl.pallas_call(
        _dsa_kernel,
        out_shape=jax.ShapeDtypeStruct((bsz, w, seq), F32),
        grid=(bsz, seq // TQ),
        in_specs=[pl.BlockSpec((1, IDX_HEADS * IDX_K, TQ), blk_t),
                  pl.BlockSpec((1, seq, IDX_K), full),
                  pl.BlockSpec((1, 16, TQ), blk_t),
                  pl.BlockSpec((1, w, TQ), blk_t),
                  pl.BlockSpec((1, seq, A_KV_HEADS * AUG), full),
                  pl.BlockSpec((seq, AUG), lambda b, i: (0, 0)),
                  pl.BlockSpec((1, 128, seq), full)],
        out_specs=pl.BlockSpec((1, w, TQ), blk_t),
        scratch_shapes=[pltpu.VMEM((seq, TQ), F32), pltpu.VMEM((1, TQ), F32),
                        pltpu.VMEM((2, 8, TQ), F32), pltpu.VMEM((A_HEADS, AUG, TQ), BF16),
                        pltpu.VMEM((A_HEADS, 1, TQ), F32), pltpu.VMEM((A_HEADS, 1, TQ), F32),
                        pltpu.VMEM((A_HEADS * HEAD_DIM, TQ), F32)],
        compiler_params=pltpu.CompilerParams(dimension_semantics=("arbitrary", "arbitrary"),
                                             vmem_limit_bytes=VMEM_LIMIT),
        name="dsa_attn",
    )(iqt, ik4, wft, aqt, ak, posx, avt)


def _rms_cols(xt, g_col):
    return xt * lax.rsqrt(jnp.mean(xt * xt, axis=0, keepdims=True) + EPS) * g_col


MOE_TMG = 256
MOE_TMC = 256
ROW_TILE = 8
ROUTE_COLS = 128


def _first_index_of_max(vals, row, big):
    m = jnp.max(vals, axis=0, keepdims=True)
    idx = jnp.min(jnp.where(vals == m, row, big), axis=0, keepdims=True)
    return m, idx


def _route(h, wr_hi_ref, wr_lo_ref, br_ref):
    h_hi, h_lo = _split_bf16(h)
    logits_t = (_dot3(h_hi, h_lo, wr_hi_ref[...], wr_lo_ref[...]) + br_ref[...]).T
    tm = h.shape[0]
    ninf = -jnp.inf
    row8 = lax.broadcasted_iota(jnp.int32, (8, tm), 0)
    gl = jnp.where(row8 < N_GROUPS, logits_t[0:8], ninf)
    gmax, gsel = _first_index_of_max(gl, row8, 1 << 20)
    p_group = 1.0 / jnp.sum(jnp.exp(gl - gmax), axis=0, keepdims=True)
    expert = lax.broadcasted_iota(jnp.int32, (N_EXPERTS, tm), 0)
    assert EXPERTS_PER_GROUP == 8
    el = jnp.where((expert >> 3) == gsel, logits_t[32:32 + N_EXPERTS], ninf)
    v1, i1 = _first_index_of_max(el, expert, 1 << 20)
    el2 = jnp.where(expert == i1, ninf, el)
    v2, i2 = _first_index_of_max(el2, expert, 1 << 20)
    e2 = jnp.exp(v2 - v1)
    w1 = 1.0 / (1.0 + e2)
    w2 = e2 / (1.0 + e2)
    return i1, i2, w1 * p_group, w2 * p_group


def _token_rows(s, n_tokens, first_token=0):
    return pl.ds(first_token * ROW_TILE + s, n_tokens, stride=ROW_TILE)


def _to_token_tiles(ref, x):
    for s in range(ROW_TILE):
        ref[_token_rows(s, x.shape[0]), :] = x[:, s * 128:(s + 1) * 128]


def _token_tile(ref, t):
    start = t * ROW_TILE if isinstance(t, int) else pl.multiple_of(t * ROW_TILE, ROW_TILE)
    return ref.at[pl.ds(start, ROW_TILE), :]


def _moe_route_kernel(oat_ref, obt_ref, x_ref, mod_ref, ga_ref, gb_ref, wo_ref,
                      gf_ref, wr_hi_ref, wr_lo_ref, br_ref,
                      x1_ref, h_ref, info_ref, infot_ref, cnt_ref, run_ref):
    @pl.when(pl.program_id(0) == 0)
    def _():
        run_ref[...] = jnp.zeros_like(run_ref)

    oa = _rms_cols(oat_ref[0], ga_ref[...]).astype(BF16)
    ob = _rms_cols(obt_ref[0], gb_ref[...]).astype(BF16)
    y = _dot_tn(oa, wo_ref[0:512, :]) + _dot_tn(ob, wo_ref[512:1024, :])
    x1 = x_ref[...] + mod_ref[0, 2:3, :] * y
    x1_ref[...] = x1

    h = _rms(x1, gf_ref[...]) * (1.0 + mod_ref[0, 4:5, :]) + mod_ref[0, 3:4, :]
    _to_token_tiles(h_ref, h)
    i1, i2, w1, w2 = _route(h, wr_hi_ref, wr_lo_ref, br_ref)
    tm = h.shape[0]
    expert = lax.broadcasted_iota(jnp.int32, (N_EXPERTS, tm), 0)
    picked = jnp.where((expert == i1) | (expert == i2), 1.0, 0.0)
    earlier = jnp.where(lax.broadcasted_iota(jnp.int32, (tm, tm), 0)
                        < lax.broadcasted_iota(jnp.int32, (tm, tm), 1), 1.0, 0.0).astype(BF16)
    before = _dot(picked.astype(BF16), earlier) + run_ref[...]
    rank1 = jnp.sum(jnp.where(expert == i1, before, 0.0), axis=0, keepdims=True)
    rank2 = jnp.sum(jnp.where(expert == i2, before, 0.0), axis=0, keepdims=True)
    run_ref[...] += jnp.sum(picked, axis=1, keepdims=True)
    cnt_ref[...] = run_ref[...]
    row8 = lax.broadcasted_iota(jnp.int32, (8, tm), 0)
    info_t = jnp.where(row8 == 0, i1.astype(F32), 0.0)
    info_t = jnp.where(row8 == 1, i2.astype(F32), info_t)
    info_t = jnp.where(row8 == 2, rank1, info_t)
    info_t = jnp.where(row8 == 3, rank2, info_t)
    info_t = jnp.where(row8 == 4, w1, info_t)
    info_t = jnp.where(row8 == 5, w2, info_t)
    infot_ref[...] = info_t
    info_ref[...] = jnp.concatenate([info_t, jnp.zeros((ROUTE_COLS - 8, tm), F32)], axis=0).T


def _moe_route(oat, obt, x2, mod3, g_out_a, g_out_b, w_out_bf, g_ffn, wr_hi, wr_lo, b_route, seq):
    n, d = x2.shape
    tm = 512
    per_b = seq // tm
    row = lambda i: (i, 0)
    const = lambda i: (0, 0)
    blk_t = lambda i: (i // per_b, 0, i % per_b)
    return pl.pallas_call(
        _moe_route_kernel,
        out_shape=[jax.ShapeDtypeStruct((n, d), F32),
                   jax.ShapeDtypeStruct((n * ROW_TILE, d // ROW_TILE), F32),
                   jax.ShapeDtypeStruct((n, ROUTE_COLS), F32), jax.ShapeDtypeStruct((8, n), F32),
                   jax.ShapeDtypeStruct((N_EXPERTS, 1), F32)],
        grid=(n // tm,),
        in_specs=[pl.BlockSpec((1, 512, tm), blk_t), pl.BlockSpec((1, 512, tm), blk_t),
                  pl.BlockSpec((tm, d), row),
                  pl.BlockSpec((1, N_MOD, d), lambda i: (i // per_b, 0, 0)),
                  pl.BlockSpec((512, 1), const), pl.BlockSpec((512, 1), const),
                  pl.BlockSpec((d, d), const),
                  pl.BlockSpec((1, d), const),
                  pl.BlockSpec((d, ROUTE_COLS), const), pl.BlockSpec((d, ROUTE_COLS), const),
                  pl.BlockSpec((1, ROUTE_COLS), const)],
        out_specs=[pl.BlockSpec((tm, d), row),
                   pl.BlockSpec((tm * ROW_TILE, d // ROW_TILE), row),
                   pl.BlockSpec((tm, ROUTE_COLS), row), pl.BlockSpec((8, tm), lambda i: (0, i)),
                   pl.BlockSpec((N_EXPERTS, 1), const)],
        scratch_shapes=[pltpu.VMEM((N_EXPERTS, 1), F32)],
        compiler_params=pltpu.CompilerParams(dimension_semantics=("arbitrary",),
                                             vmem_limit_bytes=VMEM_LIMIT),
        name="mix_out_moe_route",
    )(oat, obt, x2, mod3, g_out_a.reshape(-1, 1), g_out_b.reshape(-1, 1), w_out_bf,
      g_ffn.reshape(1, d), wr_hi, wr_lo, b_route)


def _wait_token_copies(src_hbm, dst, sem, n_tokens):
    pltpu.make_async_copy(src_hbm.at[pl.ds(0, n_tokens * ROW_TILE), :], dst, sem).wait()


def _moe_scatter_kernel(dest_ref, pad_start_ref, pad_len_ref, nu_ref, h_ref, xs_hbm, zero_ref, sem):
    i = pl.program_id(0)
    n_pairs = 2 * MOE_TMC

    @pl.when(i == 0)
    def _():
        zero_ref[...] = jnp.zeros_like(zero_ref)

        def pad_copies(e):
            n = pad_len_ref[e]
            size = MOE_TMG // 2
            while size >= 1:
                first = pad_start_ref[e] + (n & ~(2 * size - 1))
                copy = pltpu.make_async_copy(
                    zero_ref.at[pl.ds(0, size * ROW_TILE), :],
                    xs_hbm.at[pl.ds(pl.multiple_of(first * ROW_TILE, ROW_TILE), size * ROW_TILE), :],
                    sem.at[1])
                yield (n & size) != 0, copy
                size //= 2

        for e in range(N_EXPERTS):
            for present, copy in pad_copies(e):
                pl.when(present)(copy.start)
        for e in range(N_EXPERTS):
            for present, copy in pad_copies(e):
                pl.when(present)(copy.wait)

        def row_tile(t):
            return xs_hbm.at[pl.ds(pl.multiple_of(t * (MOE_TMG * ROW_TILE), MOE_TMG * ROW_TILE),
                                   MOE_TMG * ROW_TILE), :]

        n_tiles = xs_hbm.shape[0] // (MOE_TMG * ROW_TILE)

        def fill_tile(t, _):
            pltpu.make_async_copy(zero_ref, row_tile(t), sem.at[1]).start()
            return 0

        def drain_tile(t, _):
            pltpu.make_async_copy(zero_ref, row_tile(0), sem.at[1]).wait()
            return 0

        lax.fori_loop(nu_ref[0], n_tiles, fill_tile, 0)
        lax.fori_loop(nu_ref[0], n_tiles, drain_tile, 0)

    for r in range(MOE_TMC):
        for j in range(2):
            pltpu.make_async_copy(_token_tile(h_ref, r),
                                  _token_tile(xs_hbm, dest_ref[i * n_pairs + j * MOE_TMC + r]),
                                  sem.at[0]).start(priority=j)
    _wait_token_copies(xs_hbm, xs_hbm.at[pl.ds(0, n_pairs * ROW_TILE), :], sem.at[0], n_pairs)


def _moe_scatter(dest, pad_start, pad_len, n_used, h2, n_rows):
    n = h2.shape[0] // ROW_TILE
    grid_spec = pltpu.PrefetchScalarGridSpec(
        num_scalar_prefetch=4,
        grid=(n // MOE_TMC,),
        in_specs=[pl.BlockSpec((MOE_TMC * ROW_TILE, h2.shape[1]), lambda i, *_: (i, 0))],
        out_specs=pl.BlockSpec(memory_space=pl.ANY),
        scratch_shapes=[pltpu.VMEM((MOE_TMG * ROW_TILE, h2.shape[1]), F32),
                        pltpu.SemaphoreType.DMA((2,))])
    return pl.pallas_call(
        _moe_scatter_kernel,
        out_shape=jax.ShapeDtypeStruct((n_rows * ROW_TILE, h2.shape[1]), F32),
        grid_spec=grid_spec,
        compiler_params=pltpu.CompilerParams(dimension_semantics=("arbitrary",),
                                             vmem_limit_bytes=VMEM_LIMIT),
        name="moe_scatter",
    )(dest, pad_start, pad_len, n_used, h2)


TILES_PER_STEP = 2


def _moe_expert_kernel(te_ref, nu_ref, x_ref, *refs):
    n = TILES_PER_STEP
    wg_refs, wu_refs, wd_refs = (refs[k * n:(k + 1) * n] for k in range(3))
    y_ref, xs_ref, wgb_ref, wub_ref, wdb_ref = refs[3 * n:]
    i = pl.program_id(0)
    step_used = n * i < nu_ref[0]

    @pl.when(jnp.logical_not(step_used))
    def _():
        y_ref[...] = jnp.zeros_like(y_ref)

    for slot in range(n):
        tile = n * i + slot
        new_expert = jnp.logical_or(i == 0, te_ref[tile] != te_ref[jnp.maximum(tile - n, 0)])

        @pl.when(jnp.logical_and(step_used, new_expert))
        def _(slot=slot):
            wgb_ref[slot] = wg_refs[slot][0].astype(BF16)
            wub_ref[slot] = wu_refs[slot][0].astype(BF16)
            wdb_ref[slot] = wd_refs[slot][0].astype(BF16)

    @pl.when(step_used)
    def _():
        for slot in range(n):
            for s in range(ROW_TILE):
                rows = _token_rows(s, MOE_TMG, first_token=slot * MOE_TMG)
                xs_ref[slot, :, s * 128:(s + 1) * 128] = x_ref[rows, :].astype(BF16)
        acts = []
        for slot in range(n):
            x = xs_ref[slot]
            hg = _dot(x, wgb_ref[slot])
            hu = _dot(x, wub_ref[slot])
            acts.append((hg * jax.nn.sigmoid(hg) * hu).astype(BF16))
        for slot in range(n):
            y = _dot(acts[slot], wdb_ref[slot])
            for s in range(ROW_TILE):
                rows = _token_rows(s, MOE_TMG, first_token=slot * MOE_TMG)
                y_ref[rows, :] = y[:, s * 128:(s + 1) * 128]


def _moe_experts(tile_expert, n_used, xsorted, wg, wu, wd):
    n_rows = xsorted.shape[0] // ROW_TILE
    d, ff = wg.shape[1], wg.shape[2]
    n = TILES_PER_STEP
    block = (n * MOE_TMG * ROW_TILE, xsorted.shape[1])
    slots = range(n)
    w_in_specs = [pl.BlockSpec((1, d, ff), lambda i, te, nu, s=s: (te[n * i + s], 0, 0)) for s in slots]
    w_out_specs = [pl.BlockSpec((1, ff, d), lambda i, te, nu, s=s: (te[n * i + s], 0, 0)) for s in slots]
    grid_spec = pltpu.PrefetchScalarGridSpec(
        num_scalar_prefetch=2,
        grid=(n_rows // (MOE_TMG * n),),
        in_specs=[pl.BlockSpec(block, lambda i, te, nu: (jnp.minimum(i, (nu[0] - 1) // n), 0))]
        + w_in_specs + w_in_specs + w_out_specs,
        out_specs=pl.BlockSpec(block, lambda i, te, nu: (i, 0)),
        scratch_shapes=[pltpu.VMEM((n, MOE_TMG, d), BF16), pltpu.VMEM((n, d, ff), BF16),
                        pltpu.VMEM((n, d, ff), BF16), pltpu.VMEM((n, ff, d), BF16)])
    return pl.pallas_call(
        _moe_expert_kernel,
        out_shape=jax.ShapeDtypeStruct(xsorted.shape, F32),
        grid_spec=grid_spec,
        compiler_params=pltpu.CompilerParams(dimension_semantics=("arbitrary",),
                                             vmem_limit_bytes=VMEM_LIMIT),
        name="moe_experts",
    )(tile_expert, n_used, xsorted, *([wg] * n), *([wu] * n), *([wd] * n))


def _moe_combine_kernel(dest_ref, y_hbm, x_ref, info_ref, mod_ref, gfin_ref, o_ref, ybuf, x2_ref, sem):
    i = pl.program_id(0)
    nt = pl.num_programs(0)
    slot = lax.rem(i, 2)
    n_pairs = 2 * MOE_TMC

    def start_gather(tile, to_slot):
        for r in range(n_pairs):
            pltpu.make_async_copy(_token_tile(y_hbm, dest_ref[tile * n_pairs + r]),
                                  _token_tile(ybuf.at[to_slot], r), sem.at[to_slot]).start(priority=r % 2)

    @pl.when(i == 0)
    def _():
        start_gather(0, 0)

    _wait_token_copies(y_hbm, ybuf.at[slot], sem.at[slot], n_pairs)

    @pl.when(i + 1 < nt)
    def _():
        start_gather(i + 1, 1 - slot)

    w1 = info_ref[:, 4:5]
    w2 = info_ref[:, 5:6]
    sumsq = jnp.zeros((MOE_TMC, 1), F32)
    for s in range(ROW_TILE):
        cols = slice(s * 128, (s + 1) * 128)
        y = (w1 * ybuf[slot, _token_rows(s, MOE_TMC), :]
             + w2 * ybuf[slot, _token_rows(s, MOE_TMC, first_token=MOE_TMC), :])
        x2 = x_ref[:, cols] + mod_ref[0, 5:6, cols] * y
        x2_ref[:, cols] = x2
        sumsq = sumsq + jnp.sum(x2 * x2, axis=-1, keepdims=True)
    d = x2_ref.shape[1]
    o_ref[...] = x2_ref[...] * lax.rsqrt(sumsq / d + EPS) * gfin_ref[...]


def _moe_combine(dest, ysorted, x1, info, mod3, g_final, seq):
    n, d = x1.shape
    per_b = seq // MOE_TMC
    grid_spec = pltpu.PrefetchScalarGridSpec(
        num_scalar_prefetch=1,
        grid=(n // MOE_TMC,),
        in_specs=[pl.BlockSpec(memory_space=pl.ANY),
                  pl.BlockSpec((MOE_TMC, d), lambda i, ds: (i, 0)),
                  pl.BlockSpec((MOE_TMC, ROUTE_COLS), lambda i, ds: (i, 0)),
                  pl.BlockSpec((1, N_MOD, d), lambda i, ds: (i // per_b, 0, 0)),
                  pl.BlockSpec((1, d), lambda i, ds: (0, 0))],
        out_specs=pl.BlockSpec((MOE_TMC, d), lambda i, ds: (i, 0)),
        scratch_shapes=[pltpu.VMEM((2, 2 * MOE_TMC * ROW_TILE, ysorted.shape[1]), F32),
                        pltpu.VMEM((MOE_TMC, d), F32), pltpu.SemaphoreType.DMA((2,))])
    return pl.pallas_call(
        _moe_combine_kernel,
        out_shape=jax.ShapeDtypeStruct((n, d), F32),
        grid_spec=grid_spec,
        compiler_params=pltpu.CompilerParams(dimension_semantics=("arbitrary",),
                                             vmem_limit_bytes=VMEM_LIMIT),
        name="moe_combine",
    )(dest, ysorted, x1, info, mod3, g_final.reshape(1, d))


def _mix_out_and_moe(oat, obt, x2, mod3, g_out_a, g_out_b, w_out_bf, g_ffn, wr_hi, wr_lo, b_route,
                     wg, wu, wd, g_final, seq):
    n, d = x2.shape
    x1, h2, info, infot, counts = _moe_route(oat, obt, x2, mod3, g_out_a, g_out_b, w_out_bf,
                                             g_ffn, wr_hi, wr_lo, b_route, seq)

    e1, e2, rank1, rank2 = [infot[k].astype(jnp.int32) for k in range(4)]
    cnt = counts[:, 0].astype(jnp.int32)
    padded = ((cnt + MOE_TMG - 1) // MOE_TMG) * MOE_TMG
    seg_end = jnp.cumsum(padded)
    expert_ids = jnp.arange(N_EXPERTS, dtype=jnp.int32)

    def seg_start_of(e):
        return jnp.sum(jnp.where(expert_ids[None, :] < e[:, None], padded[None, :], 0), axis=1)

    dest1 = seg_start_of(e1) + rank1
    dest2 = seg_start_of(e2) + rank2
    n_rows = 2 * n + N_EXPERTS * MOE_TMG
    tile_start = jnp.arange(n_rows // MOE_TMG, dtype=jnp.int32) * MOE_TMG
    tile_expert = jnp.minimum(jnp.sum((tile_start[:, None] >= seg_end[None, :]).astype(jnp.int32), axis=1),
                              N_EXPERTS - 1)
    n_used = (seg_end[N_EXPERTS - 1:] // MOE_TMG).astype(jnp.int32)
    dest = jnp.concatenate([dest1.reshape(-1, MOE_TMC), dest2.reshape(-1, MOE_TMC)], axis=1).reshape(-1)

    seg_start = seg_end - padded
    xsorted = _moe_scatter(dest, seg_start + cnt, padded - cnt, n_used, h2, n_rows)
    ysorted = _moe_experts(tile_expert, n_used, xsorted, wg, wu, wd)
    return _moe_combine(dest, ysorted, x1, info, mod3, g_final, seq)


def _layer(x3, c, w_ada, b_ada, g_mix, w_in, b_forget, g_out_a, g_out_b, w_out,
           g_ffn, w_group, b_group, w_router, b_router, w_gate, w_up, w_down, g_final):
    bsz, seq, d = x3.shape
    mod3 = _ada_mod(c, w_ada, b_ada).reshape(bsz, N_MOD, d)

    w_t = w_in.T

    def pad_heads(w, n_heads):
        w = w.reshape(d, n_heads, HEAD_DIM)
        return jnp.concatenate([w, jnp.zeros_like(w)], axis=-1).reshape(d, n_heads * AUG)

    w_wf_t = jnp.concatenate([w_t[1344:1352], w_t[2888:2896]], axis=0)
    weights = [w_t[0:512].astype(BF16), pad_heads(w_in[:, 512:640], A_KV_HEADS).astype(BF16),
               w_t[640:768].astype(BF16), w_t[1352:1864].astype(BF16),
               w_in[:, 1864:2376].astype(BF16), w_t[2376:2888].astype(BF16),
               w_t[768:1280].astype(BF16), w_in[:, 1280:1344].astype(BF16), *_split_bf16(w_wf_t)]
    aqt, ak, avt, bqt, bk, bvt, iqt, ik4, wft = _in_proj(x3, mod3, g_mix, weights)

    cumt, kaug = _fox_cum(wft, b_forget, bk)
    obt = _fox_attn(bqt, kaug, bvt, cumt)

    pos = jnp.arange(seq, dtype=jnp.int32)[:, None]
    lane = jnp.arange(AUG, dtype=jnp.int32)[None, :] - HEAD_DIM
    posx = jnp.where((lane >= 0) & (lane < 3), pos >> 7,
                     jnp.where((lane >= 3) & (lane < 6), pos & 127,
                               jnp.where((lane >= 6) & (lane < 9), 1, 0))).astype(BF16)
    oat = _dsa_attn(iqt, ik4, wft, aqt, ak, posx, avt)

    w_r = jnp.concatenate([w_group, jnp.zeros((d, 32 - N_GROUPS), F32),
                           jnp.transpose(w_router, (1, 0, 2)).reshape(d, N_EXPERTS),
                           jnp.zeros((d, ROUTE_COLS - 64), F32)], axis=1)
    b_r = jnp.concatenate([b_group, jnp.zeros((32 - N_GROUPS,), F32), b_router.reshape(-1),
                           jnp.zeros((ROUTE_COLS - 64,), F32)]).reshape(1, ROUTE_COLS)
    wr_hi, wr_lo = _split_bf16(w_r)
    out = _mix_out_and_moe(oat, obt, x3.reshape(bsz * seq, d), mod3, g_out_a, g_out_b,
                           w_out.astype(BF16), g_ffn, wr_hi, wr_lo, b_r, w_gate, w_up, w_down,
                           g_final, seq)
    return out.reshape(bsz, seq, d)


def kernel(x, c, w_ada, b_ada, g_mix, w_in, b_forget, g_out_a, g_out_b, w_out, g_ffn, w_group,
           b_group, w_router, b_router, w_gate, w_up, w_down, g_final):
    depth = w_ada.shape[0]
    assert depth == 1, "final norm is fused into the single layer's MoE kernel"
    return _layer(x, c, w_ada[0], b_ada[0], g_mix[0], w_in[0], b_forget[0], g_out_a[0], g_out_b[0],
                  w_out[0], g_ffn[0], w_group[0], b_group[0], w_router[0], b_router[0], w_gate[0],
                  w_up[0], w_down[0], g_final)
```

```python
import math

import jax
import jax.numpy as jnp
import numpy as np
from jax import lax
from jax.experimental import pallas as pl
from jax.experimental.pallas import tpu as pltpu

F32 = jnp.float32
BF16 = jnp.bfloat16

EPS = 1e-6
A_HEADS = 8
A_KV_HEADS = 2
HEAD_DIM = 64
IDX_HEADS = 8
IDX_DIM = 64
TOPK = 256
B_HEADS = 8
N_GROUPS = 4
EXPERTS_PER_GROUP = 8
N_EXPERTS = N_GROUPS * EXPERTS_PER_GROUP
N_MOD = 6

NEG_BIG = -1e30
LOWEST = float(np.finfo(np.float32).min)
LOG2E = math.log2(math.e)
Q_SCALE = HEAD_DIM ** -0.5 * LOG2E
VMEM_LIMIT = 48 * 1024 * 1024

TQ = 256
KC = 256
AUG = 128
IDX_K = IDX_DIM
ONES_ROWS = 16
COARSE_STEPS = 14
FOX_LOOKAHEAD = 6
DSA_LOOKAHEAD = 4
assert TQ == KC == TOPK


def _split_bf16(x):
    hi = x.astype(BF16)
    lo = (x - hi.astype(F32)).astype(BF16)
    return hi, lo


def _split3_f32(x):
    p1 = x.astype(BF16).astype(F32)
    r1 = x - p1
    p2 = r1.astype(BF16).astype(F32)
    p3 = (r1 - p2).astype(BF16).astype(F32)
    return p1, p2, p3


def _dot(a, b):
    return jnp.dot(a, b, preferred_element_type=F32)


def _dot_nt(a, b):
    return lax.dot_general(a, b, (((1,), (1,)), ((), ())), preferred_element_type=F32)


def _dot_tn(a, b):
    return lax.dot_general(a, b, (((0,), (0,)), ((), ())), preferred_element_type=F32)


def _dot3(a_hi, a_lo, b_hi, b_lo):
    return _dot(a_hi, b_hi) + _dot(a_lo, b_hi) + _dot(a_hi, b_lo)


def _dot3_nt(a_hi, a_lo, b_hi, b_lo):
    return _dot_nt(a_hi, b_hi) + _dot_nt(a_lo, b_hi) + _dot_nt(a_hi, b_lo)


def _rms(x, g):
    return x * lax.rsqrt(jnp.mean(x * x, axis=-1, keepdims=True) + EPS) * g


def _chunk(kc):
    return pl.ds(pl.multiple_of(kc * KC, KC), KC)


def _key_minus_query():
    return (lax.broadcasted_iota(jnp.int32, (KC, TQ), 0)
            - lax.broadcasted_iota(jnp.int32, (KC, TQ), 1))


def _ada_kernel(c_ref, w_ref, b_ref, o_ref):
    c = c_ref[...]
    s = c * jax.nn.sigmoid(c)
    s_hi, s_lo = _split_bf16(s)
    w_hi, w_lo = _split_bf16(w_ref[...])
    o_ref[...] = _dot3(s_hi, s_lo, w_hi, w_lo) + b_ref[...]


def _ada_mod(c, w_ada, b_ada):
    bsz, d = c.shape
    n = w_ada.shape[1]
    tn = 1024
    return pl.pallas_call(
        _ada_kernel,
        out_shape=jax.ShapeDtypeStruct((bsz, n), F32),
        grid=(n // tn,),
        in_specs=[pl.BlockSpec((bsz, d), lambda j: (0, 0)),
                  pl.BlockSpec((d, tn), lambda j: (0, j)),
                  pl.BlockSpec((1, tn), lambda j: (0, j))],
        out_specs=pl.BlockSpec((bsz, tn), lambda j: (0, j)),
        compiler_params=pltpu.CompilerParams(dimension_semantics=("arbitrary",),
                                             vmem_limit_bytes=VMEM_LIMIT),
        name="ada_mod",
    )(c, w_ada, b_ada.reshape(1, n))


def _in_proj_kernel(x_ref, mod_ref, g_ref,
                    waq_ref, wak_ref, wav_ref, wbq_ref, wbk_ref, wbv_ref,
                    wiq_ref, wik_ref, wwfh_ref,
                    aqt_ref, ak_ref, avt_ref, bqt_ref, bk_ref, bvt_ref, iqt_ref, ik_ref, wft_ref):
    x = x_ref[0]
    h = _rms(x, g_ref[...]) * (1.0 + mod_ref[0, 1:2, :]) + mod_ref[0, 0:1, :]
    h_hi, h_lo = _split_bf16(h)
    aqt_ref[0] = (_dot_nt(waq_ref[...], h_hi) * Q_SCALE).astype(BF16)
    ak_ref[0] = _dot(h_hi, wak_ref[...]).astype(BF16)
    avt_ref[0] = _dot_nt(wav_ref[...], h_hi).astype(BF16)
    bqt_ref[0] = (_dot_nt(wbq_ref[...], h_hi) * Q_SCALE).astype(BF16)
    bk_ref[0] = _dot(h_hi, wbk_ref[...]).astype(BF16)
    bvt_ref[0] = _dot_nt(wbv_ref[...], h_hi).astype(BF16)
    nq = IDX_HEADS * IDX_DIM
    both = _dot_nt(wiq_ref[...], h_hi)
    iqt_ref[0] = both[0:nq].astype(BF16)
    wft_ref[0] = both[nq:nq + 16] + both[nq + 16:nq + 32] + _dot_nt(wwfh_ref[...], h_lo)
    ik_ref[0] = _dot(h_hi, wik_ref[...]).astype(BF16)


def _in_proj(x3, mod3, g_mix, weights):
    bsz, seq, d = x3.shape
    tm = 512
    blk_t = lambda b, i: (b, 0, i)
    blk_r = lambda b, i: (b, i, 0)
    const = lambda b, i: (0, 0)
    ak_w = A_KV_HEADS * AUG
    bk_w = B_HEADS * HEAD_DIM
    outs = [jax.ShapeDtypeStruct((bsz, 512, seq), BF16), jax.ShapeDtypeStruct((bsz, seq, ak_w), BF16),
            jax.ShapeDtypeStruct((bsz, 128, seq), BF16), jax.ShapeDtypeStruct((bsz, 512, seq), BF16),
            jax.ShapeDtypeStruct((bsz, seq, bk_w), BF16), jax.ShapeDtypeStruct((bsz, 512, seq), BF16),
            jax.ShapeDtypeStruct((bsz, IDX_HEADS * IDX_K, seq), BF16),
            jax.ShapeDtypeStruct((bsz, seq, IDX_K), BF16), jax.ShapeDtypeStruct((bsz, 16, seq), F32)]
    out_specs = [pl.BlockSpec((1, 512, tm), blk_t), pl.BlockSpec((1, tm, ak_w), blk_r),
                 pl.BlockSpec((1, 128, tm), blk_t), pl.BlockSpec((1, 512, tm), blk_t),
                 pl.BlockSpec((1, tm, bk_w), blk_r), pl.BlockSpec((1, 512, tm), blk_t),
                 pl.BlockSpec((1, IDX_HEADS * IDX_K, tm), blk_t),
                 pl.BlockSpec((1, tm, IDX_K), blk_r), pl.BlockSpec((1, 16, tm), blk_t)]
    return pl.pallas_call(
        _in_proj_kernel,
        out_shape=outs,
        grid=(bsz, seq // tm),
        in_specs=[pl.BlockSpec((1, tm, d), blk_r),
                  pl.BlockSpec((1, N_MOD, d), lambda b, i: (b, 0, 0)),
                  pl.BlockSpec((1, d), const)] + [pl.BlockSpec(w.shape, const) for w in weights],
        out_specs=out_specs,
        compiler_params=pltpu.CompilerParams(dimension_semantics=("arbitrary", "arbitrary"),
                                             vmem_limit_bytes=VMEM_LIMIT),
        name="in_proj",
    )(x3, mod3, g_mix.reshape(1, d), *weights)


CB = 256


def _cum_kernel(wft_ref, bfor_ref, k_ref, cumt_ref, kaug_ref):
    seq = wft_ref.shape[2]
    r = lax.broadcasted_iota(jnp.int32, (CB, CB), 0)
    cidx = lax.broadcasted_iota(jnp.int32, (CB, CB), 1)
    tri = jnp.where(r <= cidx, 1.0, 0.0).astype(BF16)
    row128 = lax.broadcasted_iota(jnp.int32, (AUG, CB), 0)
    ones_rows = jnp.where((row128 >= HEAD_DIM + 3) & (row128 < HEAD_DIM + 6), 1.0, 0.0)
    src = lax.broadcasted_iota(jnp.int32, (AUG, AUG), 0)
    dst = lax.broadcasted_iota(jnp.int32, (AUG, AUG), 1)
    place = [jnp.where((dst < HEAD_DIM) & (src == dst + half * HEAD_DIM), 1.0, 0.0).astype(BF16)
             for half in range(2)]
    carry = jnp.zeros((8, 1), F32)
    for blk in range(seq // CB):
        cols = slice(blk * CB, (blk + 1) * CB)
        z = wft_ref[0, 8:16, cols] + bfor_ref[...]
        logf = jnp.minimum(z, 0.0) - jnp.log(1.0 + jnp.exp(-jnp.abs(z)))
        p1, p2, p3 = _split3_f32(logf)
        pieces = jnp.concatenate([p1, p2, p3, jnp.zeros_like(p1)], axis=0).astype(BF16)
        parts = _dot(pieces, tri)
        cum = parts[0:8] + parts[8:16] + parts[16:24] + carry
        carry = cum[:, CB - 1:CB]
        cum2 = cum * LOG2E
        cumt_ref[0, :, cols] = cum2
        c1, c2, c3 = _split3_f32(cum2)
        for h in range(B_HEADS):
            spare = jnp.where(row128 == HEAD_DIM, -c1[h:h + 1], ones_rows)
            spare = jnp.where(row128 == HEAD_DIM + 1, -c2[h:h + 1], spare)
            spare = jnp.where(row128 == HEAD_DIM + 2, -c3[h:h + 1], spare)
            pair = k_ref[0, cols, (h // 2) * AUG:(h // 2 + 1) * AUG]
            k_h = _dot(pair, place[h % 2]).astype(BF16)
            kaug_ref[0, cols, h * AUG:(h + 1) * AUG] = k_h + spare.T.astype(BF16)


def _fox_cum(wft, b_forget, bk):
    bsz, _, seq = wft.shape
    nh = B_HEADS
    kw = bk.shape[-1]
    kaug_w = nh * AUG
    return pl.pallas_call(
        _cum_kernel,
        out_shape=[jax.ShapeDtypeStruct((bsz, nh, seq), F32),
                   jax.ShapeDtypeStruct((bsz, seq, kaug_w), BF16)],
        grid=(bsz,),
        in_specs=[pl.BlockSpec((1, 16, seq), lambda b: (b, 0, 0)),
                  pl.BlockSpec((nh, 1), lambda b: (0, 0)),
                  pl.BlockSpec((1, seq, kw), lambda b: (b, 0, 0))],
        out_specs=[pl.BlockSpec((1, nh, seq), lambda b: (b, 0, 0)),
                   pl.BlockSpec((1, seq, kaug_w), lambda b: (b, 0, 0))],
        compiler_params=pltpu.CompilerParams(dimension_semantics=("arbitrary",),
                                             vmem_limit_bytes=VMEM_LIMIT),
        name="fox_cum",
    )(wft, b_forget.reshape(nh, 1), bk)


def _softmax_init(m_ref, l_ref, acc_ref):
    m_ref[...] = jnp.full(m_ref.shape, NEG_BIG, F32)
    l_ref[...] = jnp.zeros(l_ref.shape, F32)
    acc_ref[...] = jnp.zeros(acc_ref.shape, F32)


def _attend_chunks(items, score_fn, vt_fn, m_ref, l_ref, acc_ref, lookahead):
    ahead = min(lookahead, len(items))
    scores = {i: score_fn(items[i]) for i in range(ahead)}
    ones = jnp.ones((ONES_ROWS, KC), BF16)
    for i, item in enumerate(items):
        if i + ahead < len(items):
            scores[i + ahead] = score_fn(items[i + ahead])
        s = scores.pop(i)
        h = item[1]
        m_old = m_ref[h]
        m_new = jnp.maximum(m_old, jnp.max(s, axis=0, keepdims=True))
        alpha = jnp.exp2(m_old - m_new)
        p = jnp.exp2(s - m_new).astype(BF16)
        pv = _dot(jnp.concatenate([vt_fn(item), ones], axis=0), p)
        m_ref[h] = m_new
        l_ref[h] = alpha * l_ref[h] + pv[HEAD_DIM:HEAD_DIM + 1, :]
        rows = slice(h * HEAD_DIM, (h + 1) * HEAD_DIM)
        acc_ref[rows, :] = alpha * acc_ref[rows, :] + pv[0:HEAD_DIM, :]


def _softmax_finish(o_ref, n_heads, l_ref, acc_ref):
    for h in range(n_heads):
        rows = slice(h * HEAD_DIM, (h + 1) * HEAD_DIM)
        o_ref[0, rows, :] = acc_ref[rows, :] / l_ref[h]


def _fox_kernel(qt_ref, k_ref, vt_ref, cumt_ref, o_ref, w_ref, m_ref, l_ref, acc_ref):
    qi = pl.program_id(1)
    _softmax_init(m_ref, l_ref, acc_ref)

    row64 = lax.broadcasted_iota(jnp.int32, (AUG - HEAD_DIM, TQ), 0)
    for h in range(B_HEADS):
        c1, c2, c3 = _split3_f32(cumt_ref[0, h:h + 1, :])
        spare = jnp.where(row64 < 3, 1.0, 0.0)
        spare = jnp.where(row64 == 3, c1, spare)
        spare = jnp.where(row64 == 4, c2, spare)
        spare = jnp.where(row64 == 5, c3, spare)
        w_ref[h] = jnp.concatenate([qt_ref[0, h * HEAD_DIM:(h + 1) * HEAD_DIM, :],
                                    spare.astype(BF16)], axis=0)

    def tiles(chunks):
        rows = [_chunk(kc) for kc, _ in chunks]

        def score_fn(item):
            j, h = item
            s = _dot(k_ref[0, rows[j], h * AUG:(h + 1) * AUG], w_ref[h])
            if chunks[j][1]:
                s = jnp.where(_key_minus_query() <= 0, s, NEG_BIG)
            return s

        def vt_fn(item):
            j, h = item
            return vt_ref[0, h * HEAD_DIM:(h + 1) * HEAD_DIM, rows[j]]

        items = [(j, h) for j in range(len(chunks)) for h in range(B_HEADS)]
        _attend_chunks(items, score_fn, vt_fn, m_ref, l_ref, acc_ref, FOX_LOOKAHEAD)

    def body(pair, _):
        tiles([(2 * pair, False), (2 * pair + 1, False)])
        return 0

    lax.fori_loop(0, qi // 2, body, 0)

    @pl.when(qi % 2 == 1)
    def _():
        tiles([(qi - 1, False), (qi, True)])

    @pl.when(qi % 2 == 0)
    def _():
        tiles([(qi, True)])

    _softmax_finish(o_ref, B_HEADS, l_ref, acc_ref)


def _fox_attn(bqt, kaug, bvt, cumt):
    bsz, w, seq = bqt.shape
    blk_t = lambda b, i: (b, 0, i)
    full = lambda b, i: (b, 0, 0)
    return pl.pallas_call(
        _fox_kernel,
        out_shape=jax.ShapeDtypeStruct((bsz, w, seq), F32),
        grid=(bsz, seq // TQ),
        in_specs=[pl.BlockSpec((1, w, TQ), blk_t),
                  pl.BlockSpec((1, seq, B_HEADS * AUG), full),
                  pl.BlockSpec((1, w, seq), full),
                  pl.BlockSpec((1, B_HEADS, TQ), blk_t)],
        out_specs=pl.BlockSpec((1, w, TQ), blk_t),
        scratch_shapes=[pltpu.VMEM((B_HEADS, AUG, TQ), BF16), pltpu.VMEM((B_HEADS, 1, TQ), F32),
                        pltpu.VMEM((B_HEADS, 1, TQ), F32), pltpu.VMEM((B_HEADS * HEAD_DIM, TQ), F32)],
        compiler_params=pltpu.CompilerParams(dimension_semantics=("arbitrary", "arbitrary"),
                                             vmem_limit_bytes=VMEM_LIMIT),
        name="fox_attn",
    )(bqt, kaug, bvt, cumt)


def _bf16_pieces(value):
    pieces = []
    rest = np.float32(value)
    for _ in range(3):
        piece = np.asarray(rest).astype(BF16).astype(np.float32)
        pieces.append(float(piece))
        rest = np.float32(rest - piece)
    return pieces


def _dsa_kernel(iqt_ref, ik_ref, wft_ref, qt_ref, k_ref, posx_ref, vt_ref, o_ref,
                s_ref, thr_ref, lohi_ref, w_ref, m_ref, l_ref, acc_ref):
    qi = pl.program_id(1)
    nch = qi + 1
    kmq = _key_minus_query()

    def part(x, op):
        return op(x.reshape(KC // 8, 8, TQ), axis=0)

    def fold(x, op):
        return op(x, axis=0, keepdims=True)

    lohi_ref[0] = jnp.full((8, TQ), jnp.inf, F32)
    lohi_ref[1] = jnp.full((8, TQ), -jnp.inf, F32)

    def score_chunk(kc):
        ik = ik_ref[0, _chunk(kc), :]
        acc = jnp.zeros((KC, TQ), F32)
        for h in range(IDX_HEADS):
            d = _dot(ik, iqt_ref[0, h * IDX_K:(h + 1) * IDX_K, :])
            acc = acc + wft_ref[0, h:h + 1, :] * jnp.maximum(d, 0.0)
        causal = kmq <= (qi - kc) * KC
        s_ref[_chunk(kc), :] = jnp.where(causal, acc, -jnp.inf)
        lohi_ref[0] = jnp.minimum(lohi_ref[0], part(jnp.where(causal, acc, jnp.inf), jnp.min))
        lohi_ref[1] = jnp.maximum(lohi_ref[1], part(jnp.where(causal, acc, -jnp.inf), jnp.max))

    def score_body(pair, _):
        score_chunk(2 * pair)
        score_chunk(2 * pair + 1)
        return 0

    lax.fori_loop(0, nch // 2, score_body, 0)

    @pl.when(nch % 2 == 1)
    def _():
        score_chunk(nch - 1)
        s_ref[_chunk(nch), :] = jnp.full((KC, TQ), -jnp.inf, F32)

    def scan(fn, init):
        def pair_body(pair, c):
            c = fn(2 * pair, s_ref[_chunk(2 * pair), :], c)
            return fn(2 * pair + 1, s_ref[_chunk(2 * pair + 1), :], c)

        return lax.fori_loop(0, (nch + 1) // 2, pair_body, init)

    @pl.when(qi == 0)
    def _():
        thr_ref[...] = jnp.full(thr_ref.shape, LOWEST, F32)

    @pl.when(qi > 0)
    def _():
        def cmin(x):
            return part(x, jnp.min)

        def cmax(x):
            return part(x, jnp.max)

        def csum(x):
            return part(x, jnp.sum)

        zeros = jnp.zeros((8, TQ), F32)
        pinf = jnp.full((8, TQ), jnp.inf, F32)
        ninf = jnp.full((8, TQ), -jnp.inf, F32)
        lo, hi = fold(lohi_ref[0], jnp.min), fold(lohi_ref[1], jnp.max)

        def coarse_step(_, carry):
            lo, hi = carry
            mid = lo + (hi - lo) * 0.5
            cnt = scan(lambda _, s, c: c + csum(jnp.where(s >= mid, 1.0, 0.0)), zeros)
            enough = fold(cnt, jnp.sum) >= float(TOPK)
            return jnp.where(enough, mid, lo), jnp.where(enough, hi, mid)

        lo, hi = lax.fori_loop(0, COARSE_STEPS, coarse_step, (lo, hi))

        def cond(carry):
            return carry[2] > 0

        def step(carry):
            lo, hi, _ = carry
            mid = lo + (hi - lo) * 0.5
            mid = jnp.where(mid <= lo, hi, mid)

            def fn(_, s, c):
                cnt, a, b = c
                ge = s >= mid
                cnt = cnt + csum(jnp.where(ge, 1.0, 0.0))
                b = jnp.minimum(b, cmin(jnp.where(ge, s, jnp.inf)))
                a = jnp.maximum(a, cmax(jnp.where(ge, -jnp.inf, s)))
                return cnt, a, b

            cnt, a, b = scan(fn, (zeros, ninf, pinf))
            cnt, a, b = fold(cnt, jnp.sum), fold(a, jnp.max), fold(b, jnp.min)
            enough = cnt >= float(TOPK)
            new_lo = jnp.where(enough, b, jnp.where(cnt == float(TOPK - 1), a, lo))
            new_hi = jnp.where(enough, jnp.where(cnt == float(TOPK), b, hi), a)
            active = jnp.max(jnp.where(new_lo < new_hi, 1, 0))
            return new_lo, new_hi, active

        first_active = jnp.max(jnp.where(lo < hi, 1, 0))
        thr, _, _ = lax.while_loop(cond, step, (lo, hi, first_active))

        n_ge = fold(scan(lambda _, s, c: c + csum(jnp.where(s >= thr, 1.0, 0.0)), zeros), jnp.sum)
        thr_ref[...] = thr

        @pl.when(jnp.max(jnp.where(n_ge > float(TOPK), 1, 0)) > 0)
        def _():
            n_gt = fold(scan(lambda _, s, c: c + csum(jnp.where(s > thr, 1.0, 0.0)), zeros), jnp.sum)
            need = float(TOPK) - n_gt
            lower = jnp.where(lax.broadcasted_iota(jnp.int32, (KC, KC), 1)
                              < lax.broadcasted_iota(jnp.int32, (KC, KC), 0), 1.0, 0.0).astype(BF16)

            def sel_fn(kc, s, run):
                eq = s == thr
                eqf = jnp.where(eq, 1.0, 0.0)
                before = _dot(lower, eqf.astype(BF16)) + run
                sel = (s > thr) | (eq & (before < need))
                s_ref[_chunk(kc), :] = jnp.where(sel, 0.0, NEG_BIG)
                return run + jnp.sum(eqf, axis=0, keepdims=True)

            scan(sel_fn, jnp.zeros((1, TQ), F32))
            thr_ref[...] = jnp.full(thr_ref.shape, 0.5 * NEG_BIG, F32)

    _softmax_init(m_ref, l_ref, acc_ref)
    rep = A_HEADS // A_KV_HEADS
    row64 = lax.broadcasted_iota(jnp.int32, (AUG - HEAD_DIM, TQ), 0)
    qpos = (qi * TQ + lax.broadcasted_iota(jnp.int32, (1, TQ), 1)).astype(F32)
    for h in range(A_HEADS):
        slope = np.float32(2.0 ** (-8.0 * (h + 1) / A_HEADS) * LOG2E)
        u1, u2, u3 = _split3_f32(-slope * qpos)
        spare = jnp.zeros((AUG - HEAD_DIM, TQ), F32)
        for p, s_p in enumerate(_bf16_pieces(slope)):
            spare = jnp.where(row64 == p, 128.0 * s_p, spare)
            spare = jnp.where(row64 == 3 + p, s_p, spare)
        spare = jnp.where(row64 == 6, u1, spare)
        spare = jnp.where(row64 == 7, u2, spare)
        spare = jnp.where(row64 == 8, u3, spare)
        w_ref[h] = jnp.concatenate([qt_ref[0, h * HEAD_DIM:(h + 1) * HEAD_DIM, :],
                                    spare.astype(BF16)], axis=0)

    def attn_tiles(chunks):
        rows = [_chunk(kc) for kc in chunks]
        lhs = [[k_ref[0, r, g * AUG:(g + 1) * AUG] + posx_ref[r, :] for g in range(A_KV_HEADS)]
               for r in rows]
        bias = [jnp.where(s_ref[r, :] >= thr_ref[...], 0.0, NEG_BIG) for r in rows]

        def score_fn(item):
            j, h = item
            return _dot(lhs[j][h // rep], w_ref[h]) + bias[j]

        def vt_fn(item):
            j, h = item
            g = h // rep
            return vt_ref[0, g * HEAD_DIM:(g + 1) * HEAD_DIM, rows[j]]

        items = [(j, h) for j in range(len(chunks)) for h in range(A_HEADS)]
        _attend_chunks(items, score_fn, vt_fn, m_ref, l_ref, acc_ref, DSA_LOOKAHEAD)

    def attn_body(pair, _):
        attn_tiles([2 * pair, 2 * pair + 1])
        return 0

    lax.fori_loop(0, nch // 2, attn_body, 0)

    @pl.when(nch % 2 == 1)
    def _():
        attn_tiles([nch - 1])

    _softmax_finish(o_ref, A_HEADS, l_ref, acc_ref)


def _dsa_attn(iqt, ik4, wft, aqt, ak, posx, avt):
    bsz, w, seq = aqt.shape
    blk_t = lambda b, i: (b, 0, i)
    full = lambda b, i: (b, 0, 0)
    return pl.pallas_call(
        _dsa_kernel,
        out_shape=jax.ShapeDtypeStruct((bsz, w, seq), F32),
        grid=(bsz, seq // TQ),
        in_specs=[pl.BlockSpec((1, IDX_HEADS * IDX_K, TQ), blk_t),
                  pl.BlockSpec((1, seq, IDX_K), full),
                  pl.BlockSpec((1, 16, TQ), blk_t),
                  pl.BlockSpec((1, w, TQ), blk_t),
                  pl.BlockSpec((1, seq, A_KV_HEADS * AUG), full),
                  pl.BlockSpec((seq, AUG), lambda b, i: (0, 0)),
                  pl.BlockSpec((1, 128, seq), full)],
        out_specs=pl.BlockSpec((1, w, TQ), blk_t),
        scratch_shapes=[pltpu.VMEM((seq, TQ), F32), pltpu.VMEM((1, TQ), F32),
                        pltpu.VMEM((2, 8, TQ), F32), pltpu.VMEM((A_HEADS, AUG, TQ), BF16),
                        pltpu.VMEM((A_HEADS, 1, TQ), F32), pltpu.VMEM((A_HEADS, 1, TQ), F32),
                        pltpu.VMEM((A_HEADS * HEAD_DIM, TQ), F32)],
        compiler_params=pltpu.CompilerParams(dimension_semantics=("arbitrary", "arbitrary"),
                                             vmem_limit_bytes=VMEM_LIMIT),
        name="dsa_attn",
    )(iqt, ik4, wft, aqt, ak, posx, avt)


def _rms_cols(xt, g_col):
    return xt * lax.rsqrt(jnp.mean(xt * xt, axis=0, keepdims=True) + EPS) * g_col


MOE_TMG = 256
MOE_TMC = 256
ROW_TILE = 8
ROUTE_COLS = 128


def _first_index_of_max(vals, row, big):
    m = jnp.max(vals, axis=0, keepdims=True)
    idx = jnp.min(jnp.where(vals == m, row, big), axis=0, keepdims=True)
    return m, idx


def _route(h, wr_hi_ref, wr_lo_ref, br_ref):
    h_hi, h_lo = _split_bf16(h)
    logits_t = (_dot3(h_hi, h_lo, wr_hi_ref[...], wr_lo_ref[...]) + br_ref[...]).T
    tm = h.shape[0]
    ninf = -jnp.inf
    row8 = lax.broadcasted_iota(jnp.int32, (8, tm), 0)
    gl = jnp.where(row8 < N_GROUPS, logits_t[0:8], ninf)
    gmax, gsel = _first_index_of_max(gl, row8, 1 << 20)
    p_group = 1.0 / jnp.sum(jnp.exp(gl - gmax), axis=0, keepdims=True)
    expert = lax.broadcasted_iota(jnp.int32, (N_EXPERTS, tm), 0)
    assert EXPERTS_PER_GROUP == 8
    el = jnp.where((expert >> 3) == gsel, logits_t[32:32 + N_EXPERTS], ninf)
    v1, i1 = _first_index_of_max(el, expert, 1 << 20)
    el2 = jnp.where(expert == i1, ninf, el)
    v2, i2 = _first_index_of_max(el2, expert, 1 << 20)
    e2 = jnp.exp(v2 - v1)
    w1 = 1.0 / (1.0 + e2)
    w2 = e2 / (1.0 + e2)
    return i1, i2, w1 * p_group, w2 * p_group


def _token_rows(s, n_tokens, first_token=0):
    return pl.ds(first_token * ROW_TILE + s, n_tokens, stride=ROW_TILE)


def _to_token_tiles(ref, x):
    for s in range(ROW_TILE):
        ref[_token_rows(s, x.shape[0]), :] = x[:, s * 128:(s + 1) * 128]


def _token_tile(ref, t):
    start = t * ROW_TILE if isinstance(t, int) else pl.multiple_of(t * ROW_TILE, ROW_TILE)
    return ref.at[pl.ds(start, ROW_TILE), :]


def _moe_route_kernel(oat_ref, obt_ref, x_ref, mod_ref, ga_ref, gb_ref, wo_ref,
                      gf_ref, wr_hi_ref, wr_lo_ref, br_ref,
                      x1_ref, h_ref, info_ref, infot_ref, cnt_ref, run_ref):
    @pl.when(pl.program_id(0) == 0)
    def _():
        run_ref[...] = jnp.zeros_like(run_ref)

    oa = _rms_cols(oat_ref[0], ga_ref[...]).astype(BF16)
    ob = _rms_cols(obt_ref[0], gb_ref[...]).astype(BF16)
    y = _dot_tn(oa, wo_ref[0:512, :]) + _dot_tn(ob, wo_ref[512:1024, :])
    x1 = x_ref[...] + mod_ref[0, 2:3, :] * y
    x1_ref[...] = x1

    h = _rms(x1, gf_ref[...]) * (1.0 + mod_ref[0, 4:5, :]) + mod_ref[0, 3:4, :]
    _to_token_tiles(h_ref, h)
    i1, i2, w1, w2 = _route(h, wr_hi_ref, wr_lo_ref, br_ref)
    tm = h.shape[0]
    expert = lax.broadcasted_iota(jnp.int32, (N_EXPERTS, tm), 0)
    picked = jnp.where((expert == i1) | (expert == i2), 1.0, 0.0)
    earlier = jnp.where(lax.broadcasted_iota(jnp.int32, (tm, tm), 0)
                        < lax.broadcasted_iota(jnp.int32, (tm, tm), 1), 1.0, 0.0).astype(BF16)
    before = _dot(picked.astype(BF16), earlier) + run_ref[...]
    rank1 = jnp.sum(jnp.where(expert == i1, before, 0.0), axis=0, keepdims=True)
    rank2 = jnp.sum(jnp.where(expert == i2, before, 0.0), axis=0, keepdims=True)
    run_ref[...] += jnp.sum(picked, axis=1, keepdims=True)
    cnt_ref[...] = run_ref[...]
    row8 = lax.broadcasted_iota(jnp.int32, (8, tm), 0)
    info_t = jnp.where(row8 == 0, i1.astype(F32), 0.0)
    info_t = jnp.where(row8 == 1, i2.astype(F32), info_t)
    info_t = jnp.where(row8 == 2, rank1, info_t)
    info_t = jnp.where(row8 == 3, rank2, info_t)
    info_t = jnp.where(row8 == 4, w1, info_t)
    info_t = jnp.where(row8 == 5, w2, info_t)
    infot_ref[...] = info_t
    info_ref[...] = jnp.concatenate([info_t, jnp.zeros((ROUTE_COLS - 8, tm), F32)], axis=0).T


def _moe_route(oat, obt, x2, mod3, g_out_a, g_out_b, w_out_bf, g_ffn, wr_hi, wr_lo, b_route, seq):
    n, d = x2.shape
    tm = 512
    per_b = seq // tm
    row = lambda i: (i, 0)
    const = lambda i: (0, 0)
    blk_t = lambda i: (i // per_b, 0, i % per_b)
    return pl.pallas_call(
        _moe_route_kernel,
        out_shape=[jax.ShapeDtypeStruct((n, d), F32),
                   jax.ShapeDtypeStruct((n * ROW_TILE, d // ROW_TILE), F32),
                   jax.ShapeDtypeStruct((n, ROUTE_COLS), F32), jax.ShapeDtypeStruct((8, n), F32),
                   jax.ShapeDtypeStruct((N_EXPERTS, 1), F32)],
        grid=(n // tm,),
        in_specs=[pl.BlockSpec((1, 512, tm), blk_t), pl.BlockSpec((1, 512, tm), blk_t),
                  pl.BlockSpec((tm, d), row),
                  pl.BlockSpec((1, N_MOD, d), lambda i: (i // per_b, 0, 0)),
                  pl.BlockSpec((512, 1), const), pl.BlockSpec((512, 1), const),
                  pl.BlockSpec((d, d), const),
                  pl.BlockSpec((1, d), const),
                  pl.BlockSpec((d, ROUTE_COLS), const), pl.BlockSpec((d, ROUTE_COLS), const),
                  pl.BlockSpec((1, ROUTE_COLS), const)],
        out_specs=[pl.BlockSpec((tm, d), row),
                   pl.BlockSpec((tm * ROW_TILE, d // ROW_TILE), row),
                   pl.BlockSpec((tm, ROUTE_COLS), row), pl.BlockSpec((8, tm), lambda i: (0, i)),
                   pl.BlockSpec((N_EXPERTS, 1), const)],
        scratch_shapes=[pltpu.VMEM((N_EXPERTS, 1), F32)],
        compiler_params=pltpu.CompilerParams(dimension_semantics=("arbitrary",),
                                             vmem_limit_bytes=VMEM_LIMIT),
        name="mix_out_moe_route",
    )(oat, obt, x2, mod3, g_out_a.reshape(-1, 1), g_out_b.reshape(-1, 1), w_out_bf,
      g_ffn.reshape(1, d), wr_hi, wr_lo, b_route)


def _wait_token_copies(src_hbm, dst, sem, n_tokens):
    pltpu.make_async_copy(src_hbm.at[pl.ds(0, n_tokens * ROW_TILE), :], dst, sem).wait()


def _moe_scatter_kernel(dest_ref, pad_start_ref, pad_len_ref, nu_ref, h_ref, xs_hbm, zero_ref, sem):
    i = pl.program_id(0)
    n_pairs = 2 * MOE_TMC

    @pl.when(i == 0)
    def _():
        zero_ref[...] = jnp.zeros_like(zero_ref)

        def pad_copies(e):
            n = pad_len_ref[e]
            size = MOE_TMG // 2
            while size >= 1:
                first = pad_start_ref[e] + (n & ~(2 * size - 1))
                copy = pltpu.make_async_copy(
                    zero_ref.at[pl.ds(0, size * ROW_TILE), :],
                    xs_hbm.at[pl.ds(pl.multiple_of(first * ROW_TILE, ROW_TILE), size * ROW_TILE), :],
                    sem.at[1])
                yield (n & size) != 0, copy
                size //= 2

        for e in range(N_EXPERTS):
            for present, copy in pad_copies(e):
                pl.when(present)(copy.start)
        for e in range(N_EXPERTS):
            for present, copy in pad_copies(e):
                pl.when(present)(copy.wait)

        def row_tile(t):
            return xs_hbm.at[pl.ds(pl.multiple_of(t * (MOE_TMG * ROW_TILE), MOE_TMG * ROW_TILE),
                                   MOE_TMG * ROW_TILE), :]

        n_tiles = xs_hbm.shape[0] // (MOE_TMG * ROW_TILE)

        def fill_tile(t, _):
            pltpu.make_async_copy(zero_ref, row_tile(t), sem.at[1]).start()
            return 0

        def drain_tile(t, _):
            pltpu.make_async_copy(zero_ref, row_tile(0), sem.at[1]).wait()
            return 0

        lax.fori_loop(nu_ref[0], n_tiles, fill_tile, 0)
        lax.fori_loop(nu_ref[0], n_tiles, drain_tile, 0)

    for r in range(MOE_TMC):
        for j in range(2):
            pltpu.make_async_copy(_token_tile(h_ref, r),
                                  _token_tile(xs_hbm, dest_ref[i * n_pairs + j * MOE_TMC + r]),
                                  sem.at[0]).start(priority=j)
    _wait_token_copies(xs_hbm, xs_hbm.at[pl.ds(0, n_pairs * ROW_TILE), :], sem.at[0], n_pairs)


def _moe_scatter(dest, pad_start, pad_len, n_used, h2, n_rows):
    n = h2.shape[0] // ROW_TILE
    grid_spec = pltpu.PrefetchScalarGridSpec(
        num_scalar_prefetch=4,
        grid=(n // MOE_TMC,),
        in_specs=[pl.BlockSpec((MOE_TMC * ROW_TILE, h2.shape[1]), lambda i, *_: (i, 0))],
        out_specs=pl.BlockSpec(memory_space=pl.ANY),
        scratch_shapes=[pltpu.VMEM((MOE_TMG * ROW_TILE, h2.shape[1]), F32),
                        pltpu.SemaphoreType.DMA((2,))])
    return pl.pallas_call(
        _moe_scatter_kernel,
        out_shape=jax.ShapeDtypeStruct((n_rows * ROW_TILE, h2.shape[1]), F32),
        grid_spec=grid_spec,
        compiler_params=pltpu.CompilerParams(dimension_semantics=("arbitrary",),
                                             vmem_limit_bytes=VMEM_LIMIT),
        name="moe_scatter",
    )(dest, pad_start, pad_len, n_used, h2)


TILES_PER_STEP = 2


def _moe_expert_kernel(te_ref, nu_ref, x_ref, *refs):
    n = TILES_PER_STEP
    wg_refs, wu_refs, wd_refs = (refs[k * n:(k + 1) * n] for k in range(3))
    y_ref, xs_ref, wgb_ref, wub_ref, wdb_ref = refs[3 * n:]
    i = pl.program_id(0)
    step_used = n * i < nu_ref[0]

    @pl.when(jnp.logical_not(step_used))
    def _():
        y_ref[...] = jnp.zeros_like(y_ref)

    for slot in range(n):
        tile = n * i + slot
        new_expert = jnp.logical_or(i == 0, te_ref[tile] != te_ref[jnp.maximum(tile - n, 0)])

        @pl.when(jnp.logical_and(step_used, new_expert))
        def _(slot=slot):
            wgb_ref[slot] = wg_refs[slot][0].astype(BF16)
            wub_ref[slot] = wu_refs[slot][0].astype(BF16)
            wdb_ref[slot] = wd_refs[slot][0].astype(BF16)

    @pl.when(step_used)
    def _():
        for slot in range(n):
            for s in range(ROW_TILE):
                rows = _token_rows(s, MOE_TMG, first_token=slot * MOE_TMG)
                xs_ref[slot, :, s * 128:(s + 1) * 128] = x_ref[rows, :].astype(BF16)
        acts = []
        for slot in range(n):
            x = xs_ref[slot]
            hg = _dot(x, wgb_ref[slot])
            hu = _dot(x, wub_ref[slot])
            acts.append((hg * jax.nn.sigmoid(hg) * hu).astype(BF16))
        for slot in range(n):
            y = _dot(acts[slot], wdb_ref[slot])
            for s in range(ROW_TILE):
                rows = _token_rows(s, MOE_TMG, first_token=slot * MOE_TMG)
                y_ref[rows, :] = y[:, s * 128:(s + 1) * 128]


def _moe_experts(tile_expert, n_used, xsorted, wg, wu, wd):
    n_rows = xsorted.shape[0] // ROW_TILE
    d, ff = wg.shape[1], wg.shape[2]
    n = TILES_PER_STEP
    block = (n * MOE_TMG * ROW_TILE, xsorted.shape[1])
    slots = range(n)
    w_in_specs = [pl.BlockSpec((1, d, ff), lambda i, te, nu, s=s: (te[n * i + s], 0, 0)) for s in slots]
    w_out_specs = [pl.BlockSpec((1, ff, d), lambda i, te, nu, s=s: (te[n * i + s], 0, 0)) for s in slots]
    grid_spec = pltpu.PrefetchScalarGridSpec(
        num_scalar_prefetch=2,
        grid=(n_rows // (MOE_TMG * n),),
        in_specs=[pl.BlockSpec(block, lambda i, te, nu: (jnp.minimum(i, (nu[0] - 1) // n), 0))]
        + w_in_specs + w_in_specs + w_out_specs,
        out_specs=pl.BlockSpec(block, lambda i, te, nu: (i, 0)),
        scratch_shapes=[pltpu.VMEM((n, MOE_TMG, d), BF16), pltpu.VMEM((n, d, ff), BF16),
                        pltpu.VMEM((n, d, ff), BF16), pltpu.VMEM((n, ff, d), BF16)])
    return pl.pallas_call(
        _moe_expert_kernel,
        out_shape=jax.ShapeDtypeStruct(xsorted.shape, F32),
        grid_spec=grid_spec,
        compiler_params=pltpu.CompilerParams(dimension_semantics=("arbitrary",),
                                             vmem_limit_bytes=VMEM_LIMIT),
        name="moe_experts",
    )(tile_expert, n_used, xsorted, *([wg] * n), *([wu] * n), *([wd] * n))


def _moe_combine_kernel(dest_ref, y_hbm, x_ref, info_ref, mod_ref, gfin_ref, o_ref, ybuf, x2_ref, sem):
    i = pl.program_id(0)
    nt = pl.num_programs(0)
    slot = lax.rem(i, 2)
    n_pairs = 2 * MOE_TMC

    def start_gather(tile, to_slot):
        for r in range(n_pairs):
            pltpu.make_async_copy(_token_tile(y_hbm, dest_ref[tile * n_pairs + r]),
                                  _token_tile(ybuf.at[to_slot], r), sem.at[to_slot]).start(priority=r % 2)

    @pl.when(i == 0)
    def _():
        start_gather(0, 0)

    _wait_token_copies(y_hbm, ybuf.at[slot], sem.at[slot], n_pairs)

    @pl.when(i + 1 < nt)
    def _():
        start_gather(i + 1, 1 - slot)

    w1 = info_ref[:, 4:5]
    w2 = info_ref[:, 5:6]
    sumsq = jnp.zeros((MOE_TMC, 1), F32)
    for s in range(ROW_TILE):
        cols = slice(s * 128, (s + 1) * 128)
        y = (w1 * ybuf[slot, _token_rows(s, MOE_TMC), :]
             + w2 * ybuf[slot, _token_rows(s, MOE_TMC, first_token=MOE_TMC), :])
        x2 = x_ref[:, cols] + mod_ref[0, 5:6, cols] * y
        x2_ref[:, cols] = x2
        sumsq = sumsq + jnp.sum(x2 * x2, axis=-1, keepdims=True)
    d = x2_ref.shape[1]
    o_ref[...] = x2_ref[...] * lax.rsqrt(sumsq / d + EPS) * gfin_ref[...]


def _moe_combine(dest, ysorted, x1, info, mod3, g_final, seq):
    n, d = x1.shape
    per_b = seq // MOE_TMC
    grid_spec = pltpu.PrefetchScalarGridSpec(
        num_scalar_prefetch=1,
        grid=(n // MOE_TMC,),
        in_specs=[pl.BlockSpec(memory_space=pl.ANY),
                  pl.BlockSpec((MOE_TMC, d), lambda i, ds: (i, 0)),
                  pl.BlockSpec((MOE_TMC, ROUTE_COLS), lambda i, ds: (i, 0)),
                  pl.BlockSpec((1, N_MOD, d), lambda i, ds: (i // per_b, 0, 0)),
                  pl.BlockSpec((1, d), lambda i, ds: (0, 0))],
        out_specs=pl.BlockSpec((MOE_TMC, d), lambda i, ds: (i, 0)),
        scratch_shapes=[pltpu.VMEM((2, 2 * MOE_TMC * ROW_TILE, ysorted.shape[1]), F32),
                        pltpu.VMEM((MOE_TMC, d), F32), pltpu.SemaphoreType.DMA((2,))])
    return pl.pallas_call(
        _moe_combine_kernel,
        out_shape=jax.ShapeDtypeStruct((n, d), F32),
        grid_spec=grid_spec,
        compiler_params=pltpu.CompilerParams(dimension_semantics=("arbitrary",),
                                             vmem_limit_bytes=VMEM_LIMIT),
        name="moe_combine",
    )(dest, ysorted, x1, info, mod3, g_final.reshape(1, d))


def _mix_out_and_moe(oat, obt, x2, mod3, g_out_a, g_out_b, w_out_bf, g_ffn, wr_hi, wr_lo, b_route,
                     wg, wu, wd, g_final, seq):
    n, d = x2.shape
    x1, h2, info, infot, counts = _moe_route(oat, obt, x2, mod3, g_out_a, g_out_b, w_out_bf,
                                             g_ffn, wr_hi, wr_lo, b_route, seq)

    e1, e2, rank1, rank2 = [infot[k].astype(jnp.int32) for k in range(4)]
    cnt = counts[:, 0].astype(jnp.int32)
    padded = ((cnt + MOE_TMG - 1) // MOE_TMG) * MOE_TMG
    seg_end = jnp.cumsum(padded)
    expert_ids = jnp.arange(N_EXPERTS, dtype=jnp.int32)

    def seg_start_of(e):
        return jnp.sum(jnp.where(expert_ids[None, :] < e[:, None], padded[None, :], 0), axis=1)

    dest1 = seg_start_of(e1) + rank1
    dest2 = seg_start_of(e2) + rank2
    n_rows = 2 * n + N_EXPERTS * MOE_TMG
    tile_start = jnp.arange(n_rows // MOE_TMG, dtype=jnp.int32) * MOE_TMG
    tile_expert = jnp.minimum(jnp.sum((tile_start[:, None] >= seg_end[None, :]).astype(jnp.int32), axis=1),
                              N_EXPERTS - 1)
    n_used = (seg_end[N_EXPERTS - 1:] // MOE_TMG).astype(jnp.int32)
    dest = jnp.concatenate([dest1.reshape(-1, MOE_TMC), dest2.reshape(-1, MOE_TMC)], axis=1).reshape(-1)

    seg_start = seg_end - padded
    xsorted = _moe_scatter(dest, seg_start + cnt, padded - cnt, n_used, h2, n_rows)
    ysorted = _moe_experts(tile_expert, n_used, xsorted, wg, wu, wd)
    return _moe_combine(dest, ysorted, x1, info, mod3, g_final, seq)


def _layer(x3, c, w_ada, b_ada, g_mix, w_in, b_forget, g_out_a, g_out_b, w_out,
           g_ffn, w_group, b_group, w_router, b_router, w_gate, w_up, w_down, g_final):
    bsz, seq, d = x3.shape
    mod3 = _ada_mod(c, w_ada, b_ada).reshape(bsz, N_MOD, d)

    w_t = w_in.T

    def pad_heads(w, n_heads):
        w = w.reshape(d, n_heads, HEAD_DIM)
        return jnp.concatenate([w, jnp.zeros_like(w)], axis=-1).reshape(d, n_heads * AUG)

    w_wf_hi, w_wf_lo = _split_bf16(jnp.concatenate([w_t[1344:1352], w_t[2888:2896]], axis=0))
    w_iq_wf = jnp.concatenate([w_t[768:1280].astype(BF16), w_wf_hi, w_wf_lo], axis=0)
    weights = [w_t[0:512].astype(BF16), pad_heads(w_in[:, 512:640], A_KV_HEADS).astype(BF16),
               w_t[640:768].astype(BF16), w_t[1352:1864].astype(BF16),
               w_in[:, 1864:2376].astype(BF16), w_t[2376:2888].astype(BF16),
               w_iq_wf, w_in[:, 1280:1344].astype(BF16), w_wf_hi]
    aqt, ak, avt, bqt, bk, bvt, iqt, ik4, wft = _in_proj(x3, mod3, g_mix, weights)

    cumt, kaug = _fox_cum(wft, b_forget, bk)
    obt = _fox_attn(bqt, kaug, bvt, cumt)

    pos = jnp.arange(seq, dtype=jnp.int32)[:, None]
    lane = jnp.arange(AUG, dtype=jnp.int32)[None, :] - HEAD_DIM
    posx = jnp.where((lane >= 0) & (lane < 3), pos >> 7,
                     jnp.where((lane >= 3) & (lane < 6), pos & 127,
                               jnp.where((lane >= 6) & (lane < 9), 1, 0))).astype(BF16)
    oat = _dsa_attn(iqt, ik4, wft, aqt, ak, posx, avt)

    w_r = jnp.concatenate([w_group, jnp.zeros((d, 32 - N_GROUPS), F32),
                           jnp.transpose(w_router, (1, 0, 2)).reshape(d, N_EXPERTS),
                           jnp.zeros((d, ROUTE_COLS - 64), F32)], axis=1)
    b_r = jnp.concatenate([b_group, jnp.zeros((32 - N_GROUPS,), F32), b_router.reshape(-1),
                           jnp.zeros((ROUTE_COLS - 64,), F32)]).reshape(1, ROUTE_COLS)
    wr_hi, wr_lo = _split_bf16(w_r)
    out = _mix_out_and_moe(oat, obt, x3.reshape(bsz * seq, d), mod3, g_out_a, g_out_b,
                           w_out.astype(BF16), g_ffn, wr_hi, wr_lo, b_r, w_gate, w_up, w_down,
                           g_final, seq)
    return out.reshape(bsz, seq, d)


def kernel(x, c, w_ada, b_ada, g_mix, w_in, b_forget, g_out_a, g_out_b, w_out, g_ffn, w_group,
           b_group, w_router, b_router, w_gate, w_up, w_down, g_final):
    depth = w_ada.shape[0]
    assert depth == 1, "final norm is fused into the single layer's MoE kernel"
    return _layer(x, c, w_ada[0], b_ada[0], g_mix[0], w_in[0], b_forget[0], g_out_a[0], g_out_b[0],
                  w_out[0], g_ffn[0], w_group[0], b_group[0], w_router[0], b_router[0], w_gate[0],
                  w_up[0], w_down[0], g_final)
```

```python
import math

import jax
import jax.numpy as jnp
import numpy as np
from jax import lax
from jax.experimental import pallas as pl
from jax.experimental.pallas import tpu as pltpu

F32 = jnp.float32
BF16 = jnp.bfloat16

EPS = 1e-6
A_HEADS = 8
A_KV_HEADS = 2
HEAD_DIM = 64
IDX_HEADS = 8
IDX_DIM = 64
TOPK = 256
B_HEADS = 8
N_GROUPS = 4
EXPERTS_PER_GROUP = 8
N_EXPERTS = N_GROUPS * EXPERTS_PER_GROUP
N_MOD = 6

NEG_BIG = -1e30
LOWEST = float(np.finfo(np.float32).min)
LOG2E = math.log2(math.e)
Q_SCALE = HEAD_DIM ** -0.5 * LOG2E
VMEM_LIMIT = 48 * 1024 * 1024

TQ = 256
KC = 256
AUG = 128
IDX_K = IDX_DIM
ONES_ROWS = 16
COARSE_STEPS = 14
FOX_LOOKAHEAD = 6
DSA_LOOKAHEAD = 4
assert TQ == KC == TOPK


def _split_bf16(x):
    hi = x.astype(BF16)
    lo = (x - hi.astype(F32)).astype(BF16)
    return hi, lo


def _split3_f32(x):
    p1 = x.astype(BF16).astype(F32)
    r1 = x - p1
    p2 = r1.astype(BF16).astype(F32)
    p3 = (r1 - p2).astype(BF16).astype(F32)
    return p1, p2, p3


def _dot(a, b):
    return jnp.dot(a, b, preferred_element_type=F32)


def _dot_nt(a, b):
    return lax.dot_general(a, b, (((1,), (1,)), ((), ())), preferred_element_type=F32)


def _dot_tn(a, b):
    return lax.dot_general(a, b, (((0,), (0,)), ((), ())), preferred_element_type=F32)


def _dot3(a_hi, a_lo, b_hi, b_lo):
    return _dot(a_hi, b_hi) + _dot(a_lo, b_hi) + _dot(a_hi, b_lo)


def _dot3_nt(a_hi, a_lo, b_hi, b_lo):
    return _dot_nt(a_hi, b_hi) + _dot_nt(a_lo, b_hi) + _dot_nt(a_hi, b_lo)


def _rms(x, g):
    return x * lax.rsqrt(jnp.mean(x * x, axis=-1, keepdims=True) + EPS) * g


def _chunk(kc):
    return pl.ds(pl.multiple_of(kc * KC, KC), KC)


def _key_minus_query():
    return (lax.broadcasted_iota(jnp.int32, (KC, TQ), 0)
            - lax.broadcasted_iota(jnp.int32, (KC, TQ), 1))


def _ada_kernel(c_ref, w_ref, b_ref, o_ref):
    c = c_ref[...]
    s = c * jax.nn.sigmoid(c)
    s_hi, s_lo = _split_bf16(s)
    w_hi, w_lo = _split_bf16(w_ref[...])
    o_ref[...] = _dot3(s_hi, s_lo, w_hi, w_lo) + b_ref[...]


def _ada_mod(c, w_ada, b_ada):
    bsz, d = c.shape
    n = w_ada.shape[1]
    tn = 1024
    return pl.pallas_call(
        _ada_kernel,
        out_shape=jax.ShapeDtypeStruct((bsz, n), F32),
        grid=(n // tn,),
        in_specs=[pl.BlockSpec((bsz, d), lambda j: (0, 0)),
                  pl.BlockSpec((d, tn), lambda j: (0, j)),
                  pl.BlockSpec((1, tn), lambda j: (0, j))],
        out_specs=pl.BlockSpec((bsz, tn), lambda j: (0, j)),
        compiler_params=pltpu.CompilerParams(dimension_semantics=("arbitrary",),
                                             vmem_limit_bytes=VMEM_LIMIT),
        name="ada_mod",
    )(c, w_ada, b_ada.reshape(1, n))


def _in_proj_kernel(x_ref, mod_ref, g_ref,
                    waq_ref, wak_ref, wav_ref, wbq_ref, wbk_ref, wbv_ref,
                    wiq_ref, wik_ref, wwfh_ref,
                    aqt_ref, ak_ref, avt_ref, bqt_ref, bk_ref, bvt_ref, iqt_ref, ik_ref, wft_ref):
    x = x_ref[0]
    h = _rms(x, g_ref[...]) * (1.0 + mod_ref[0, 1:2, :]) + mod_ref[0, 0:1, :]
    h_hi, h_lo = _split_bf16(h)
    aqt_ref[0] = (_dot_nt(waq_ref[...], h_hi) * Q_SCALE).astype(BF16)
    ak_ref[0] = _dot(h_hi, wak_ref[...]).astype(BF16)
    avt_ref[0] = _dot_nt(wav_ref[...], h_hi).astype(BF16)
    bqt_ref[0] = (_dot_nt(wbq_ref[...], h_hi) * Q_SCALE).astype(BF16)
    bk_ref[0] = _dot(h_hi, wbk_ref[...]).astype(BF16)
    bvt_ref[0] = _dot_nt(wbv_ref[...], h_hi).astype(BF16)
    nq = IDX_HEADS * IDX_DIM
    both = _dot_nt(wiq_ref[...], h_hi)
    iqt_ref[0] = both[0:nq].astype(BF16)
    wft_ref[0] = both[nq:nq + 16] + both[nq + 16:nq + 32] + _dot_nt(wwfh_ref[...], h_lo)
    ik_ref[0] = _dot(h_hi, wik_ref[...]).astype(BF16)


def _in_proj(x3, mod3, g_mix, weights):
    bsz, seq, d = x3.shape
    tm = 512
    blk_t = lambda b, i: (b, 0, i)
    blk_r = lambda b, i: (b, i, 0)
    const = lambda b, i: (0, 0)
    ak_w = A_KV_HEADS * AUG
    bk_w = B_HEADS * HEAD_DIM
    outs = [jax.ShapeDtypeStruct((bsz, 512, seq), BF16), jax.ShapeDtypeStruct((bsz, seq, ak_w), BF16),
            jax.ShapeDtypeStruct((bsz, 128, seq), BF16), jax.ShapeDtypeStruct((bsz, 512, seq), BF16),
            jax.ShapeDtypeStruct((bsz, seq, bk_w), BF16), jax.ShapeDtypeStruct((bsz, 512, seq), BF16),
            jax.ShapeDtypeStruct((bsz, IDX_HEADS * IDX_K, seq), BF16),
            jax.ShapeDtypeStruct((bsz, seq, IDX_K), BF16), jax.ShapeDtypeStruct((bsz, 16, seq), F32)]
    out_specs = [pl.BlockSpec((1, 512, tm), blk_t), pl.BlockSpec((1, tm, ak_w), blk_r),
                 pl.BlockSpec((1, 128, tm), blk_t), pl.BlockSpec((1, 512, tm), blk_t),
                 pl.BlockSpec((1, tm, bk_w), blk_r), pl.BlockSpec((1, 512, tm), blk_t),
                 pl.BlockSpec((1, IDX_HEADS * IDX_K, tm), blk_t),
                 pl.BlockSpec((1, tm, IDX_K), blk_r), pl.BlockSpec((1, 16, tm), blk_t)]
    return pl.pallas_call(
        _in_proj_kernel,
        out_shape=outs,
        grid=(bsz, seq // tm),
        in_specs=[pl.BlockSpec((1, tm, d), blk_r),
                  pl.BlockSpec((1, N_MOD, d), lambda b, i: (b, 0, 0)),
                  pl.BlockSpec((1, d), const)] + [pl.BlockSpec(w.shape, const) for w in weights],
        out_specs=out_specs,
        compiler_params=pltpu.CompilerParams(dimension_semantics=("arbitrary", "arbitrary"),
                                             vmem_limit_bytes=VMEM_LIMIT),
        name="in_proj",
    )(x3, mod3, g_mix.reshape(1, d), *weights)


CB = 256


def _cum_kernel(wft_ref, bfor_ref, k_ref, cumt_ref, kaug_ref):
    seq = wft_ref.shape[2]
    r = lax.broadcasted_iota(jnp.int32, (CB, CB), 0)
    cidx = lax.broadcasted_iota(jnp.int32, (CB, CB), 1)
    tri = jnp.where(r <= cidx, 1.0, 0.0).astype(BF16)
    row128 = lax.broadcasted_iota(jnp.int32, (AUG, CB), 0)
    ones_rows = jnp.where((row128 >= HEAD_DIM + 3) & (row128 < HEAD_DIM + 6), 1.0, 0.0)
    src = lax.broadcasted_iota(jnp.int32, (AUG, AUG), 0)
    dst = lax.broadcasted_iota(jnp.int32, (AUG, AUG), 1)
    place = [jnp.where((dst < HEAD_DIM) & (src == dst + half * HEAD_DIM), 1.0, 0.0).astype(BF16)
             for half in range(2)]
    carry = jnp.zeros((8, 1), F32)
    for blk in range(seq // CB):
        cols = slice(blk * CB, (blk + 1) * CB)
        z = wft_ref[0, 8:16, cols] + bfor_ref[...]
        logf = jnp.minimum(z, 0.0) - jnp.log(1.0 + jnp.exp(-jnp.abs(z)))
        p1, p2, p3 = _split3_f32(logf)
        pieces = jnp.concatenate([p1, p2, p3, jnp.zeros_like(p1)], axis=0).astype(BF16)
        parts = _dot(pieces, tri)
        cum = parts[0:8] + parts[8:16] + parts[16:24] + carry
        carry = cum[:, CB - 1:CB]
        cum2 = cum * LOG2E
        cumt_ref[0, :, cols] = cum2
        c1, c2, c3 = _split3_f32(cum2)
        for h in range(B_HEADS):
            spare = jnp.where(row128 == HEAD_DIM, -c1[h:h + 1], ones_rows)
            spare = jnp.where(row128 == HEAD_DIM + 1, -c2[h:h + 1], spare)
            spare = jnp.where(row128 == HEAD_DIM + 2, -c3[h:h + 1], spare)
            pair = k_ref[0, cols, (h // 2) * AUG:(h // 2 + 1) * AUG]
            k_h = _dot(pair, place[h % 2]).astype(BF16)
            kaug_ref[0, cols, h * AUG:(h + 1) * AUG] = k_h + spare.T.astype(BF16)


def _fox_cum(wft, b_forget, bk):
    bsz, _, seq = wft.shape
    nh = B_HEADS
    kw = bk.shape[-1]
    kaug_w = nh * AUG
    return pl.pallas_call(
        _cum_kernel,
        out_shape=[jax.ShapeDtypeStruct((bsz, nh, seq), F32),
                   jax.ShapeDtypeStruct((bsz, seq, kaug_w), BF16)],
        grid=(bsz,),
        in_specs=[pl.BlockSpec((1, 16, seq), lambda b: (b, 0, 0)),
                  pl.BlockSpec((nh, 1), lambda b: (0, 0)),
                  pl.BlockSpec((1, seq, kw), lambda b: (b, 0, 0))],
        out_specs=[pl.BlockSpec((1, nh, seq), lambda b: (b, 0, 0)),
                   pl.BlockSpec((1, seq, kaug_w), lambda b: (b, 0, 0))],
        compiler_params=pltpu.CompilerParams(dimension_semantics=("arbitrary",),
                                             vmem_limit_bytes=VMEM_LIMIT),
        name="fox_cum",
    )(wft, b_forget.reshape(nh, 1), bk)


def _softmax_init(m_ref, l_ref, acc_ref):
    m_ref[...] = jnp.full(m_ref.shape, NEG_BIG, F32)
    l_ref[...] = jnp.zeros(l_ref.shape, F32)
    acc_ref[...] = jnp.zeros(acc_ref.shape, F32)


def _attend_chunks(items, score_fn, vt_fn, m_ref, l_ref, acc_ref, lookahead):
    ahead = min(lookahead, len(items))
    scores = {i: score_fn(items[i]) for i in range(ahead)}
    ones = jnp.ones((ONES_ROWS, KC), BF16)
    for i, item in enumerate(items):
        if i + ahead < len(items):
            scores[i + ahead] = score_fn(items[i + ahead])
        s = scores.pop(i)
        h = item[1]
        m_old = m_ref[h]
        m_new = jnp.maximum(m_old, jnp.max(s, axis=0, keepdims=True))
        alpha = jnp.exp2(m_old - m_new)
        p = jnp.exp2(s - m_new).astype(BF16)
        pv = _dot(jnp.concatenate([vt_fn(item), ones], axis=0), p)
        m_ref[h] = m_new
        l_ref[h] = alpha * l_ref[h] + pv[HEAD_DIM:HEAD_DIM + 1, :]
        rows = slice(h * HEAD_DIM, (h + 1) * HEAD_DIM)
        acc_ref[rows, :] = alpha * acc_ref[rows, :] + pv[0:HEAD_DIM, :]


def _softmax_finish(o_ref, n_heads, l_ref, acc_ref):
    for h in range(n_heads):
        rows = slice(h * HEAD_DIM, (h + 1) * HEAD_DIM)
        o_ref[0, rows, :] = acc_ref[rows, :] / l_ref[h]


def _fox_kernel(qt_ref, k_ref, vt_ref, cumt_ref, o_ref, w_ref, m_ref, l_ref, acc_ref):
    qi = pl.program_id(1)
    _softmax_init(m_ref, l_ref, acc_ref)

    row64 = lax.broadcasted_iota(jnp.int32, (AUG - HEAD_DIM, TQ), 0)
    for h in range(B_HEADS):
        c1, c2, c3 = _split3_f32(cumt_ref[0, h:h + 1, :])
        spare = jnp.where(row64 < 3, 1.0, 0.0)
        spare = jnp.where(row64 == 3, c1, spare)
        spare = jnp.where(row64 == 4, c2, spare)
        spare = jnp.where(row64 == 5, c3, spare)
        w_ref[h] = jnp.concatenate([qt_ref[0, h * HEAD_DIM:(h + 1) * HEAD_DIM, :],
                                    spare.astype(BF16)], axis=0)

    def tiles(chunks):
        rows = [_chunk(kc) for kc, _ in chunks]

        def score_fn(item):
            j, h = item
            s = _dot(k_ref[0, rows[j], h * AUG:(h + 1) * AUG], w_ref[h])
            if chunks[j][1]:
                s = jnp.where(_key_minus_query() <= 0, s, NEG_BIG)
            return s

        def vt_fn(item):
            j, h = item
            return vt_ref[0, h * HEAD_DIM:(h + 1) * HEAD_DIM, rows[j]]

        items = [(j, h) for j in range(len(chunks)) for h in range(B_HEADS)]
        _attend_chunks(items, score_fn, vt_fn, m_ref, l_ref, acc_ref, FOX_LOOKAHEAD)

    def body(pair, _):
        tiles([(2 * pair, False), (2 * pair + 1, False)])
        return 0

    lax.fori_loop(0, qi // 2, body, 0)

    @pl.when(qi % 2 == 1)
    def _():
        tiles([(qi - 1, False), (qi, True)])

    @pl.when(qi % 2 == 0)
    def _():
        tiles([(qi, True)])

    _softmax_finish(o_ref, B_HEADS, l_ref, acc_ref)


def _fox_attn(bqt, kaug, bvt, cumt):
    bsz, w, seq = bqt.shape
    blk_t = lambda b, i: (b, 0, i)
    full = lambda b, i: (b, 0, 0)
    return pl.pallas_call(
        _fox_kernel,
        out_shape=jax.ShapeDtypeStruct((bsz, w, seq), F32),
        grid=(bsz, seq // TQ),
        in_specs=[pl.BlockSpec((1, w, TQ), blk_t),
                  pl.BlockSpec((1, seq, B_HEADS * AUG), full),
                  pl.BlockSpec((1, w, seq), full),
                  pl.BlockSpec((1, B_HEADS, TQ), blk_t)],
        out_specs=pl.BlockSpec((1, w, TQ), blk_t),
        scratch_shapes=[pltpu.VMEM((B_HEADS, AUG, TQ), BF16), pltpu.VMEM((B_HEADS, 1, TQ), F32),
                        pltpu.VMEM((B_HEADS, 1, TQ), F32), pltpu.VMEM((B_HEADS * HEAD_DIM, TQ), F32)],
        compiler_params=pltpu.CompilerParams(dimension_semantics=("arbitrary", "arbitrary"),
                                             vmem_limit_bytes=VMEM_LIMIT),
        name="fox_attn",
    )(bqt, kaug, bvt, cumt)


def _bf16_pieces(value):
    pieces = []
    rest = np.float32(value)
    for _ in range(3):
        piece = np.asarray(rest).astype(BF16).astype(np.float32)
        pieces.append(float(piece))
        rest = np.float32(rest - piece)
    return pieces


def _dsa_kernel(iqt_ref, ik_ref, wft_ref, qt_ref, k_ref, posx_ref, vt_ref, o_ref,
                s_ref, thr_ref, lohi_ref, w_ref, m_ref, l_ref, acc_ref):
    qi = pl.program_id(1)
    nch = qi + 1
    kmq = _key_minus_query()

    def part(x, op):
        return op(x.reshape(KC // 8, 8, TQ), axis=0)

    def fold(x, op):
        return op(x, axis=0, keepdims=True)

    lohi_ref[0] = jnp.full((8, TQ), jnp.inf, F32)
    lohi_ref[1] = jnp.full((8, TQ), -jnp.inf, F32)

    def score_chunk(kc):
        ik = ik_ref[0, _chunk(kc), :]
        acc = jnp.zeros((KC, TQ), F32)
        for h in range(IDX_HEADS):
            d = _dot(ik, iqt_ref[0, h * IDX_K:(h + 1) * IDX_K, :])
            acc = acc + wft_ref[0, h:h + 1, :] * jnp.maximum(d, 0.0)
        causal = kmq <= (qi - kc) * KC
        s_ref[_chunk(kc), :] = jnp.where(causal, acc, -jnp.inf)
        lohi_ref[0] = jnp.minimum(lohi_ref[0], part(jnp.where(causal, acc, jnp.inf), jnp.min))
        lohi_ref[1] = jnp.maximum(lohi_ref[1], part(jnp.where(causal, acc, -jnp.inf), jnp.max))

    def score_body(pair, _):
        score_chunk(2 * pair)
        score_chunk(2 * pair + 1)
        return 0

    lax.fori_loop(0, nch // 2, score_body, 0)

    @pl.when(nch % 2 == 1)
    def _():
        score_chunk(nch - 1)
        s_ref[_chunk(nch), :] = jnp.full((KC, TQ), -jnp.inf, F32)

    def scan(fn, init):
        def pair_body(pair, c):
            c = fn(2 * pair, s_ref[_chunk(2 * pair), :], c)
            return fn(2 * pair + 1, s_ref[_chunk(2 * pair + 1), :], c)

        return lax.fori_loop(0, (nch + 1) // 2, pair_body, init)

    @pl.when(qi == 0)
    def _():
        thr_ref[...] = jnp.full(thr_ref.shape, LOWEST, F32)

    @pl.when(qi > 0)
    def _():
        def cmin(x):
            return part(x, jnp.min)

        def cmax(x):
            return part(x, jnp.max)

        def csum(x):
            return part(x, jnp.sum)

        zeros = jnp.zeros((8, TQ), F32)
        pinf = jnp.full((8, TQ), jnp.inf, F32)
        ninf = jnp.full((8, TQ), -jnp.inf, F32)
        lo, hi = fold(lohi_ref[0], jnp.min), fold(lohi_ref[1], jnp.max)

        def coarse_step(_, carry):
            lo, hi = carry
            mid = lo + (hi - lo) * 0.5
            cnt = scan(lambda _, s, c: c + csum(jnp.where(s >= mid, 1.0, 0.0)), zeros)
            enough = fold(cnt, jnp.sum) >= float(TOPK)
            return jnp.where(enough, mid, lo), jnp.where(enough, hi, mid)

        lo, hi = lax.fori_loop(0, COARSE_STEPS, coarse_step, (lo, hi))

        def cond(carry):
            return carry[2] > 0

        def step(carry):
            lo, hi, _ = carry
            mid = lo + (hi - lo) * 0.5
            mid = jnp.where(mid <= lo, hi, mid)

            def fn(_, s, c):
                cnt, a, b = c
                ge = s >= mid
                cnt = cnt + csum(jnp.where(ge, 1.0, 0.0))
                b = jnp.minimum(b, cmin(jnp.where(ge, s, jnp.inf)))
                a = jnp.maximum(a, cmax(jnp.where(ge, -jnp.inf, s)))
                return cnt, a, b

            cnt, a, b = scan(fn, (zeros, ninf, pinf))
            cnt, a, b = fold(cnt, jnp.sum), fold(a, jnp.max), fold(b, jnp.min)
            enough = cnt >= float(TOPK)
            new_lo = jnp.where(enough, b, jnp.where(cnt == float(TOPK - 1), a, lo))
            new_hi = jnp.where(enough, jnp.where(cnt == float(TOPK), b, hi), a)
            active = jnp.max(jnp.where(new_lo < new_hi, 1, 0))
            return new_lo, new_hi, active

        first_active = jnp.max(jnp.where(lo < hi, 1, 0))
        thr, _, _ = lax.while_loop(cond, step, (lo, hi, first_active))

        n_ge = fold(scan(lambda _, s, c: c + csum(jnp.where(s >= thr, 1.0, 0.0)), zeros), jnp.sum)
        thr_ref[...] = thr

        @pl.when(jnp.max(jnp.where(n_ge > float(TOPK), 1, 0)) > 0)
        def _():
            n_gt = fold(scan(lambda _, s, c: c + csum(jnp.where(s > thr, 1.0, 0.0)), zeros), jnp.sum)
            need = float(TOPK) - n_gt
            lower = jnp.where(lax.broadcasted_iota(jnp.int32, (KC, KC), 1)
                              < lax.broadcasted_iota(jnp.int32, (KC, KC), 0), 1.0, 0.0).astype(BF16)

            def sel_fn(kc, s, run):
                eq = s == thr
                eqf = jnp.where(eq, 1.0, 0.0)
                before = _dot(lower, eqf.astype(BF16)) + run
                sel = (s > thr) | (eq & (before < need))
                s_ref[_chunk(kc), :] = jnp.where(sel, 0.0, NEG_BIG)
                return run + jnp.sum(eqf, axis=0, keepdims=True)

            scan(sel_fn, jnp.zeros((1, TQ), F32))
            thr_ref[...] = jnp.full(thr_ref.shape, 0.5 * NEG_BIG, F32)

    _softmax_init(m_ref, l_ref, acc_ref)
    rep = A_HEADS // A_KV_HEADS
    row64 = lax.broadcasted_iota(jnp.int32, (AUG - HEAD_DIM, TQ), 0)
    qpos = (qi * TQ + lax.broadcasted_iota(jnp.int32, (1, TQ), 1)).astype(F32)
    for h in range(A_HEADS):
        slope = np.float32(2.0 ** (-8.0 * (h + 1) / A_HEADS) * LOG2E)
        u1, u2, u3 = _split3_f32(-slope * qpos)
        spare = jnp.zeros((AUG - HEAD_DIM, TQ), F32)
        for p, s_p in enumerate(_bf16_pieces(slope)):
            spare = jnp.where(row64 == p, 128.0 * s_p, spare)
            spare = jnp.where(row64 == 3 + p, s_p, spare)
        spare = jnp.where(row64 == 6, u1, spare)
        spare = jnp.where(row64 == 7, u2, spare)
        spare = jnp.where(row64 == 8, u3, spare)
        w_ref[h] = jnp.concatenate([qt_ref[0, h * HEAD_DIM:(h + 1) * HEAD_DIM, :],
                                    spare.astype(BF16)], axis=0)

    def attn_tiles(chunks):
        rows = [_chunk(kc) for kc in chunks]
        lhs = [[k_ref[0, r, g * AUG:(g + 1) * AUG] + posx_ref[r, :] for g in range(A_KV_HEADS)]
               for r in rows]
        bias = [jnp.where(s_ref[r, :] >= thr_ref[...], 0.0, NEG_BIG) for r in rows]

        def score_fn(item):
            j, h = item
            return _dot(lhs[j][h // rep], w_ref[h]) + bias[j]

        def vt_fn(item):
            j, h = item
            g = h // rep
            return vt_ref[0, g * HEAD_DIM:(g + 1) * HEAD_DIM, rows[j]]

        items = [(j, h) for j in range(len(chunks)) for h in range(A_HEADS)]
        _attend_chunks(items, score_fn, vt_fn, m_ref, l_ref, acc_ref, DSA_LOOKAHEAD)

    def attn_body(pair, _):
        attn_tiles([2 * pair, 2 * pair + 1])
        return 0

    lax.fori_loop(0, nch // 2, attn_body, 0)

    @pl.when(nch % 2 == 1)
    def _():
        attn_tiles([nch - 1])

    _softmax_finish(o_ref, A_HEADS, l_ref, acc_ref)


def _dsa_attn(iqt, ik4, wft, aqt, ak, posx, avt):
    bsz, w, seq = aqt.shape
    blk_t = lambda b, i: (b, 0, i)
    full = lambda b, i: (b, 0, 0)
    return pl.pallas_call(
        _dsa_kernel,
        out_shape=jax.ShapeDtypeStruct((bsz, w, seq), F32),
        grid=(bsz, seq // TQ),
        in_specs=[pl.BlockSpec((1, IDX_HEADS * IDX_K, TQ), blk_t),
                  pl.BlockSpec((1, seq, IDX_K), full),
                  pl.BlockSpec((1, 16, TQ), blk_t),
                  pl.BlockSpec((1, w, TQ), blk_t),
                  pl.BlockSpec((1, seq, A_KV_HEADS * AUG), full),
                  pl.BlockSpec((seq, AUG), lambda b, i: (0, 0)),
                  pl.BlockSpec((1, 128, seq), full)],
        out_specs=pl.BlockSpec((1, w, TQ), blk_t),
        scratch_shapes=[pltpu.VMEM((seq, TQ), F32), pltpu.VMEM((1, TQ), F32),
                        pltpu.VMEM((2, 8, TQ), F32), pltpu.VMEM((A_HEADS, AUG, TQ), BF16),
                        pltpu.VMEM((A_HEADS, 1, TQ), F32), pltpu.VMEM((A_HEADS, 1, TQ), F32),
                        pltpu.VMEM((A_HEADS * HEAD_DIM, TQ), F32)],
        compiler_params=pltpu.CompilerParams(dimension_semantics=("arbitrary", "arbitrary"),
                                             vmem_limit_bytes=VMEM_LIMIT),
        name="dsa_attn",
    )(iqt, ik4, wft, aqt, ak, posx, avt)


def _rms_cols(xt, g_col):
    return xt * lax.rsqrt(jnp.mean(xt * xt, axis=0, keepdims=True) + EPS) * g_col


MOE_TMG = 256
MOE_TMC = 256
ROW_TILE = 8
ROUTE_COLS = 128


def _first_index_of_max(vals, row, big):
    m = jnp.max(vals, axis=0, keepdims=True)
    idx = jnp.min(jnp.where(vals == m, row, big), axis=0, keepdims=True)
    return m, idx


def _route(h, wr_ref, br_ref):
    h_hi, h_lo = _split_bf16(h)
    both = _dot(h_hi, wr_ref[...])
    logits = (both[:, 0:ROUTE_COLS] + both[:, ROUTE_COLS:2 * ROUTE_COLS]
              + _dot(h_lo, wr_ref[:, 0:ROUTE_COLS]) + br_ref[...])
    logits_t = logits.T
    tm = h.shape[0]
    ninf = -jnp.inf
    row8 = lax.broadcasted_iota(jnp.int32, (8, tm), 0)
    gl = jnp.where(row8 < N_GROUPS, logits_t[0:8], ninf)
    gmax, gsel = _first_index_of_max(gl, row8, 1 << 20)
    p_group = 1.0 / jnp.sum(jnp.exp(gl - gmax), axis=0, keepdims=True)
    expert = lax.broadcasted_iota(jnp.int32, (N_EXPERTS, tm), 0)
    assert EXPERTS_PER_GROUP == 8
    el = jnp.where((expert >> 3) == gsel, logits_t[32:32 + N_EXPERTS], ninf)
    v1, i1 = _first_index_of_max(el, expert, 1 << 20)
    el2 = jnp.where(expert == i1, ninf, el)
    v2, i2 = _first_index_of_max(el2, expert, 1 << 20)
    e2 = jnp.exp(v2 - v1)
    w1 = 1.0 / (1.0 + e2)
    w2 = e2 / (1.0 + e2)
    return i1, i2, w1 * p_group, w2 * p_group


def _token_rows(s, n_tokens, first_token=0):
    return pl.ds(first_token * ROW_TILE + s, n_tokens, stride=ROW_TILE)


def _to_token_tiles(ref, x):
    for s in range(ROW_TILE):
        ref[_token_rows(s, x.shape[0]), :] = x[:, s * 128:(s + 1) * 128]


def _token_tile(ref, t):
    start = t * ROW_TILE if isinstance(t, int) else pl.multiple_of(t * ROW_TILE, ROW_TILE)
    return ref.at[pl.ds(start, ROW_TILE), :]


def _moe_route_kernel(oat_ref, obt_ref, x_ref, mod_ref, ga_ref, gb_ref, wo_ref,
                      gf_ref, wr_ref, br_ref,
                      x1_ref, h_ref, info_ref, infot_ref, cnt_ref, run_ref):
    @pl.when(pl.program_id(0) == 0)
    def _():
        run_ref[...] = jnp.zeros_like(run_ref)

    oa = _rms_cols(oat_ref[0], ga_ref[...]).astype(BF16)
    ob = _rms_cols(obt_ref[0], gb_ref[...]).astype(BF16)
    y = _dot_tn(oa, wo_ref[0:512, :]) + _dot_tn(ob, wo_ref[512:1024, :])
    x1 = x_ref[...] + mod_ref[0, 2:3, :] * y
    x1_ref[...] = x1

    h = _rms(x1, gf_ref[...]) * (1.0 + mod_ref[0, 4:5, :]) + mod_ref[0, 3:4, :]
    _to_token_tiles(h_ref, h)
    i1, i2, w1, w2 = _route(h, wr_ref, br_ref)
    tm = h.shape[0]
    expert = lax.broadcasted_iota(jnp.int32, (N_EXPERTS, tm), 0)
    picked = jnp.where((expert == i1) | (expert == i2), 1.0, 0.0)
    earlier = jnp.where(lax.broadcasted_iota(jnp.int32, (tm, tm), 0)
                        < lax.broadcasted_iota(jnp.int32, (tm, tm), 1), 1.0, 0.0).astype(BF16)
    before = _dot(picked.astype(BF16), earlier) + run_ref[...]
    rank1 = jnp.sum(jnp.where(expert == i1, before, 0.0), axis=0, keepdims=True)
    rank2 = jnp.sum(jnp.where(expert == i2, before, 0.0), axis=0, keepdims=True)
    run_ref[...] += jnp.sum(picked, axis=1, keepdims=True)
    cnt_ref[...] = run_ref[...]
    row8 = lax.broadcasted_iota(jnp.int32, (8, tm), 0)
    info_t = jnp.where(row8 == 0, i1.astype(F32), 0.0)
    info_t = jnp.where(row8 == 1, i2.astype(F32), info_t)
    info_t = jnp.where(row8 == 2, rank1, info_t)
    info_t = jnp.where(row8 == 3, rank2, info_t)
    info_t = jnp.where(row8 == 4, w1, info_t)
    info_t = jnp.where(row8 == 5, w2, info_t)
    infot_ref[...] = info_t
    info_ref[...] = jnp.concatenate([info_t, jnp.zeros((ROUTE_COLS - 8, tm), F32)], axis=0).T


def _moe_route(oat, obt, x2, mod3, g_out_a, g_out_b, w_out_bf, g_ffn, wr_hi, wr_lo, b_route, seq):
    n, d = x2.shape
    tm = 512
    per_b = seq // tm
    row = lambda i: (i, 0)
    const = lambda i: (0, 0)
    blk_t = lambda i: (i // per_b, 0, i % per_b)
    return pl.pallas_call(
        _moe_route_kernel,
        out_shape=[jax.ShapeDtypeStruct((n, d), F32),
                   jax.ShapeDtypeStruct((n * ROW_TILE, d // ROW_TILE), F32),
                   jax.ShapeDtypeStruct((n, ROUTE_COLS), F32), jax.ShapeDtypeStruct((8, n), F32),
                   jax.ShapeDtypeStruct((N_EXPERTS, 1), F32)],
        grid=(n // tm,),
        in_specs=[pl.BlockSpec((1, 512, tm), blk_t), pl.BlockSpec((1, 512, tm), blk_t),
                  pl.BlockSpec((tm, d), row),
                  pl.BlockSpec((1, N_MOD, d), lambda i: (i // per_b, 0, 0)),
                  pl.BlockSpec((512, 1), const), pl.BlockSpec((512, 1), const),
                  pl.BlockSpec((d, d), const),
                  pl.BlockSpec((1, d), const),
                  pl.BlockSpec((d, 2 * ROUTE_COLS), const),
                  pl.BlockSpec((1, ROUTE_COLS), const)],
        out_specs=[pl.BlockSpec((tm, d), row),
                   pl.BlockSpec((tm * ROW_TILE, d // ROW_TILE), row),
                   pl.BlockSpec((tm, ROUTE_COLS), row), pl.BlockSpec((8, tm), lambda i: (0, i)),
                   pl.BlockSpec((N_EXPERTS, 1), const)],
        scratch_shapes=[pltpu.VMEM((N_EXPERTS, 1), F32)],
        compiler_params=pltpu.CompilerParams(dimension_semantics=("arbitrary",),
                                             vmem_limit_bytes=VMEM_LIMIT),
        name="mix_out_moe_route",
    )(oat, obt, x2, mod3, g_out_a.reshape(-1, 1), g_out_b.reshape(-1, 1), w_out_bf,
      g_ffn.reshape(1, d), jnp.concatenate([wr_hi, wr_lo], axis=1), b_route)


def _wait_token_copies(src_hbm, dst, sem, n_tokens):
    pltpu.make_async_copy(src_hbm.at[pl.ds(0, n_tokens * ROW_TILE), :], dst, sem).wait()


def _moe_scatter_kernel(dest_ref, pad_start_ref, pad_len_ref, nu_ref, h_ref, xs_hbm, zero_ref, sem):
    i = pl.program_id(0)
    n_pairs = 2 * MOE_TMC

    @pl.when(i == 0)
    def _():
        zero_ref[...] = jnp.zeros_like(zero_ref)

        def pad_copies(e):
            n = pad_len_ref[e]
            size = MOE_TMG // 2
            while size >= 1:
                first = pad_start_ref[e] + (n & ~(2 * size - 1))
                copy = pltpu.make_async_copy(
                    zero_ref.at[pl.ds(0, size * ROW_TILE), :],
                    xs_hbm.at[pl.ds(pl.multiple_of(first * ROW_TILE, ROW_TILE), size * ROW_TILE), :],
                    sem.at[1])
                yield (n & size) != 0, copy
                size //= 2

        for e in range(N_EXPERTS):
            for present, copy in pad_copies(e):
                pl.when(present)(copy.start)
        for e in range(N_EXPERTS):
            for present, copy in pad_copies(e):
                pl.when(present)(copy.wait)

        def row_tile(t):
            return xs_hbm.at[pl.ds(pl.multiple_of(t * (MOE_TMG * ROW_TILE), MOE_TMG * ROW_TILE),
                                   MOE_TMG * ROW_TILE), :]

        n_tiles = xs_hbm.shape[0] // (MOE_TMG * ROW_TILE)

        def fill_tile(t, _):
            pltpu.make_async_copy(zero_ref, row_tile(t), sem.at[1]).start()
            return 0

        def drain_tile(t, _):
            pltpu.make_async_copy(zero_ref, row_tile(0), sem.at[1]).wait()
            return 0

        lax.fori_loop(nu_ref[0], n_tiles, fill_tile, 0)
        lax.fori_loop(nu_ref[0], n_tiles, drain_tile, 0)

    for r in range(MOE_TMC):
        for j in range(2):
            pltpu.make_async_copy(_token_tile(h_ref, r),
                                  _token_tile(xs_hbm, dest_ref[i * n_pairs + j * MOE_TMC + r]),
                                  sem.at[0]).start(priority=j)
    _wait_token_copies(xs_hbm, xs_hbm.at[pl.ds(0, n_pairs * ROW_TILE), :], sem.at[0], n_pairs)


def _moe_scatter(dest, pad_start, pad_len, n_used, h2, n_rows):
    n = h2.shape[0] // ROW_TILE
    grid_spec = pltpu.PrefetchScalarGridSpec(
        num_scalar_prefetch=4,
        grid=(n // MOE_TMC,),
        in_specs=[pl.BlockSpec((MOE_TMC * ROW_TILE, h2.shape[1]), lambda i, *_: (i, 0))],
        out_specs=pl.BlockSpec(memory_space=pl.ANY),
        scratch_shapes=[pltpu.VMEM((MOE_TMG * ROW_TILE, h2.shape[1]), F32),
                        pltpu.SemaphoreType.DMA((2,))])
    return pl.pallas_call(
        _moe_scatter_kernel,
        out_shape=jax.ShapeDtypeStruct((n_rows * ROW_TILE, h2.shape[1]), F32),
        grid_spec=grid_spec,
        compiler_params=pltpu.CompilerParams(dimension_semantics=("arbitrary",),
                                             vmem_limit_bytes=VMEM_LIMIT),
        name="moe_scatter",
    )(dest, pad_start, pad_len, n_used, h2)


TILES_PER_STEP = 2


def _moe_expert_kernel(te_ref, nu_ref, x_ref, *refs):
    n = TILES_PER_STEP
    wg_refs, wu_refs, wd_refs = (refs[k * n:(k + 1) * n] for k in range(3))
    y_ref, xs_ref, wgb_ref, wub_ref, wdb_ref = refs[3 * n:]
    i = pl.program_id(0)
    step_used = n * i < nu_ref[0]

    @pl.when(jnp.logical_not(step_used))
    def _():
        y_ref[...] = jnp.zeros_like(y_ref)

    for slot in range(n):
        tile = n * i + slot
        new_expert = jnp.logical_or(i == 0, te_ref[tile] != te_ref[jnp.maximum(tile - n, 0)])

        @pl.when(jnp.logical_and(step_used, new_expert))
        def _(slot=slot):
            wgb_ref[slot] = wg_refs[slot][0].astype(BF16)
            wub_ref[slot] = wu_refs[slot][0].astype(BF16)
            wdb_ref[slot] = wd_refs[slot][0].astype(BF16)

    @pl.when(step_used)
    def _():
        for slot in range(n):
            for s in range(ROW_TILE):
                rows = _token_rows(s, MOE_TMG, first_token=slot * MOE_TMG)
                xs_ref[slot, :, s * 128:(s + 1) * 128] = x_ref[rows, :].astype(BF16)
        acts = []
        for slot in range(n):
            x = xs_ref[slot]
            hg = _dot(x, wgb_ref[slot])
            hu = _dot(x, wub_ref[slot])
            acts.append((hg * jax.nn.sigmoid(hg) * hu).astype(BF16))
        for slot in range(n):
            y = _dot(acts[slot], wdb_ref[slot])
            for s in range(ROW_TILE):
                rows = _token_rows(s, MOE_TMG, first_token=slot * MOE_TMG)
                y_ref[rows, :] = y[:, s * 128:(s + 1) * 128]


def _moe_experts(tile_expert, n_used, xsorted, wg, wu, wd):
    n_rows = xsorted.shape[0] // ROW_TILE
    d, ff = wg.shape[1], wg.shape[2]
    n = TILES_PER_STEP
    block = (n * MOE_TMG * ROW_TILE, xsorted.shape[1])
    slots = range(n)
    w_in_specs = [pl.BlockSpec((1, d, ff), lambda i, te, nu, s=s: (te[n * i + s], 0, 0)) for s in slots]
    w_out_specs = [pl.BlockSpec((1, ff, d), lambda i, te, nu, s=s: (te[n * i + s], 0, 0)) for s in slots]
    grid_spec = pltpu.PrefetchScalarGridSpec(
        num_scalar_prefetch=2,
        grid=(n_rows // (MOE_TMG * n),),
        in_specs=[pl.BlockSpec(block, lambda i, te, nu: (jnp.minimum(i, (nu[0] - 1) // n), 0))]
        + w_in_specs + w_in_specs + w_out_specs,
        out_specs=pl.BlockSpec(block, lambda i, te, nu: (i, 0)),
        scratch_shapes=[pltpu.VMEM((n, MOE_TMG, d), BF16), pltpu.VMEM((n, d, ff), BF16),
                        pltpu.VMEM((n, d, ff), BF16), pltpu.VMEM((n, ff, d), BF16)])
    return pl.pallas_call(
        _moe_expert_kernel,
        out_shape=jax.ShapeDtypeStruct(xsorted.shape, F32),
        grid_spec=grid_spec,
        compiler_params=pltpu.CompilerParams(dimension_semantics=("arbitrary",),
                                             vmem_limit_bytes=VMEM_LIMIT),
        name="moe_experts",
    )(tile_expert, n_used, xsorted, *([wg] * n), *([wu] * n), *([wd] * n))


def _moe_combine_kernel(dest_ref, y_hbm, x_ref, info_ref, mod_ref, gfin_ref, o_ref, ybuf, x2_ref, sem):
    i = pl.program_id(0)
    nt = pl.num_programs(0)
    slot = lax.rem(i, 2)
    n_pairs = 2 * MOE_TMC

    def start_gather(tile, to_slot):
        for r in range(n_pairs):
            pltpu.make_async_copy(_token_tile(y_hbm, dest_ref[tile * n_pairs + r]),
                                  _token_tile(ybuf.at[to_slot], r), sem.at[to_slot]).start(priority=r % 2)

    @pl.when(i == 0)
    def _():
        start_gather(0, 0)

    _wait_token_copies(y_hbm, ybuf.at[slot], sem.at[slot], n_pairs)

    @pl.when(i + 1 < nt)
    def _():
        start_gather(i + 1, 1 - slot)

    w1 = info_ref[:, 4:5]
    w2 = info_ref[:, 5:6]
    sumsq = jnp.zeros((MOE_TMC, 1), F32)
    for s in range(ROW_TILE):
        cols = slice(s * 128, (s + 1) * 128)
        y = (w1 * ybuf[slot, _token_rows(s, MOE_TMC), :]
             + w2 * ybuf[slot, _token_rows(s, MOE_TMC, first_token=MOE_TMC), :])
        x2 = x_ref[:, cols] + mod_ref[0, 5:6, cols] * y
        x2_ref[:, cols] = x2
        sumsq = sumsq + jnp.sum(x2 * x2, axis=-1, keepdims=True)
    d = x2_ref.shape[1]
    o_ref[...] = x2_ref[...] * lax.rsqrt(sumsq / d + EPS) * gfin_ref[...]


def _moe_combine(dest, ysorted, x1, info, mod3, g_final, seq):
    n, d = x1.shape
    per_b = seq // MOE_TMC
    grid_spec = pltpu.PrefetchScalarGridSpec(
        num_scalar_prefetch=1,
        grid=(n // MOE_TMC,),
        in_specs=[pl.BlockSpec(memory_space=pl.ANY),
                  pl.BlockSpec((MOE_TMC, d), lambda i, ds: (i, 0)),
                  pl.BlockSpec((MOE_TMC, ROUTE_COLS), lambda i, ds: (i, 0)),
                  pl.BlockSpec((1, N_MOD, d), lambda i, ds: (i // per_b, 0, 0)),
                  pl.BlockSpec((1, d), lambda i, ds: (0, 0))],
        out_specs=pl.BlockSpec((MOE_TMC, d), lambda i, ds: (i, 0)),
        scratch_shapes=[pltpu.VMEM((2, 2 * MOE_TMC * ROW_TILE, ysorted.shape[1]), F32),
                        pltpu.VMEM((MOE_TMC, d), F32), pltpu.SemaphoreType.DMA((2,))])
    return pl.pallas_call(
        _moe_combine_kernel,
        out_shape=jax.ShapeDtypeStruct((n, d), F32),
        grid_spec=grid_spec,
        compiler_params=pltpu.CompilerParams(dimension_semantics=("arbitrary",),
                                             vmem_limit_bytes=VMEM_LIMIT),
        name="moe_combine",
    )(dest, ysorted, x1, info, mod3, g_final.reshape(1, d))


def _mix_out_and_moe(oat, obt, x2, mod3, g_out_a, g_out_b, w_out_bf, g_ffn, wr_hi, wr_lo, b_route,
                     wg, wu, wd, g_final, seq):
    n, d = x2.shape
    x1, h2, info, infot, counts = _moe_route(oat, obt, x2, mod3, g_out_a, g_out_b, w_out_bf,
                                             g_ffn, wr_hi, wr_lo, b_route, seq)

    e1, e2, rank1, rank2 = [infot[k].astype(jnp.int32) for k in range(4)]
    cnt = counts[:, 0].astype(jnp.int32)
    padded = ((cnt + MOE_TMG - 1) // MOE_TMG) * MOE_TMG
    seg_end = jnp.cumsum(padded)
    expert_ids = jnp.arange(N_EXPERTS, dtype=jnp.int32)

    def seg_start_of(e):
        return jnp.sum(jnp.where(expert_ids[None, :] < e[:, None], padded[None, :], 0), axis=1)

    dest1 = seg_start_of(e1) + rank1
    dest2 = seg_start_of(e2) + rank2
    n_rows = 2 * n + N_EXPERTS * MOE_TMG
    tile_start = jnp.arange(n_rows // MOE_TMG, dtype=jnp.int32) * MOE_TMG
    tile_expert = jnp.minimum(jnp.sum((tile_start[:, None] >= seg_end[None, :]).astype(jnp.int32), axis=1),
                              N_EXPERTS - 1)
    n_used = (seg_end[N_EXPERTS - 1:] // MOE_TMG).astype(jnp.int32)
    dest = jnp.concatenate([dest1.reshape(-1, MOE_TMC), dest2.reshape(-1, MOE_TMC)], axis=1).reshape(-1)

    seg_start = seg_end - padded
    xsorted = _moe_scatter(dest, seg_start + cnt, padded - cnt, n_used, h2, n_rows)
    ysorted = _moe_experts(tile_expert, n_used, xsorted, wg, wu, wd)
    return _moe_combine(dest, ysorted, x1, info, mod3, g_final, seq)


def _layer(x3, c, w_ada, b_ada, g_mix, w_in, b_forget, g_out_a, g_out_b, w_out,
           g_ffn, w_group, b_group, w_router, b_router, w_gate, w_up, w_down, g_final):
    bsz, seq, d = x3.shape
    mod3 = _ada_mod(c, w_ada, b_ada).reshape(bsz, N_MOD, d)

    w_t = w_in.T

    def pad_heads(w, n_heads):
        w = w.reshape(d, n_heads, HEAD_DIM)
        return jnp.concatenate([w, jnp.zeros_like(w)], axis=-1).reshape(d, n_heads * AUG)

    w_wf_hi, w_wf_lo = _split_bf16(jnp.concatenate([w_t[1344:1352], w_t[2888:2896]], axis=0))
    w_iq_wf = jnp.concatenate([w_t[768:1280].astype(BF16), w_wf_hi, w_wf_lo], axis=0)
    weights = [w_t[0:512].astype(BF16), pad_heads(w_in[:, 512:640], A_KV_HEADS).astype(BF16),
               w_t[640:768].astype(BF16), w_t[1352:1864].astype(BF16),
               w_in[:, 1864:2376].astype(BF16), w_t[2376:2888].astype(BF16),
               w_iq_wf, w_in[:, 1280:1344].astype(BF16), w_wf_hi]
    aqt, ak, avt, bqt, bk, bvt, iqt, ik4, wft = _in_proj(x3, mod3, g_mix, weights)

    cumt, kaug = _fox_cum(wft, b_forget, bk)
    obt = _fox_attn(bqt, kaug, bvt, cumt)

    pos = jnp.arange(seq, dtype=jnp.int32)[:, None]
    lane = jnp.arange(AUG, dtype=jnp.int32)[None, :] - HEAD_DIM
    posx = jnp.where((lane >= 0) & (lane < 3), pos >> 7,
                     jnp.where((lane >= 3) & (lane < 6), pos & 127,
                               jnp.where((lane >= 6) & (lane < 9), 1, 0))).astype(BF16)
    oat = _dsa_attn(iqt, ik4, wft, aqt, ak, posx, avt)

    w_r = jnp.concatenate([w_group, jnp.zeros((d, 32 - N_GROUPS), F32),
                           jnp.transpose(w_router, (1, 0, 2)).reshape(d, N_EXPERTS),
                           jnp.zeros((d, ROUTE_COLS - 64), F32)], axis=1)
    b_r = jnp.concatenate([b_group, jnp.zeros((32 - N_GROUPS,), F32), b_router.reshape(-1),
                           jnp.zeros((ROUTE_COLS - 64,), F32)]).reshape(1, ROUTE_COLS)
    wr_hi, wr_lo = _split_bf16(w_r)
    out = _mix_out_and_moe(oat, obt, x3.reshape(bsz * seq, d), mod3, g_out_a, g_out_b,
                           w_out.astype(BF16), g_ffn, wr_hi, wr_lo, b_r, w_gate, w_up, w_down,
                           g_final, seq)
    return out.reshape(bsz, seq, d)


def kernel(x, c, w_ada, b_ada, g_mix, w_in, b_forget, g_out_a, g_out_b, w_out, g_ffn, w_group,
           b_group, w_router, b_router, w_gate, w_up, w_down, g_final):
    depth = w_ada.shape[0]
    assert depth == 1, "final norm is fused into the single layer's MoE kernel"
    return _layer(x, c, w_ada[0], b_ada[0], g_mix[0], w_in[0], b_forget[0], g_out_a[0], g_out_b[0],
                  w_out[0], g_ffn[0], w_group[0], b_group[0], w_router[0], b_router[0], w_gate[0],
                  w_up[0], w_down[0], g_final)
```
